```python
import math
import jax, jax.numpy as jnp
from jax import lax
import numpy as np

D_MODEL = 1024
BATCH = 2
SEQ = 8192
DEPTH = 1
DEC_BATCH = 128
DEC_SEQ = 4
PAST_LEN = 8192
PAGE_SIZE = 128

NORM_EPS = 1e-6
PLE_DIM = 256
FFN_DIM = ((8 * D_MODEL) // 3 + 127) // 128 * 128
GDN_HEADS = D_MODEL // 256
GDN_DK = 128
GDN_DV = 128
GDN_KEY_DIM = GDN_HEADS * GDN_DK
GDN_VAL_DIM = GDN_HEADS * GDN_DV
CONV_DIM = 2 * GDN_KEY_DIM + GDN_VAL_DIM
CONV_WIDTH = 4
GDN_CHUNK = 64
SWA_HEADS = D_MODEL // 128
SWA_KV_HEADS = SWA_HEADS // 4
SWA_GROUP = SWA_HEADS // SWA_KV_HEADS
SWA_HEAD_DIM = 64
SWA_Q_DIM = SWA_HEADS * SWA_HEAD_DIM
SWA_KV_DIM = SWA_KV_HEADS * SWA_HEAD_DIM
SWA_WINDOW = 128
NUM_BUCKETS = 32
REL_MAX_DISTANCE = 128
MIX_DIM = GDN_VAL_DIM + SWA_Q_DIM
PROJ_SPLITS = (CONV_DIM, GDN_VAL_DIM, GDN_HEADS, GDN_HEADS, SWA_Q_DIM, SWA_KV_DIM, SWA_KV_DIM)
PROJ_DIM = sum(PROJ_SPLITS)

kernel_name = 'hymba_gdn_swa_macaron_step'


def rmsnorm(x, gain):
    x32 = x.astype(jnp.float32)
    y = x32 * lax.rsqrt(jnp.mean(x32 * x32, axis=-1, keepdims=True) + NORM_EPS)
    return (y * gain.astype(jnp.float32)).astype(x.dtype)


def l2norm(t):
    return t * lax.rsqrt(jnp.sum(t * t, axis=-1, keepdims=True) + 1e-6)


def swiglu(h, w_gate, w_up, w_down):
    return (jax.nn.silu(h @ w_gate) * (h @ w_up)) @ w_down


def _chunk(t, n, c):
    return jnp.moveaxis(t.reshape((t.shape[0], n, c) + t.shape[2:]), 3, 1)


def gated_delta_chunked(q, k, v, g, beta, s0):
    B, L, H, DK = q.shape
    DV = v.shape[-1]
    C = min(GDN_CHUNK, L)
    n = -(-L // C)
    pad = n * C - L
    if pad:
        q, k, v, g, beta = [jnp.pad(t, [(0, 0), (0, pad)] + [(0, 0)] * (t.ndim - 2)) for t in (q, k, v, g, beta)]
    q, k, v, g, beta = [_chunk(t, n, C) for t in (q, k, v, g, beta)]
    gc = jnp.cumsum(g, axis=-1)
    idx = jnp.arange(C)
    incl = idx[:, None] >= idx[None, :]
    strict = idx[:, None] > idx[None, :]
    decay = jnp.exp(jnp.where(incl, gc[..., :, None] - gc[..., None, :], -jnp.inf))
    kb = k * beta[..., None]
    lmat = jnp.where(strict, jnp.einsum('bhncd,bhnsd->bhncs', kb, k) * decay, 0.0)
    a_mat = lmat + jnp.eye(C, dtype=lmat.dtype)
    rhs = jnp.concatenate([v * beta[..., None], kb * jnp.exp(gc)[..., None]], axis=-1)
    sol = lax.linalg.triangular_solve(a_mat, rhs, left_side=True, lower=True, unit_diagonal=True)
    u, w = sol[..., :DV], sol[..., DV:]
    qk = jnp.einsum('bhncd,bhnsd->bhncs', q, k) * decay
    q_dec = q * jnp.exp(gc)[..., None]
    k_dec = k * jnp.exp(gc[..., -1:] - gc)[..., None]
    chunk_decay = jnp.exp(gc[..., -1])

    def step(s, xs):
        qk_c, qd_c, kd_c, u_c, w_c, cd_c = xs
        v_new = u_c - jnp.einsum('bhcd,bhde->bhce', w_c, s)
        o_c = jnp.einsum('bhcd,bhde->bhce', qd_c, s) + jnp.einsum('bhcs,bhse->bhce', qk_c, v_new)
        s = s * cd_c[..., None, None] + jnp.einsum('bhcd,bhce->bhde', kd_c, v_new)
        return s, o_c

    xs = tuple(jnp.moveaxis(t, 2, 0) for t in (qk, q_dec, k_dec, u, w, chunk_decay))
    s_final, o = lax.scan(step, s0, xs)
    o = jnp.moveaxis(jnp.moveaxis(o, 0, 2).reshape(B, H, n * C, DV), 1, 2)[:, :L]
    return o, s_final


def gdn_branch(qkv_raw, z, b, a, conv_prev, s0, conv_w, a_log, dt_bias, norm_gain):
    B, L, _ = qkv_raw.shape
    xp = jnp.concatenate([conv_prev.astype(qkv_raw.dtype), qkv_raw], axis=1)
    y = xp[:, 0:L] * conv_w[0]
    for j in range(1, CONV_WIDTH):
        y = y + xp[:, j:j + L] * conv_w[j]
    y = jax.nn.silu(y).astype(jnp.float32)
    q = l2norm(y[..., :GDN_KEY_DIM].reshape(B, L, GDN_HEADS, GDN_DK)) * GDN_DK ** -0.5
    k = l2norm(y[..., GDN_KEY_DIM:2 * GDN_KEY_DIM].reshape(B, L, GDN_HEADS, GDN_DK))
    v = y[..., 2 * GDN_KEY_DIM:].reshape(B, L, GDN_HEADS, GDN_DV)
    beta = jax.nn.sigmoid(b.astype(jnp.float32))
    g = -jnp.exp(a_log.astype(jnp.float32)) * jax.nn.softplus(a.astype(jnp.float32) + dt_bias.astype(jnp.float32))
    o, s_new = gated_delta_chunked(q, k, v, g, beta, s0.astype(jnp.float32))
    o = rmsnorm(o, norm_gain) * jax.nn.silu(z.astype(jnp.float32).reshape(B, L, GDN_HEADS, GDN_DV))
    return o.reshape(B, L, GDN_VAL_DIM).astype(qkv_raw.dtype), xp[:, L:], s_new


def t5_bias(dist, rel_bias):
    d = jnp.maximum(dist, 0)
    exact = NUM_BUCKETS // 2
    log_ratio = jnp.log(jnp.maximum(d, 1).astype(jnp.float32) / exact) / math.log(REL_MAX_DISTANCE / exact)
    large = jnp.minimum(exact + (log_ratio * (NUM_BUCKETS - exact)).astype(jnp.int32), NUM_BUCKETS - 1)
    bucket = jnp.where(d < exact, d, large)
    bias = jnp.moveaxis(rel_bias[bucket].astype(jnp.float32), -1, 0)
    return bias.reshape((SWA_KV_HEADS, SWA_GROUP) + dist.shape)


def sink_softmax(s, sinks):
    sink = jnp.broadcast_to(sinks.reshape(SWA_KV_HEADS, SWA_GROUP, 1, 1).astype(jnp.float32), s.shape[:-1] + (1,))
    return jax.nn.softmax(jnp.concatenate([s, sink], axis=-1), axis=-1)[..., :-1]


def swa_banded(q, k, v, sinks, rel_bias):
    B, L = q.shape[:2]
    W = SWA_WINDOW
    nb = L // W
    qb = q.reshape(B, nb, W, SWA_KV_HEADS, SWA_GROUP, SWA_HEAD_DIM)

    def two_blocks(t):
        tp = jnp.pad(t, [(0, 0), (W, 0), (0, 0), (0, 0)]).reshape(B, nb + 1, W, SWA_KV_HEADS, SWA_HEAD_DIM)
        return jnp.concatenate([tp[:, :-1], tp[:, 1:]], axis=2)

    kb, vb = two_blocks(k), two_blocks(v)
    dist = W + jnp.arange(W)[:, None] - jnp.arange(2 * W)[None, :]
    key_pos = (jnp.arange(nb)[:, None] - 1) * W + jnp.arange(2 * W)[None, :]
    valid = ((dist >= 0) & (dist < W))[None] & (key_pos >= 0)[:, None, :]
    bias = t5_bias(dist, rel_bias)
    s = jnp.einsum('bnqhgd,bnkhd->bnhgqk', qb, kb).astype(jnp.float32) * SWA_HEAD_DIM ** -0.5 + bias
    s = jnp.where(valid[None, :, None, None], s, -jnp.inf)
    p = sink_softmax(s, sinks).astype(v.dtype)
    o = jnp.einsum('bnhgqk,bnkhd->bnqhgd', p, vb)
    return o.reshape(B, L, SWA_Q_DIM)


def swa_buffered(q, k, v, k_buf, v_buf, sinks, rel_bias):
    B, T = q.shape[:2]
    wb = k_buf.shape[1]
    k_all = jnp.concatenate([k_buf.astype(k.dtype), k], axis=1)
    v_all = jnp.concatenate([v_buf.astype(v.dtype), v], axis=1)
    dist = wb + jnp.arange(T)[:, None] - jnp.arange(wb + T)[None, :]
    valid = (dist >= 0) & (dist < SWA_WINDOW)
    bias = t5_bias(dist, rel_bias)
    s = jnp.einsum('bqhgd,bkhd->bhgqk', q, k_all).astype(jnp.float32) * SWA_HEAD_DIM ** -0.5 + bias
    s = jnp.where(valid, s, -jnp.inf)
    p = sink_softmax(s, sinks).astype(v.dtype)
    o = jnp.einsum('bhgqk,bkhd->bqhgd', p, v_all).reshape(B, T, SWA_Q_DIM)
    return o, k_all[:, T:], v_all[:, T:]


def decoder_layer(x, ple, past, lw, rel_bias):
    B, L, _ = x.shape
    h = rmsnorm(x, lw['norm_ffn1_pre'])
    x = x + 0.5 * rmsnorm(swiglu(h, lw['ffn1_w_gate'], lw['ffn1_w_up'], lw['ffn1_w_down']), lw['norm_ffn1_post'])

    h = rmsnorm(x, lw['norm_mix_pre'])
    split_points = np.cumsum(PROJ_SPLITS)[:-1].tolist()
    qkv_raw, z, b, a, q_s, k_s, v_s = jnp.split(h @ lw['w_in'], split_points, axis=-1)
    if past is None:
        conv_prev = jnp.zeros((B, CONV_WIDTH - 1, CONV_DIM), x.dtype)
        s0 = jnp.zeros((B, GDN_HEADS, GDN_DK, GDN_DV), jnp.float32)
    else:
        conv_prev, s0, k_buf, v_buf = past
    gdn_out, conv_new, s_new = gdn_branch(qkv_raw, z, b, a, conv_prev, s0, lw['conv_w'],
                                          lw['gdn_a_log'], lw['gdn_dt_bias'], lw['gdn_norm'])
    q_s = q_s.reshape(B, L, SWA_KV_HEADS, SWA_GROUP, SWA_HEAD_DIM)
    k_s = k_s.reshape(B, L, SWA_KV_HEADS, SWA_HEAD_DIM)
    v_s = v_s.reshape(B, L, SWA_KV_HEADS, SWA_HEAD_DIM)
    if past is None:
        swa_out = swa_banded(q_s, k_s, v_s, lw['swa_sinks'], rel_bias)
        wb = min(SWA_WINDOW, L)
        k_new, v_new = k_s[:, L - wb:], v_s[:, L - wb:]
    else:
        swa_out, k_new, v_new = swa_buffered(q_s, k_s, v_s, k_buf, v_buf, lw['swa_sinks'], rel_bias)
    mix = jnp.concatenate([gdn_out, swa_out], axis=-1) @ lw['w_out']
    x = x + rmsnorm(mix, lw['norm_mix_post'])

    h = rmsnorm(x, lw['norm_ffn2_pre'])
    x = x + 0.5 * rmsnorm(swiglu(h, lw['ffn2_w_gate'], lw['ffn2_w_up'], lw['ffn2_w_down']), lw['norm_ffn2_post'])

    gate = jax.nn.sigmoid(x @ lw['ple_gate'])
    x = x + rmsnorm(gate * (ple @ lw['ple_proj']), lw['norm_ple_post'])
    return x, conv_new, s_new, k_new, v_new


def setup_inputs(seed: int = 0) -> dict:
    key = jax.random.key(seed)
    keys = iter(jax.random.split(key, 48))

    def nrm(shape, scale):
        return scale * jax.random.normal(next(keys), shape, jnp.float32)

    def gain(shape):
        return 1.0 + nrm(shape, 0.05)

    swa_buf = min(SWA_WINDOW, PAST_LEN)
    dt = jnp.exp(jax.random.uniform(next(keys), (DEPTH, GDN_HEADS), jnp.float32, math.log(1e-3), math.log(1e-1)))
    a_init = jax.random.uniform(next(keys), (DEPTH, GDN_HEADS), jnp.float32, 1.0, 16.0)
    return {
        'x_prompt': nrm((BATCH, SEQ, D_MODEL), 1.0),
        'x_sample': nrm((DEC_BATCH, DEC_SEQ, D_MODEL), 1.0),
        'state_conv': nrm((DEPTH, DEC_BATCH, CONV_WIDTH - 1, CONV_DIM), 1.0),
        'state_gdn': nrm((DEPTH, DEC_BATCH, GDN_HEADS, GDN_DK, GDN_DV), 0.5),
        'cache_swa_k': nrm((DEPTH, DEC_BATCH, swa_buf, SWA_KV_HEADS, SWA_HEAD_DIM), 1.0),
        'cache_swa_v': nrm((DEPTH, DEC_BATCH, swa_buf, SWA_KV_HEADS, SWA_HEAD_DIM), 1.0),
        'p_prompt': nrm((DEPTH, BATCH, SEQ, PLE_DIM), 1.0),
        'p_sample': nrm((DEPTH, DEC_BATCH, DEC_SEQ, PLE_DIM), 1.0),
        'rel_bias': nrm((NUM_BUCKETS, SWA_HEADS), 0.5),
        'norm_ffn1_pre': gain((DEPTH, D_MODEL)),
        'norm_ffn1_post': gain((DEPTH, D_MODEL)),
        'ffn1_w_gate': nrm((DEPTH, D_MODEL, FFN_DIM), D_MODEL ** -0.5),
        'ffn1_w_up': nrm((DEPTH, D_MODEL, FFN_DIM), D_MODEL ** -0.5),
        'ffn1_w_down': nrm((DEPTH, FFN_DIM, D_MODEL), FFN_DIM ** -0.5),
        'norm_mix_pre': gain((DEPTH, D_MODEL)),
        'norm_mix_post': gain((DEPTH, D_MODEL)),
        'w_in': nrm((DEPTH, D_MODEL, PROJ_DIM), D_MODEL ** -0.5),
        'conv_w': nrm((DEPTH, CONV_WIDTH, CONV_DIM), CONV_WIDTH ** -0.5),
        'gdn_a_log': jnp.log(a_init),
        'gdn_dt_bias': dt + jnp.log(-jnp.expm1(-dt)),
        'gdn_norm': gain((DEPTH, GDN_DV)),
        'swa_sinks': nrm((DEPTH, SWA_HEADS), 0.5),
        'w_out': nrm((DEPTH, MIX_DIM, D_MODEL), MIX_DIM ** -0.5),
        'norm_ffn2_pre': gain((DEPTH, D_MODEL)),
        'norm_ffn2_post': gain((DEPTH, D_MODEL)),
        'ffn2_w_gate': nrm((DEPTH, D_MODEL, FFN_DIM), D_MODEL ** -0.5),
        'ffn2_w_up': nrm((DEPTH, D_MODEL, FFN_DIM), D_MODEL ** -0.5),
        'ffn2_w_down': nrm((DEPTH, FFN_DIM, D_MODEL), FFN_DIM ** -0.5),
        'ple_gate': nrm((DEPTH, D_MODEL, D_MODEL), D_MODEL ** -0.5),
        'ple_proj': nrm((DEPTH, PLE_DIM, D_MODEL), PLE_DIM ** -0.5),
        'norm_ple_post': gain((DEPTH, D_MODEL)),
    }


def reference(x_prompt, x_sample, state_conv, state_gdn, cache_swa_k, cache_swa_v, p_prompt, p_sample,
              rel_bias, norm_ffn1_pre, norm_ffn1_post, ffn1_w_gate, ffn1_w_up, ffn1_w_down,
              norm_mix_pre, norm_mix_post, w_in, conv_w, gdn_a_log, gdn_dt_bias, gdn_norm, swa_sinks,
              w_out, norm_ffn2_pre, norm_ffn2_post, ffn2_w_gate, ffn2_w_up, ffn2_w_down,
              ple_gate, ple_proj, norm_ple_post):
    yp, ys = x_prompt, x_sample
    conv_p, gdn_p, k_p, v_p = [], [], [], []
    conv_s, gdn_s, k_s, v_s = [], [], [], []
    for i in range(DEPTH):
        lw = dict(norm_ffn1_pre=norm_ffn1_pre[i], norm_ffn1_post=norm_ffn1_post[i],
                  ffn1_w_gate=ffn1_w_gate[i], ffn1_w_up=ffn1_w_up[i], ffn1_w_down=ffn1_w_down[i],
                  norm_mix_pre=norm_mix_pre[i], norm_mix_post=norm_mix_post[i], w_in=w_in[i],
                  conv_w=conv_w[i], gdn_a_log=gdn_a_log[i], gdn_dt_bias=gdn_dt_bias[i],
                  gdn_norm=gdn_norm[i], swa_sinks=swa_sinks[i], w_out=w_out[i],
                  norm_ffn2_pre=norm_ffn2_pre[i], norm_ffn2_post=norm_ffn2_post[i],
                  ffn2_w_gate=ffn2_w_gate[i], ffn2_w_up=ffn2_w_up[i], ffn2_w_down=ffn2_w_down[i],
                  ple_gate=ple_gate[i], ple_proj=ple_proj[i], norm_ple_post=norm_ple_post[i])
        yp, c1, s1, k1, v1 = decoder_layer(yp, p_prompt[i], None, lw, rel_bias)
        ys, c2, s2, k2, v2 = decoder_layer(ys, p_sample[i],
                                           (state_conv[i], state_gdn[i], cache_swa_k[i], cache_swa_v[i]),
                                           lw, rel_bias)
        conv_p.append(c1); gdn_p.append(s1); k_p.append(k1); v_p.append(v1)
        conv_s.append(c2); gdn_s.append(s2); k_s.append(k2); v_s.append(v2)
    return (yp, ys,
            jnp.stack(conv_p), jnp.stack(gdn_p), jnp.stack(k_p), jnp.stack(v_p),
            jnp.stack(conv_s), jnp.stack(gdn_s), jnp.stack(k_s), jnp.stack(v_s))
```

```python
import functools
import math

import jax
import jax.numpy as jnp
from jax import lax
from jax.experimental import pallas as pl
from jax.experimental.pallas import tpu as pltpu

F32 = jnp.float32
BF16 = jnp.bfloat16

D_MODEL = 1024
NORM_EPS = 1e-6
PLE_DIM = 256
FFN_DIM = 2816
GDN_HEADS = 4
GDN_DK = 128
GDN_DV = 128
GDN_KEY_DIM = GDN_HEADS * GDN_DK
GDN_VAL_DIM = GDN_HEADS * GDN_DV
CONV_DIM = 2 * GDN_KEY_DIM + GDN_VAL_DIM
CONV_WIDTH = 4
SWA_HEADS = 8
SWA_KV_HEADS = 2
SWA_GROUP = SWA_HEADS // SWA_KV_HEADS
SWA_HEAD_DIM = 64
SWA_Q_DIM = SWA_HEADS * SWA_HEAD_DIM
SWA_KV_DIM = SWA_KV_HEADS * SWA_HEAD_DIM
SWA_WINDOW = 128
NUM_BUCKETS = 32
REL_MAX_DISTANCE = 128

QKVZ_DIM = CONV_DIM + GDN_VAL_DIM
BA_DIM = 128
PROJ_PACKED = QKVZ_DIM + BA_DIM + SWA_Q_DIM + 2 * SWA_KV_DIM

GDN_CHUNK = 64
ROW_TILE = 256
DEC_SEQ_BLOCK = 8
VMEM_LIMIT = 56 * 1024 * 1024

NEG_INF = float("-inf")


def _resident(shape):
    nd = len(shape)
    return pl.BlockSpec(shape, lambda *_: (0,) * nd, pipeline_mode=pl.Buffered(1))


def _rms(x, gain):
    ms = jnp.mean(x * x, axis=-1, keepdims=True)
    return (x * lax.rsqrt(ms + NORM_EPS)) * gain


def _sigmoid(x):
    return 1.0 / (1.0 + jnp.exp(-x))


def _silu(x):
    return x * _sigmoid(x)


def _dot(a, b):
    return jnp.dot(a.astype(BF16), b.astype(BF16), preferred_element_type=F32)


def _dot_nt(a, b):
    return lax.dot_general(a.astype(BF16), b.astype(BF16), (((1,), (1,)), ((), ())),
                           preferred_element_type=F32)


def _dot_tn(a, b):
    return lax.dot_general(a.astype(BF16), b.astype(BF16), (((0,), (0,)), ((), ())),
                           preferred_element_type=F32)


def _split(a):
    hi = a.astype(BF16)
    lo = (a - hi.astype(F32)).astype(BF16)
    return hi, lo


def _dot3(a, b):
    ah, al = _split(a)
    bh, bl = _split(b)
    d = functools.partial(jnp.dot, preferred_element_type=F32)
    return d(ah, bh) + (d(ah, bl) + d(al, bh))


def _swiglu_block(h, wg_ref, wu_ref, wd_ref):
    g = jnp.dot(h, wg_ref[...], preferred_element_type=F32)
    u = jnp.dot(h, wu_ref[...], preferred_element_type=F32)
    a = (_silu(g) * u).astype(BF16)
    return jnp.dot(a, wd_ref[...], preferred_element_type=F32)


def _head_kernel(x_ref, g1pre_ref, g1post_ref, wg_ref, wu_ref, wd_ref, gmix_ref, win_ref,
                 x1_ref, qkvz_ref, ba_ref, qs_ref, kv_ref):
    x = x_ref[...]
    h = _rms(x, g1pre_ref[...]).astype(BF16)
    y = _swiglu_block(h, wg_ref, wu_ref, wd_ref)
    x1 = x + 0.5 * _rms(y, g1post_ref[...])
    x1_ref[...] = x1
    h2 = _rms(x1, gmix_ref[...]).astype(BF16)
    c0, c1, c2 = QKVZ_DIM, QKVZ_DIM + BA_DIM, QKVZ_DIM + BA_DIM + SWA_Q_DIM
    qkvz_ref[...] = jnp.dot(h2, win_ref[:, :c0], preferred_element_type=F32)
    ba_ref[...] = jnp.dot(h2, win_ref[:, c0:c1], preferred_element_type=F32)
    qs_ref[...] = jnp.dot(h2, win_ref[:, c1:c2], preferred_element_type=F32)
    kv_ref[...] = jnp.dot(h2, win_ref[:, c2:], preferred_element_type=F32)


def _head_call(x, lw):
    n = x.shape[0]
    tm = min(ROW_TILE, n)
    row = lambda w: pl.BlockSpec((tm, w), lambda i: (i, 0))
    return pl.pallas_call(
        _head_kernel,
        grid=(n // tm,),
        in_specs=[row(D_MODEL), _resident((1, D_MODEL)), _resident((1, D_MODEL)),
                  _resident((D_MODEL, FFN_DIM)), _resident((D_MODEL, FFN_DIM)),
                  _resident((FFN_DIM, D_MODEL)), _resident((1, D_MODEL)),
                  _resident((D_MODEL, PROJ_PACKED))],
        out_specs=[row(D_MODEL), row(QKVZ_DIM), row(BA_DIM), row(SWA_Q_DIM), row(2 * SWA_KV_DIM)],
        out_shape=[jax.ShapeDtypeStruct((n, D_MODEL), F32), jax.ShapeDtypeStruct((n, QKVZ_DIM), F32),
                   jax.ShapeDtypeStruct((n, BA_DIM), F32), jax.ShapeDtypeStruct((n, SWA_Q_DIM), F32),
                   jax.ShapeDtypeStruct((n, 2 * SWA_KV_DIM), F32)],
        compiler_params=pltpu.CompilerParams(dimension_semantics=("arbitrary",),
                                             vmem_limit_bytes=VMEM_LIMIT),
        name="ffn1_inproj",
    )(x, lw["g1pre"], lw["g1post"], lw["wg1"], lw["wu1"], lw["wd1"], lw["gmix"], lw["win"])


def _tail_kernel(x_ref, gdn_ref, swa_ref, p_ref, wo_ref, gmixpost_ref, g2pre_ref, g2post_ref,
                 wg_ref, wu_ref, wd_ref, wpg_ref, wpp_ref, gple_ref, y_ref):
    x = x_ref[...]
    mix = (jnp.dot(gdn_ref[...].astype(BF16), wo_ref[:GDN_VAL_DIM, :], preferred_element_type=F32)
           + jnp.dot(swa_ref[...].astype(BF16), wo_ref[GDN_VAL_DIM:, :], preferred_element_type=F32))
    x = x + _rms(mix, gmixpost_ref[...])
    h = _rms(x, g2pre_ref[...]).astype(BF16)
    y = _swiglu_block(h, wg_ref, wu_ref, wd_ref)
    x = x + 0.5 * _rms(y, g2post_ref[...])
    gate = _sigmoid(jnp.dot(x.astype(BF16), wpg_ref[...], preferred_element_type=F32))
    pp = jnp.dot(p_ref[...].astype(BF16), wpp_ref[...], preferred_element_type=F32)
    y_ref[...] = x + _rms(gate * pp, gple_ref[...])


def _tail_call(x1, gdn_o, swa_o, p, lw):
    n = x1.shape[0]
    tm = min(ROW_TILE, n)
    row = lambda w: pl.BlockSpec((tm, w), lambda i: (i, 0))
    return pl.pallas_call(
        _tail_kernel,
        grid=(n // tm,),
        in_specs=[row(D_MODEL), row(GDN_VAL_DIM), row(SWA_Q_DIM), row(PLE_DIM),
                  _resident((GDN_VAL_DIM + SWA_Q_DIM, D_MODEL)), _resident((1, D_MODEL)),
                  _resident((1, D_MODEL)), _resident((1, D_MODEL)),
                  _resident((D_MODEL, FFN_DIM)), _resident((D_MODEL, FFN_DIM)),
                  _resident((FFN_DIM, D_MODEL)), _resident((D_MODEL, D_MODEL)),
                  _resident((PLE_DIM, D_MODEL)), _resident((1, D_MODEL))],
        out_specs=row(D_MODEL),
        out_shape=jax.ShapeDtypeStruct((n, D_MODEL), F32),
        compiler_params=pltpu.CompilerParams(dimension_semantics=("arbitrary",),
                                             vmem_limit_bytes=VMEM_LIMIT),
        name="outproj_ffn2_ple",
    )(x1, gdn_o, swa_o, p, lw["wo"], lw["gmixpost"], lw["g2pre"], lw["g2post"],
      lw["wg2"], lw["wu2"], lw["wd2"], lw["wpg"], lw["wpp"], lw["gple"])


def _iota2(c):
    return (lax.broadcasted_iota(jnp.int32, (c, c), 0), lax.broadcasted_iota(jnp.int32, (c, c), 1))


def _gates(ba, alog_row, dtb_row):
    beta = _sigmoid(ba)
    xa = ba + dtb_row
    softplus = jnp.maximum(xa, 0.0) + jnp.log1p(jnp.exp(-jnp.abs(xa)))
    g = -jnp.exp(alog_row) * softplus
    return beta, g


def _gdn_head_prep(y, h, beta_c, gc_c, gc_r, glast_c, group):
    c = y.shape[0]
    ii, jj = _iota2(c)
    same = (ii // group) == (jj // group)
    incl = jnp.logical_and(same, ii >= jj)
    strict = jnp.logical_and(same, ii > jj)
    q = y[:, h * GDN_DK:(h + 1) * GDN_DK]
    k = y[:, GDN_KEY_DIM + h * GDN_DK:GDN_KEY_DIM + (h + 1) * GDN_DK]
    v = y[:, 2 * GDN_KEY_DIM + h * GDN_DV:2 * GDN_KEY_DIM + (h + 1) * GDN_DV]
    q = q * lax.rsqrt(jnp.sum(q * q, axis=-1, keepdims=True) + 1e-6) * (GDN_DK ** -0.5)
    k = k * lax.rsqrt(jnp.sum(k * k, axis=-1, keepdims=True) + 1e-6)
    decay = jnp.exp(jnp.where(incl, gc_c - gc_r, NEG_INF))
    kb = k * beta_c
    kq = _dot_nt(jnp.concatenate([kb, q], axis=0), k)
    a_mat = jnp.where(strict, kq[:c] * decay, 0.0)
    qk = kq[c:] * decay
    eye = jnp.where(ii == jj, 1.0, 0.0).astype(F32)
    t_mat = eye
    b = 1
    while b < group:
        lower = jnp.logical_and((ii // (2 * b)) == (jj // (2 * b)),
                                jnp.logical_and((ii % (2 * b)) >= b, (jj % (2 * b)) < b))
        m = jnp.where(lower, a_mat, 0.0)
        if b == 1:
            t_mat = eye - m
        else:
            t_mat = t_mat - _dot3(_dot3(t_mat, m), t_mat)
        b *= 2
    eg = jnp.exp(gc_c)
    rhs = jnp.concatenate([v * beta_c, kb * eg], axis=-1)
    sol = _dot3(t_mat, rhs)
    u = sol[:, :GDN_DV]
    w = sol[:, GDN_DV:]
    return u, w, qk, q * eg, k * jnp.exp(glast_c - gc_c)


def _gdn_out(o, z, gnorm_row):
    ms = jnp.mean(o * o, axis=-1, keepdims=True)
    return (o * lax.rsqrt(ms + NORM_EPS)) * gnorm_row * _silu(z)


def _conv_silu(xbuf_ref, cw_ref, c):
    y = xbuf_ref[pl.ds(5, c), :] * cw_ref[0:1, :]
    for j in range(1, CONV_WIDTH):
        y = y + xbuf_ref[pl.ds(5 + j, c), :] * cw_ref[j:j + 1, :]
    return _silu(y)


def _gdn_prompt_kernel(qkvz_ref, ba_ref, cw_ref, alog_ref, dtb_ref, gnorm_ref,
                       o_ref, sfin_ref, s_ref, xbuf_ref):
    c = GDN_CHUNK
    step = pl.program_id(1)

    @pl.when(step == 0)
    def _():
        s_ref[...] = jnp.zeros_like(s_ref)
        xbuf_ref[0:8, :] = jnp.zeros((8, CONV_DIM), F32)

    xbuf_ref[8:8 + c, :] = qkvz_ref[:, :CONV_DIM]
    y = _conv_silu(xbuf_ref, cw_ref, c)
    xbuf_ref[0:8, :] = xbuf_ref[c:c + 8, :]

    beta, g = _gates(ba_ref[...], alog_ref[...], dtb_ref[...])
    ii, jj = _iota2(c)
    tril = jnp.where(ii >= jj, 1.0, 0.0).astype(F32)
    gc = jnp.dot(tril, g, precision=lax.Precision.HIGHEST, preferred_element_type=F32)
    gc_t = gc.T
    for h in range(GDN_HEADS):
        beta_c = beta[:, h:h + 1]
        gc_c = gc[:, GDN_HEADS + h:GDN_HEADS + h + 1]
        gc_r = gc_t[GDN_HEADS + h:GDN_HEADS + h + 1, :]
        glast = gc[c - 1:c, GDN_HEADS + h:GDN_HEADS + h + 1]
        u, w, qk, qd, kd = _gdn_head_prep(y, h, beta_c, gc_c, gc_r, glast, c)
        s_h = s_ref[h]
        ws = _dot(jnp.concatenate([w, qd], axis=0), s_h)
        v_new = u - ws[:c]
        o = ws[c:] + _dot(qk, v_new)
        s_ref[h] = s_h * jnp.exp(glast) + _dot_tn(kd, v_new)
        z = qkvz_ref[:, CONV_DIM + h * GDN_DV:CONV_DIM + (h + 1) * GDN_DV]
        o_ref[:, h * GDN_DV:(h + 1) * GDN_DV] = _gdn_out(o, z, gnorm_ref[...])
    sfin_ref[0] = s_ref[...]


def _gdn_prompt_call(qkvz, ba, lw, batch, seq):
    c = GDN_CHUNK
    nc = seq // c
    return pl.pallas_call(
        _gdn_prompt_kernel,
        grid=(batch, nc),
        in_specs=[pl.BlockSpec((c, QKVZ_DIM), lambda b, s: (b * nc + s, 0)),
                  pl.BlockSpec((c, BA_DIM), lambda b, s: (b * nc + s, 0)),
                  _resident((CONV_WIDTH, CONV_DIM)), _resident((1, BA_DIM)), _resident((1, BA_DIM)),
                  _resident((1, GDN_DV))],
        out_specs=[pl.BlockSpec((c, GDN_VAL_DIM), lambda b, s: (b * nc + s, 0)),
                   pl.BlockSpec((1, GDN_HEADS, GDN_DK, GDN_DV), lambda b, s: (b, 0, 0, 0))],
        out_shape=[jax.ShapeDtypeStruct((batch * seq, GDN_VAL_DIM), F32),
                   jax.ShapeDtypeStruct((batch, GDN_HEADS, GDN_DK, GDN_DV), F32)],
        scratch_shapes=[pltpu.VMEM((GDN_HEADS, GDN_DK, GDN_DV), F32),
                        pltpu.VMEM((c + 8, CONV_DIM), F32)],
        compiler_params=pltpu.CompilerParams(dimension_semantics=("arbitrary", "arbitrary"),
                                             vmem_limit_bytes=VMEM_LIMIT),
        name="gdn_prompt",
    )(qkvz, ba, lw["conv_w"], lw["alog_row"], lw["dtb_row"], lw["gnorm"])


def _gdn_decode_kernel(qkvz_ref, ba_ref, sconv_ref, s0_ref, cw_ref, alog_ref, dtb_ref, gnorm_ref,
                       o_ref, snew_ref, xbuf_ref, *, dec_seq):
    nb = DEC_SEQ_BLOCK
    t = dec_seq
    c = nb * t
    ys = []
    for s in range(nb):
        xbuf_ref[s, 5:8, :] = sconv_ref[s]
        xbuf_ref[s, 8:8 + t, :] = qkvz_ref[s * t:(s + 1) * t, :CONV_DIM]
        ys.append(_conv_silu(xbuf_ref.at[s], cw_ref, t))
    y = jnp.concatenate(ys, axis=0)

    beta, g = _gates(ba_ref[...], alog_ref[...], dtb_ref[...])
    ii, jj = _iota2(c)
    same = (ii // t) == (jj // t)
    tril = jnp.where(jnp.logical_and(same, ii >= jj), 1.0, 0.0).astype(F32)
    ones = jnp.where(same, 1.0, 0.0).astype(F32)
    gc = jnp.dot(tril, g, precision=lax.Precision.HIGHEST, preferred_element_type=F32)
    gl = jnp.dot(ones, g, precision=lax.Precision.HIGHEST, preferred_element_type=F32)
    gc_t = gc.T
    for h in range(GDN_HEADS):
        lane = GDN_HEADS + h
        beta_c = beta[:, h:h + 1]
        gc_c = gc[:, lane:lane + 1]
        gc_r = gc_t[lane:lane + 1, :]
        gl_c = gl[:, lane:lane + 1]
        u, w, qk, qd, kd = _gdn_head_prep(y, h, beta_c, gc_c, gc_r, gl_c, t)
        cd = jnp.exp(gl_c)
        outs = []
        for s in range(nb):
            r0 = s * t
            s_h = s0_ref[s, h]
            ws = _dot(jnp.concatenate([w[r0:r0 + t], qd[r0:r0 + t]], axis=0), s_h)
            v_new = u[r0:r0 + t] - ws[:t]
            o = ws[t:]
            for j in range(t):
                o = o + qk[r0:r0 + t, r0 + j:r0 + j + 1] * v_new[j:j + 1, :]
            snew_ref[s, h] = s_h * cd[r0:r0 + 1, :] + _dot_tn(kd[r0:r0 + t], v_new)
            outs.append(o)
        o_all = jnp.concatenate(outs, axis=0)
        z = qkvz_ref[:, CONV_DIM + h * GDN_DV:CONV_DIM + (h + 1) * GDN_DV]
        o_ref[:, h * GDN_DV:(h + 1) * GDN_DV] = _gdn_out(o_all, z, gnorm_ref[...])


def _gdn_decode_call(qkvz, ba, state_conv, state_gdn, lw, dec_batch, dec_seq):
    nb = DEC_SEQ_BLOCK
    rows = nb * dec_seq
    return pl.pallas_call(
        functools.partial(_gdn_decode_kernel, dec_seq=dec_seq),
        grid=(dec_batch // nb,),
        in_specs=[pl.BlockSpec((rows, QKVZ_DIM), lambda i: (i, 0)),
                  pl.BlockSpec((rows, BA_DIM), lambda i: (i, 0)),
                  pl.BlockSpec((nb, CONV_WIDTH - 1, CONV_DIM), lambda i: (i, 0, 0)),
                  pl.BlockSpec((nb, GDN_HEADS, GDN_DK, GDN_DV), lambda i: (i, 0, 0, 0)),
                  _resident((CONV_WIDTH, CONV_DIM)), _resident((1, BA_DIM)), _resident((1, BA_DIM)),
                  _resident((1, GDN_DV))],
        out_specs=[pl.BlockSpec((rows, GDN_VAL_DIM), lambda i: (i, 0)),
                   pl.BlockSpec((nb, GDN_HEADS, GDN_DK, GDN_DV), lambda i: (i, 0, 0, 0))],
        out_shape=[jax.ShapeDtypeStruct((dec_batch * dec_seq, GDN_VAL_DIM), F32),
                   jax.ShapeDtypeStruct((dec_batch, GDN_HEADS, GDN_DK, GDN_DV), F32)],
        scratch_shapes=[pltpu.VMEM((nb, 16, CONV_DIM), F32)],
        compiler_params=pltpu.CompilerParams(dimension_semantics=("arbitrary",),
                                             vmem_limit_bytes=VMEM_LIMIT),
        name="gdn_decode",
    )(qkvz, ba, state_conv, state_gdn, lw["conv_w"], lw["alog_row"], lw["dtb_row"], lw["gnorm"])


def _bias_table_kernel(bucket_ref, rb_ref, out_ref):
    bucket = bucket_ref[...]
    for h in range(SWA_HEADS):
        acc = jnp.zeros(bucket.shape, F32)
        for b in range(NUM_BUCKETS):
            acc = jnp.where(bucket == b, rb_ref[b, h], acc)
        out_ref[h] = jnp.where(bucket < 0, NEG_INF, acc)


def _bias_table_call(bucket, rel_bias):
    r, c = bucket.shape
    return pl.pallas_call(
        _bias_table_kernel,
        in_specs=[pl.BlockSpec(memory_space=pltpu.VMEM), pl.BlockSpec(memory_space=pltpu.SMEM)],
        out_specs=pl.BlockSpec(memory_space=pltpu.VMEM),
        out_shape=jax.ShapeDtypeStruct((SWA_HEADS, r, c), F32),
        name="t5_bias_table",
    )(bucket, rel_bias)


def _t5_bucket(dist, valid):
    d = jnp.maximum(dist, 0)
    exact = NUM_BUCKETS // 2
    log_ratio = jnp.log(jnp.maximum(d, 1).astype(F32) / exact) / math.log(REL_MAX_DISTANCE / exact)
    large = jnp.minimum(exact + (log_ratio * (NUM_BUCKETS - exact)).astype(jnp.int32), NUM_BUCKETS - 1)
    bucket = jnp.where(d < exact, d, large)
    return jnp.where(valid, bucket, -1).astype(jnp.int32)


def _swa_prompt_kernel(q_ref, kvp_ref, kvc_ref, bias_ref, sink_ref, o_ref):
    w = SWA_WINDOW
    first = pl.program_id(1) == 0
    col = lax.broadcasted_iota(jnp.int32, (w, 2 * w), 1)
    drop_prev = jnp.logical_and(first, col < w)
    for h in range(SWA_KV_HEADS):
        k_all = jnp.concatenate([kvp_ref[:, h * SWA_HEAD_DIM:(h + 1) * SWA_HEAD_DIM],
                                 kvc_ref[:, h * SWA_HEAD_DIM:(h + 1) * SWA_HEAD_DIM]], axis=0)
        v_all = jnp.concatenate(
            [kvp_ref[:, SWA_KV_DIM + h * SWA_HEAD_DIM:SWA_KV_DIM + (h + 1) * SWA_HEAD_DIM],
             kvc_ref[:, SWA_KV_DIM + h * SWA_HEAD_DIM:SWA_KV_DIM + (h + 1) * SWA_HEAD_DIM]], axis=0)
        for g in range(SWA_GROUP):
            hg = h * SWA_GROUP + g
            q = q_ref[:, hg * SWA_HEAD_DIM:(hg + 1) * SWA_HEAD_DIM]
            s = _dot_nt(q, k_all) * (SWA_HEAD_DIM ** -0.5) + bias_ref[hg]
            s = jnp.where(drop_prev, NEG_INF, s)
            sink = sink_ref[hg]
            m = jnp.maximum(jnp.max(s, axis=-1, keepdims=True), sink)
            p = jnp.exp(s - m)
            denom = jnp.sum(p, axis=-1, keepdims=True) + jnp.exp(sink - m)
            o_ref[:, hg * SWA_HEAD_DIM:(hg + 1) * SWA_HEAD_DIM] = _dot(p, v_all) / denom


def _swa_prompt_call(qs, kv, bias, sinks, batch, seq):
    w = SWA_WINDOW
    nb = seq // w
    return pl.pallas_call(
        _swa_prompt_kernel,
        grid=(batch, nb),
        in_specs=[pl.BlockSpec((w, SWA_Q_DIM), lambda b, n: (b * nb + n, 0)),
                  pl.BlockSpec((w, 2 * SWA_KV_DIM), lambda b, n: (b * nb + jnp.maximum(n - 1, 0), 0)),
                  pl.BlockSpec((w, 2 * SWA_KV_DIM), lambda b, n: (b * nb + n, 0)),
                  _resident((SWA_HEADS, w, 2 * w)),
                  pl.BlockSpec(memory_space=pltpu.SMEM)],
        out_specs=pl.BlockSpec((w, SWA_Q_DIM), lambda b, n: (b * nb + n, 0)),
        out_shape=jax.ShapeDtypeStruct((batch * seq, SWA_Q_DIM), F32),
        compiler_params=pltpu.CompilerParams(dimension_semantics=("arbitrary", "arbitrary"),
                                             vmem_limit_bytes=VMEM_LIMIT),
        name="swa_prompt",
    )(qs, kv, kv, bias, sinks)


def _swa_decode_kernel(q_ref, kvn_ref, kbuf_ref, vbuf_ref, bbuf_ref, bnew_ref, sink_ref, o_ref,
                       *, dec_seq):
    nb = DEC_SEQ_BLOCK
    t = dec_seq
    rows = nb * t
    m_rows = SWA_GROUP * rows
    wb = kbuf_ref.shape[1]
    ri = lax.broadcasted_iota(jnp.int32, (m_rows, 1), 0)
    row_seq = (ri % rows) // t
    row_grp = ri // rows
    for h in range(SWA_KV_HEADS):
        lo, hi = h * SWA_HEAD_DIM, (h + 1) * SWA_HEAD_DIM
        q = jnp.concatenate(
            [q_ref[:, (h * SWA_GROUP + g) * SWA_HEAD_DIM:(h * SWA_GROUP + g + 1) * SWA_HEAD_DIM]
             for g in range(SWA_GROUP)], axis=0)
        sink = jnp.zeros((m_rows, 1), F32)
        for g in range(SWA_GROUP):
            sink = jnp.where(row_grp == g, sink_ref[h * SWA_GROUP + g], sink)
        k_new = jnp.concatenate([kvn_ref[:, lo:hi], jnp.zeros((m_rows - rows, SWA_HEAD_DIM), F32)], axis=0)
        v_new = jnp.concatenate([kvn_ref[:, SWA_KV_DIM + lo:SWA_KV_DIM + hi],
                                 jnp.zeros((m_rows - rows, SWA_HEAD_DIM), F32)], axis=0)
        s_new = _dot_nt(q, k_new) * (SWA_HEAD_DIM ** -0.5) + bnew_ref[h]
        s_buf = jnp.zeros((m_rows, wb), F32)
        for s in range(nb):
            s_full = _dot_nt(q, kbuf_ref[s, :, lo:hi])
            s_buf = jnp.where(row_seq == s, s_full, s_buf)
        s_buf = s_buf * (SWA_HEAD_DIM ** -0.5) + bbuf_ref[h]
        m = jnp.maximum(jnp.maximum(jnp.max(s_buf, axis=-1, keepdims=True),
                                    jnp.max(s_new, axis=-1, keepdims=True)), sink)
        p_buf = jnp.exp(s_buf - m)
        p_new = jnp.exp(s_new - m)
        denom = (jnp.sum(p_buf, axis=-1, keepdims=True) + jnp.sum(p_new, axis=-1, keepdims=True)
                 + jnp.exp(sink - m))
        o = _dot(p_new, v_new)
        for s in range(nb):
            o = o + _dot(jnp.where(row_seq == s, p_buf, 0.0), vbuf_ref[s, :, lo:hi])
        o = o / denom
        for g in range(SWA_GROUP):
            hg = h * SWA_GROUP + g
            o_ref[:, hg * SWA_HEAD_DIM:(hg + 1) * SWA_HEAD_DIM] = o[g * rows:(g + 1) * rows]


def _swa_decode_call(qs, kvn, kbuf, vbuf, bias_buf, bias_new, sinks, dec_batch, dec_seq):
    nb = DEC_SEQ_BLOCK
    rows = nb * dec_seq
    wb = kbuf.shape[1]
    m_rows = SWA_GROUP * rows
    return pl.pallas_call(
        functools.partial(_swa_decode_kernel, dec_seq=dec_seq),
        grid=(dec_batch // nb,),
        in_specs=[pl.BlockSpec((rows, SWA_Q_DIM), lambda i: (i, 0)),
                  pl.BlockSpec((rows, 2 * SWA_KV_DIM), lambda i: (i, 0)),
                  pl.BlockSpec((nb, wb, SWA_KV_DIM), lambda i: (i, 0, 0)),
                  pl.BlockSpec((nb, wb, SWA_KV_DIM), lambda i: (i, 0, 0)),
                  _resident((SWA_KV_HEADS, m_rows, wb)),
                  _resident((SWA_KV_HEADS, m_rows, m_rows)),
                  pl.BlockSpec(memory_space=pltpu.SMEM)],
        out_specs=pl.BlockSpec((rows, SWA_Q_DIM), lambda i: (i, 0)),
        out_shape=jax.ShapeDtypeStruct((dec_batch * dec_seq, SWA_Q_DIM), F32),
        compiler_params=pltpu.CompilerParams(dimension_semantics=("arbitrary",),
                                             vmem_limit_bytes=VMEM_LIMIT),
        name="swa_decode",
    )(qs, kvn, kbuf, vbuf, bias_buf, bias_new, sinks)


def _prompt_bias(rel_bias):
    w = SWA_WINDOW
    dist = w + jnp.arange(w)[:, None] - jnp.arange(2 * w)[None, :]
    valid = (dist >= 0) & (dist < w)
    return _bias_table_call(_t5_bucket(dist, valid), rel_bias)


def _decode_bias(rel_bias, wb, dec_seq):
    nb, t = DEC_SEQ_BLOCK, dec_seq
    rows = nb * t
    m_rows = SWA_GROUP * rows
    r = jnp.arange(m_rows)
    r_tok, r_seq = r % t, (r % rows) // t
    dist_b = wb + r_tok[:, None] - jnp.arange(wb)[None, :]
    tab_b = _bias_table_call(_t5_bucket(dist_b, (dist_b >= 0) & (dist_b < SWA_WINDOW)), rel_bias)
    cidx = jnp.arange(m_rows)
    dist_n = r_tok[:, None] - (cidx % t)[None, :]
    ok_n = (dist_n >= 0) & (dist_n < SWA_WINDOW) & (r_seq[:, None] == (cidx // t)[None, :]) & (cidx < rows)[None, :]
    tab_n = _bias_table_call(_t5_bucket(dist_n, ok_n), rel_bias)

    def pick(tab):
        return jnp.stack([jnp.concatenate([tab[h * SWA_GROUP + g, g * rows:(g + 1) * rows]
                                           for g in range(SWA_GROUP)], axis=0)
                          for h in range(SWA_KV_HEADS)])
    return pick(tab_b), pick(tab_n)


def _pack_layer(i, norm_ffn1_pre, norm_ffn1_post, ffn1_w_gate, ffn1_w_up, ffn1_w_down, norm_mix_pre,
                norm_mix_post, w_in, conv_w, gdn_a_log, gdn_dt_bias, gdn_norm, swa_sinks, w_out,
                norm_ffn2_pre, norm_ffn2_post, ffn2_w_gate, ffn2_w_up, ffn2_w_down, ple_gate, ple_proj,
                norm_ple_post):
    row = lambda g: g[i].reshape(1, -1).astype(F32)
    win = w_in[i]
    n_gdn = QKVZ_DIM + 2 * GDN_HEADS
    win = jnp.concatenate([win[:, :n_gdn], jnp.zeros((D_MODEL, BA_DIM - 2 * GDN_HEADS), win.dtype),
                           win[:, n_gdn:]], axis=1)
    lane_pad = lambda v: jnp.zeros((1, BA_DIM), F32).at[0, GDN_HEADS:2 * GDN_HEADS].set(v[i].astype(F32))
    return dict(
        g1pre=row(norm_ffn1_pre), g1post=row(norm_ffn1_post),
        wg1=ffn1_w_gate[i].astype(BF16), wu1=ffn1_w_up[i].astype(BF16), wd1=ffn1_w_down[i].astype(BF16),
        gmix=row(norm_mix_pre), gmixpost=row(norm_mix_post), win=win.astype(BF16),
        conv_w=conv_w[i].astype(F32), alog_row=lane_pad(gdn_a_log), dtb_row=lane_pad(gdn_dt_bias),
        gnorm=row(gdn_norm), sinks=swa_sinks[i].astype(F32), wo=w_out[i].astype(BF16),
        g2pre=row(norm_ffn2_pre), g2post=row(norm_ffn2_post),
        wg2=ffn2_w_gate[i].astype(BF16), wu2=ffn2_w_up[i].astype(BF16), wd2=ffn2_w_down[i].astype(BF16),
        wpg=ple_gate[i].astype(BF16), wpp=ple_proj[i].astype(BF16), gple=row(norm_ple_post))


def kernel(x_prompt, x_sample, state_conv, state_gdn, cache_swa_k, cache_swa_v, p_prompt, p_sample,
           rel_bias, norm_ffn1_pre, norm_ffn1_post, ffn1_w_gate, ffn1_w_up, ffn1_w_down,
           norm_mix_pre, norm_mix_post, w_in, conv_w, gdn_a_log, gdn_dt_bias, gdn_norm, swa_sinks,
           w_out, norm_ffn2_pre, norm_ffn2_post, ffn2_w_gate, ffn2_w_up, ffn2_w_down,
           ple_gate, ple_proj, norm_ple_post):
    depth = w_in.shape[0]
    batch, seq, _ = x_prompt.shape
    dec_batch, dec_seq, _ = x_sample.shape
    wb = cache_swa_k.shape[2]
    wp = min(SWA_WINDOW, seq)
    rel_bias = rel_bias.astype(F32)
    bias_p = _prompt_bias(rel_bias)
    bias_db, bias_dn = _decode_bias(rel_bias, wb, dec_seq)

    yp = x_prompt.reshape(batch * seq, D_MODEL)
    ys = x_sample.reshape(dec_batch * dec_seq, D_MODEL)
    outs = [[] for _ in range(8)]
    for i in range(depth):
        lw = _pack_layer(i, norm_ffn1_pre, norm_ffn1_post, ffn1_w_gate, ffn1_w_up, ffn1_w_down,
                         norm_mix_pre, norm_mix_post, w_in, conv_w, gdn_a_log, gdn_dt_bias, gdn_norm,
                         swa_sinks, w_out, norm_ffn2_pre, norm_ffn2_post, ffn2_w_gate, ffn2_w_up,
                         ffn2_w_down, ple_gate, ple_proj, norm_ple_post)
        x1, qkvz, ba, qs, kv = _head_call(yp, lw)
        gdn_o, s_fin = _gdn_prompt_call(qkvz, ba, lw, batch, seq)
        swa_o = _swa_prompt_call(qs, kv, bias_p, lw["sinks"], batch, seq)
        yp = _tail_call(x1, gdn_o, swa_o, p_prompt[i].reshape(batch * seq, PLE_DIM), lw)
        kv3 = kv.reshape(batch, seq, 2 * SWA_KV_DIM)
        outs[0].append(qkvz.reshape(batch, seq, QKVZ_DIM)[:, seq - (CONV_WIDTH - 1):, :CONV_DIM])
        outs[1].append(s_fin)
        outs[2].append(kv3[:, seq - wp:, :SWA_KV_DIM].reshape(batch, wp, SWA_KV_HEADS, SWA_HEAD_DIM))
        outs[3].append(kv3[:, seq - wp:, SWA_KV_DIM:].reshape(batch, wp, SWA_KV_HEADS, SWA_HEAD_DIM))
        x1, qkvz, ba, qs, kv = _head_call(ys, lw)
        gdn_o, s_new = _gdn_decode_call(qkvz, ba, state_conv[i], state_gdn[i], lw, dec_batch, dec_seq)
        kbuf = cache_swa_k[i].reshape(dec_batch, wb, SWA_KV_DIM)
        vbuf = cache_swa_v[i].reshape(dec_batch, wb, SWA_KV_DIM)
        swa_o = _swa_decode_call(qs, kv, kbuf, vbuf, bias_db, bias_dn, lw["sinks"], dec_batch, dec_seq)
        ys = _tail_call(x1, gdn_o, swa_o, p_sample[i].reshape(dec_batch * dec_seq, PLE_DIM), lw)
        kv3 = kv.reshape(dec_batch, dec_seq, 2 * SWA_KV_DIM)
        xp = jnp.concatenate([state_conv[i], qkvz.reshape(dec_batch, dec_seq, QKVZ_DIM)[:, :, :CONV_DIM]], axis=1)
        outs[4].append(xp[:, dec_seq:])
        outs[5].append(s_new)
        outs[6].append(jnp.concatenate([kbuf, kv3[:, :, :SWA_KV_DIM]], axis=1)[:, dec_seq:]
                       .reshape(dec_batch, wb, SWA_KV_HEADS, SWA_HEAD_DIM))
        outs[7].append(jnp.concatenate([vbuf, kv3[:, :, SWA_KV_DIM:]], axis=1)[:, dec_seq:]
                       .reshape(dec_batch, wb, SWA_KV_HEADS, SWA_HEAD_DIM))
    return (yp.reshape(batch, seq, D_MODEL), ys.reshape(dec_batch, dec_seq, D_MODEL),
            *[jnp.stack(o) for o in outs])
```

```python
import functools
import math

import jax
import jax.numpy as jnp
from jax import lax
from jax.experimental import pallas as pl
from jax.experimental.pallas import tpu as pltpu

F32 = jnp.float32
BF16 = jnp.bfloat16

D_MODEL = 1024
NORM_EPS = 1e-6
PLE_DIM = 256
FFN_DIM = 2816
GDN_HEADS = 4
GDN_DK = 128
GDN_DV = 128
GDN_KEY_DIM = GDN_HEADS * GDN_DK
GDN_VAL_DIM = GDN_HEADS * GDN_DV
CONV_DIM = 2 * GDN_KEY_DIM + GDN_VAL_DIM
CONV_WIDTH = 4
SWA_HEADS = 8
SWA_KV_HEADS = 2
SWA_GROUP = SWA_HEADS // SWA_KV_HEADS
SWA_HEAD_DIM = 64
SWA_Q_DIM = SWA_HEADS * SWA_HEAD_DIM
SWA_KV_DIM = SWA_KV_HEADS * SWA_HEAD_DIM
SWA_WINDOW = 128
NUM_BUCKETS = 32
REL_MAX_DISTANCE = 128

QKVZ_DIM = CONV_DIM + GDN_VAL_DIM
BA_DIM = 128
PROJ_PACKED = QKVZ_DIM + BA_DIM + SWA_Q_DIM + 2 * SWA_KV_DIM

GDN_CHUNK = 64
GDN_STEP_CHUNKS = 4
ROW_TILE = 256
DEC_SEQ_BLOCK = 8
VMEM_LIMIT = 56 * 1024 * 1024

NEG_INF = float("-inf")


def _resident(shape):
    nd = len(shape)
    return pl.BlockSpec(shape, lambda *_: (0,) * nd, pipeline_mode=pl.Buffered(1))


def _rms(x, gain):
    ms = jnp.mean(x * x, axis=-1, keepdims=True)
    return (x * lax.rsqrt(ms + NORM_EPS)) * gain


def _sigmoid(x):
    return 1.0 / (1.0 + jnp.exp(-x))


def _silu(x):
    return x * _sigmoid(x)


def _dot(a, b):
    return jnp.dot(a.astype(BF16), b.astype(BF16), preferred_element_type=F32)


def _dot_nt(a, b):
    return lax.dot_general(a.astype(BF16), b.astype(BF16), (((1,), (1,)), ((), ())),
                           preferred_element_type=F32)


def _dot_tn(a, b):
    return lax.dot_general(a.astype(BF16), b.astype(BF16), (((0,), (0,)), ((), ())),
                           preferred_element_type=F32)


def _split(a):
    hi = a.astype(BF16)
    lo = (a - hi.astype(F32)).astype(BF16)
    return hi, lo


def _dot3(a, b):
    ah, al = _split(a)
    bh, bl = _split(b)
    d = functools.partial(jnp.dot, preferred_element_type=F32)
    return d(ah, bh) + (d(ah, bl) + d(al, bh))


def _swiglu_block(h, wg_ref, wu_ref, wd_ref):
    g = jnp.dot(h, wg_ref[...], preferred_element_type=F32)
    u = jnp.dot(h, wu_ref[...], preferred_element_type=F32)
    a = (_silu(g) * u).astype(BF16)
    return jnp.dot(a, wd_ref[...], preferred_element_type=F32)


def _head_kernel(x_ref, g1pre_ref, g1post_ref, wg_ref, wu_ref, wd_ref, gmix_ref, win_ref,
                 x1_ref, qkvz_ref, ba_ref, qs_ref, kv_ref):
    x = x_ref[...]
    h = _rms(x, g1pre_ref[...]).astype(BF16)
    y = _swiglu_block(h, wg_ref, wu_ref, wd_ref)
    x1 = x + 0.5 * _rms(y, g1post_ref[...])
    x1_ref[...] = x1
    h2 = _rms(x1, gmix_ref[...]).astype(BF16)
    c0, c1, c2 = QKVZ_DIM, QKVZ_DIM + BA_DIM, QKVZ_DIM + BA_DIM + SWA_Q_DIM
    qkvz_ref[...] = jnp.dot(h2, win_ref[:, :c0], preferred_element_type=F32)
    ba_ref[...] = jnp.dot(h2, win_ref[:, c0:c1], preferred_element_type=F32)
    qs_ref[...] = jnp.dot(h2, win_ref[:, c1:c2], preferred_element_type=F32)
    kv_ref[...] = jnp.dot(h2, win_ref[:, c2:], preferred_element_type=F32)


def _head_call(x, lw):
    n = x.shape[0]
    tm = min(ROW_TILE, n)
    row = lambda w: pl.BlockSpec((tm, w), lambda i: (i, 0))
    return pl.pallas_call(
        _head_kernel,
        grid=(n // tm,),
        in_specs=[row(D_MODEL), _resident((1, D_MODEL)), _resident((1, D_MODEL)),
                  _resident((D_MODEL, FFN_DIM)), _resident((D_MODEL, FFN_DIM)),
                  _resident((FFN_DIM, D_MODEL)), _resident((1, D_MODEL)),
                  _resident((D_MODEL, PROJ_PACKED))],
        out_specs=[row(D_MODEL), row(QKVZ_DIM), row(BA_DIM), row(SWA_Q_DIM), row(2 * SWA_KV_DIM)],
        out_shape=[jax.ShapeDtypeStruct((n, D_MODEL), F32), jax.ShapeDtypeStruct((n, QKVZ_DIM), F32),
                   jax.ShapeDtypeStruct((n, BA_DIM), F32), jax.ShapeDtypeStruct((n, SWA_Q_DIM), F32),
                   jax.ShapeDtypeStruct((n, 2 * SWA_KV_DIM), F32)],
        compiler_params=pltpu.CompilerParams(dimension_semantics=("arbitrary",),
                                             vmem_limit_bytes=VMEM_LIMIT),
        name="ffn1_inproj",
    )(x, lw["g1pre"], lw["g1post"], lw["wg1"], lw["wu1"], lw["wd1"], lw["gmix"], lw["win"])


def _tail_kernel(x_ref, gdn_ref, swa_ref, p_ref, wo_ref, gmixpost_ref, g2pre_ref, g2post_ref,
                 wg_ref, wu_ref, wd_ref, wpg_ref, wpp_ref, gple_ref, y_ref):
    x = x_ref[...]
    mix = (jnp.dot(gdn_ref[...].astype(BF16), wo_ref[:GDN_VAL_DIM, :], preferred_element_type=F32)
           + jnp.dot(swa_ref[...].astype(BF16), wo_ref[GDN_VAL_DIM:, :], preferred_element_type=F32))
    x = x + _rms(mix, gmixpost_ref[...])
    h = _rms(x, g2pre_ref[...]).astype(BF16)
    y = _swiglu_block(h, wg_ref, wu_ref, wd_ref)
    x = x + 0.5 * _rms(y, g2post_ref[...])
    gate = _sigmoid(jnp.dot(x.astype(BF16), wpg_ref[...], preferred_element_type=F32))
    pp = jnp.dot(p_ref[...].astype(BF16), wpp_ref[...], preferred_element_type=F32)
    y_ref[...] = x + _rms(gate * pp, gple_ref[...])


def _tail_call(x1, gdn_o, swa_o, p, lw):
    n = x1.shape[0]
    tm = min(ROW_TILE, n)
    row = lambda w: pl.BlockSpec((tm, w), lambda i: (i, 0))
    return pl.pallas_call(
        _tail_kernel,
        grid=(n // tm,),
        in_specs=[row(D_MODEL), row(GDN_VAL_DIM), row(SWA_Q_DIM), row(PLE_DIM),
                  _resident((GDN_VAL_DIM + SWA_Q_DIM, D_MODEL)), _resident((1, D_MODEL)),
                  _resident((1, D_MODEL)), _resident((1, D_MODEL)),
                  _resident((D_MODEL, FFN_DIM)), _resident((D_MODEL, FFN_DIM)),
                  _resident((FFN_DIM, D_MODEL)), _resident((D_MODEL, D_MODEL)),
                  _resident((PLE_DIM, D_MODEL)), _resident((1, D_MODEL))],
        out_specs=row(D_MODEL),
        out_shape=jax.ShapeDtypeStruct((n, D_MODEL), F32),
        compiler_params=pltpu.CompilerParams(dimension_semantics=("arbitrary",),
                                             vmem_limit_bytes=VMEM_LIMIT),
        name="outproj_ffn2_ple",
    )(x1, gdn_o, swa_o, p, lw["wo"], lw["gmixpost"], lw["g2pre"], lw["g2post"],
      lw["wg2"], lw["wu2"], lw["wd2"], lw["wpg"], lw["wpp"], lw["gple"])


def _iota2(c):
    return (lax.broadcasted_iota(jnp.int32, (c, c), 0), lax.broadcasted_iota(jnp.int32, (c, c), 1))


def _gates(ba, alog_row, dtb_row):
    beta = _sigmoid(ba)
    xa = ba + dtb_row
    softplus = jnp.maximum(xa, 0.0) + jnp.log1p(jnp.exp(-jnp.abs(xa)))
    g = -jnp.exp(alog_row) * softplus
    return beta, g


def _cumsum_rows(mask01, g):
    hi, lo = _split(g)
    m = mask01.astype(BF16)
    return jnp.dot(m, hi, preferred_element_type=F32) + jnp.dot(m, lo, preferred_element_type=F32)


def _gdn_prep(y, beta, gc, gc_t, gl, chunk, group):
    c = chunk
    n_chunks = y.shape[0] // c
    units = [(ci, h) for ci in range(n_chunks) for h in range(GDN_HEADS)]
    ii, jj = _iota2(c)
    same = (ii // group) == (jj // group)
    incl = jnp.logical_and(same, ii >= jj)
    strict = jnp.logical_and(same, ii > jj)

    def rows(t, ci, lo, hi):
        return t[ci * c:(ci + 1) * c, lo:hi]

    beta_c = [rows(beta, ci, h, h + 1) for ci, h in units]
    gc_c = [rows(gc, ci, GDN_HEADS + h, GDN_HEADS + h + 1) for ci, h in units]
    gl_c = [rows(gl, ci, GDN_HEADS + h, GDN_HEADS + h + 1) for ci, h in units]
    gc_r = [gc_t[GDN_HEADS + h:GDN_HEADS + h + 1, ci * c:(ci + 1) * c] for ci, h in units]
    q = [rows(y, ci, h * GDN_DK, (h + 1) * GDN_DK) for ci, h in units]
    k = [rows(y, ci, GDN_KEY_DIM + h * GDN_DK, GDN_KEY_DIM + (h + 1) * GDN_DK) for ci, h in units]
    v = [rows(y, ci, 2 * GDN_KEY_DIM + h * GDN_DV, 2 * GDN_KEY_DIM + (h + 1) * GDN_DV) for ci, h in units]
    n = range(len(units))
    q = [t * lax.rsqrt(jnp.sum(t * t, axis=-1, keepdims=True) + 1e-6) * (GDN_DK ** -0.5) for t in q]
    k = [t * lax.rsqrt(jnp.sum(t * t, axis=-1, keepdims=True) + 1e-6) for t in k]
    decay = [jnp.exp(jnp.where(incl, gc_c[i] - gc_r[i], NEG_INF)) for i in n]
    kb = [k[i] * beta_c[i] for i in n]
    kq = [_dot_nt(jnp.concatenate([kb[i], q[i]], axis=0), k[i]) for i in n]
    a_mat = [jnp.where(strict, kq[i][:c] * decay[i], 0.0) for i in n]
    qk = [kq[i][c:] * decay[i] for i in n]
    eye = jnp.where(ii == jj, 1.0, 0.0).astype(F32)
    t_mat = [eye for _ in n]
    b = 1
    while b < group:
        lower = jnp.logical_and((ii // (2 * b)) == (jj // (2 * b)),
                                jnp.logical_and((ii % (2 * b)) >= b, (jj % (2 * b)) < b))
        m = [jnp.where(lower, a_mat[i], 0.0) for i in n]
        if b == 1:
            t_mat = [eye - m[i] for i in n]
        else:
            tm = [_dot(t_mat[i], m[i]) for i in n]
            t_mat = [t_mat[i] - _dot(tm[i], t_mat[i]) for i in n]
        b *= 2
    eg = [jnp.exp(gc_c[i]) for i in n]
    rhs = [jnp.concatenate([v[i] * beta_c[i], kb[i] * eg[i]], axis=-1) for i in n]
    sol = [_dot(t_mat[i], rhs[i]) for i in n]
    nest = lambda xs: [xs[ci * GDN_HEADS:(ci + 1) * GDN_HEADS] for ci in range(n_chunks)]
    u = nest([s[:, :GDN_DV] for s in sol])
    w = nest([s[:, GDN_DV:] for s in sol])
    qd = nest([q[i] * eg[i] for i in n])
    kd = nest([k[i] * jnp.exp(gl_c[i] - gc_c[i]) for i in n])
    return u, w, nest(qk), qd, kd


def _gdn_out(o, z, gnorm_row):
    ms = jnp.mean(o * o, axis=-1, keepdims=True)
    return (o * lax.rsqrt(ms + NORM_EPS)) * gnorm_row * _silu(z)


def _conv_silu(xbuf_ref, cw_ref, c):
    y = xbuf_ref[pl.ds(5, c), :] * cw_ref[0:1, :]
    for j in range(1, CONV_WIDTH):
        y = y + xbuf_ref[pl.ds(5 + j, c), :] * cw_ref[j:j + 1, :]
    return _silu(y)


def _gdn_prompt_kernel(qkvz_ref, ba_ref, cw_ref, alog_ref, dtb_ref, gnorm_ref,
                       o_ref, sfin_ref, s_ref, xbuf_ref):
    c = GDN_CHUNK
    n_chunks = GDN_STEP_CHUNKS
    r = n_chunks * c
    hs = range(GDN_HEADS)

    @pl.when(pl.program_id(1) == 0)
    def _():
        s_ref[...] = jnp.zeros_like(s_ref)
        xbuf_ref[0:8, :] = jnp.zeros((8, CONV_DIM), F32)

    xbuf_ref[8:8 + r, :] = qkvz_ref[:, :CONV_DIM]
    y = _conv_silu(xbuf_ref, cw_ref, r)
    xbuf_ref[0:8, :] = xbuf_ref[r:r + 8, :]

    beta, g = _gates(ba_ref[...], alog_ref[...], dtb_ref[...])
    ii, jj = _iota2(r)
    tril = jnp.where(jnp.logical_and((ii // c) == (jj // c), ii >= jj), 1.0, 0.0)
    gc = _cumsum_rows(tril, g)
    gc_t = gc.T
    glast = [gc[(ci + 1) * c - 1:(ci + 1) * c, :] for ci in range(n_chunks)]
    gl = jnp.concatenate([jnp.broadcast_to(t, (c, BA_DIM)) for t in glast], axis=0)
    u, w, qk, qd, kd = _gdn_prep(y, beta, gc, gc_t, gl, c, c)
    s_cur = [s_ref[h] for h in hs]
    for ci in range(n_chunks):
        ws = [_dot(jnp.concatenate([w[ci][h], qd[ci][h]], axis=0), s_cur[h]) for h in hs]
        v_new = [u[ci][h] - ws[h][:c] for h in hs]
        o = [ws[h][c:] + _dot(qk[ci][h], v_new[h]) for h in hs]
        s_cur = [s_cur[h] * jnp.exp(glast[ci][:, GDN_HEADS + h:GDN_HEADS + h + 1])
                 + _dot_tn(kd[ci][h], v_new[h]) for h in hs]
        for h in hs:
            z = qkvz_ref[ci * c:(ci + 1) * c, CONV_DIM + h * GDN_DV:CONV_DIM + (h + 1) * GDN_DV]
            o_ref[ci * c:(ci + 1) * c, h * GDN_DV:(h + 1) * GDN_DV] = _gdn_out(o[h], z, gnorm_ref[...])
    for h in hs:
        s_ref[h] = s_cur[h]
        sfin_ref[0, h] = s_cur[h]


def _gdn_prompt_call(qkvz, ba, lw, batch, seq):
    c = GDN_CHUNK * GDN_STEP_CHUNKS
    nc = seq // c
    return pl.pallas_call(
        _gdn_prompt_kernel,
        grid=(batch, nc),
        in_specs=[pl.BlockSpec((c, QKVZ_DIM), lambda b, s: (b * nc + s, 0)),
                  pl.BlockSpec((c, BA_DIM), lambda b, s: (b * nc + s, 0)),
                  _resident((CONV_WIDTH, CONV_DIM)), _resident((1, BA_DIM)), _resident((1, BA_DIM)),
                  _resident((1, GDN_DV))],
        out_specs=[pl.BlockSpec((c, GDN_VAL_DIM), lambda b, s: (b * nc + s, 0)),
                   pl.BlockSpec((1, GDN_HEADS, GDN_DK, GDN_DV), lambda b, s: (b, 0, 0, 0))],
        out_shape=[jax.ShapeDtypeStruct((batch * seq, GDN_VAL_DIM), F32),
                   jax.ShapeDtypeStruct((batch, GDN_HEADS, GDN_DK, GDN_DV), F32)],
        scratch_shapes=[pltpu.VMEM((GDN_HEADS, GDN_DK, GDN_DV), F32),
                        pltpu.VMEM((c + 8, CONV_DIM), F32)],
        compiler_params=pltpu.CompilerParams(dimension_semantics=("arbitrary", "arbitrary"),
                                             vmem_limit_bytes=VMEM_LIMIT),
        name="gdn_prompt",
    )(qkvz, ba, lw["conv_w"], lw["alog_row"], lw["dtb_row"], lw["gnorm"])


def _gdn_decode_kernel(qkvz_ref, ba_ref, sconv_ref, s0_ref, cw_ref, alog_ref, dtb_ref, gnorm_ref,
                       o_ref, snew_ref, xbuf_ref, *, dec_seq):
    nb = DEC_SEQ_BLOCK
    t = dec_seq
    c = nb * t
    ys = []
    for s in range(nb):
        xbuf_ref[s, 5:8, :] = sconv_ref[s]
        xbuf_ref[s, 8:8 + t, :] = qkvz_ref[s * t:(s + 1) * t, :CONV_DIM]
        ys.append(_conv_silu(xbuf_ref.at[s], cw_ref, t))
    y = jnp.concatenate(ys, axis=0)

    beta, g = _gates(ba_ref[...], alog_ref[...], dtb_ref[...])
    ii, jj = _iota2(c)
    same = (ii // t) == (jj // t)
    tril = jnp.where(jnp.logical_and(same, ii >= jj), 1.0, 0.0).astype(F32)
    ones = jnp.where(same, 1.0, 0.0).astype(F32)
    gc = _cumsum_rows(tril, g)
    gl = _cumsum_rows(ones, g)
    gc_t = gc.T
    us, ws_, qks, qds, kds = _gdn_prep(y, beta, gc, gc_t, gl, c, t)
    for h in range(GDN_HEADS):
        u, w, qk, qd, kd = us[0][h], ws_[0][h], qks[0][h], qds[0][h], kds[0][h]
        cd = jnp.exp(gl[:, GDN_HEADS + h:GDN_HEADS + h + 1])
        outs = []
        for s in range(nb):
            r0 = s * t
            s_h = s0_ref[s, h]
            ws = _dot(jnp.concatenate([w[r0:r0 + t], qd[r0:r0 + t]], axis=0), s_h)
            v_new = u[r0:r0 + t] - ws[:t]
            o = ws[t:]
            for j in range(t):
                o = o + qk[r0:r0 + t, r0 + j:r0 + j + 1] * v_new[j:j + 1, :]
            snew_ref[s, h] = s_h * cd[r0:r0 + 1, :] + _dot_tn(kd[r0:r0 + t], v_new)
            outs.append(o)
        o_all = jnp.concatenate(outs, axis=0)
        z = qkvz_ref[:, CONV_DIM + h * GDN_DV:CONV_DIM + (h + 1) * GDN_DV]
        o_ref[:, h * GDN_DV:(h + 1) * GDN_DV] = _gdn_out(o_all, z, gnorm_ref[...])


def _gdn_decode_call(qkvz, ba, state_conv, state_gdn, lw, dec_batch, dec_seq):
    nb = DEC_SEQ_BLOCK
    rows = nb * dec_seq
    return pl.pallas_call(
        functools.partial(_gdn_decode_kernel, dec_seq=dec_seq),
        grid=(dec_batch // nb,),
        in_specs=[pl.BlockSpec((rows, QKVZ_DIM), lambda i: (i, 0)),
                  pl.BlockSpec((rows, BA_DIM), lambda i: (i, 0)),
                  pl.BlockSpec((nb, CONV_WIDTH - 1, CONV_DIM), lambda i: (i, 0, 0)),
                  pl.BlockSpec((nb, GDN_HEADS, GDN_DK, GDN_DV), lambda i: (i, 0, 0, 0)),
                  _resident((CONV_WIDTH, CONV_DIM)), _resident((1, BA_DIM)), _resident((1, BA_DIM)),
                  _resident((1, GDN_DV))],
        out_specs=[pl.BlockSpec((rows, GDN_VAL_DIM), lambda i: (i, 0)),
                   pl.BlockSpec((nb, GDN_HEADS, GDN_DK, GDN_DV), lambda i: (i, 0, 0, 0))],
        out_shape=[jax.ShapeDtypeStruct((dec_batch * dec_seq, GDN_VAL_DIM), F32),
                   jax.ShapeDtypeStruct((dec_batch, GDN_HEADS, GDN_DK, GDN_DV), F32)],
        scratch_shapes=[pltpu.VMEM((nb, 16, CONV_DIM), F32)],
        compiler_params=pltpu.CompilerParams(dimension_semantics=("arbitrary",),
                                             vmem_limit_bytes=VMEM_LIMIT),
        name="gdn_decode",
    )(qkvz, ba, state_conv, state_gdn, lw["conv_w"], lw["alog_row"], lw["dtb_row"], lw["gnorm"])


def _bias_table_kernel(bucket_ref, rb_ref, out_ref):
    bucket = bucket_ref[...]
    for h in range(SWA_HEADS):
        acc = jnp.zeros(bucket.shape, F32)
        for b in range(NUM_BUCKETS):
            acc = jnp.where(bucket == b, rb_ref[b, h], acc)
        out_ref[h] = jnp.where(bucket < 0, NEG_INF, acc)


def _bias_table_call(bucket, rel_bias):
    r, c = bucket.shape
    return pl.pallas_call(
        _bias_table_kernel,
        in_specs=[pl.BlockSpec(memory_space=pltpu.VMEM), pl.BlockSpec(memory_space=pltpu.SMEM)],
        out_specs=pl.BlockSpec(memory_space=pltpu.VMEM),
        out_shape=jax.ShapeDtypeStruct((SWA_HEADS, r, c), F32),
        name="t5_bias_table",
    )(bucket, rel_bias)


def _t5_bucket(dist, valid):
    d = jnp.maximum(dist, 0)
    exact = NUM_BUCKETS // 2
    log_ratio = jnp.log(jnp.maximum(d, 1).astype(F32) / exact) / math.log(REL_MAX_DISTANCE / exact)
    large = jnp.minimum(exact + (log_ratio * (NUM_BUCKETS - exact)).astype(jnp.int32), NUM_BUCKETS - 1)
    bucket = jnp.where(d < exact, d, large)
    return jnp.where(valid, bucket, -1).astype(jnp.int32)


def _swa_prompt_kernel(q_ref, kvp_ref, kvc_ref, bias_ref, sink_ref, o_ref):
    w = SWA_WINDOW
    first = pl.program_id(1) == 0
    col = lax.broadcasted_iota(jnp.int32, (w, 2 * w), 1)
    drop_prev = jnp.logical_and(first, col < w)
    for h in range(SWA_KV_HEADS):
        k_all = jnp.concatenate([kvp_ref[:, h * SWA_HEAD_DIM:(h + 1) * SWA_HEAD_DIM],
                                 kvc_ref[:, h * SWA_HEAD_DIM:(h + 1) * SWA_HEAD_DIM]], axis=0)
        v_all = jnp.concatenate(
            [kvp_ref[:, SWA_KV_DIM + h * SWA_HEAD_DIM:SWA_KV_DIM + (h + 1) * SWA_HEAD_DIM],
             kvc_ref[:, SWA_KV_DIM + h * SWA_HEAD_DIM:SWA_KV_DIM + (h + 1) * SWA_HEAD_DIM]], axis=0)
        for g in range(SWA_GROUP):
            hg = h * SWA_GROUP + g
            q = q_ref[:, hg * SWA_HEAD_DIM:(hg + 1) * SWA_HEAD_DIM]
            s = _dot_nt(q, k_all) * (SWA_HEAD_DIM ** -0.5) + bias_ref[hg]
            s = jnp.where(drop_prev, NEG_INF, s)
            sink = sink_ref[hg]
            m = jnp.maximum(jnp.max(s, axis=-1, keepdims=True), sink)
            p = jnp.exp(s - m)
            denom = jnp.sum(p, axis=-1, keepdims=True) + jnp.exp(sink - m)
            o_ref[:, hg * SWA_HEAD_DIM:(hg + 1) * SWA_HEAD_DIM] = _dot(p, v_all) / denom


def _swa_prompt_call(qs, kv, bias, sinks, batch, seq):
    w = SWA_WINDOW
    nb = seq // w
    return pl.pallas_call(
        _swa_prompt_kernel,
        grid=(batch, nb),
        in_specs=[pl.BlockSpec((w, SWA_Q_DIM), lambda b, n: (b * nb + n, 0)),
                  pl.BlockSpec((w, 2 * SWA_KV_DIM), lambda b, n: (b * nb + jnp.maximum(n - 1, 0), 0)),
                  pl.BlockSpec((w, 2 * SWA_KV_DIM), lambda b, n: (b * nb + n, 0)),
                  _resident((SWA_HEADS, w, 2 * w)),
                  pl.BlockSpec(memory_space=pltpu.SMEM)],
        out_specs=pl.BlockSpec((w, SWA_Q_DIM), lambda b, n: (b * nb + n, 0)),
        out_shape=jax.ShapeDtypeStruct((batch * seq, SWA_Q_DIM), F32),
        compiler_params=pltpu.CompilerParams(dimension_semantics=("arbitrary", "arbitrary"),
                                             vmem_limit_bytes=VMEM_LIMIT),
        name="swa_prompt",
    )(qs, kv, kv, bias, sinks)


def _swa_decode_kernel(q_ref, kvn_ref, kbuf_ref, vbuf_ref, bbuf_ref, bnew_ref, sink_ref, o_ref,
                       *, dec_seq):
    nb = DEC_SEQ_BLOCK
    t = dec_seq
    rows = nb * t
    m_rows = SWA_GROUP * rows
    wb = kbuf_ref.shape[1]
    ri = lax.broadcasted_iota(jnp.int32, (m_rows, 1), 0)
    row_seq = (ri % rows) // t
    row_grp = ri // rows
    for h in range(SWA_KV_HEADS):
        lo, hi = h * SWA_HEAD_DIM, (h + 1) * SWA_HEAD_DIM
        q = jnp.concatenate(
            [q_ref[:, (h * SWA_GROUP + g) * SWA_HEAD_DIM:(h * SWA_GROUP + g + 1) * SWA_HEAD_DIM]
             for g in range(SWA_GROUP)], axis=0)
        sink = jnp.zeros((m_rows, 1), F32)
        for g in range(SWA_GROUP):
            sink = jnp.where(row_grp == g, sink_ref[h * SWA_GROUP + g], sink)
        k_new = jnp.concatenate([kvn_ref[:, lo:hi], jnp.zeros((m_rows - rows, SWA_HEAD_DIM), F32)], axis=0)
        v_new = jnp.concatenate([kvn_ref[:, SWA_KV_DIM + lo:SWA_KV_DIM + hi],
                                 jnp.zeros((m_rows - rows, SWA_HEAD_DIM), F32)], axis=0)
        s_new = _dot_nt(q, k_new) * (SWA_HEAD_DIM ** -0.5) + bnew_ref[h]
        s_buf = jnp.zeros((m_rows, wb), F32)
        for s in range(nb):
            s_full = _dot_nt(q, kbuf_ref[s, :, lo:hi])
            s_buf = jnp.where(row_seq == s, s_full, s_buf)
        s_buf = s_buf * (SWA_HEAD_DIM ** -0.5) + bbuf_ref[h]
        m = jnp.maximum(jnp.maximum(jnp.max(s_buf, axis=-1, keepdims=True),
                                    jnp.max(s_new, axis=-1, keepdims=True)), sink)
        p_buf = jnp.exp(s_buf - m)
        p_new = jnp.exp(s_new - m)
        denom = (jnp.sum(p_buf, axis=-1, keepdims=True) + jnp.sum(p_new, axis=-1, keepdims=True)
                 + jnp.exp(sink - m))
        o = _dot(p_new, v_new)
        for s in range(nb):
            o = o + _dot(jnp.where(row_seq == s, p_buf, 0.0), vbuf_ref[s, :, lo:hi])
        o = o / denom
        for g in range(SWA_GROUP):
            hg = h * SWA_GROUP + g
            o_ref[:, hg * SWA_HEAD_DIM:(hg + 1) * SWA_HEAD_DIM] = o[g * rows:(g + 1) * rows]


def _swa_decode_call(qs, kvn, kbuf, vbuf, bias_buf, bias_new, sinks, dec_batch, dec_seq):
    nb = DEC_SEQ_BLOCK
    rows = nb * dec_seq
    wb = kbuf.shape[1]
    m_rows = SWA_GROUP * rows
    return pl.pallas_call(
        functools.partial(_swa_decode_kernel, dec_seq=dec_seq),
        grid=(dec_batch // nb,),
        in_specs=[pl.BlockSpec((rows, SWA_Q_DIM), lambda i: (i, 0)),
                  pl.BlockSpec((rows, 2 * SWA_KV_DIM), lambda i: (i, 0)),
                  pl.BlockSpec((nb, wb, SWA_KV_DIM), lambda i: (i, 0, 0)),
                  pl.BlockSpec((nb, wb, SWA_KV_DIM), lambda i: (i, 0, 0)),
                  _resident((SWA_KV_HEADS, m_rows, wb)),
                  _resident((SWA_KV_HEADS, m_rows, m_rows)),
                  pl.BlockSpec(memory_space=pltpu.SMEM)],
        out_specs=pl.BlockSpec((rows, SWA_Q_DIM), lambda i: (i, 0)),
        out_shape=jax.ShapeDtypeStruct((dec_batch * dec_seq, SWA_Q_DIM), F32),
        compiler_params=pltpu.CompilerParams(dimension_semantics=("arbitrary",),
                                             vmem_limit_bytes=VMEM_LIMIT),
        name="swa_decode",
    )(qs, kvn, kbuf, vbuf, bias_buf, bias_new, sinks)


def _prompt_bias(rel_bias):
    w = SWA_WINDOW
    dist = w + jnp.arange(w)[:, None] - jnp.arange(2 * w)[None, :]
    valid = (dist >= 0) & (dist < w)
    return _bias_table_call(_t5_bucket(dist, valid), rel_bias)


def _decode_bias(rel_bias, wb, dec_seq):
    nb, t = DEC_SEQ_BLOCK, dec_seq
    rows = nb * t
    m_rows = SWA_GROUP * rows
    r = jnp.arange(m_rows)
    r_tok, r_seq = r % t, (r % rows) // t
    dist_b = wb + r_tok[:, None] - jnp.arange(wb)[None, :]
    tab_b = _bias_table_call(_t5_bucket(dist_b, (dist_b >= 0) & (dist_b < SWA_WINDOW)), rel_bias)
    cidx = jnp.arange(m_rows)
    dist_n = r_tok[:, None] - (cidx % t)[None, :]
    ok_n = (dist_n >= 0) & (dist_n < SWA_WINDOW) & (r_seq[:, None] == (cidx // t)[None, :]) & (cidx < rows)[None, :]
    tab_n = _bias_table_call(_t5_bucket(dist_n, ok_n), rel_bias)

    def pick(tab):
        return jnp.stack([jnp.concatenate([tab[h * SWA_GROUP + g, g * rows:(g + 1) * rows]
                                           for g in range(SWA_GROUP)], axis=0)
                          for h in range(SWA_KV_HEADS)])
    return pick(tab_b), pick(tab_n)


def _pack_layer(i, norm_ffn1_pre, norm_ffn1_post, ffn1_w_gate, ffn1_w_up, ffn1_w_down, norm_mix_pre,
                norm_mix_post, w_in, conv_w, gdn_a_log, gdn_dt_bias, gdn_norm, swa_sinks, w_out,
                norm_ffn2_pre, norm_ffn2_post, ffn2_w_gate, ffn2_w_up, ffn2_w_down, ple_gate, ple_proj,
                norm_ple_post):
    row = lambda g: g[i].reshape(1, -1).astype(F32)
    win = w_in[i]
    n_gdn = QKVZ_DIM + 2 * GDN_HEADS
    win = jnp.concatenate([win[:, :n_gdn], jnp.zeros((D_MODEL, BA_DIM - 2 * GDN_HEADS), win.dtype),
                           win[:, n_gdn:]], axis=1)
    lane_pad = lambda v: jnp.zeros((1, BA_DIM), F32).at[0, GDN_HEADS:2 * GDN_HEADS].set(v[i].astype(F32))
    return dict(
        g1pre=row(norm_ffn1_pre), g1post=row(norm_ffn1_post),
        wg1=ffn1_w_gate[i].astype(BF16), wu1=ffn1_w_up[i].astype(BF16), wd1=ffn1_w_down[i].astype(BF16),
        gmix=row(norm_mix_pre), gmixpost=row(norm_mix_post), win=win.astype(BF16),
        conv_w=conv_w[i].astype(F32), alog_row=lane_pad(gdn_a_log), dtb_row=lane_pad(gdn_dt_bias),
        gnorm=row(gdn_norm), sinks=swa_sinks[i].astype(F32), wo=w_out[i].astype(BF16),
        g2pre=row(norm_ffn2_pre), g2post=row(norm_ffn2_post),
        wg2=ffn2_w_gate[i].astype(BF16), wu2=ffn2_w_up[i].astype(BF16), wd2=ffn2_w_down[i].astype(BF16),
        wpg=ple_gate[i].astype(BF16), wpp=ple_proj[i].astype(BF16), gple=row(norm_ple_post))


def kernel(x_prompt, x_sample, state_conv, state_gdn, cache_swa_k, cache_swa_v, p_prompt, p_sample,
           rel_bias, norm_ffn1_pre, norm_ffn1_post, ffn1_w_gate, ffn1_w_up, ffn1_w_down,
           norm_mix_pre, norm_mix_post, w_in, conv_w, gdn_a_log, gdn_dt_bias, gdn_norm, swa_sinks,
           w_out, norm_ffn2_pre, norm_ffn2_post, ffn2_w_gate, ffn2_w_up, ffn2_w_down,
           ple_gate, ple_proj, norm_ple_post):
    depth = w_in.shape[0]
    batch, seq, _ = x_prompt.shape
    dec_batch, dec_seq, _ = x_sample.shape
    wb = cache_swa_k.shape[2]
    wp = min(SWA_WINDOW, seq)
    rel_bias = rel_bias.astype(F32)
    bias_p = _prompt_bias(rel_bias)
    bias_db, bias_dn = _decode_bias(rel_bias, wb, dec_seq)

    yp = x_prompt.reshape(batch * seq, D_MODEL)
    ys = x_sample.reshape(dec_batch * dec_seq, D_MODEL)
    outs = [[] for _ in range(8)]
    for i in range(depth):
        lw = _pack_layer(i, norm_ffn1_pre, norm_ffn1_post, ffn1_w_gate, ffn1_w_up, ffn1_w_down,
                         norm_mix_pre, norm_mix_post, w_in, conv_w, gdn_a_log, gdn_dt_bias, gdn_norm,
                         swa_sinks, w_out, norm_ffn2_pre, norm_ffn2_post, ffn2_w_gate, ffn2_w_up,
                         ffn2_w_down, ple_gate, ple_proj, norm_ple_post)
        x1, qkvz, ba, qs, kv = _head_call(yp, lw)
        gdn_o, s_fin = _gdn_prompt_call(qkvz, ba, lw, batch, seq)
        swa_o = _swa_prompt_call(qs, kv, bias_p, lw["sinks"], batch, seq)
        yp = _tail_call(x1, gdn_o, swa_o, p_prompt[i].reshape(batch * seq, PLE_DIM), lw)
        kv3 = kv.reshape(batch, seq, 2 * SWA_KV_DIM)
        outs[0].append(qkvz.reshape(batch, seq, QKVZ_DIM)[:, seq - (CONV_WIDTH - 1):, :CONV_DIM])
        outs[1].append(s_fin)
        outs[2].append(kv3[:, seq - wp:, :SWA_KV_DIM].reshape(batch, wp, SWA_KV_HEADS, SWA_HEAD_DIM))
        outs[3].append(kv3[:, seq - wp:, SWA_KV_DIM:].reshape(batch, wp, SWA_KV_HEADS, SWA_HEAD_DIM))
        x1, qkvz, ba, qs, kv = _head_call(ys, lw)
        gdn_o, s_new = _gdn_decode_call(qkvz, ba, state_conv[i], state_gdn[i], lw, dec_batch, dec_seq)
        kbuf = cache_swa_k[i].reshape(dec_batch, wb, SWA_KV_DIM)
        vbuf = cache_swa_v[i].reshape(dec_batch, wb, SWA_KV_DIM)
        swa_o = _swa_decode_call(qs, kv, kbuf, vbuf, bias_db, bias_dn, lw["sinks"], dec_batch, dec_seq)
        ys = _tail_call(x1, gdn_o, swa_o, p_sample[i].reshape(dec_batch * dec_seq, PLE_DIM), lw)
        kv3 = kv.reshape(dec_batch, dec_seq, 2 * SWA_KV_DIM)
        xp = jnp.concatenate([state_conv[i], qkvz.reshape(dec_batch, dec_seq, QKVZ_DIM)[:, :, :CONV_DIM]], axis=1)
        outs[4].append(xp[:, dec_seq:])
        outs[5].append(s_new)
        outs[6].append(jnp.concatenate([kbuf, kv3[:, :, :SWA_KV_DIM]], axis=1)[:, dec_seq:]
                       .reshape(dec_batch, wb, SWA_KV_HEADS, SWA_HEAD_DIM))
        outs[7].append(jnp.concatenate([vbuf, kv3[:, :, SWA_KV_DIM:]], axis=1)[:, dec_seq:]
                       .reshape(dec_batch, wb, SWA_KV_HEADS, SWA_HEAD_DIM))
    return (yp.reshape(batch, seq, D_MODEL), ys.reshape(dec_batch, dec_seq, D_MODEL),
            *[jnp.stack(o) for o in outs])
```

```python
import functools
import math

import numpy as np
import jax
import jax.numpy as jnp
from jax import lax
from jax.experimental import pallas as pl
from jax.experimental.pallas import tpu as pltpu

F32 = jnp.float32
BF16 = jnp.bfloat16

D_MODEL = 1024
NORM_EPS = 1e-6
PLE_DIM = 256
FFN_DIM = 2816
GDN_HEADS = 4
GDN_DK = 128
GDN_DV = 128
GDN_KEY_DIM = GDN_HEADS * GDN_DK
GDN_VAL_DIM = GDN_HEADS * GDN_DV
CONV_DIM = 2 * GDN_KEY_DIM + GDN_VAL_DIM
CONV_WIDTH = 4
SWA_HEADS = 8
SWA_KV_HEADS = 2
SWA_GROUP = SWA_HEADS // SWA_KV_HEADS
SWA_HEAD_DIM = 64
SWA_Q_DIM = SWA_HEADS * SWA_HEAD_DIM
SWA_KV_DIM = SWA_KV_HEADS * SWA_HEAD_DIM
SWA_WINDOW = 128
NUM_BUCKETS = 32
REL_MAX_DISTANCE = 128

QKVZ_DIM = CONV_DIM + GDN_VAL_DIM
BA_DIM = 128
PROJ_PACKED = QKVZ_DIM + BA_DIM + SWA_Q_DIM + 2 * SWA_KV_DIM

GDN_CHUNK = 64
GDN_STEP_CHUNKS = 4
SWA_STEP_BLOCKS = 4
ROW_TILE = 256
DEC_SEQ_BLOCK = 8
VMEM_LIMIT = 56 * 1024 * 1024

NEG_INF = float("-inf")


def _resident(shape):
    nd = len(shape)
    return pl.BlockSpec(shape, lambda *_: (0,) * nd, pipeline_mode=pl.Buffered(1))


def _rms(x, gain):
    ms = jnp.mean(x * x, axis=-1, keepdims=True)
    return (x * lax.rsqrt(ms + NORM_EPS)) * gain


def _sigmoid(x):
    return 1.0 / (1.0 + jnp.exp(-x))


def _silu(x):
    h = 0.5 * x
    return h + h * jnp.tanh(h)


def _dot(a, b):
    return jnp.dot(a, b, preferred_element_type=F32)


def _dot_nt(a, b):
    return lax.dot_general(a, b, (((1,), (1,)), ((), ())), preferred_element_type=F32)


def _dot_tn(a, b):
    return lax.dot_general(a, b, (((0,), (0,)), ((), ())), preferred_element_type=F32)


def _split(a):
    hi = a.astype(BF16)
    lo = (a - hi.astype(F32)).astype(BF16)
    return hi, lo


def _swiglu_block(h, wg_ref, wu_ref, wd_ref):
    g = jnp.dot(h, wg_ref[...], preferred_element_type=F32)
    u = jnp.dot(h, wu_ref[...], preferred_element_type=F32)
    a = (_silu(g) * u).astype(BF16)
    return jnp.dot(a, wd_ref[...], preferred_element_type=F32)


def _head_kernel(x_ref, g1pre_ref, g1post_ref, wg_ref, wu_ref, wd_ref, gmix_ref, win_ref,
                 x1_ref, qkvz_ref, ba_ref, qs_ref, kv_ref):
    x = x_ref[...]
    h = _rms(x, g1pre_ref[...]).astype(BF16)
    y = _swiglu_block(h, wg_ref, wu_ref, wd_ref)
    x1 = x + 0.5 * _rms(y, g1post_ref[...])
    x1_ref[...] = x1
    h2 = _rms(x1, gmix_ref[...]).astype(BF16)
    c0, c1, c2 = QKVZ_DIM, QKVZ_DIM + BA_DIM, QKVZ_DIM + BA_DIM + SWA_Q_DIM
    qkvz_ref[...] = jnp.dot(h2, win_ref[:, :c0], preferred_element_type=F32)
    ba_ref[...] = jnp.dot(h2, win_ref[:, c0:c1], preferred_element_type=F32)
    qs_ref[...] = jnp.dot(h2, win_ref[:, c1:c2], preferred_element_type=F32)
    kv_ref[...] = jnp.dot(h2, win_ref[:, c2:], preferred_element_type=F32)


def _head_call(x, lw):
    n = x.shape[0]
    tm = min(ROW_TILE, n)
    row = lambda w: pl.BlockSpec((tm, w), lambda i: (i, 0))
    return pl.pallas_call(
        _head_kernel,
        grid=(n // tm,),
        in_specs=[row(D_MODEL), _resident((1, D_MODEL)), _resident((1, D_MODEL)),
                  _resident((D_MODEL, FFN_DIM)), _resident((D_MODEL, FFN_DIM)),
                  _resident((FFN_DIM, D_MODEL)), _resident((1, D_MODEL)),
                  _resident((D_MODEL, PROJ_PACKED))],
        out_specs=[row(D_MODEL), row(QKVZ_DIM), row(BA_DIM), row(SWA_Q_DIM), row(2 * SWA_KV_DIM)],
        out_shape=[jax.ShapeDtypeStruct((n, D_MODEL), F32), jax.ShapeDtypeStruct((n, QKVZ_DIM), F32),
                   jax.ShapeDtypeStruct((n, BA_DIM), F32), jax.ShapeDtypeStruct((n, SWA_Q_DIM), F32),
                   jax.ShapeDtypeStruct((n, 2 * SWA_KV_DIM), F32)],
        compiler_params=pltpu.CompilerParams(dimension_semantics=("arbitrary",),
                                             vmem_limit_bytes=VMEM_LIMIT),
        name="ffn1_inproj",
    )(x, lw["g1pre"], lw["g1post"], lw["wg1"], lw["wu1"], lw["wd1"], lw["gmix"], lw["win"])


def _tail_kernel(x_ref, gdn_ref, swa_ref, p_ref, wo_ref, gmixpost_ref, g2pre_ref, g2post_ref,
                 wg_ref, wu_ref, wd_ref, wpg_ref, wpp_ref, gple_ref, y_ref):
    x = x_ref[...]
    mix = (jnp.dot(gdn_ref[...].astype(BF16), wo_ref[:GDN_VAL_DIM, :], preferred_element_type=F32)
           + jnp.dot(swa_ref[...].astype(BF16), wo_ref[GDN_VAL_DIM:, :], preferred_element_type=F32))
    x = x + _rms(mix, gmixpost_ref[...])
    h = _rms(x, g2pre_ref[...]).astype(BF16)
    y = _swiglu_block(h, wg_ref, wu_ref, wd_ref)
    x = x + 0.5 * _rms(y, g2post_ref[...])
    gate = _sigmoid(jnp.dot(x.astype(BF16), wpg_ref[...], preferred_element_type=F32))
    pp = jnp.dot(p_ref[...].astype(BF16), wpp_ref[...], preferred_element_type=F32)
    y_ref[...] = x + _rms(gate * pp, gple_ref[...])


def _tail_call(x1, gdn_o, swa_o, p, lw):
    n = x1.shape[0]
    tm = min(ROW_TILE, n)
    row = lambda w: pl.BlockSpec((tm, w), lambda i: (i, 0))
    return pl.pallas_call(
        _tail_kernel,
        grid=(n // tm,),
        in_specs=[row(D_MODEL), row(GDN_VAL_DIM), row(SWA_Q_DIM), row(PLE_DIM),
                  _resident((GDN_VAL_DIM + SWA_Q_DIM, D_MODEL)), _resident((1, D_MODEL)),
                  _resident((1, D_MODEL)), _resident((1, D_MODEL)),
                  _resident((D_MODEL, FFN_DIM)), _resident((D_MODEL, FFN_DIM)),
                  _resident((FFN_DIM, D_MODEL)), _resident((D_MODEL, D_MODEL)),
                  _resident((PLE_DIM, D_MODEL)), _resident((1, D_MODEL))],
        out_specs=row(D_MODEL),
        out_shape=jax.ShapeDtypeStruct((n, D_MODEL), F32),
        compiler_params=pltpu.CompilerParams(dimension_semantics=("arbitrary",),
                                             vmem_limit_bytes=VMEM_LIMIT),
        name="outproj_ffn2_ple",
    )(x1, gdn_o, swa_o, p, lw["wo"], lw["gmixpost"], lw["g2pre"], lw["g2post"],
      lw["wg2"], lw["wu2"], lw["wd2"], lw["wpg"], lw["wpp"], lw["gple"])


def _iota2(c):
    return (lax.broadcasted_iota(jnp.int32, (c, c), 0), lax.broadcasted_iota(jnp.int32, (c, c), 1))


def _gates(ba, alog_row, dtb_row):
    beta = _sigmoid(ba)
    xa = ba + dtb_row
    softplus = jnp.maximum(xa, 0.0) + jnp.log1p(jnp.exp(-jnp.abs(xa)))
    g = -jnp.exp(alog_row) * softplus
    return beta, g


def _cumsum_rows(mask01, g):
    hi, lo = _split(g)
    m = mask01.astype(BF16)
    return jnp.dot(m, hi, preferred_element_type=F32) + jnp.dot(m, lo, preferred_element_type=F32)


def _hi_lo_lanes(x):
    hi, lo = _split(x)
    return jnp.concatenate([hi, lo], axis=1)


def _rowsum_bcast(x, ones_ref):
    return jnp.dot(x, ones_ref[...], preferred_element_type=F32)


def _lane_bcast(x, sel_ref, lanes):
    out = jnp.dot(_hi_lo_lanes(x), sel_ref[...], preferred_element_type=F32)
    return [out[:, l * 128:(l + 1) * 128] for l in lanes]


def _gdn_prep(y, beta_b, gc_b, gl_b, gc_t, chunk, group, ones_ref):
    c = chunk
    n_chunks = y.shape[0] // c
    units = [(ci, h) for ci in range(n_chunks) for h in range(GDN_HEADS)]
    n = range(len(units))
    ii, jj = _iota2(c)
    same = (ii // group) == (jj // group)
    incl = jnp.logical_and(same, ii >= jj)
    strict = jnp.logical_and(same, ii > jj)

    def rows(t, ci, lo, hi):
        return t[ci * c:(ci + 1) * c, lo:hi]

    eg_b = [jnp.exp(t) for t in gc_b]
    ekd_b = [jnp.exp(gl_b[h] - gc_b[h]) for h in range(GDN_HEADS)]
    beta_u = [rows(beta_b[h], ci, 0, GDN_DK) for ci, h in units]
    eg_u = [rows(eg_b[h], ci, 0, GDN_DK) for ci, h in units]
    ekd_u = [rows(ekd_b[h], ci, 0, GDN_DK) for ci, h in units]
    gc_col = [rows(gc_b[h], ci, 0, c) for ci, h in units]
    gc_row = [gc_t[GDN_HEADS + h:GDN_HEADS + h + 1, ci * c:(ci + 1) * c] for ci, h in units]
    q = [rows(y, ci, h * GDN_DK, (h + 1) * GDN_DK) for ci, h in units]
    k = [rows(y, ci, GDN_KEY_DIM + h * GDN_DK, GDN_KEY_DIM + (h + 1) * GDN_DK) for ci, h in units]
    v = [rows(y, ci, 2 * GDN_KEY_DIM + h * GDN_DV, 2 * GDN_KEY_DIM + (h + 1) * GDN_DV) for ci, h in units]
    q = [t * lax.rsqrt(_rowsum_bcast(t * t, ones_ref) + 1e-6) * (GDN_DK ** -0.5) for t in q]
    k = [t * lax.rsqrt(_rowsum_bcast(t * t, ones_ref) + 1e-6) for t in k]
    decay = [jnp.exp(jnp.where(incl, gc_col[i] - gc_row[i], NEG_INF)) for i in n]
    kb = [k[i] * beta_u[i] for i in n]
    kq = [_dot_nt(jnp.concatenate([kb[i], q[i]], axis=0), k[i]) for i in n]
    a_mat = [jnp.where(strict, kq[i][:c] * decay[i], 0.0) for i in n]
    qk = [kq[i][c:] * decay[i] for i in n]
    eye = jnp.where(ii == jj, 1.0, 0.0).astype(F32)
    t_mat = [eye for _ in n]
    b = 1
    while b < group:
        lower = jnp.logical_and((ii // (2 * b)) == (jj // (2 * b)),
                                jnp.logical_and((ii % (2 * b)) >= b, (jj % (2 * b)) < b))
        m = [jnp.where(lower, a_mat[i], 0.0) for i in n]
        if b == 1:
            t_mat = [eye - m[i] for i in n]
        else:
            tm = [_dot(t_mat[i], m[i]) for i in n]
            t_mat = [t_mat[i] - _dot(tm[i], t_mat[i]) for i in n]
        b *= 2
    rhs = [jnp.concatenate([v[i] * beta_u[i], kb[i] * eg_u[i]], axis=-1) for i in n]
    sol = [_dot(t_mat[i], rhs[i]) for i in n]
    nest = lambda xs: [xs[ci * GDN_HEADS:(ci + 1) * GDN_HEADS] for ci in range(n_chunks)]
    u = nest([s[:, :GDN_DV] for s in sol])
    w = nest([s[:, GDN_DV:] for s in sol])
    qd = nest([q[i] * eg_u[i] for i in n])
    kd = nest([k[i] * ekd_u[i] for i in n])
    return u, w, nest(qk), qd, kd


def _gdn_out(o, z, gnorm_row, ones_ref):
    ms = _rowsum_bcast(o * o, ones_ref) * (1.0 / GDN_DV)
    return (o * lax.rsqrt(ms + NORM_EPS)) * gnorm_row * _silu(z)


def _conv_silu(xbuf_ref, cw_ref, c):
    y = xbuf_ref[pl.ds(5, c), :] * cw_ref[0:1, :]
    for j in range(1, CONV_WIDTH):
        y = y + xbuf_ref[pl.ds(5 + j, c), :] * cw_ref[j:j + 1, :]
    return _silu(y)


def _gdn_prompt_kernel(qkvz_ref, ba_ref, cw_ref, alog_ref, dtb_ref, gnorm_ref, ones_ref, sel_ref,
                       o_ref, sfin_ref, s_ref, xbuf_ref, y_ref):
    c = GDN_CHUNK
    n_chunks = GDN_STEP_CHUNKS
    r = n_chunks * c
    hs = range(GDN_HEADS)

    @pl.when(pl.program_id(1) == 0)
    def _():
        s_ref[...] = jnp.zeros_like(s_ref)
        xbuf_ref[0:8, :] = jnp.zeros((8, CONV_DIM), F32)

    xbuf_ref[8:8 + r, :] = qkvz_ref[:, :CONV_DIM]
    y_ref[...] = _conv_silu(xbuf_ref, cw_ref, r)
    xbuf_ref[0:8, :] = xbuf_ref[r:r + 8, :]

    beta, g = _gates(ba_ref[...], alog_ref[...], dtb_ref[...])
    ii, jj = _iota2(r)
    tril = jnp.where(jnp.logical_and((ii // c) == (jj // c), ii >= jj), 1.0, 0.0)
    gc = _cumsum_rows(tril, g)
    gc_t = gc.T
    lane = lax.broadcasted_iota(jnp.int32, (r, BA_DIM), 1)
    bcast = _lane_bcast(jnp.where(lane < GDN_HEADS, beta, gc), sel_ref, range(2 * GDN_HEADS))
    beta_b, gc_b = bcast[:GDN_HEADS], bcast[GDN_HEADS:]
    glast = [[gc_b[h][(ci + 1) * c - 1:(ci + 1) * c, :] for h in hs] for ci in range(n_chunks)]
    gl_b = [jnp.concatenate([jnp.broadcast_to(glast[ci][h], (c, GDN_DK)) for ci in range(n_chunks)], axis=0)
            for h in hs]
    u, w, qk, qd, kd = _gdn_prep(y_ref, beta_b, gc_b, gl_b, gc_t, c, c, ones_ref)
    s_cur = [s_ref[h] for h in hs]
    for ci in range(n_chunks):
        ws = [_dot(jnp.concatenate([w[ci][h], qd[ci][h]], axis=0), s_cur[h]) for h in hs]
        v_new = [u[ci][h] - ws[h][:c] for h in hs]
        o = [ws[h][c:] + _dot(qk[ci][h], v_new[h]) for h in hs]
        s_cur = [s_cur[h] * jnp.exp(glast[ci][h]) + _dot_tn(kd[ci][h], v_new[h]) for h in hs]
        for h in hs:
            z = qkvz_ref[ci * c:(ci + 1) * c, CONV_DIM + h * GDN_DV:CONV_DIM + (h + 1) * GDN_DV]
            o_ref[ci * c:(ci + 1) * c, h * GDN_DV:(h + 1) * GDN_DV] = _gdn_out(o[h], z, gnorm_ref[...], ones_ref)
    for h in hs:
        s_ref[h] = s_cur[h]
        sfin_ref[0, h] = s_cur[h]


def _gdn_prompt_call(qkvz, ba, lw, batch, seq):
    c = GDN_CHUNK * GDN_STEP_CHUNKS
    nc = seq // c
    return pl.pallas_call(
        _gdn_prompt_kernel,
        grid=(batch, nc),
        in_specs=[pl.BlockSpec((c, QKVZ_DIM), lambda b, s: (b * nc + s, 0)),
                  pl.BlockSpec((c, BA_DIM), lambda b, s: (b * nc + s, 0)),
                  _resident((CONV_WIDTH, CONV_DIM)), _resident((1, BA_DIM)), _resident((1, BA_DIM)),
                  _resident((1, GDN_DV)), _resident(lw["ones"].shape), _resident(lw["sel"].shape)],
        out_specs=[pl.BlockSpec((c, GDN_VAL_DIM), lambda b, s: (b * nc + s, 0)),
                   pl.BlockSpec((1, GDN_HEADS, GDN_DK, GDN_DV), lambda b, s: (b, 0, 0, 0))],
        out_shape=[jax.ShapeDtypeStruct((batch * seq, GDN_VAL_DIM), F32),
                   jax.ShapeDtypeStruct((batch, GDN_HEADS, GDN_DK, GDN_DV), F32)],
        scratch_shapes=[pltpu.VMEM((GDN_HEADS, GDN_DK, GDN_DV), F32),
                        pltpu.VMEM((c + 8, CONV_DIM), F32), pltpu.VMEM((c, CONV_DIM), F32)],
        compiler_params=pltpu.CompilerParams(dimension_semantics=("arbitrary", "arbitrary"),
                                             vmem_limit_bytes=VMEM_LIMIT),
        name="gdn_prompt",
    )(qkvz, ba, lw["conv_w"], lw["alog_row"], lw["dtb_row"], lw["gnorm"], lw["ones"], lw["sel"])


def _gdn_decode_kernel(qkvz_ref, ba_ref, sconv_ref, s0_ref, cw_ref, alog_ref, dtb_ref, gnorm_ref,
                       ones_ref, sel_ref, o_ref, snew_ref, xbuf_ref, *, dec_seq):
    nb = DEC_SEQ_BLOCK
    t = dec_seq
    c = nb * t
    ys = []
    for s in range(nb):
        xbuf_ref[s, 5:8, :] = sconv_ref[s]
        xbuf_ref[s, 8:8 + t, :] = qkvz_ref[s * t:(s + 1) * t, :CONV_DIM]
        ys.append(_conv_silu(xbuf_ref.at[s], cw_ref, t))
    y = jnp.concatenate(ys, axis=0)

    beta, g = _gates(ba_ref[...], alog_ref[...], dtb_ref[...])
    ii, jj = _iota2(c)
    same = (ii // t) == (jj // t)
    tril = jnp.where(jnp.logical_and(same, ii >= jj), 1.0, 0.0).astype(F32)
    ones = jnp.where(same, 1.0, 0.0).astype(F32)
    gc = _cumsum_rows(tril, g)
    gl = _cumsum_rows(ones, g)
    gc_t = gc.T
    lane = lax.broadcasted_iota(jnp.int32, (c, BA_DIM), 1)
    bcast = _lane_bcast(jnp.where(lane < GDN_HEADS, beta, gc), sel_ref, range(2 * GDN_HEADS))
    beta_b, gc_b = bcast[:GDN_HEADS], bcast[GDN_HEADS:]
    gl_b = _lane_bcast(gl, sel_ref, range(GDN_HEADS, 2 * GDN_HEADS))
    us, ws_, qks, qds, kds = _gdn_prep(y, beta_b, gc_b, gl_b, gc_t, c, t, ones_ref)
    u, w, qk, qd, kd = us[0], ws_[0], qks[0], qds[0], kds[0]
    hs = range(GDN_HEADS)
    units = [(s, h) for h in hs for s in range(nb)]
    s_old = {(s, h): s0_ref[s, h] for s, h in units}
    ws = {(s, h): _dot(jnp.concatenate([w[h][s * t:(s + 1) * t], qd[h][s * t:(s + 1) * t]], axis=0),
                       s_old[s, h]) for s, h in units}
    v_new = [jnp.concatenate([u[h][s * t:(s + 1) * t] - ws[s, h][:t] for s in range(nb)], axis=0) for h in hs]
    o = [jnp.concatenate([ws[s, h][t:] for s in range(nb)], axis=0) + _dot(qk[h], v_new[h]) for h in hs]
    cd = [jnp.exp(gl_b[h]) for h in hs]
    kd_t = [kd[h].T for h in hs]
    col_seq = lax.broadcasted_iota(jnp.int32, (GDN_DK, c), 1) // t
    for s, h in units:
        upd = _dot(jnp.where(col_seq == s, kd_t[h], 0.0), v_new[h])
        snew_ref[s, h] = s_old[s, h] * cd[h][s * t:s * t + 1, :] + upd
    for h in hs:
        z = qkvz_ref[:, CONV_DIM + h * GDN_DV:CONV_DIM + (h + 1) * GDN_DV]
        o_ref[:, h * GDN_DV:(h + 1) * GDN_DV] = _gdn_out(o[h], z, gnorm_ref[...], ones_ref)


def _gdn_decode_call(qkvz, ba, state_conv, state_gdn, lw, dec_batch, dec_seq):
    nb = DEC_SEQ_BLOCK
    rows = nb * dec_seq
    return pl.pallas_call(
        functools.partial(_gdn_decode_kernel, dec_seq=dec_seq),
        grid=(dec_batch // nb,),
        in_specs=[pl.BlockSpec((rows, QKVZ_DIM), lambda i: (i, 0)),
                  pl.BlockSpec((rows, BA_DIM), lambda i: (i, 0)),
                  pl.BlockSpec((nb, CONV_WIDTH - 1, CONV_DIM), lambda i: (i, 0, 0)),
                  pl.BlockSpec((nb, GDN_HEADS, GDN_DK, GDN_DV), lambda i: (i, 0, 0, 0)),
                  _resident((CONV_WIDTH, CONV_DIM)), _resident((1, BA_DIM)), _resident((1, BA_DIM)),
                  _resident((1, GDN_DV)), _resident(lw["ones"].shape), _resident(lw["sel"].shape)],
        out_specs=[pl.BlockSpec((rows, GDN_VAL_DIM), lambda i: (i, 0)),
                   pl.BlockSpec((nb, GDN_HEADS, GDN_DK, GDN_DV), lambda i: (i, 0, 0, 0))],
        out_shape=[jax.ShapeDtypeStruct((dec_batch * dec_seq, GDN_VAL_DIM), F32),
                   jax.ShapeDtypeStruct((dec_batch, GDN_HEADS, GDN_DK, GDN_DV), F32)],
        scratch_shapes=[pltpu.VMEM((nb, 16, CONV_DIM), F32)],
        compiler_params=pltpu.CompilerParams(dimension_semantics=("arbitrary",),
                                             vmem_limit_bytes=VMEM_LIMIT),
        name="gdn_decode",
    )(qkvz, ba, state_conv, state_gdn, lw["conv_w"], lw["alog_row"], lw["dtb_row"], lw["gnorm"],
      lw["ones"], lw["sel"])


def _bias_table_kernel(bucket_ref, rb_ref, out_ref):
    bucket = bucket_ref[...]
    for h in range(SWA_HEADS):
        acc = jnp.zeros(bucket.shape, F32)
        for b in range(NUM_BUCKETS):
            acc = jnp.where(bucket == b, rb_ref[b, h], acc)
        out_ref[h] = jnp.where(bucket < 0, NEG_INF, acc)


def _bias_table_call(bucket, rel_bias):
    r, c = bucket.shape
    return pl.pallas_call(
        _bias_table_kernel,
        in_specs=[pl.BlockSpec(memory_space=pltpu.VMEM), pl.BlockSpec(memory_space=pltpu.SMEM)],
        out_specs=pl.BlockSpec(memory_space=pltpu.VMEM),
        out_shape=jax.ShapeDtypeStruct((SWA_HEADS, r, c), F32),
        name="t5_bias_table",
    )(bucket, rel_bias)


def _t5_bucket(dist, valid):
    d = jnp.maximum(dist, 0)
    exact = NUM_BUCKETS // 2
    log_ratio = jnp.log(jnp.maximum(d, 1).astype(F32) / exact) / math.log(REL_MAX_DISTANCE / exact)
    large = jnp.minimum(exact + (log_ratio * (NUM_BUCKETS - exact)).astype(jnp.int32), NUM_BUCKETS - 1)
    bucket = jnp.where(d < exact, d, large)
    return jnp.where(valid, bucket, -1).astype(jnp.int32)


def _swa_prompt_kernel(q_ref, kvp_ref, kvc_ref, bias_ref, sink_ref, o_ref):
    w, dh, nq = SWA_WINDOW, SWA_HEAD_DIM, SWA_STEP_BLOCKS
    first = pl.program_id(1) == 0
    col = lax.broadcasted_iota(jnp.int32, (w, 2 * w), 1)
    drop_prev = jnp.logical_and(first, col < w)
    kv = jnp.concatenate([kvp_ref[...], kvc_ref[...]], axis=0).astype(BF16)
    k_h = [kv[:, h * dh:(h + 1) * dh] for h in range(SWA_KV_HEADS)]
    v_h = [kv[:, SWA_KV_DIM + h * dh:SWA_KV_DIM + (h + 1) * dh] for h in range(SWA_KV_HEADS)]
    units = [(b, hg) for b in range(nq) for hg in range(SWA_HEADS)]
    n = range(len(units))
    q = [(q_ref[b * w:(b + 1) * w, hg * dh:(hg + 1) * dh] * (dh ** -0.5)).astype(BF16) for b, hg in units]
    s = [lax.dot_general(q[i], k_h[hg // SWA_GROUP][b * w:(b + 2) * w], (((1,), (1,)), ((), ())),
                         preferred_element_type=F32) + bias_ref[hg] for i, (b, hg) in enumerate(units)]
    s = [jnp.where(drop_prev, NEG_INF, s[i]) if b == 0 else s[i] for i, (b, hg) in enumerate(units)]
    sink = [sink_ref[hg] for b, hg in units]
    m = [jnp.maximum(jnp.max(s[i], axis=-1, keepdims=True), sink[i]) for i in n]
    p = [jnp.exp(s[i] - m[i]) for i in n]
    denom = [jnp.sum(p[i], axis=-1, keepdims=True) + jnp.exp(sink[i] - m[i]) for i in n]
    o = [jnp.dot(p[i].astype(BF16), v_h[hg // SWA_GROUP][b * w:(b + 2) * w], preferred_element_type=F32)
         / denom[i] for i, (b, hg) in enumerate(units)]
    for i, (b, hg) in enumerate(units):
        o_ref[b * w:(b + 1) * w, hg * dh:(hg + 1) * dh] = o[i]


def _swa_prompt_call(qs, kv, bias, sinks, batch, seq):
    w, nq = SWA_WINDOW, SWA_STEP_BLOCKS
    nb = seq // (w * nq)
    return pl.pallas_call(
        _swa_prompt_kernel,
        grid=(batch, nb),
        in_specs=[pl.BlockSpec((nq * w, SWA_Q_DIM), lambda b, n: (b * nb + n, 0)),
                  pl.BlockSpec((w, 2 * SWA_KV_DIM),
                               lambda b, n: ((b * nb + n) * nq - jnp.minimum(n, 1), 0)),
                  pl.BlockSpec((nq * w, 2 * SWA_KV_DIM), lambda b, n: (b * nb + n, 0)),
                  _resident((SWA_HEADS, w, 2 * w)),
                  pl.BlockSpec(memory_space=pltpu.SMEM)],
        out_specs=pl.BlockSpec((nq * w, SWA_Q_DIM), lambda b, n: (b * nb + n, 0)),
        out_shape=jax.ShapeDtypeStruct((batch * seq, SWA_Q_DIM), F32),
        compiler_params=pltpu.CompilerParams(dimension_semantics=("arbitrary", "arbitrary"),
                                             vmem_limit_bytes=VMEM_LIMIT),
        name="swa_prompt",
    )(qs, kv, kv, bias, sinks)


def _swa_decode_kernel(q_ref, kvn_ref, kbuf_ref, vbuf_ref, bbuf_ref, bnew_ref, sink_ref, o_ref,
                       *, dec_seq):
    nb = DEC_SEQ_BLOCK
    t = dec_seq
    rows = nb * t
    m_rows = SWA_GROUP * rows
    wb = kbuf_ref.shape[1]
    ri = lax.broadcasted_iota(jnp.int32, (m_rows, 1), 0)
    row_seq = (ri % rows) // t
    row_grp = ri // rows
    for h in range(SWA_KV_HEADS):
        lo, hi = h * SWA_HEAD_DIM, (h + 1) * SWA_HEAD_DIM
        q = jnp.concatenate(
            [q_ref[:, (h * SWA_GROUP + g) * SWA_HEAD_DIM:(h * SWA_GROUP + g + 1) * SWA_HEAD_DIM]
             for g in range(SWA_GROUP)], axis=0)
        sink = jnp.zeros((m_rows, 1), F32)
        for g in range(SWA_GROUP):
            sink = jnp.where(row_grp == g, sink_ref[h * SWA_GROUP + g], sink)
        k_new = jnp.concatenate([kvn_ref[:, lo:hi], jnp.zeros((m_rows - rows, SWA_HEAD_DIM), F32)], axis=0)
        v_new = jnp.concatenate([kvn_ref[:, SWA_KV_DIM + lo:SWA_KV_DIM + hi],
                                 jnp.zeros((m_rows - rows, SWA_HEAD_DIM), F32)], axis=0)
        s_new = _dot_nt(q, k_new) * (SWA_HEAD_DIM ** -0.5) + bnew_ref[h]
        s_buf = jnp.zeros((m_rows, wb), F32)
        for s in range(nb):
            s_full = _dot_nt(q, kbuf_ref[s, :, lo:hi])
            s_buf = jnp.where(row_seq == s, s_full, s_buf)
        s_buf = s_buf * (SWA_HEAD_DIM ** -0.5) + bbuf_ref[h]
        m = jnp.maximum(jnp.maximum(jnp.max(s_buf, axis=-1, keepdims=True),
                                    jnp.max(s_new, axis=-1, keepdims=True)), sink)
        p_buf = jnp.exp(s_buf - m)
        p_new = jnp.exp(s_new - m)
        denom = (jnp.sum(p_buf, axis=-1, keepdims=True) + jnp.sum(p_new, axis=-1, keepdims=True)
                 + jnp.exp(sink - m))
        o = _dot(p_new, v_new)
        for s in range(nb):
            o = o + _dot(jnp.where(row_seq == s, p_buf, 0.0), vbuf_ref[s, :, lo:hi])
        o = o / denom
        for g in range(SWA_GROUP):
            hg = h * SWA_GROUP + g
            o_ref[:, hg * SWA_HEAD_DIM:(hg + 1) * SWA_HEAD_DIM] = o[g * rows:(g + 1) * rows]


def _swa_decode_call(qs, kvn, kbuf, vbuf, bias_buf, bias_new, sinks, dec_batch, dec_seq):
    nb = DEC_SEQ_BLOCK
    rows = nb * dec_seq
    wb = kbuf.shape[1]
    m_rows = SWA_GROUP * rows
    return pl.pallas_call(
        functools.partial(_swa_decode_kernel, dec_seq=dec_seq),
        grid=(dec_batch // nb,),
        in_specs=[pl.BlockSpec((rows, SWA_Q_DIM), lambda i: (i, 0)),
                  pl.BlockSpec((rows, 2 * SWA_KV_DIM), lambda i: (i, 0)),
                  pl.BlockSpec((nb, wb, SWA_KV_DIM), lambda i: (i, 0, 0)),
                  pl.BlockSpec((nb, wb, SWA_KV_DIM), lambda i: (i, 0, 0)),
                  _resident((SWA_KV_HEADS, m_rows, wb)),
                  _resident((SWA_KV_HEADS, m_rows, m_rows)),
                  pl.BlockSpec(memory_space=pltpu.SMEM)],
        out_specs=pl.BlockSpec((rows, SWA_Q_DIM), lambda i: (i, 0)),
        out_shape=jax.ShapeDtypeStruct((dec_batch * dec_seq, SWA_Q_DIM), F32),
        compiler_params=pltpu.CompilerParams(dimension_semantics=("arbitrary",),
                                             vmem_limit_bytes=VMEM_LIMIT),
        name="swa_decode",
    )(qs, kvn, kbuf, vbuf, bias_buf, bias_new, sinks)


def _prompt_bias(rel_bias):
    w = SWA_WINDOW
    dist = w + jnp.arange(w)[:, None] - jnp.arange(2 * w)[None, :]
    valid = (dist >= 0) & (dist < w)
    return _bias_table_call(_t5_bucket(dist, valid), rel_bias)


def _decode_bias(rel_bias, wb, dec_seq):
    nb, t = DEC_SEQ_BLOCK, dec_seq
    rows = nb * t
    m_rows = SWA_GROUP * rows
    r = jnp.arange(m_rows)
    r_tok, r_seq = r % t, (r % rows) // t
    dist_b = wb + r_tok[:, None] - jnp.arange(wb)[None, :]
    tab_b = _bias_table_call(_t5_bucket(dist_b, (dist_b >= 0) & (dist_b < SWA_WINDOW)), rel_bias)
    cidx = jnp.arange(m_rows)
    dist_n = r_tok[:, None] - (cidx % t)[None, :]
    ok_n = (dist_n >= 0) & (dist_n < SWA_WINDOW) & (r_seq[:, None] == (cidx // t)[None, :]) & (cidx < rows)[None, :]
    tab_n = _bias_table_call(_t5_bucket(dist_n, ok_n), rel_bias)

    def pick(tab):
        return jnp.stack([jnp.concatenate([tab[h * SWA_GROUP + g, g * rows:(g + 1) * rows]
                                           for g in range(SWA_GROUP)], axis=0)
                          for h in range(SWA_KV_HEADS)])
    return pick(tab_b), pick(tab_n)


def _pack_layer(i, norm_ffn1_pre, norm_ffn1_post, ffn1_w_gate, ffn1_w_up, ffn1_w_down, norm_mix_pre,
                norm_mix_post, w_in, conv_w, gdn_a_log, gdn_dt_bias, gdn_norm, swa_sinks, w_out,
                norm_ffn2_pre, norm_ffn2_post, ffn2_w_gate, ffn2_w_up, ffn2_w_down, ple_gate, ple_proj,
                norm_ple_post):
    row = lambda g: g[i].reshape(1, -1).astype(F32)
    win = w_in[i]
    n_gdn = QKVZ_DIM + 2 * GDN_HEADS
    win = jnp.concatenate([win[:, :n_gdn], jnp.zeros((D_MODEL, BA_DIM - 2 * GDN_HEADS), win.dtype),
                           win[:, n_gdn:]], axis=1)
    lane_pad = lambda v: jnp.zeros((1, BA_DIM), F32).at[0, GDN_HEADS:2 * GDN_HEADS].set(v[i].astype(F32))
    kk = np.arange(2 * 128)[:, None] % 128
    sel = (kk == (np.arange(2 * GDN_HEADS * 128)[None, :] // 128)).astype(np.float32)
    return dict(
        ones=jnp.ones((128, 128), F32), sel=jnp.asarray(sel, BF16),
        g1pre=row(norm_ffn1_pre), g1post=row(norm_ffn1_post),
        wg1=ffn1_w_gate[i].astype(BF16), wu1=ffn1_w_up[i].astype(BF16), wd1=ffn1_w_down[i].astype(BF16),
        gmix=row(norm_mix_pre), gmixpost=row(norm_mix_post), win=win.astype(BF16),
        conv_w=conv_w[i].astype(F32), alog_row=lane_pad(gdn_a_log), dtb_row=lane_pad(gdn_dt_bias),
        gnorm=row(gdn_norm), sinks=swa_sinks[i].astype(F32), wo=w_out[i].astype(BF16),
        g2pre=row(norm_ffn2_pre), g2post=row(norm_ffn2_post),
        wg2=ffn2_w_gate[i].astype(BF16), wu2=ffn2_w_up[i].astype(BF16), wd2=ffn2_w_down[i].astype(BF16),
        wpg=ple_gate[i].astype(BF16), wpp=ple_proj[i].astype(BF16), gple=row(norm_ple_post))


def kernel(x_prompt, x_sample, state_conv, state_gdn, cache_swa_k, cache_swa_v, p_prompt, p_sample,
           rel_bias, norm_ffn1_pre, norm_ffn1_post, ffn1_w_gate, ffn1_w_up, ffn1_w_down,
           norm_mix_pre, norm_mix_post, w_in, conv_w, gdn_a_log, gdn_dt_bias, gdn_norm, swa_sinks,
           w_out, norm_ffn2_pre, norm_ffn2_post, ffn2_w_gate, ffn2_w_up, ffn2_w_down,
           ple_gate, ple_proj, norm_ple_post):
    depth = w_in.shape[0]
    batch, seq, _ = x_prompt.shape
    dec_batch, dec_seq, _ = x_sample.shape
    wb = cache_swa_k.shape[2]
    wp = min(SWA_WINDOW, seq)
    rel_bias = rel_bias.astype(F32)
    bias_p = _prompt_bias(rel_bias)
    bias_db, bias_dn = _decode_bias(rel_bias, wb, dec_seq)

    yp = x_prompt.reshape(batch * seq, D_MODEL)
    ys = x_sample.reshape(dec_batch * dec_seq, D_MODEL)
    outs = [[] for _ in range(8)]
    for i in range(depth):
        lw = _pack_layer(i, norm_ffn1_pre, norm_ffn1_post, ffn1_w_gate, ffn1_w_up, ffn1_w_down,
                         norm_mix_pre, norm_mix_post, w_in, conv_w, gdn_a_log, gdn_dt_bias, gdn_norm,
                         swa_sinks, w_out, norm_ffn2_pre, norm_ffn2_post, ffn2_w_gate, ffn2_w_up,
                         ffn2_w_down, ple_gate, ple_proj, norm_ple_post)
        x1, qkvz, ba, qs, kv = _head_call(yp, lw)
        gdn_o, s_fin = _gdn_prompt_call(qkvz, ba, lw, batch, seq)
        swa_o = _swa_prompt_call(qs, kv, bias_p, lw["sinks"], batch, seq)
        yp = _tail_call(x1, gdn_o, swa_o, p_prompt[i].reshape(batch * seq, PLE_DIM), lw)
        kv3 = kv.reshape(batch, seq, 2 * SWA_KV_DIM)
        outs[0].append(qkvz.reshape(batch, seq, QKVZ_DIM)[:, seq - (CONV_WIDTH - 1):, :CONV_DIM])
        outs[1].append(s_fin)
        outs[2].append(kv3[:, seq - wp:, :SWA_KV_DIM].reshape(batch, wp, SWA_KV_HEADS, SWA_HEAD_DIM))
        outs[3].append(kv3[:, seq - wp:, SWA_KV_DIM:].reshape(batch, wp, SWA_KV_HEADS, SWA_HEAD_DIM))
        x1, qkvz, ba, qs, kv = _head_call(ys, lw)
        gdn_o, s_new = _gdn_decode_call(qkvz, ba, state_conv[i], state_gdn[i], lw, dec_batch, dec_seq)
        kbuf = cache_swa_k[i].reshape(dec_batch, wb, SWA_KV_DIM)
        vbuf = cache_swa_v[i].reshape(dec_batch, wb, SWA_KV_DIM)
        swa_o = _swa_decode_call(qs, kv, kbuf, vbuf, bias_db, bias_dn, lw["sinks"], dec_batch, dec_seq)
        ys = _tail_call(x1, gdn_o, swa_o, p_sample[i].reshape(dec_batch * dec_seq, PLE_DIM), lw)
        kv3 = kv.reshape(dec_batch, dec_seq, 2 * SWA_KV_DIM)
        xp = jnp.concatenate([state_conv[i], qkvz.reshape(dec_batch, dec_seq, QKVZ_DIM)[:, :, :CONV_DIM]], axis=1)
        outs[4].append(xp[:, dec_seq:])
        outs[5].append(s_new)
        outs[6].append(jnp.concatenate([kbuf, kv3[:, :, :SWA_KV_DIM]], axis=1)[:, dec_seq:]
                       .reshape(dec_batch, wb, SWA_KV_HEADS, SWA_HEAD_DIM))
        outs[7].append(jnp.concatenate([vbuf, kv3[:, :, SWA_KV_DIM:]], axis=1)[:, dec_seq:]
                       .reshape(dec_batch, wb, SWA_KV_HEADS, SWA_HEAD_DIM))
    return (yp.reshape(batch, seq, D_MODEL), ys.reshape(dec_batch, dec_seq, D_MODEL),
            *[jnp.stack(o) for o in outs])
```

```python
import functools
import math

import numpy as np
import jax
import jax.numpy as jnp
from jax import lax
from jax.experimental import pallas as pl
from jax.experimental.pallas import tpu as pltpu

F32 = jnp.float32
BF16 = jnp.bfloat16

D_MODEL = 1024
NORM_EPS = 1e-6
PLE_DIM = 256
FFN_DIM = 2816
GDN_HEADS = 4
GDN_DK = 128
GDN_DV = 128
GDN_KEY_DIM = GDN_HEADS * GDN_DK
GDN_VAL_DIM = GDN_HEADS * GDN_DV
CONV_DIM = 2 * GDN_KEY_DIM + GDN_VAL_DIM
CONV_WIDTH = 4
SWA_HEADS = 8
SWA_KV_HEADS = 2
SWA_GROUP = SWA_HEADS // SWA_KV_HEADS
SWA_HEAD_DIM = 64
SWA_Q_DIM = SWA_HEADS * SWA_HEAD_DIM
SWA_KV_DIM = SWA_KV_HEADS * SWA_HEAD_DIM
SWA_WINDOW = 128
NUM_BUCKETS = 32
REL_MAX_DISTANCE = 128

QKVZ_DIM = CONV_DIM + GDN_VAL_DIM
BA_DIM = 128
PROJ_PACKED = QKVZ_DIM + BA_DIM + SWA_Q_DIM + 2 * SWA_KV_DIM

GDN_CHUNK = 64
GDN_STEP_CHUNKS = 4
SWA_STEP_BLOCKS = 4
ROW_TILE = 256
DEC_SEQ_BLOCK = 8
VMEM_LIMIT = 56 * 1024 * 1024

NEG_INF = float("-inf")
LANES = 128


def _resident(shape):
    nd = len(shape)
    return pl.BlockSpec(shape, lambda *_: (0,) * nd, pipeline_mode=pl.Buffered(1))


def _rms(x, gain):
    ms = jnp.mean(x * x, axis=-1, keepdims=True)
    return (x * lax.rsqrt(ms + NORM_EPS)) * gain


def _sigmoid(x):
    return 1.0 / (1.0 + jnp.exp(-x))


def _silu(x):
    h = 0.5 * x
    return h + h * jnp.tanh(h)


def _dot(a, b):
    return jnp.dot(a, b, preferred_element_type=F32)


def _dot_nt(a, b):
    return lax.dot_general(a, b, (((1,), (1,)), ((), ())), preferred_element_type=F32)


def _dot_tn(a, b):
    return lax.dot_general(a, b, (((0,), (0,)), ((), ())), preferred_element_type=F32)


def _split(a):
    hi = a.astype(BF16)
    lo = (a - hi.astype(F32)).astype(BF16)
    return hi, lo


def _swiglu_block(h, wg_ref, wu_ref, wd_ref):
    g = jnp.dot(h, wg_ref[...], preferred_element_type=F32)
    u = jnp.dot(h, wu_ref[...], preferred_element_type=F32)
    a = (_silu(g) * u).astype(BF16)
    return jnp.dot(a, wd_ref[...], preferred_element_type=F32)


def _head_kernel(x_ref, g1pre_ref, g1post_ref, wg_ref, wu_ref, wd_ref, gmix_ref, win_ref,
                 x1_ref, qkvz_ref, ba_ref, qs_ref, kv_ref):
    x = x_ref[...]
    h = _rms(x, g1pre_ref[...]).astype(BF16)
    y = _swiglu_block(h, wg_ref, wu_ref, wd_ref)
    x1 = x + 0.5 * _rms(y, g1post_ref[...])
    x1_ref[...] = x1
    h2 = _rms(x1, gmix_ref[...]).astype(BF16)
    c0, c1, c2 = QKVZ_DIM, QKVZ_DIM + BA_DIM, QKVZ_DIM + BA_DIM + SWA_Q_DIM
    qkvz_ref[...] = jnp.dot(h2, win_ref[:, :c0], preferred_element_type=F32)
    ba_ref[...] = jnp.dot(h2, win_ref[:, c0:c1], preferred_element_type=F32)
    qs_ref[...] = jnp.dot(h2, win_ref[:, c1:c2], preferred_element_type=F32)
    kv_ref[...] = jnp.dot(h2, win_ref[:, c2:], preferred_element_type=F32)


def _head_call(x, lw):
    n = x.shape[0]
    tm = min(ROW_TILE, n)
    row = lambda w: pl.BlockSpec((tm, w), lambda i: (i, 0))
    return pl.pallas_call(
        _head_kernel,
        grid=(n // tm,),
        in_specs=[row(D_MODEL), _resident((1, D_MODEL)), _resident((1, D_MODEL)),
                  _resident((D_MODEL, FFN_DIM)), _resident((D_MODEL, FFN_DIM)),
                  _resident((FFN_DIM, D_MODEL)), _resident((1, D_MODEL)),
                  _resident((D_MODEL, PROJ_PACKED))],
        out_specs=[row(D_MODEL), row(QKVZ_DIM), row(BA_DIM), row(SWA_Q_DIM), row(2 * SWA_KV_DIM)],
        out_shape=[jax.ShapeDtypeStruct((n, D_MODEL), F32), jax.ShapeDtypeStruct((n, QKVZ_DIM), F32),
                   jax.ShapeDtypeStruct((n, BA_DIM), F32), jax.ShapeDtypeStruct((n, SWA_Q_DIM), F32),
                   jax.ShapeDtypeStruct((n, 2 * SWA_KV_DIM), F32)],
        compiler_params=pltpu.CompilerParams(dimension_semantics=("arbitrary",),
                                             vmem_limit_bytes=VMEM_LIMIT),
        name="ffn1_inproj",
    )(x, lw["g1pre"], lw["g1post"], lw["wg1"], lw["wu1"], lw["wd1"], lw["gmix"], lw["win"])


def _tail_kernel(x_ref, gdn_ref, swa_ref, p_ref, wo_ref, gmixpost_ref, g2pre_ref, g2post_ref,
                 wg_ref, wu_ref, wd_ref, wpg_ref, wpp_ref, gple_ref, y_ref):
    x = x_ref[...]
    mix = (jnp.dot(gdn_ref[...].astype(BF16), wo_ref[:GDN_VAL_DIM, :], preferred_element_type=F32)
           + jnp.dot(swa_ref[...].astype(BF16), wo_ref[GDN_VAL_DIM:, :], preferred_element_type=F32))
    x = x + _rms(mix, gmixpost_ref[...])
    h = _rms(x, g2pre_ref[...]).astype(BF16)
    y = _swiglu_block(h, wg_ref, wu_ref, wd_ref)
    x = x + 0.5 * _rms(y, g2post_ref[...])
    gate = _sigmoid(jnp.dot(x.astype(BF16), wpg_ref[...], preferred_element_type=F32))
    pp = jnp.dot(p_ref[...].astype(BF16), wpp_ref[...], preferred_element_type=F32)
    y_ref[...] = x + _rms(gate * pp, gple_ref[...])


def _tail_call(x1, gdn_o, swa_o, p, lw):
    n = x1.shape[0]
    tm = min(ROW_TILE, n)
    row = lambda w: pl.BlockSpec((tm, w), lambda i: (i, 0))
    return pl.pallas_call(
        _tail_kernel,
        grid=(n // tm,),
        in_specs=[row(D_MODEL), row(GDN_VAL_DIM), row(SWA_Q_DIM), row(PLE_DIM),
                  _resident((GDN_VAL_DIM + SWA_Q_DIM, D_MODEL)), _resident((1, D_MODEL)),
                  _resident((1, D_MODEL)), _resident((1, D_MODEL)),
                  _resident((D_MODEL, FFN_DIM)), _resident((D_MODEL, FFN_DIM)),
                  _resident((FFN_DIM, D_MODEL)), _resident((D_MODEL, D_MODEL)),
                  _resident((PLE_DIM, D_MODEL)), _resident((1, D_MODEL))],
        out_specs=row(D_MODEL),
        out_shape=jax.ShapeDtypeStruct((n, D_MODEL), F32),
        compiler_params=pltpu.CompilerParams(dimension_semantics=("arbitrary",),
                                             vmem_limit_bytes=VMEM_LIMIT),
        name="outproj_ffn2_ple",
    )(x1, gdn_o, swa_o, p, lw["wo"], lw["gmixpost"], lw["g2pre"], lw["g2post"],
      lw["wg2"], lw["wu2"], lw["wd2"], lw["wpg"], lw["wpp"], lw["gple"])


def _iota2(c):
    return (lax.broadcasted_iota(jnp.int32, (c, c), 0), lax.broadcasted_iota(jnp.int32, (c, c), 1))


def _gates(ba, alog_row, dtb_row):
    beta = _sigmoid(ba)
    xa = ba + dtb_row
    softplus = jnp.maximum(xa, 0.0) + jnp.log1p(jnp.exp(-jnp.abs(xa)))
    g = -jnp.exp(alog_row) * softplus
    return beta, g


def _cumsum_rows(mask01, g):
    hi, lo = _split(g)
    m = mask01.astype(BF16)
    return jnp.dot(m, hi, preferred_element_type=F32) + jnp.dot(m, lo, preferred_element_type=F32)


def _hi_lo_lanes(x):
    hi, lo = _split(x)
    return jnp.concatenate([hi, lo], axis=1)


def _rowsum_bcast(x, ones_ref):
    return jnp.dot(x, ones_ref[...], preferred_element_type=F32)


def _lane_bcast(x, sel_ref, lanes):
    out = jnp.dot(_hi_lo_lanes(x), sel_ref[...], preferred_element_type=F32)
    return [out[:, l * 128:(l + 1) * 128] for l in lanes]


def _gdn_prep_steps(out, y, beta_b, gc_b, gl_b, gc_t, chunk, group, ones_ref, normalized):
    c = chunk
    n_chunks = y.shape[0] // c
    units = [(ci, h) for ci in range(n_chunks) for h in range(GDN_HEADS)]
    n = range(len(units))
    ii, jj = _iota2(c)
    same = (ii // group) == (jj // group)
    incl = jnp.logical_and(same, ii >= jj)
    strict = jnp.logical_and(same, ii > jj)

    def rows(t, ci, lo, hi):
        return t[ci * c:(ci + 1) * c, lo:hi]

    eg_b = [jnp.exp(t) for t in gc_b]
    ekd_b = [jnp.exp(gl_b[h] - gc_b[h]) for h in range(GDN_HEADS)]
    beta_u = [rows(beta_b[h], ci, 0, GDN_DK) for ci, h in units]
    eg_u = [rows(eg_b[h], ci, 0, GDN_DK) for ci, h in units]
    ekd_u = [rows(ekd_b[h], ci, 0, GDN_DK) for ci, h in units]
    gc_col = [rows(gc_b[h], ci, 0, c) for ci, h in units]
    gc_row = [gc_t[GDN_HEADS + h:GDN_HEADS + h + 1, ci * c:(ci + 1) * c] for ci, h in units]
    q = [rows(y, ci, h * GDN_DK, (h + 1) * GDN_DK) for ci, h in units]
    k = [rows(y, ci, GDN_KEY_DIM + h * GDN_DK, GDN_KEY_DIM + (h + 1) * GDN_DK) for ci, h in units]
    v = [rows(y, ci, 2 * GDN_KEY_DIM + h * GDN_DV, 2 * GDN_KEY_DIM + (h + 1) * GDN_DV) for ci, h in units]
    if not normalized:
        q = [_l2norm(t, ones_ref) * (GDN_DK ** -0.5) for t in q]
        k = [_l2norm(t, ones_ref) for t in k]
    decay = [jnp.exp(jnp.where(incl, gc_col[i] - gc_row[i], NEG_INF)) for i in n]
    kb = [k[i] * beta_u[i] for i in n]
    kq = [_dot_nt(jnp.concatenate([kb[i], q[i]], axis=0), k[i]) for i in n]
    yield
    a_mat = [jnp.where(strict, kq[i][:c] * decay[i], 0.0) for i in n]
    qk = [kq[i][c:] * decay[i] for i in n]
    eye = jnp.where(ii == jj, 1.0, 0.0).astype(F32)
    t_mat = [eye for _ in n]
    b = 1
    while b < group:
        lower = jnp.logical_and((ii // (2 * b)) == (jj // (2 * b)),
                                jnp.logical_and((ii % (2 * b)) >= b, (jj % (2 * b)) < b))
        m = [jnp.where(lower, a_mat[i], 0.0) for i in n]
        if b == 1:
            t_mat = [eye - m[i] for i in n]
        else:
            tm = [_dot(t_mat[i], m[i]) for i in n]
            yield
            t_mat = [t_mat[i] - _dot(tm[i], t_mat[i]) for i in n]
            yield
        b *= 2
    rhs = [jnp.concatenate([v[i] * beta_u[i], kb[i] * eg_u[i]], axis=-1) for i in n]
    sol = [_dot(t_mat[i], rhs[i]) for i in n]
    yield
    nest = lambda xs: [xs[ci * GDN_HEADS:(ci + 1) * GDN_HEADS] for ci in range(n_chunks)]
    out.update(u=nest([s[:, :GDN_DV] for s in sol]), w=nest([s[:, GDN_DV:] for s in sol]), qk=nest(qk),
               qd=nest([q[i] * eg_u[i] for i in n]), kd=nest([k[i] * ekd_u[i] for i in n]))


def _l2norm(t, ones_ref):
    return t * lax.rsqrt(_rowsum_bcast(t * t, ones_ref) + 1e-6)


def _gdn_out(o, z, gnorm_row, ones_ref):
    ms = _rowsum_bcast(o * o, ones_ref) * (1.0 / GDN_DV)
    return (o * lax.rsqrt(ms + NORM_EPS)) * gnorm_row * _silu(z)


def _conv_silu(xbuf_ref, cw_ref, c):
    y = xbuf_ref[pl.ds(5, c), :] * cw_ref[0:1, :]
    for j in range(1, CONV_WIDTH):
        y = y + xbuf_ref[pl.ds(5 + j, c), :] * cw_ref[j:j + 1, :]
    return _silu(y)


def _gdn_prompt_kernel(qkv_ref, z_ref, ba_ref, cw_ref, alog_ref, dtb_ref, gnorm_ref, ones_ref, sel_ref,
                       o_ref, sfin_ref, s_ref, xbuf_ref, ya_ref, ga_ref, gta_ref, yb_ref, gb_ref, gtb_ref):
    c = GDN_CHUNK
    n_chunks = GDN_STEP_CHUNKS
    r = n_chunks * c
    hs = range(GDN_HEADS)

    @pl.when(pl.program_id(1) == 0)
    def _():
        s_ref[...] = jnp.zeros_like(s_ref)
        xbuf_ref[0:8, :] = jnp.zeros((8, CONV_DIM), F32)
        ya_ref[...] = jnp.zeros_like(ya_ref)
        ga_ref[...] = jnp.zeros_like(ga_ref)
        gta_ref[...] = jnp.zeros_like(gta_ref)

    @pl.when(pl.program_id(1) >= 0)
    def _():
        yb_ref[...] = ya_ref[...]
        gb_ref[...] = ga_ref[...]
        gtb_ref[...] = gta_ref[...]

    def stage_a():
        xbuf_ref[8:8 + r, :] = qkv_ref[...]
        beta, g = _gates(ba_ref[...], alog_ref[...], dtb_ref[...])
        ii, jj = _iota2(r)
        tril = jnp.where(jnp.logical_and((ii // c) == (jj // c), ii >= jj), 1.0, 0.0)
        gc = _cumsum_rows(tril, g)
        gta_ref[...] = gc.T
        lane = lax.broadcasted_iota(jnp.int32, (r, BA_DIM), 1)
        ga_ref[...] = jnp.where(lane < GDN_HEADS, beta, gc)
        yield
        for ci in range(n_chunks):
            rows = slice(ci * c, (ci + 1) * c)
            for slab in range(CONV_DIM // LANES):
                cols = slice(slab * LANES, (slab + 1) * LANES)
                yc = xbuf_ref[pl.ds(5 + ci * c, c), cols] * cw_ref[0:1, cols]
                for j in range(1, CONV_WIDTH):
                    yc = yc + xbuf_ref[pl.ds(5 + j + ci * c, c), cols] * cw_ref[j:j + 1, cols]
                ya_ref[rows, cols] = _silu(yc)
                if slab < GDN_KEY_DIM // LANES:
                    ya_ref[rows, cols] = _l2norm(ya_ref[rows, cols], ones_ref) * (GDN_DK ** -0.5)
                elif slab < 2 * GDN_KEY_DIM // LANES:
                    ya_ref[rows, cols] = _l2norm(ya_ref[rows, cols], ones_ref)
                yield
        xbuf_ref[0:8, :] = xbuf_ref[r:r + 8, :]
        yield

    def stage_b():
        bcast = _lane_bcast(gb_ref[...], sel_ref, range(2 * GDN_HEADS))
        beta_b, gc_b = bcast[:GDN_HEADS], bcast[GDN_HEADS:]
        glast = [[gc_b[h][(ci + 1) * c - 1:(ci + 1) * c, :] for h in hs] for ci in range(n_chunks)]
        gl_b = [jnp.concatenate([jnp.broadcast_to(glast[ci][h], (c, GDN_DK)) for ci in range(n_chunks)],
                                axis=0) for h in hs]
        wy = {}
        yield from _gdn_prep_steps(wy, yb_ref, beta_b, gc_b, gl_b, gtb_ref[...], c, c, ones_ref,
                                   normalized=True)
        u, w, qk, qd, kd = (wy[name] for name in ("u", "w", "qk", "qd", "kd"))
        s_cur = [s_ref[h] for h in hs]
        for ci in range(n_chunks):
            ws = [_dot(jnp.concatenate([w[ci][h], qd[ci][h]], axis=0), s_cur[h]) for h in hs]
            yield
            v_new = [u[ci][h] - ws[h][:c] for h in hs]
            o = [ws[h][c:] + _dot(qk[ci][h], v_new[h]) for h in hs]
            s_cur = [s_cur[h] * jnp.exp(glast[ci][h]) + _dot_tn(kd[ci][h], v_new[h]) for h in hs]
            yield
            for h in hs:
                z = z_ref[ci * c:(ci + 1) * c, h * GDN_DV:(h + 1) * GDN_DV]
                o_ref[ci * c:(ci + 1) * c, h * GDN_DV:(h + 1) * GDN_DV] = _gdn_out(o[h], z, gnorm_ref[...], ones_ref)
        for h in hs:
            s_ref[h] = s_cur[h]
            sfin_ref[0, h] = s_cur[h]

    a_pieces = 2 + n_chunks * (CONV_DIM // LANES)
    b_levels = 2 * (int(math.log2(c)) - 1) + 2 + 2 * n_chunks
    a_steps = stage_a()
    done = 0
    for i, _ in enumerate(stage_b()):
        target = -(-(i + 1) * a_pieces // b_levels)
        for _ in range(target - done):
            next(a_steps, None)
        done = target
    for _ in a_steps:
        pass


def _gdn_prompt_call(qkvz, ba, lw, batch, seq):
    c = GDN_CHUNK * GDN_STEP_CHUNKS
    nc = seq // c
    z_col = CONV_DIM // GDN_VAL_DIM
    return pl.pallas_call(
        _gdn_prompt_kernel,
        grid=(batch, nc + 1),
        in_specs=[pl.BlockSpec((c, CONV_DIM), lambda b, s: (b * nc + jnp.minimum(s, nc - 1), 0)),
                  pl.BlockSpec((c, GDN_VAL_DIM), lambda b, s: (b * nc + jnp.maximum(s - 1, 0), z_col)),
                  pl.BlockSpec((c, BA_DIM), lambda b, s: (b * nc + jnp.minimum(s, nc - 1), 0)),
                  _resident((CONV_WIDTH, CONV_DIM)), _resident((1, BA_DIM)), _resident((1, BA_DIM)),
                  _resident((1, GDN_DV)), _resident(lw["ones"].shape), _resident(lw["sel"].shape)],
        out_specs=[pl.BlockSpec((c, GDN_VAL_DIM), lambda b, s: (b * nc + jnp.maximum(s - 1, 0), 0)),
                   pl.BlockSpec((1, GDN_HEADS, GDN_DK, GDN_DV), lambda b, s: (b, 0, 0, 0))],
        out_shape=[jax.ShapeDtypeStruct((batch * seq, GDN_VAL_DIM), F32),
                   jax.ShapeDtypeStruct((batch, GDN_HEADS, GDN_DK, GDN_DV), F32)],
        scratch_shapes=[pltpu.VMEM((GDN_HEADS, GDN_DK, GDN_DV), F32),
                        pltpu.VMEM((c + 8, CONV_DIM), F32),
                        pltpu.VMEM((c, CONV_DIM), F32), pltpu.VMEM((c, BA_DIM), F32), pltpu.VMEM((BA_DIM, c), F32),
                        pltpu.VMEM((c, CONV_DIM), F32), pltpu.VMEM((c, BA_DIM), F32), pltpu.VMEM((BA_DIM, c), F32)],
        compiler_params=pltpu.CompilerParams(dimension_semantics=("arbitrary", "arbitrary"),
                                             vmem_limit_bytes=VMEM_LIMIT),
        name="gdn_prompt",
    )(qkvz, qkvz, ba, lw["conv_w"], lw["alog_row"], lw["dtb_row"], lw["gnorm"], lw["ones"], lw["sel"])


def _gdn_decode_kernel(qkvz_ref, ba_ref, sconv_ref, s0_ref, cw_ref, alog_ref, dtb_ref, gnorm_ref,
                       ones_ref, sel_ref, o_ref, snew_ref, xbuf_ref, *, dec_seq):
    nb = DEC_SEQ_BLOCK
    t = dec_seq
    c = nb * t
    ys = []
    for s in range(nb):
        xbuf_ref[s, 5:8, :] = sconv_ref[s]
        xbuf_ref[s, 8:8 + t, :] = qkvz_ref[s * t:(s + 1) * t, :CONV_DIM]
        ys.append(_conv_silu(xbuf_ref.at[s], cw_ref, t))
    y = jnp.concatenate(ys, axis=0)

    beta, g = _gates(ba_ref[...], alog_ref[...], dtb_ref[...])
    ii, jj = _iota2(c)
    same = (ii // t) == (jj // t)
    tril = jnp.where(jnp.logical_and(same, ii >= jj), 1.0, 0.0).astype(F32)
    ones = jnp.where(same, 1.0, 0.0).astype(F32)
    gc = _cumsum_rows(tril, g)
    gl = _cumsum_rows(ones, g)
    gc_t = gc.T
    lane = lax.broadcasted_iota(jnp.int32, (c, BA_DIM), 1)
    bcast = _lane_bcast(jnp.where(lane < GDN_HEADS, beta, gc), sel_ref, range(2 * GDN_HEADS))
    beta_b, gc_b = bcast[:GDN_HEADS], bcast[GDN_HEADS:]
    gl_b = _lane_bcast(gl, sel_ref, range(GDN_HEADS, 2 * GDN_HEADS))
    wy = {}
    for _ in _gdn_prep_steps(wy, y, beta_b, gc_b, gl_b, gc_t, c, t, ones_ref, normalized=False):
        pass
    u, w, qk, qd, kd = (wy[name][0] for name in ("u", "w", "qk", "qd", "kd"))
    hs = range(GDN_HEADS)
    units = [(s, h) for h in hs for s in range(nb)]
    s_old = {(s, h): s0_ref[s, h] for s, h in units}
    ws = {(s, h): _dot(jnp.concatenate([w[h][s * t:(s + 1) * t], qd[h][s * t:(s + 1) * t]], axis=0),
                       s_old[s, h]) for s, h in units}
    v_new = [jnp.concatenate([u[h][s * t:(s + 1) * t] - ws[s, h][:t] for s in range(nb)], axis=0) for h in hs]
    o = [jnp.concatenate([ws[s, h][t:] for s in range(nb)], axis=0) + _dot(qk[h], v_new[h]) for h in hs]
    cd = [jnp.exp(gl_b[h]) for h in hs]
    kd_t = [kd[h].T for h in hs]
    col_seq = lax.broadcasted_iota(jnp.int32, (GDN_DK, c), 1) // t
    for s, h in units:
        upd = _dot(jnp.where(col_seq == s, kd_t[h], 0.0), v_new[h])
        snew_ref[s, h] = s_old[s, h] * cd[h][s * t:s * t + 1, :] + upd
    for h in hs:
        z = qkvz_ref[:, CONV_DIM + h * GDN_DV:CONV_DIM + (h + 1) * GDN_DV]
        o_ref[:, h * GDN_DV:(h + 1) * GDN_DV] = _gdn_out(o[h], z, gnorm_ref[...], ones_ref)


def _gdn_decode_call(qkvz, ba, state_conv, state_gdn, lw, dec_batch, dec_seq):
    nb = DEC_SEQ_BLOCK
    rows = nb * dec_seq
    return pl.pallas_call(
        functools.partial(_gdn_decode_kernel, dec_seq=dec_seq),
        grid=(dec_batch // nb,),
        in_specs=[pl.BlockSpec((rows, QKVZ_DIM), lambda i: (i, 0)),
                  pl.BlockSpec((rows, BA_DIM), lambda i: (i, 0)),
                  pl.BlockSpec((nb, CONV_WIDTH - 1, CONV_DIM), lambda i: (i, 0, 0)),
                  pl.BlockSpec((nb, GDN_HEADS, GDN_DK, GDN_DV), lambda i: (i, 0, 0, 0)),
                  _resident((CONV_WIDTH, CONV_DIM)), _resident((1, BA_DIM)), _resident((1, BA_DIM)),
                  _resident((1, GDN_DV)), _resident(lw["ones"].shape), _resident(lw["sel"].shape)],
        out_specs=[pl.BlockSpec((rows, GDN_VAL_DIM), lambda i: (i, 0)),
                   pl.BlockSpec((nb, GDN_HEADS, GDN_DK, GDN_DV), lambda i: (i, 0, 0, 0))],
        out_shape=[jax.ShapeDtypeStruct((dec_batch * dec_seq, GDN_VAL_DIM), F32),
                   jax.ShapeDtypeStruct((dec_batch, GDN_HEADS, GDN_DK, GDN_DV), F32)],
        scratch_shapes=[pltpu.VMEM((nb, 16, CONV_DIM), F32)],
        compiler_params=pltpu.CompilerParams(dimension_semantics=("arbitrary",),
                                             vmem_limit_bytes=VMEM_LIMIT),
        name="gdn_decode",
    )(qkvz, ba, state_conv, state_gdn, lw["conv_w"], lw["alog_row"], lw["dtb_row"], lw["gnorm"],
      lw["ones"], lw["sel"])


def _bias_table_kernel(bucket_ref, rb_ref, out_ref):
    bucket = bucket_ref[...]
    for h in range(SWA_HEADS):
        acc = jnp.zeros(bucket.shape, F32)
        for b in range(NUM_BUCKETS):
            acc = jnp.where(bucket == b, rb_ref[b, h], acc)
        out_ref[h] = jnp.where(bucket < 0, NEG_INF, acc)


def _bias_table_call(bucket, rel_bias):
    r, c = bucket.shape
    return pl.pallas_call(
        _bias_table_kernel,
        in_specs=[pl.BlockSpec(memory_space=pltpu.VMEM), pl.BlockSpec(memory_space=pltpu.SMEM)],
        out_specs=pl.BlockSpec(memory_space=pltpu.VMEM),
        out_shape=jax.ShapeDtypeStruct((SWA_HEADS, r, c), F32),
        name="t5_bias_table",
    )(bucket, rel_bias)


def _t5_bucket(dist, valid):
    d = jnp.maximum(dist, 0)
    exact = NUM_BUCKETS // 2
    log_ratio = jnp.log(jnp.maximum(d, 1).astype(F32) / exact) / math.log(REL_MAX_DISTANCE / exact)
    large = jnp.minimum(exact + (log_ratio * (NUM_BUCKETS - exact)).astype(jnp.int32), NUM_BUCKETS - 1)
    bucket = jnp.where(d < exact, d, large)
    return jnp.where(valid, bucket, -1).astype(jnp.int32)


def _swa_prompt_kernel(q_ref, kvp_ref, kvc_ref, bias_ref, sink_ref, o_ref):
    w, dh, nq = SWA_WINDOW, SWA_HEAD_DIM, SWA_STEP_BLOCKS
    first = pl.program_id(1) == 0
    col = lax.broadcasted_iota(jnp.int32, (w, 2 * w), 1)
    drop_prev = jnp.logical_and(first, col < w)
    kv = jnp.concatenate([kvp_ref[...], kvc_ref[...]], axis=0).astype(BF16)
    k_h = [kv[:, h * dh:(h + 1) * dh] for h in range(SWA_KV_HEADS)]
    v_h = [kv[:, SWA_KV_DIM + h * dh:SWA_KV_DIM + (h + 1) * dh] for h in range(SWA_KV_HEADS)]
    units = [(b, hg) for b in range(nq) for hg in range(SWA_HEADS)]
    n = range(len(units))
    q = [(q_ref[b * w:(b + 1) * w, hg * dh:(hg + 1) * dh] * (dh ** -0.5)).astype(BF16) for b, hg in units]
    s = [lax.dot_general(q[i], k_h[hg // SWA_GROUP][b * w:(b + 2) * w], (((1,), (1,)), ((), ())),
                         preferred_element_type=F32) + bias_ref[hg] for i, (b, hg) in enumerate(units)]
    s = [jnp.where(drop_prev, NEG_INF, s[i]) if b == 0 else s[i] for i, (b, hg) in enumerate(units)]
    sink = [sink_ref[hg] for b, hg in units]
    m = [jnp.maximum(jnp.max(s[i], axis=-1, keepdims=True), sink[i]) for i in n]
    p = [jnp.exp(s[i] - m[i]) for i in n]
    denom = [jnp.sum(p[i], axis=-1, keepdims=True) + jnp.exp(sink[i] - m[i]) for i in n]
    o = [jnp.dot(p[i].astype(BF16), v_h[hg // SWA_GROUP][b * w:(b + 2) * w], preferred_element_type=F32)
         / denom[i] for i, (b, hg) in enumerate(units)]
    for i, (b, hg) in enumerate(units):
        o_ref[b * w:(b + 1) * w, hg * dh:(hg + 1) * dh] = o[i]


def _swa_prompt_call(qs, kv, bias, sinks, batch, seq):
    w, nq = SWA_WINDOW, SWA_STEP_BLOCKS
    nb = seq // (w * nq)
    return pl.pallas_call(
        _swa_prompt_kernel,
        grid=(batch, nb),
        in_specs=[pl.BlockSpec((nq * w, SWA_Q_DIM), lambda b, n: (b * nb + n, 0)),
                  pl.BlockSpec((w, 2 * SWA_KV_DIM),
                               lambda b, n: ((b * nb + n) * nq - jnp.minimum(n, 1), 0)),
                  pl.BlockSpec((nq * w, 2 * SWA_KV_DIM), lambda b, n: (b * nb + n, 0)),
                  _resident((SWA_HEADS, w, 2 * w)),
                  pl.BlockSpec(memory_space=pltpu.SMEM)],
        out_specs=pl.BlockSpec((nq * w, SWA_Q_DIM), lambda b, n: (b * nb + n, 0)),
        out_shape=jax.ShapeDtypeStruct((batch * seq, SWA_Q_DIM), F32),
        compiler_params=pltpu.CompilerParams(dimension_semantics=("arbitrary", "arbitrary"),
                                             vmem_limit_bytes=VMEM_LIMIT),
        name="swa_prompt",
    )(qs, kv, kv, bias, sinks)


def _swa_decode_kernel(q_ref, kvn_ref, kbuf_ref, vbuf_ref, bbuf_ref, bnew_ref, sink_ref, o_ref,
                       *, dec_seq):
    nb = DEC_SEQ_BLOCK
    t = dec_seq
    rows = nb * t
    m_rows = SWA_GROUP * rows
    wb = kbuf_ref.shape[1]
    ri = lax.broadcasted_iota(jnp.int32, (m_rows, 1), 0)
    row_seq = (ri % rows) // t
    row_grp = ri // rows
    for h in range(SWA_KV_HEADS):
        lo, hi = h * SWA_HEAD_DIM, (h + 1) * SWA_HEAD_DIM
        q = jnp.concatenate(
            [q_ref[:, (h * SWA_GROUP + g) * SWA_HEAD_DIM:(h * SWA_GROUP + g + 1) * SWA_HEAD_DIM]
             for g in range(SWA_GROUP)], axis=0)
        sink = jnp.zeros((m_rows, 1), F32)
        for g in range(SWA_GROUP):
            sink = jnp.where(row_grp == g, sink_ref[h * SWA_GROUP + g], sink)
        k_new = jnp.concatenate([kvn_ref[:, lo:hi], jnp.zeros((m_rows - rows, SWA_HEAD_DIM), F32)], axis=0)
        v_new = jnp.concatenate([kvn_ref[:, SWA_KV_DIM + lo:SWA_KV_DIM + hi],
                                 jnp.zeros((m_rows - rows, SWA_HEAD_DIM), F32)], axis=0)
        s_new = _dot_nt(q, k_new) * (SWA_HEAD_DIM ** -0.5) + bnew_ref[h]
        s_buf = jnp.zeros((m_rows, wb), F32)
        for s in range(nb):
            s_full = _dot_nt(q, kbuf_ref[s, :, lo:hi])
            s_buf = jnp.where(row_seq == s, s_full, s_buf)
        s_buf = s_buf * (SWA_HEAD_DIM ** -0.5) + bbuf_ref[h]
        m = jnp.maximum(jnp.maximum(jnp.max(s_buf, axis=-1, keepdims=True),
                                    jnp.max(s_new, axis=-1, keepdims=True)), sink)
        p_buf = jnp.exp(s_buf - m)
        p_new = jnp.exp(s_new - m)
        denom = (jnp.sum(p_buf, axis=-1, keepdims=True) + jnp.sum(p_new, axis=-1, keepdims=True)
                 + jnp.exp(sink - m))
        o = _dot(p_new, v_new)
        for s in range(nb):
            o = o + _dot(jnp.where(row_seq == s, p_buf, 0.0), vbuf_ref[s, :, lo:hi])
        o = o / denom
        for g in range(SWA_GROUP):
            hg = h * SWA_GROUP + g
            o_ref[:, hg * SWA_HEAD_DIM:(hg + 1) * SWA_HEAD_DIM] = o[g * rows:(g + 1) * rows]


def _swa_decode_call(qs, kvn, kbuf, vbuf, bias_buf, bias_new, sinks, dec_batch, dec_seq):
    nb = DEC_SEQ_BLOCK
    rows = nb * dec_seq
    wb = kbuf.shape[1]
    m_rows = SWA_GROUP * rows
    return pl.pallas_call(
        functools.partial(_swa_decode_kernel, dec_seq=dec_seq),
        grid=(dec_batch // nb,),
        in_specs=[pl.BlockSpec((rows, SWA_Q_DIM), lambda i: (i, 0)),
                  pl.BlockSpec((rows, 2 * SWA_KV_DIM), lambda i: (i, 0)),
                  pl.BlockSpec((nb, wb, SWA_KV_DIM), lambda i: (i, 0, 0)),
                  pl.BlockSpec((nb, wb, SWA_KV_DIM), lambda i: (i, 0, 0)),
                  _resident((SWA_KV_HEADS, m_rows, wb)),
                  _resident((SWA_KV_HEADS, m_rows, m_rows)),
                  pl.BlockSpec(memory_space=pltpu.SMEM)],
        out_specs=pl.BlockSpec((rows, SWA_Q_DIM), lambda i: (i, 0)),
        out_shape=jax.ShapeDtypeStruct((dec_batch * dec_seq, SWA_Q_DIM), F32),
        compiler_params=pltpu.CompilerParams(dimension_semantics=("arbitrary",),
                                             vmem_limit_bytes=VMEM_LIMIT),
        name="swa_decode",
    )(qs, kvn, kbuf, vbuf, bias_buf, bias_new, sinks)


def _prompt_bias(rel_bias):
    w = SWA_WINDOW
    dist = w + jnp.arange(w)[:, None] - jnp.arange(2 * w)[None, :]
    valid = (dist >= 0) & (dist < w)
    return _bias_table_call(_t5_bucket(dist, valid), rel_bias)


def _decode_bias(rel_bias, wb, dec_seq):
    nb, t = DEC_SEQ_BLOCK, dec_seq
    rows = nb * t
    m_rows = SWA_GROUP * rows
    r = jnp.arange(m_rows)
    r_tok, r_seq = r % t, (r % rows) // t
    dist_b = wb + r_tok[:, None] - jnp.arange(wb)[None, :]
    tab_b = _bias_table_call(_t5_bucket(dist_b, (dist_b >= 0) & (dist_b < SWA_WINDOW)), rel_bias)
    cidx = jnp.arange(m_rows)
    dist_n = r_tok[:, None] - (cidx % t)[None, :]
    ok_n = (dist_n >= 0) & (dist_n < SWA_WINDOW) & (r_seq[:, None] == (cidx // t)[None, :]) & (cidx < rows)[None, :]
    tab_n = _bias_table_call(_t5_bucket(dist_n, ok_n), rel_bias)

    def pick(tab):
        return jnp.stack([jnp.concatenate([tab[h * SWA_GROUP + g, g * rows:(g + 1) * rows]
                                           for g in range(SWA_GROUP)], axis=0)
                          for h in range(SWA_KV_HEADS)])
    return pick(tab_b), pick(tab_n)


def _pack_layer(i, norm_ffn1_pre, norm_ffn1_post, ffn1_w_gate, ffn1_w_up, ffn1_w_down, norm_mix_pre,
                norm_mix_post, w_in, conv_w, gdn_a_log, gdn_dt_bias, gdn_norm, swa_sinks, w_out,
                norm_ffn2_pre, norm_ffn2_post, ffn2_w_gate, ffn2_w_up, ffn2_w_down, ple_gate, ple_proj,
                norm_ple_post):
    row = lambda g: g[i].reshape(1, -1).astype(F32)
    win = w_in[i]
    n_gdn = QKVZ_DIM + 2 * GDN_HEADS
    win = jnp.concatenate([win[:, :n_gdn], jnp.zeros((D_MODEL, BA_DIM - 2 * GDN_HEADS), win.dtype),
                           win[:, n_gdn:]], axis=1)
    lane_pad = lambda v: jnp.zeros((1, BA_DIM), F32).at[0, GDN_HEADS:2 * GDN_HEADS].set(v[i].astype(F32))
    kk = np.arange(2 * LANES)[:, None] % LANES
    sel = (kk == (np.arange(2 * GDN_HEADS * LANES)[None, :] // LANES)).astype(np.float32)
    return dict(
        ones=jnp.ones((LANES, LANES), F32), sel=jnp.asarray(sel, BF16),
        g1pre=row(norm_ffn1_pre), g1post=row(norm_ffn1_post),
        wg1=ffn1_w_gate[i].astype(BF16), wu1=ffn1_w_up[i].astype(BF16), wd1=ffn1_w_down[i].astype(BF16),
        gmix=row(norm_mix_pre), gmixpost=row(norm_mix_post), win=win.astype(BF16),
        conv_w=conv_w[i].astype(F32), alog_row=lane_pad(gdn_a_log), dtb_row=lane_pad(gdn_dt_bias),
        gnorm=row(gdn_norm), sinks=swa_sinks[i].astype(F32), wo=w_out[i].astype(BF16),
        g2pre=row(norm_ffn2_pre), g2post=row(norm_ffn2_post),
        wg2=ffn2_w_gate[i].astype(BF16), wu2=ffn2_w_up[i].astype(BF16), wd2=ffn2_w_down[i].astype(BF16),
        wpg=ple_gate[i].astype(BF16), wpp=ple_proj[i].astype(BF16), gple=row(norm_ple_post))


def kernel(x_prompt, x_sample, state_conv, state_gdn, cache_swa_k, cache_swa_v, p_prompt, p_sample,
           rel_bias, norm_ffn1_pre, norm_ffn1_post, ffn1_w_gate, ffn1_w_up, ffn1_w_down,
           norm_mix_pre, norm_mix_post, w_in, conv_w, gdn_a_log, gdn_dt_bias, gdn_norm, swa_sinks,
           w_out, norm_ffn2_pre, norm_ffn2_post, ffn2_w_gate, ffn2_w_up, ffn2_w_down,
           ple_gate, ple_proj, norm_ple_post):
    depth = w_in.shape[0]
    batch, seq, _ = x_prompt.shape
    dec_batch, dec_seq, _ = x_sample.shape
    wb = cache_swa_k.shape[2]
    wp = min(SWA_WINDOW, seq)
    rel_bias = rel_bias.astype(F32)
    bias_p = _prompt_bias(rel_bias)
    bias_db, bias_dn = _decode_bias(rel_bias, wb, dec_seq)

    yp = x_prompt.reshape(batch * seq, D_MODEL)
    ys = x_sample.reshape(dec_batch * dec_seq, D_MODEL)
    outs = [[] for _ in range(8)]
    for i in range(depth):
        lw = _pack_layer(i, norm_ffn1_pre, norm_ffn1_post, ffn1_w_gate, ffn1_w_up, ffn1_w_down,
                         norm_mix_pre, norm_mix_post, w_in, conv_w, gdn_a_log, gdn_dt_bias, gdn_norm,
                         swa_sinks, w_out, norm_ffn2_pre, norm_ffn2_post, ffn2_w_gate, ffn2_w_up,
                         ffn2_w_down, ple_gate, ple_proj, norm_ple_post)
        x1, qkvz, ba, qs, kv = _head_call(yp, lw)
        gdn_o, s_fin = _gdn_prompt_call(qkvz, ba, lw, batch, seq)
        swa_o = _swa_prompt_call(qs, kv, bias_p, lw["sinks"], batch, seq)
        yp = _tail_call(x1, gdn_o, swa_o, p_prompt[i].reshape(batch * seq, PLE_DIM), lw)
        kv3 = kv.reshape(batch, seq, 2 * SWA_KV_DIM)
        outs[0].append(qkvz.reshape(batch, seq, QKVZ_DIM)[:, seq - (CONV_WIDTH - 1):, :CONV_DIM])
        outs[1].append(s_fin)
        outs[2].append(kv3[:, seq - wp:, :SWA_KV_DIM].reshape(batch, wp, SWA_KV_HEADS, SWA_HEAD_DIM))
        outs[3].append(kv3[:, seq - wp:, SWA_KV_DIM:].reshape(batch, wp, SWA_KV_HEADS, SWA_HEAD_DIM))
        x1, qkvz, ba, qs, kv = _head_call(ys, lw)
        gdn_o, s_new = _gdn_decode_call(qkvz, ba, state_conv[i], state_gdn[i], lw, dec_batch, dec_seq)
        kbuf = cache_swa_k[i].reshape(dec_batch, wb, SWA_KV_DIM)
        vbuf = cache_swa_v[i].reshape(dec_batch, wb, SWA_KV_DIM)
        swa_o = _swa_decode_call(qs, kv, kbuf, vbuf, bias_db, bias_dn, lw["sinks"], dec_batch, dec_seq)
        ys = _tail_call(x1, gdn_o, swa_o, p_sample[i].reshape(dec_batch * dec_seq, PLE_DIM), lw)
        kv3 = kv.reshape(dec_batch, dec_seq, 2 * SWA_KV_DIM)
        xp = jnp.concatenate([state_conv[i], qkvz.reshape(dec_batch, dec_seq, QKVZ_DIM)[:, :, :CONV_DIM]], axis=1)
        outs[4].append(xp[:, dec_seq:])
        outs[5].append(s_new)
        outs[6].append(jnp.concatenate([kbuf, kv3[:, :, :SWA_KV_DIM]], axis=1)[:, dec_seq:]
                       .reshape(dec_batch, wb, SWA_KV_HEADS, SWA_HEAD_DIM))
        outs[7].append(jnp.concatenate([vbuf, kv3[:, :, SWA_KV_DIM:]], axis=1)[:, dec_seq:]
                       .reshape(dec_batch, wb, SWA_KV_HEADS, SWA_HEAD_DIM))
    return (yp.reshape(batch, seq, D_MODEL), ys.reshape(dec_batch, dec_seq, D_MODEL),
            *[jnp.stack(o) for o in outs])
```

```python
import functools
import math

import numpy as np
import jax
import jax.numpy as jnp
from jax import lax
from jax.experimental import pallas as pl
from jax.experimental.pallas import tpu as pltpu

F32 = jnp.float32
BF16 = jnp.bfloat16

D_MODEL = 1024
NORM_EPS = 1e-6
PLE_DIM = 256
FFN_DIM = 2816
GDN_HEADS = 4
GDN_DK = 128
GDN_DV = 128
GDN_KEY_DIM = GDN_HEADS * GDN_DK
GDN_VAL_DIM = GDN_HEADS * GDN_DV
CONV_DIM = 2 * GDN_KEY_DIM + GDN_VAL_DIM
CONV_WIDTH = 4
SWA_HEADS = 8
SWA_KV_HEADS = 2
SWA_GROUP = SWA_HEADS // SWA_KV_HEADS
SWA_HEAD_DIM = 64
SWA_Q_DIM = SWA_HEADS * SWA_HEAD_DIM
SWA_KV_DIM = SWA_KV_HEADS * SWA_HEAD_DIM
SWA_WINDOW = 128
NUM_BUCKETS = 32
REL_MAX_DISTANCE = 128

QKVZ_DIM = CONV_DIM + GDN_VAL_DIM
BA_DIM = 128
PROJ_PACKED = QKVZ_DIM + BA_DIM + SWA_Q_DIM + 2 * SWA_KV_DIM

GDN_CHUNK = 64
GDN_STEP_CHUNKS = 4
SWA_STEP_BLOCKS = 4
ROW_TILE = 512
DEC_SEQ_BLOCK = 8
VMEM_LIMIT = 56 * 1024 * 1024

NEG_INF = float("-inf")
LANES = 128


def _resident(shape):
    nd = len(shape)
    return pl.BlockSpec(shape, lambda *_: (0,) * nd, pipeline_mode=pl.Buffered(1))


def _rms(x, gain):
    ms = jnp.mean(x * x, axis=-1, keepdims=True)
    return (x * lax.rsqrt(ms + NORM_EPS)) * gain


def _sigmoid(x):
    return 1.0 / (1.0 + jnp.exp(-x))


def _silu(x):
    h = 0.5 * x
    return h + h * jnp.tanh(h)


def _dot(a, b):
    return jnp.dot(a, b, preferred_element_type=F32)


def _dot_nt(a, b):
    return lax.dot_general(a, b, (((1,), (1,)), ((), ())), preferred_element_type=F32)


def _dot_tn(a, b):
    return lax.dot_general(a, b, (((0,), (0,)), ((), ())), preferred_element_type=F32)


def _split(a):
    hi = a.astype(BF16)
    lo = (a - hi.astype(F32)).astype(BF16)
    return hi, lo


def _inv_rms(x):
    return lax.rsqrt(jnp.mean(x * x, axis=-1, keepdims=True) + NORM_EPS)


def _prenorm_dots(x, gain, w_refs):
    h = (x * gain).astype(BF16)
    inv = _inv_rms(x)
    return [jnp.dot(h, w, preferred_element_type=F32) * inv for w in w_refs]


def _swiglu_block(x, gain, wg_ref, wu_ref, wd_ref):
    g, u = _prenorm_dots(x, gain, [wg_ref[...], wu_ref[...]])
    a = (_silu(g) * u).astype(BF16)
    return jnp.dot(a, wd_ref[...], preferred_element_type=F32)


def _head_kernel(x_ref, g1pre_ref, g1post_ref, wg_ref, wu_ref, wd_ref, gmix_ref, win_ref,
                 x1_ref, qkvz_ref, ba_ref, qs_ref, kv_ref):
    x = x_ref[...]
    y = _swiglu_block(x, g1pre_ref[...], wg_ref, wu_ref, wd_ref)
    x1 = x + 0.5 * _rms(y, g1post_ref[...])
    x1_ref[...] = x1
    c0, c1, c2 = QKVZ_DIM, QKVZ_DIM + BA_DIM, QKVZ_DIM + BA_DIM + SWA_Q_DIM
    qkvz_ref[...], ba_ref[...], qs_ref[...], kv_ref[...] = _prenorm_dots(
        x1, gmix_ref[...], [win_ref[:, :c0], win_ref[:, c0:c1], win_ref[:, c1:c2], win_ref[:, c2:]])


def _head_call(x, lw):
    n = x.shape[0]
    tm = min(ROW_TILE, n)
    row = lambda w: pl.BlockSpec((tm, w), lambda i: (i, 0))
    return pl.pallas_call(
        _head_kernel,
        grid=(n // tm,),
        in_specs=[row(D_MODEL), _resident((1, D_MODEL)), _resident((1, D_MODEL)),
                  _resident((D_MODEL, FFN_DIM)), _resident((D_MODEL, FFN_DIM)),
                  _resident((FFN_DIM, D_MODEL)), _resident((1, D_MODEL)),
                  _resident((D_MODEL, PROJ_PACKED))],
        out_specs=[row(D_MODEL), row(QKVZ_DIM), row(BA_DIM), row(SWA_Q_DIM), row(2 * SWA_KV_DIM)],
        out_shape=[jax.ShapeDtypeStruct((n, D_MODEL), F32), jax.ShapeDtypeStruct((n, QKVZ_DIM), F32),
                   jax.ShapeDtypeStruct((n, BA_DIM), F32), jax.ShapeDtypeStruct((n, SWA_Q_DIM), F32),
                   jax.ShapeDtypeStruct((n, 2 * SWA_KV_DIM), F32)],
        compiler_params=pltpu.CompilerParams(dimension_semantics=("arbitrary",),
                                             vmem_limit_bytes=VMEM_LIMIT),
        name="ffn1_inproj",
    )(x, lw["g1pre"], lw["g1post"], lw["wg1"], lw["wu1"], lw["wd1"], lw["gmix"], lw["win"])


def _tail_kernel(x_ref, gdn_ref, swa_ref, p_ref, wo_ref, gmixpost_ref, g2pre_ref, g2post_ref,
                 wg_ref, wu_ref, wd_ref, wpg_ref, wpp_ref, gple_ref, y_ref):
    x = x_ref[...]
    mix = (jnp.dot(gdn_ref[...].astype(BF16), wo_ref[:GDN_VAL_DIM, :], preferred_element_type=F32)
           + jnp.dot(swa_ref[...].astype(BF16), wo_ref[GDN_VAL_DIM:, :], preferred_element_type=F32))
    pp = jnp.dot(p_ref[...].astype(BF16), wpp_ref[...], preferred_element_type=F32)
    x = x + _rms(mix, gmixpost_ref[...])
    y = _swiglu_block(x, g2pre_ref[...], wg_ref, wu_ref, wd_ref)
    x = x + 0.5 * _rms(y, g2post_ref[...])
    gate = _sigmoid(jnp.dot(x.astype(BF16), wpg_ref[...], preferred_element_type=F32))
    y_ref[...] = x + _rms(gate * pp, gple_ref[...])


def _tail_call(x1, gdn_o, swa_o, p, lw):
    n = x1.shape[0]
    tm = min(ROW_TILE, n)
    row = lambda w: pl.BlockSpec((tm, w), lambda i: (i, 0))
    return pl.pallas_call(
        _tail_kernel,
        grid=(n // tm,),
        in_specs=[row(D_MODEL), row(GDN_VAL_DIM), row(SWA_Q_DIM), row(PLE_DIM),
                  _resident((GDN_VAL_DIM + SWA_Q_DIM, D_MODEL)), _resident((1, D_MODEL)),
                  _resident((1, D_MODEL)), _resident((1, D_MODEL)),
                  _resident((D_MODEL, FFN_DIM)), _resident((D_MODEL, FFN_DIM)),
                  _resident((FFN_DIM, D_MODEL)), _resident((D_MODEL, D_MODEL)),
                  _resident((PLE_DIM, D_MODEL)), _resident((1, D_MODEL))],
        out_specs=row(D_MODEL),
        out_shape=jax.ShapeDtypeStruct((n, D_MODEL), F32),
        compiler_params=pltpu.CompilerParams(dimension_semantics=("arbitrary",),
                                             vmem_limit_bytes=VMEM_LIMIT),
        name="outproj_ffn2_ple",
    )(x1, gdn_o, swa_o, p, lw["wo"], lw["gmixpost"], lw["g2pre"], lw["g2post"],
      lw["wg2"], lw["wu2"], lw["wd2"], lw["wpg"], lw["wpp"], lw["gple"])


def _iota2(c):
    return (lax.broadcasted_iota(jnp.int32, (c, c), 0), lax.broadcasted_iota(jnp.int32, (c, c), 1))


def _gates(ba, alog_row, dtb_row):
    beta = _sigmoid(ba)
    xa = ba + dtb_row
    softplus = jnp.maximum(xa, 0.0) + jnp.log1p(jnp.exp(-jnp.abs(xa)))
    g = -jnp.exp(alog_row) * softplus
    return beta, g


def _cumsum_rows(mask01, g):
    hi, lo = _split(g)
    m = mask01.astype(BF16)
    return jnp.dot(m, hi, preferred_element_type=F32) + jnp.dot(m, lo, preferred_element_type=F32)


def _hi_lo_lanes(x):
    hi, lo = _split(x)
    return jnp.concatenate([hi, lo], axis=1)


def _rowsum_bcast(x, ones_ref):
    return jnp.dot(x, ones_ref[...], preferred_element_type=F32)


def _lane_bcast(x, sel_ref, lanes):
    out = jnp.dot(_hi_lo_lanes(x), sel_ref[...], preferred_element_type=F32)
    return [out[:, l * 128:(l + 1) * 128] for l in lanes]


def _gdn_prep_steps(out, y, beta_b, gc_b, gl_b, gc_t, chunk, group, ones_ref, normalized):
    c = chunk
    n_chunks = y.shape[0] // c
    units = [(ci, h) for ci in range(n_chunks) for h in range(GDN_HEADS)]
    n = range(len(units))
    ii, jj = _iota2(c)
    same = (ii // group) == (jj // group)
    incl = jnp.logical_and(same, ii >= jj)
    strict = jnp.logical_and(same, ii > jj)

    def rows(t, ci, lo, hi):
        return t[ci * c:(ci + 1) * c, lo:hi]

    eg_b = [jnp.exp(t) for t in gc_b]
    ekd_b = [jnp.exp(gl_b[h] - gc_b[h]) for h in range(GDN_HEADS)]
    beta_u = [rows(beta_b[h], ci, 0, GDN_DK) for ci, h in units]
    eg_u = [rows(eg_b[h], ci, 0, GDN_DK) for ci, h in units]
    ekd_u = [rows(ekd_b[h], ci, 0, GDN_DK) for ci, h in units]
    gc_col = [rows(gc_b[h], ci, 0, c) for ci, h in units]
    gc_row = [gc_t[GDN_HEADS + h:GDN_HEADS + h + 1, ci * c:(ci + 1) * c] for ci, h in units]
    q = [rows(y, ci, h * GDN_DK, (h + 1) * GDN_DK) for ci, h in units]
    k = [rows(y, ci, GDN_KEY_DIM + h * GDN_DK, GDN_KEY_DIM + (h + 1) * GDN_DK) for ci, h in units]
    v = [rows(y, ci, 2 * GDN_KEY_DIM + h * GDN_DV, 2 * GDN_KEY_DIM + (h + 1) * GDN_DV) for ci, h in units]
    if not normalized:
        q = [_l2norm(t, ones_ref) * (GDN_DK ** -0.5) for t in q]
        k = [_l2norm(t, ones_ref) for t in k]
    decay = [jnp.exp(jnp.where(incl, gc_col[i] - gc_row[i], NEG_INF)) for i in n]
    kb = [k[i] * beta_u[i] for i in n]
    kq = [_dot_nt(jnp.concatenate([kb[i], q[i]], axis=0), k[i]) for i in n]
    yield
    a_mat = [jnp.where(strict, kq[i][:c] * decay[i], 0.0) for i in n]
    qk = [kq[i][c:] * decay[i] for i in n]
    eye = jnp.where(ii == jj, 1.0, 0.0).astype(F32)
    t_mat = [eye for _ in n]
    b = 1
    while b < group:
        lower = jnp.logical_and((ii // (2 * b)) == (jj // (2 * b)),
                                jnp.logical_and((ii % (2 * b)) >= b, (jj % (2 * b)) < b))
        m = [jnp.where(lower, a_mat[i], 0.0) for i in n]
        if b == 1:
            t_mat = [eye - m[i] for i in n]
        else:
            tm = [_dot(t_mat[i], m[i]) for i in n]
            yield
            t_mat = [t_mat[i] - _dot(tm[i], t_mat[i]) for i in n]
            yield
        b *= 2
    rhs = [jnp.concatenate([v[i] * beta_u[i], kb[i] * eg_u[i]], axis=-1) for i in n]
    sol = [_dot(t_mat[i], rhs[i]) for i in n]
    yield
    nest = lambda xs: [xs[ci * GDN_HEADS:(ci + 1) * GDN_HEADS] for ci in range(n_chunks)]
    out.update(u=nest([s[:, :GDN_DV] for s in sol]), w=nest([s[:, GDN_DV:] for s in sol]), qk=nest(qk),
               qd=nest([q[i] * eg_u[i] for i in n]), kd=nest([k[i] * ekd_u[i] for i in n]))


def _l2norm(t, ones_ref):
    return t * lax.rsqrt(_rowsum_bcast(t * t, ones_ref) + 1e-6)


def _gdn_out(o, z, gnorm_row, ones_ref):
    ms = _rowsum_bcast(o * o, ones_ref) * (1.0 / GDN_DV)
    return (o * lax.rsqrt(ms + NORM_EPS)) * gnorm_row * _silu(z)


def _conv_silu(xbuf_ref, cw_ref, c):
    y = xbuf_ref[pl.ds(5, c), :] * cw_ref[0:1, :]
    for j in range(1, CONV_WIDTH):
        y = y + xbuf_ref[pl.ds(5 + j, c), :] * cw_ref[j:j + 1, :]
    return _silu(y)


def _gdn_prompt_kernel(qkv_ref, z_ref, ba_ref, cw_ref, alog_ref, dtb_ref, gnorm_ref, ones_ref, sel_ref,
                       o_ref, sfin_ref, s_ref, xbuf_ref, ya_ref, ga_ref, gta_ref, yb_ref, gb_ref, gtb_ref):
    c = GDN_CHUNK
    n_chunks = GDN_STEP_CHUNKS
    r = n_chunks * c
    hs = range(GDN_HEADS)

    @pl.when(pl.program_id(1) == 0)
    def _():
        s_ref[...] = jnp.zeros_like(s_ref)
        xbuf_ref[0:8, :] = jnp.zeros((8, CONV_DIM), F32)
        ya_ref[...] = jnp.zeros_like(ya_ref)
        ga_ref[...] = jnp.zeros_like(ga_ref)
        gta_ref[...] = jnp.zeros_like(gta_ref)

    @pl.when(pl.program_id(1) >= 0)
    def _():
        yb_ref[...] = ya_ref[...]
        gb_ref[...] = ga_ref[...]
        gtb_ref[...] = gta_ref[...]

    def stage_a():
        xbuf_ref[8:8 + r, :] = qkv_ref[...]
        beta, g = _gates(ba_ref[...], alog_ref[...], dtb_ref[...])
        ii, jj = _iota2(r)
        tril = jnp.where(jnp.logical_and((ii // c) == (jj // c), ii >= jj), 1.0, 0.0)
        gc = _cumsum_rows(tril, g)
        gta_ref[...] = gc.T
        lane = lax.broadcasted_iota(jnp.int32, (r, BA_DIM), 1)
        ga_ref[...] = jnp.where(lane < GDN_HEADS, beta, gc)
        yield
        for ci in range(n_chunks):
            rows = slice(ci * c, (ci + 1) * c)
            for slab in range(CONV_DIM // LANES):
                cols = slice(slab * LANES, (slab + 1) * LANES)
                yc = xbuf_ref[pl.ds(5 + ci * c, c), cols] * cw_ref[0:1, cols]
                for j in range(1, CONV_WIDTH):
                    yc = yc + xbuf_ref[pl.ds(5 + j + ci * c, c), cols] * cw_ref[j:j + 1, cols]
                ya_ref[rows, cols] = _silu(yc)
                if slab < GDN_KEY_DIM // LANES:
                    ya_ref[rows, cols] = _l2norm(ya_ref[rows, cols], ones_ref) * (GDN_DK ** -0.5)
                elif slab < 2 * GDN_KEY_DIM // LANES:
                    ya_ref[rows, cols] = _l2norm(ya_ref[rows, cols], ones_ref)
                yield
        xbuf_ref[0:8, :] = xbuf_ref[r:r + 8, :]
        yield

    def stage_b():
        bcast = _lane_bcast(gb_ref[...], sel_ref, range(2 * GDN_HEADS))
        beta_b, gc_b = bcast[:GDN_HEADS], bcast[GDN_HEADS:]
        glast = [[gc_b[h][(ci + 1) * c - 1:(ci + 1) * c, :] for h in hs] for ci in range(n_chunks)]
        gl_b = [jnp.concatenate([jnp.broadcast_to(glast[ci][h], (c, GDN_DK)) for ci in range(n_chunks)],
                                axis=0) for h in hs]
        wy = {}
        yield from _gdn_prep_steps(wy, yb_ref, beta_b, gc_b, gl_b, gtb_ref[...], c, c, ones_ref,
                                   normalized=True)
        u, w, qk, qd, kd = (wy[name] for name in ("u", "w", "qk", "qd", "kd"))
        s_cur = [s_ref[h] for h in hs]
        for ci in range(n_chunks):
            ws = [_dot(jnp.concatenate([w[ci][h], qd[ci][h]], axis=0), s_cur[h]) for h in hs]
            yield
            v_new = [u[ci][h] - ws[h][:c] for h in hs]
            o = [ws[h][c:] + _dot(qk[ci][h], v_new[h]) for h in hs]
            s_cur = [s_cur[h] * jnp.exp(glast[ci][h]) + _dot_tn(kd[ci][h], v_new[h]) for h in hs]
            yield
            for h in hs:
                z = z_ref[ci * c:(ci + 1) * c, h * GDN_DV:(h + 1) * GDN_DV]
                o_ref[ci * c:(ci + 1) * c, h * GDN_DV:(h + 1) * GDN_DV] = _gdn_out(o[h], z, gnorm_ref[...], ones_ref)
        for h in hs:
            s_ref[h] = s_cur[h]
            sfin_ref[0, h] = s_cur[h]

    a_pieces = 2 + n_chunks * (CONV_DIM // LANES)
    b_levels = 2 * (int(math.log2(c)) - 1) + 2 + 2 * n_chunks
    a_steps = stage_a()
    done = 0
    for i, _ in enumerate(stage_b()):
        target = -(-(i + 1) * a_pieces // b_levels)
        for _ in range(target - done):
            next(a_steps, None)
        done = target
    for _ in a_steps:
        pass


def _gdn_prompt_call(qkvz, ba, lw, batch, seq):
    c = GDN_CHUNK * GDN_STEP_CHUNKS
    nc = seq // c
    z_col = CONV_DIM // GDN_VAL_DIM
    return pl.pallas_call(
        _gdn_prompt_kernel,
        grid=(batch, nc + 1),
        in_specs=[pl.BlockSpec((c, CONV_DIM), lambda b, s: (b * nc + jnp.minimum(s, nc - 1), 0)),
                  pl.BlockSpec((c, GDN_VAL_DIM), lambda b, s: (b * nc + jnp.maximum(s - 1, 0), z_col)),
                  pl.BlockSpec((c, BA_DIM), lambda b, s: (b * nc + jnp.minimum(s, nc - 1), 0)),
                  _resident((CONV_WIDTH, CONV_DIM)), _resident((1, BA_DIM)), _resident((1, BA_DIM)),
                  _resident((1, GDN_DV)), _resident(lw["ones"].shape), _resident(lw["sel"].shape)],
        out_specs=[pl.BlockSpec((c, GDN_VAL_DIM), lambda b, s: (b * nc + jnp.maximum(s - 1, 0), 0)),
                   pl.BlockSpec((1, GDN_HEADS, GDN_DK, GDN_DV), lambda b, s: (b, 0, 0, 0))],
        out_shape=[jax.ShapeDtypeStruct((batch * seq, GDN_VAL_DIM), F32),
                   jax.ShapeDtypeStruct((batch, GDN_HEADS, GDN_DK, GDN_DV), F32)],
        scratch_shapes=[pltpu.VMEM((GDN_HEADS, GDN_DK, GDN_DV), F32),
                        pltpu.VMEM((c + 8, CONV_DIM), F32),
                        pltpu.VMEM((c, CONV_DIM), F32), pltpu.VMEM((c, BA_DIM), F32), pltpu.VMEM((BA_DIM, c), F32),
                        pltpu.VMEM((c, CONV_DIM), F32), pltpu.VMEM((c, BA_DIM), F32), pltpu.VMEM((BA_DIM, c), F32)],
        compiler_params=pltpu.CompilerParams(dimension_semantics=("arbitrary", "arbitrary"),
                                             vmem_limit_bytes=VMEM_LIMIT),
        name="gdn_prompt",
    )(qkvz, qkvz, ba, lw["conv_w"], lw["alog_row"], lw["dtb_row"], lw["gnorm"], lw["ones"], lw["sel"])


def _gdn_decode_kernel(qkvz_ref, ba_ref, sconv_ref, s0_ref, cw_ref, alog_ref, dtb_ref, gnorm_ref,
                       ones_ref, sel_ref, o_ref, snew_ref, xbuf_ref, *, dec_seq):
    nb = DEC_SEQ_BLOCK
    t = dec_seq
    c = nb * t
    ys = []
    for s in range(nb):
        xbuf_ref[s, 5:8, :] = sconv_ref[s]
        xbuf_ref[s, 8:8 + t, :] = qkvz_ref[s * t:(s + 1) * t, :CONV_DIM]
        ys.append(_conv_silu(xbuf_ref.at[s], cw_ref, t))
    y = jnp.concatenate(ys, axis=0)

    beta, g = _gates(ba_ref[...], alog_ref[...], dtb_ref[...])
    ii, jj = _iota2(c)
    same = (ii // t) == (jj // t)
    tril = jnp.where(jnp.logical_and(same, ii >= jj), 1.0, 0.0).astype(F32)
    ones = jnp.where(same, 1.0, 0.0).astype(F32)
    gc = _cumsum_rows(tril, g)
    gl = _cumsum_rows(ones, g)
    gc_t = gc.T
    lane = lax.broadcasted_iota(jnp.int32, (c, BA_DIM), 1)
    bcast = _lane_bcast(jnp.where(lane < GDN_HEADS, beta, gc), sel_ref, range(2 * GDN_HEADS))
    beta_b, gc_b = bcast[:GDN_HEADS], bcast[GDN_HEADS:]
    gl_b = _lane_bcast(gl, sel_ref, range(GDN_HEADS, 2 * GDN_HEADS))
    wy = {}
    for _ in _gdn_prep_steps(wy, y, beta_b, gc_b, gl_b, gc_t, c, t, ones_ref, normalized=False):
        pass
    u, w, qk, qd, kd = (wy[name][0] for name in ("u", "w", "qk", "qd", "kd"))
    hs = range(GDN_HEADS)
    units = [(s, h) for h in hs for s in range(nb)]
    s_old = {(s, h): s0_ref[s, h] for s, h in units}
    ws = {(s, h): _dot(jnp.concatenate([w[h][s * t:(s + 1) * t], qd[h][s * t:(s + 1) * t]], axis=0),
                       s_old[s, h]) for s, h in units}
    v_new = [jnp.concatenate([u[h][s * t:(s + 1) * t] - ws[s, h][:t] for s in range(nb)], axis=0) for h in hs]
    o = [jnp.concatenate([ws[s, h][t:] for s in range(nb)], axis=0) + _dot(qk[h], v_new[h]) for h in hs]
    cd = [jnp.exp(gl_b[h]) for h in hs]
    kd_t = [kd[h].T for h in hs]
    col_seq = lax.broadcasted_iota(jnp.int32, (GDN_DK, c), 1) // t
    for s, h in units:
        upd = _dot(jnp.where(col_seq == s, kd_t[h], 0.0), v_new[h])
        snew_ref[s, h] = s_old[s, h] * cd[h][s * t:s * t + 1, :] + upd
    for h in hs:
        z = qkvz_ref[:, CONV_DIM + h * GDN_DV:CONV_DIM + (h + 1) * GDN_DV]
        o_ref[:, h * GDN_DV:(h + 1) * GDN_DV] = _gdn_out(o[h], z, gnorm_ref[...], ones_ref)


def _gdn_decode_call(qkvz, ba, state_conv, state_gdn, lw, dec_batch, dec_seq):
    nb = DEC_SEQ_BLOCK
    rows = nb * dec_seq
    return pl.pallas_call(
        functools.partial(_gdn_decode_kernel, dec_seq=dec_seq),
        grid=(dec_batch // nb,),
        in_specs=[pl.BlockSpec((rows, QKVZ_DIM), lambda i: (i, 0)),
                  pl.BlockSpec((rows, BA_DIM), lambda i: (i, 0)),
                  pl.BlockSpec((nb, CONV_WIDTH - 1, CONV_DIM), lambda i: (i, 0, 0)),
                  pl.BlockSpec((nb, GDN_HEADS, GDN_DK, GDN_DV), lambda i: (i, 0, 0, 0)),
                  _resident((CONV_WIDTH, CONV_DIM)), _resident((1, BA_DIM)), _resident((1, BA_DIM)),
                  _resident((1, GDN_DV)), _resident(lw["ones"].shape), _resident(lw["sel"].shape)],
        out_specs=[pl.BlockSpec((rows, GDN_VAL_DIM), lambda i: (i, 0)),
                   pl.BlockSpec((nb, GDN_HEADS, GDN_DK, GDN_DV), lambda i: (i, 0, 0, 0))],
        out_shape=[jax.ShapeDtypeStruct((dec_batch * dec_seq, GDN_VAL_DIM), F32),
                   jax.ShapeDtypeStruct((dec_batch, GDN_HEADS, GDN_DK, GDN_DV), F32)],
        scratch_shapes=[pltpu.VMEM((nb, 16, CONV_DIM), F32)],
        compiler_params=pltpu.CompilerParams(dimension_semantics=("arbitrary",),
                                             vmem_limit_bytes=VMEM_LIMIT),
        name="gdn_decode",
    )(qkvz, ba, state_conv, state_gdn, lw["conv_w"], lw["alog_row"], lw["dtb_row"], lw["gnorm"],
      lw["ones"], lw["sel"])


def _bias_table_kernel(bucket_ref, rb_ref, out_ref):
    bucket = bucket_ref[...]
    for h in range(SWA_HEADS):
        acc = jnp.zeros(bucket.shape, F32)
        for b in range(NUM_BUCKETS):
            acc = jnp.where(bucket == b, rb_ref[b, h], acc)
        out_ref[h] = jnp.where(bucket < 0, NEG_INF, acc)


def _bias_table_call(bucket, rel_bias):
    r, c = bucket.shape
    return pl.pallas_call(
        _bias_table_kernel,
        in_specs=[pl.BlockSpec(memory_space=pltpu.VMEM), pl.BlockSpec(memory_space=pltpu.SMEM)],
        out_specs=pl.BlockSpec(memory_space=pltpu.VMEM),
        out_shape=jax.ShapeDtypeStruct((SWA_HEADS, r, c), F32),
        name="t5_bias_table",
    )(bucket, rel_bias)


def _t5_bucket(dist, valid):
    d = jnp.maximum(dist, 0)
    exact = NUM_BUCKETS // 2
    log_ratio = jnp.log(jnp.maximum(d, 1).astype(F32) / exact) / math.log(REL_MAX_DISTANCE / exact)
    large = jnp.minimum(exact + (log_ratio * (NUM_BUCKETS - exact)).astype(jnp.int32), NUM_BUCKETS - 1)
    bucket = jnp.where(d < exact, d, large)
    return jnp.where(valid, bucket, -1).astype(jnp.int32)


def _swa_prompt_kernel(q_ref, kvp_ref, kvc_ref, bias_ref, sink_ref, o_ref):
    w, dh, nq = SWA_WINDOW, SWA_HEAD_DIM, SWA_STEP_BLOCKS
    first = pl.program_id(1) == 0
    col = lax.broadcasted_iota(jnp.int32, (w, 2 * w), 1)
    drop_prev = jnp.logical_and(first, col < w)
    kv = jnp.concatenate([kvp_ref[...], kvc_ref[...]], axis=0).astype(BF16)
    k_h = [kv[:, h * dh:(h + 1) * dh] for h in range(SWA_KV_HEADS)]
    v_h = [kv[:, SWA_KV_DIM + h * dh:SWA_KV_DIM + (h + 1) * dh] for h in range(SWA_KV_HEADS)]
    units = [(b, hg) for b in range(nq) for hg in range(SWA_HEADS)]
    n = range(len(units))
    q = [(q_ref[b * w:(b + 1) * w, hg * dh:(hg + 1) * dh] * (dh ** -0.5)).astype(BF16) for b, hg in units]
    s = [lax.dot_general(q[i], k_h[hg // SWA_GROUP][b * w:(b + 2) * w], (((1,), (1,)), ((), ())),
                         preferred_element_type=F32) + bias_ref[hg] for i, (b, hg) in enumerate(units)]
    s = [jnp.where(drop_prev, NEG_INF, s[i]) if b == 0 else s[i] for i, (b, hg) in enumerate(units)]
    sink = [sink_ref[hg] for b, hg in units]
    m = [jnp.maximum(jnp.max(s[i], axis=-1, keepdims=True), sink[i]) for i in n]
    p = [jnp.exp(s[i] - m[i]) for i in n]
    denom = [jnp.sum(p[i], axis=-1, keepdims=True) + jnp.exp(sink[i] - m[i]) for i in n]
    o = [jnp.dot(p[i].astype(BF16), v_h[hg // SWA_GROUP][b * w:(b + 2) * w], preferred_element_type=F32)
         / denom[i] for i, (b, hg) in enumerate(units)]
    for i, (b, hg) in enumerate(units):
        o_ref[b * w:(b + 1) * w, hg * dh:(hg + 1) * dh] = o[i]


def _swa_prompt_call(qs, kv, bias, sinks, batch, seq):
    w, nq = SWA_WINDOW, SWA_STEP_BLOCKS
    nb = seq // (w * nq)
    return pl.pallas_call(
        _swa_prompt_kernel,
        grid=(batch, nb),
        in_specs=[pl.BlockSpec((nq * w, SWA_Q_DIM), lambda b, n: (b * nb + n, 0)),
                  pl.BlockSpec((w, 2 * SWA_KV_DIM),
                               lambda b, n: ((b * nb + n) * nq - jnp.minimum(n, 1), 0)),
                  pl.BlockSpec((nq * w, 2 * SWA_KV_DIM), lambda b, n: (b * nb + n, 0)),
                  _resident((SWA_HEADS, w, 2 * w)),
                  pl.BlockSpec(memory_space=pltpu.SMEM)],
        out_specs=pl.BlockSpec((nq * w, SWA_Q_DIM), lambda b, n: (b * nb + n, 0)),
        out_shape=jax.ShapeDtypeStruct((batch * seq, SWA_Q_DIM), F32),
        compiler_params=pltpu.CompilerParams(dimension_semantics=("arbitrary", "arbitrary"),
                                             vmem_limit_bytes=VMEM_LIMIT),
        name="swa_prompt",
    )(qs, kv, kv, bias, sinks)


def _swa_decode_kernel(q_ref, kvn_ref, kbuf_ref, vbuf_ref, bbuf_ref, bnew_ref, sink_ref, o_ref,
                       *, dec_seq):
    nb = DEC_SEQ_BLOCK
    t = dec_seq
    rows = nb * t
    m_rows = SWA_GROUP * rows
    wb = kbuf_ref.shape[1]
    ri = lax.broadcasted_iota(jnp.int32, (m_rows, 1), 0)
    row_seq = (ri % rows) // t
    row_grp = ri // rows
    for h in range(SWA_KV_HEADS):
        lo, hi = h * SWA_HEAD_DIM, (h + 1) * SWA_HEAD_DIM
        q = jnp.concatenate(
            [q_ref[:, (h * SWA_GROUP + g) * SWA_HEAD_DIM:(h * SWA_GROUP + g + 1) * SWA_HEAD_DIM]
             for g in range(SWA_GROUP)], axis=0)
        sink = jnp.zeros((m_rows, 1), F32)
        for g in range(SWA_GROUP):
            sink = jnp.where(row_grp == g, sink_ref[h * SWA_GROUP + g], sink)
        k_new = jnp.concatenate([kvn_ref[:, lo:hi], jnp.zeros((m_rows - rows, SWA_HEAD_DIM), F32)], axis=0)
        v_new = jnp.concatenate([kvn_ref[:, SWA_KV_DIM + lo:SWA_KV_DIM + hi],
                                 jnp.zeros((m_rows - rows, SWA_HEAD_DIM), F32)], axis=0)
        s_new = _dot_nt(q, k_new) * (SWA_HEAD_DIM ** -0.5) + bnew_ref[h]
        s_buf = jnp.zeros((m_rows, wb), F32)
        for s in range(nb):
            s_full = _dot_nt(q, kbuf_ref[s, :, lo:hi])
            s_buf = jnp.where(row_seq == s, s_full, s_buf)
        s_buf = s_buf * (SWA_HEAD_DIM ** -0.5) + bbuf_ref[h]
        m = jnp.maximum(jnp.maximum(jnp.max(s_buf, axis=-1, keepdims=True),
                                    jnp.max(s_new, axis=-1, keepdims=True)), sink)
        p_buf = jnp.exp(s_buf - m)
        p_new = jnp.exp(s_new - m)
        denom = (jnp.sum(p_buf, axis=-1, keepdims=True) + jnp.sum(p_new, axis=-1, keepdims=True)
                 + jnp.exp(sink - m))
        o = _dot(p_new, v_new)
        for s in range(nb):
            o = o + _dot(jnp.where(row_seq == s, p_buf, 0.0), vbuf_ref[s, :, lo:hi])
        o = o / denom
        for g in range(SWA_GROUP):
            hg = h * SWA_GROUP + g
            o_ref[:, hg * SWA_HEAD_DIM:(hg + 1) * SWA_HEAD_DIM] = o[g * rows:(g + 1) * rows]


def _swa_decode_call(qs, kvn, kbuf, vbuf, bias_buf, bias_new, sinks, dec_batch, dec_seq):
    nb = DEC_SEQ_BLOCK
    rows = nb * dec_seq
    wb = kbuf.shape[1]
    m_rows = SWA_GROUP * rows
    return pl.pallas_call(
        functools.partial(_swa_decode_kernel, dec_seq=dec_seq),
        grid=(dec_batch // nb,),
        in_specs=[pl.BlockSpec((rows, SWA_Q_DIM), lambda i: (i, 0)),
                  pl.BlockSpec((rows, 2 * SWA_KV_DIM), lambda i: (i, 0)),
                  pl.BlockSpec((nb, wb, SWA_KV_DIM), lambda i: (i, 0, 0)),
                  pl.BlockSpec((nb, wb, SWA_KV_DIM), lambda i: (i, 0, 0)),
                  _resident((SWA_KV_HEADS, m_rows, wb)),
                  _resident((SWA_KV_HEADS, m_rows, m_rows)),
                  pl.BlockSpec(memory_space=pltpu.SMEM)],
        out_specs=pl.BlockSpec((rows, SWA_Q_DIM), lambda i: (i, 0)),
        out_shape=jax.ShapeDtypeStruct((dec_batch * dec_seq, SWA_Q_DIM), F32),
        compiler_params=pltpu.CompilerParams(dimension_semantics=("arbitrary",),
                                             vmem_limit_bytes=VMEM_LIMIT),
        name="swa_decode",
    )(qs, kvn, kbuf, vbuf, bias_buf, bias_new, sinks)


def _prompt_bias(rel_bias):
    w = SWA_WINDOW
    dist = w + jnp.arange(w)[:, None] - jnp.arange(2 * w)[None, :]
    valid = (dist >= 0) & (dist < w)
    return _bias_table_call(_t5_bucket(dist, valid), rel_bias)


def _decode_bias(rel_bias, wb, dec_seq):
    nb, t = DEC_SEQ_BLOCK, dec_seq
    rows = nb * t
    m_rows = SWA_GROUP * rows
    r = jnp.arange(m_rows)
    r_tok, r_seq = r % t, (r % rows) // t
    dist_b = wb + r_tok[:, None] - jnp.arange(wb)[None, :]
    tab_b = _bias_table_call(_t5_bucket(dist_b, (dist_b >= 0) & (dist_b < SWA_WINDOW)), rel_bias)
    cidx = jnp.arange(m_rows)
    dist_n = r_tok[:, None] - (cidx % t)[None, :]
    ok_n = (dist_n >= 0) & (dist_n < SWA_WINDOW) & (r_seq[:, None] == (cidx // t)[None, :]) & (cidx < rows)[None, :]
    tab_n = _bias_table_call(_t5_bucket(dist_n, ok_n), rel_bias)

    def pick(tab):
        return jnp.stack([jnp.concatenate([tab[h * SWA_GROUP + g, g * rows:(g + 1) * rows]
                                           for g in range(SWA_GROUP)], axis=0)
                          for h in range(SWA_KV_HEADS)])
    return pick(tab_b), pick(tab_n)


def _pack_layer(i, norm_ffn1_pre, norm_ffn1_post, ffn1_w_gate, ffn1_w_up, ffn1_w_down, norm_mix_pre,
                norm_mix_post, w_in, conv_w, gdn_a_log, gdn_dt_bias, gdn_norm, swa_sinks, w_out,
                norm_ffn2_pre, norm_ffn2_post, ffn2_w_gate, ffn2_w_up, ffn2_w_down, ple_gate, ple_proj,
                norm_ple_post):
    row = lambda g: g[i].reshape(1, -1).astype(F32)
    win = w_in[i]
    n_gdn = QKVZ_DIM + 2 * GDN_HEADS
    win = jnp.concatenate([win[:, :n_gdn], jnp.zeros((D_MODEL, BA_DIM - 2 * GDN_HEADS), win.dtype),
                           win[:, n_gdn:]], axis=1)
    lane_pad = lambda v: jnp.zeros((1, BA_DIM), F32).at[0, GDN_HEADS:2 * GDN_HEADS].set(v[i].astype(F32))
    kk = np.arange(2 * LANES)[:, None] % LANES
    sel = (kk == (np.arange(2 * GDN_HEADS * LANES)[None, :] // LANES)).astype(np.float32)
    return dict(
        ones=jnp.ones((LANES, LANES), F32), sel=jnp.asarray(sel, BF16),
        g1pre=row(norm_ffn1_pre), g1post=row(norm_ffn1_post),
        wg1=ffn1_w_gate[i].astype(BF16), wu1=ffn1_w_up[i].astype(BF16), wd1=ffn1_w_down[i].astype(BF16),
        gmix=row(norm_mix_pre), gmixpost=row(norm_mix_post), win=win.astype(BF16),
        conv_w=conv_w[i].astype(F32), alog_row=lane_pad(gdn_a_log), dtb_row=lane_pad(gdn_dt_bias),
        gnorm=row(gdn_norm), sinks=swa_sinks[i].astype(F32), wo=w_out[i].astype(BF16),
        g2pre=row(norm_ffn2_pre), g2post=row(norm_ffn2_post),
        wg2=ffn2_w_gate[i].astype(BF16), wu2=ffn2_w_up[i].astype(BF16), wd2=ffn2_w_down[i].astype(BF16),
        wpg=ple_gate[i].astype(BF16), wpp=ple_proj[i].astype(BF16), gple=row(norm_ple_post))


def kernel(x_prompt, x_sample, state_conv, state_gdn, cache_swa_k, cache_swa_v, p_prompt, p_sample,
           rel_bias, norm_ffn1_pre, norm_ffn1_post, ffn1_w_gate, ffn1_w_up, ffn1_w_down,
           norm_mix_pre, norm_mix_post, w_in, conv_w, gdn_a_log, gdn_dt_bias, gdn_norm, swa_sinks,
           w_out, norm_ffn2_pre, norm_ffn2_post, ffn2_w_gate, ffn2_w_up, ffn2_w_down,
           ple_gate, ple_proj, norm_ple_post):
    depth = w_in.shape[0]
    batch, seq, _ = x_prompt.shape
    dec_batch, dec_seq, _ = x_sample.shape
    wb = cache_swa_k.shape[2]
    wp = min(SWA_WINDOW, seq)
    rel_bias = rel_bias.astype(F32)
    bias_p = _prompt_bias(rel_bias)
    bias_db, bias_dn = _decode_bias(rel_bias, wb, dec_seq)

    yp = x_prompt.reshape(batch * seq, D_MODEL)
    ys = x_sample.reshape(dec_batch * dec_seq, D_MODEL)
    outs = [[] for _ in range(8)]
    for i in range(depth):
        lw = _pack_layer(i, norm_ffn1_pre, norm_ffn1_post, ffn1_w_gate, ffn1_w_up, ffn1_w_down,
                         norm_mix_pre, norm_mix_post, w_in, conv_w, gdn_a_log, gdn_dt_bias, gdn_norm,
                         swa_sinks, w_out, norm_ffn2_pre, norm_ffn2_post, ffn2_w_gate, ffn2_w_up,
                         ffn2_w_down, ple_gate, ple_proj, norm_ple_post)
        x1, qkvz, ba, qs, kv = _head_call(yp, lw)
        gdn_o, s_fin = _gdn_prompt_call(qkvz, ba, lw, batch, seq)
        swa_o = _swa_prompt_call(qs, kv, bias_p, lw["sinks"], batch, seq)
        yp = _tail_call(x1, gdn_o, swa_o, p_prompt[i].reshape(batch * seq, PLE_DIM), lw)
        kv3 = kv.reshape(batch, seq, 2 * SWA_KV_DIM)
        outs[0].append(qkvz.reshape(batch, seq, QKVZ_DIM)[:, seq - (CONV_WIDTH - 1):, :CONV_DIM])
        outs[1].append(s_fin)
        outs[2].append(kv3[:, seq - wp:, :SWA_KV_DIM].reshape(batch, wp, SWA_KV_HEADS, SWA_HEAD_DIM))
        outs[3].append(kv3[:, seq - wp:, SWA_KV_DIM:].reshape(batch, wp, SWA_KV_HEADS, SWA_HEAD_DIM))
        x1, qkvz, ba, qs, kv = _head_call(ys, lw)
        gdn_o, s_new = _gdn_decode_call(qkvz, ba, state_conv[i], state_gdn[i], lw, dec_batch, dec_seq)
        kbuf = cache_swa_k[i].reshape(dec_batch, wb, SWA_KV_DIM)
        vbuf = cache_swa_v[i].reshape(dec_batch, wb, SWA_KV_DIM)
        swa_o = _swa_decode_call(qs, kv, kbuf, vbuf, bias_db, bias_dn, lw["sinks"], dec_batch, dec_seq)
        ys = _tail_call(x1, gdn_o, swa_o, p_sample[i].reshape(dec_batch * dec_seq, PLE_DIM), lw)
        kv3 = kv.reshape(dec_batch, dec_seq, 2 * SWA_KV_DIM)
        xp = jnp.concatenate([state_conv[i], qkvz.reshape(dec_batch, dec_seq, QKVZ_DIM)[:, :, :CONV_DIM]], axis=1)
        outs[4].append(xp[:, dec_seq:])
        outs[5].append(s_new)
        outs[6].append(jnp.concatenate([kbuf, kv3[:, :, :SWA_KV_DIM]], axis=1)[:, dec_seq:]
                       .reshape(dec_batch, wb, SWA_KV_HEADS, SWA_HEAD_DIM))
        outs[7].append(jnp.concatenate([vbuf, kv3[:, :, SWA_KV_DIM:]], axis=1)[:, dec_seq:]
                       .reshape(dec_batch, wb, SWA_KV_HEADS, SWA_HEAD_DIM))
    return (yp.reshape(batch, seq, D_MODEL), ys.reshape(dec_batch, dec_seq, D_MODEL),
            *[jnp.stack(o) for o in outs])
```

```python
import functools
import math

import numpy as np
import jax
import jax.numpy as jnp
from jax import lax
from jax.experimental import pallas as pl
from jax.experimental.pallas import tpu as pltpu

F32 = jnp.float32
BF16 = jnp.bfloat16

D_MODEL = 1024
NORM_EPS = 1e-6
PLE_DIM = 256
FFN_DIM = 2816
GDN_HEADS = 4
GDN_DK = 128
GDN_DV = 128
GDN_KEY_DIM = GDN_HEADS * GDN_DK
GDN_VAL_DIM = GDN_HEADS * GDN_DV
CONV_DIM = 2 * GDN_KEY_DIM + GDN_VAL_DIM
CONV_WIDTH = 4
SWA_HEADS = 8
SWA_KV_HEADS = 2
SWA_GROUP = SWA_HEADS // SWA_KV_HEADS
SWA_HEAD_DIM = 64
SWA_Q_DIM = SWA_HEADS * SWA_HEAD_DIM
SWA_KV_DIM = SWA_KV_HEADS * SWA_HEAD_DIM
SWA_WINDOW = 128
NUM_BUCKETS = 32
REL_MAX_DISTANCE = 128

QKVZ_DIM = CONV_DIM + GDN_VAL_DIM
BA_DIM = 128
PROJ_PACKED = QKVZ_DIM + BA_DIM + SWA_Q_DIM + 2 * SWA_KV_DIM

GDN_CHUNK = 64
GDN_STEP_CHUNKS = 4
SWA_STEP_BLOCKS = 4
ROW_TILE = 512
DEC_SEQ_BLOCK = 8
VMEM_LIMIT = 56 * 1024 * 1024

NEG_INF = float("-inf")
LANES = 128


def _resident(shape):
    nd = len(shape)
    return pl.BlockSpec(shape, lambda *_: (0,) * nd, pipeline_mode=pl.Buffered(1))


def _rms(x, gain):
    ms = jnp.mean(x * x, axis=-1, keepdims=True)
    return (x * lax.rsqrt(ms + NORM_EPS)) * gain


def _sigmoid(x):
    return 1.0 / (1.0 + jnp.exp(-x))


def _silu(x):
    h = 0.5 * x
    return h + h * jnp.tanh(h)


def _dot(a, b):
    return jnp.dot(a, b, preferred_element_type=F32)


def _dot_nt(a, b):
    return lax.dot_general(a, b, (((1,), (1,)), ((), ())), preferred_element_type=F32)


def _dot_tn(a, b):
    return lax.dot_general(a, b, (((0,), (0,)), ((), ())), preferred_element_type=F32)


def _split(a):
    hi = a.astype(BF16)
    lo = (a - hi.astype(F32)).astype(BF16)
    return hi, lo


def _inv_rms(x):
    return lax.rsqrt(jnp.mean(x * x, axis=-1, keepdims=True) + NORM_EPS)


def _prenorm_dots(x, gain, w_refs):
    h = (x * gain).astype(BF16)
    inv = _inv_rms(x)
    return [jnp.dot(h, w, preferred_element_type=F32) * inv for w in w_refs]


def _swiglu_block(x, gain, wg_ref, wu_ref, wd_ref):
    g, u = _prenorm_dots(x, gain, [wg_ref[...], wu_ref[...]])
    a = (_silu(g) * u).astype(BF16)
    return jnp.dot(a, wd_ref[...], preferred_element_type=F32)


def _head_kernel(x_ref, g1pre_ref, g1post_ref, wg_ref, wu_ref, wd_ref, gmix_ref, win_ref,
                 x1_ref, qkvz_ref, ba_ref, qs_ref, kv_ref):
    x = x_ref[...]
    y = _swiglu_block(x, g1pre_ref[...], wg_ref, wu_ref, wd_ref)
    x1 = x + 0.5 * _rms(y, g1post_ref[...])
    x1_ref[...] = x1
    c0, c1, c2 = QKVZ_DIM, QKVZ_DIM + BA_DIM, QKVZ_DIM + BA_DIM + SWA_Q_DIM
    qkvz_ref[...], ba_ref[...], qs_ref[...], kv_ref[...] = _prenorm_dots(
        x1, gmix_ref[...], [win_ref[:, :c0], win_ref[:, c0:c1], win_ref[:, c1:c2], win_ref[:, c2:]])


def _head_call(x, lw):
    n = x.shape[0]
    tm = min(ROW_TILE, n)
    row = lambda w: pl.BlockSpec((tm, w), lambda i: (i, 0))
    return pl.pallas_call(
        _head_kernel,
        grid=(n // tm,),
        in_specs=[row(D_MODEL), _resident((1, D_MODEL)), _resident((1, D_MODEL)),
                  _resident((D_MODEL, FFN_DIM)), _resident((D_MODEL, FFN_DIM)),
                  _resident((FFN_DIM, D_MODEL)), _resident((1, D_MODEL)),
                  _resident((D_MODEL, PROJ_PACKED))],
        out_specs=[row(D_MODEL), row(QKVZ_DIM), row(BA_DIM), row(SWA_Q_DIM), row(2 * SWA_KV_DIM)],
        out_shape=[jax.ShapeDtypeStruct((n, D_MODEL), F32), jax.ShapeDtypeStruct((n, QKVZ_DIM), F32),
                   jax.ShapeDtypeStruct((n, BA_DIM), F32), jax.ShapeDtypeStruct((n, SWA_Q_DIM), F32),
                   jax.ShapeDtypeStruct((n, 2 * SWA_KV_DIM), F32)],
        compiler_params=pltpu.CompilerParams(dimension_semantics=("arbitrary",),
                                             vmem_limit_bytes=VMEM_LIMIT),
        name="ffn1_inproj",
    )(x, lw["g1pre"], lw["g1post"], lw["wg1"], lw["wu1"], lw["wd1"], lw["gmix"], lw["win"])


def _tail_kernel(x_ref, gdn_ref, swa_ref, p_ref, wo_ref, gmixpost_ref, g2pre_ref, g2post_ref,
                 wg_ref, wu_ref, wd_ref, wpg_ref, wpp_ref, gple_ref, y_ref):
    x = x_ref[...]
    mix = (jnp.dot(gdn_ref[...].astype(BF16), wo_ref[:GDN_VAL_DIM, :], preferred_element_type=F32)
           + jnp.dot(swa_ref[...].astype(BF16), wo_ref[GDN_VAL_DIM:, :], preferred_element_type=F32))
    pp = jnp.dot(p_ref[...].astype(BF16), wpp_ref[...], preferred_element_type=F32)
    x = x + _rms(mix, gmixpost_ref[...])
    y = _swiglu_block(x, g2pre_ref[...], wg_ref, wu_ref, wd_ref)
    x = x + 0.5 * _rms(y, g2post_ref[...])
    gate = _sigmoid(jnp.dot(x.astype(BF16), wpg_ref[...], preferred_element_type=F32))
    y_ref[...] = x + _rms(gate * pp, gple_ref[...])


def _tail_call(x1, gdn_o, swa_o, p, lw):
    n = x1.shape[0]
    tm = min(ROW_TILE, n)
    row = lambda w: pl.BlockSpec((tm, w), lambda i: (i, 0))
    return pl.pallas_call(
        _tail_kernel,
        grid=(n // tm,),
        in_specs=[row(D_MODEL), row(GDN_VAL_DIM), row(SWA_Q_DIM), row(PLE_DIM),
                  _resident((GDN_VAL_DIM + SWA_Q_DIM, D_MODEL)), _resident((1, D_MODEL)),
                  _resident((1, D_MODEL)), _resident((1, D_MODEL)),
                  _resident((D_MODEL, FFN_DIM)), _resident((D_MODEL, FFN_DIM)),
                  _resident((FFN_DIM, D_MODEL)), _resident((D_MODEL, D_MODEL)),
                  _resident((PLE_DIM, D_MODEL)), _resident((1, D_MODEL))],
        out_specs=row(D_MODEL),
        out_shape=jax.ShapeDtypeStruct((n, D_MODEL), F32),
        compiler_params=pltpu.CompilerParams(dimension_semantics=("arbitrary",),
                                             vmem_limit_bytes=VMEM_LIMIT),
        name="outproj_ffn2_ple",
    )(x1, gdn_o, swa_o, p, lw["wo"], lw["gmixpost"], lw["g2pre"], lw["g2post"],
      lw["wg2"], lw["wu2"], lw["wd2"], lw["wpg"], lw["wpp"], lw["gple"])


def _iota2(c):
    return (lax.broadcasted_iota(jnp.int32, (c, c), 0), lax.broadcasted_iota(jnp.int32, (c, c), 1))


def _gates(ba, alog_row, dtb_row):
    beta = _sigmoid(ba)
    xa = ba + dtb_row
    softplus = jnp.maximum(xa, 0.0) + jnp.log1p(jnp.exp(-jnp.abs(xa)))
    g = -jnp.exp(alog_row) * softplus
    return beta, g


def _cumsum_rows(mask01, g):
    hi, lo = _split(g)
    m = mask01.astype(BF16)
    return jnp.dot(m, hi, preferred_element_type=F32) + jnp.dot(m, lo, preferred_element_type=F32)


def _hi_lo_lanes(x):
    hi, lo = _split(x)
    return jnp.concatenate([hi, lo], axis=1)


def _rowsum_bcast(x, ones_ref):
    return jnp.dot(x, ones_ref[...], preferred_element_type=F32)


def _lane_bcast(x, sel_ref, lanes):
    out = jnp.dot(_hi_lo_lanes(x), sel_ref[...], preferred_element_type=F32)
    return [out[:, l * 128:(l + 1) * 128] for l in lanes]


def _gdn_prep_steps(out, y, beta_b, gc_b, gl_b, gc_t, chunk, group, ones_ref, normalized):
    c = chunk
    n_chunks = y.shape[0] // c
    units = [(ci, h) for ci in range(n_chunks) for h in range(GDN_HEADS)]
    n = range(len(units))
    ii, jj = _iota2(c)
    same = (ii // group) == (jj // group)
    incl = jnp.logical_and(same, ii >= jj)
    strict = jnp.logical_and(same, ii > jj)

    def rows(t, ci, lo, hi):
        return t[ci * c:(ci + 1) * c, lo:hi]

    eg_b = [jnp.exp(t) for t in gc_b]
    ekd_b = [jnp.exp(gl_b[h] - gc_b[h]) for h in range(GDN_HEADS)]
    beta_u = [rows(beta_b[h], ci, 0, GDN_DK) for ci, h in units]
    eg_u = [rows(eg_b[h], ci, 0, GDN_DK) for ci, h in units]
    ekd_u = [rows(ekd_b[h], ci, 0, GDN_DK) for ci, h in units]
    gc_col = [rows(gc_b[h], ci, 0, c) for ci, h in units]
    gc_row = [gc_t[GDN_HEADS + h:GDN_HEADS + h + 1, ci * c:(ci + 1) * c] for ci, h in units]
    q = [rows(y, ci, h * GDN_DK, (h + 1) * GDN_DK) for ci, h in units]
    k = [rows(y, ci, GDN_KEY_DIM + h * GDN_DK, GDN_KEY_DIM + (h + 1) * GDN_DK) for ci, h in units]
    v = [rows(y, ci, 2 * GDN_KEY_DIM + h * GDN_DV, 2 * GDN_KEY_DIM + (h + 1) * GDN_DV) for ci, h in units]
    if not normalized:
        q = [_l2norm(t, ones_ref) * (GDN_DK ** -0.5) for t in q]
        k = [_l2norm(t, ones_ref) for t in k]
    decay = [jnp.exp(jnp.where(incl, gc_col[i] - gc_row[i], NEG_INF)) for i in n]
    kb = [k[i] * beta_u[i] for i in n]
    kq = [_dot_nt(jnp.concatenate([kb[i], q[i]], axis=0), k[i]) for i in n]
    yield
    a_mat = [jnp.where(strict, kq[i][:c] * decay[i], 0.0) for i in n]
    qk = [kq[i][c:] * decay[i] for i in n]
    eye = jnp.where(ii == jj, 1.0, 0.0).astype(F32)
    t_mat = [eye for _ in n]
    b = 1
    while b < group:
        lower = jnp.logical_and((ii // (2 * b)) == (jj // (2 * b)),
                                jnp.logical_and((ii % (2 * b)) >= b, (jj % (2 * b)) < b))
        m = [jnp.where(lower, a_mat[i], 0.0) for i in n]
        if b == 1:
            t_mat = [eye - m[i] for i in n]
        else:
            tm = [_dot(t_mat[i], m[i]) for i in n]
            yield
            t_mat = [t_mat[i] - _dot(tm[i], t_mat[i]) for i in n]
            yield
        b *= 2
    rhs = [jnp.concatenate([v[i] * beta_u[i], kb[i] * eg_u[i]], axis=-1) for i in n]
    sol = [_dot(t_mat[i], rhs[i]) for i in n]
    yield
    nest = lambda xs: [xs[ci * GDN_HEADS:(ci + 1) * GDN_HEADS] for ci in range(n_chunks)]
    out.update(u=nest([s[:, :GDN_DV] for s in sol]), w=nest([s[:, GDN_DV:] for s in sol]), qk=nest(qk),
               qd=nest([q[i] * eg_u[i] for i in n]), kd=nest([k[i] * ekd_u[i] for i in n]))


def _l2norm(t, ones_ref):
    return t * lax.rsqrt(_rowsum_bcast(t * t, ones_ref) + 1e-6)


def _gdn_out(o, z, gnorm_row, ones_ref):
    ms = _rowsum_bcast(o * o, ones_ref) * (1.0 / GDN_DV)
    return (o * lax.rsqrt(ms + NORM_EPS)) * gnorm_row * _silu(z)


def _conv_silu(xbuf_ref, cw_ref, c):
    y = xbuf_ref[pl.ds(5, c), :] * cw_ref[0:1, :]
    for j in range(1, CONV_WIDTH):
        y = y + xbuf_ref[pl.ds(5 + j, c), :] * cw_ref[j:j + 1, :]
    return _silu(y)


def _gdn_prompt_kernel(qkv_ref, z_ref, ba_ref, cw_ref, alog_ref, dtb_ref, gnorm_ref, ones_ref, sel_ref,
                       o_ref, sfin_ref, s_ref, xbuf_ref, y0_ref, g0_ref, gt0_ref, y1_ref, g1_ref, gt1_ref):
    step = pl.program_id(1)

    @pl.when(step == 0)
    def _():
        s_ref[...] = jnp.zeros_like(s_ref)
        xbuf_ref[0:8, :] = jnp.zeros((8, CONV_DIM), F32)
        y1_ref[...] = jnp.zeros_like(y1_ref)
        g1_ref[...] = jnp.zeros_like(g1_ref)
        gt1_ref[...] = jnp.zeros_like(gt1_ref)

    @pl.when(lax.rem(step, 2) == 0)
    def _():
        _gdn_prompt_step((y0_ref, g0_ref, gt0_ref), (y1_ref, g1_ref, gt1_ref), qkv_ref, z_ref, ba_ref, cw_ref,
                         alog_ref, dtb_ref, gnorm_ref, ones_ref, sel_ref, o_ref, sfin_ref, s_ref, xbuf_ref)

    @pl.when(lax.rem(step, 2) == 1)
    def _():
        _gdn_prompt_step((y1_ref, g1_ref, gt1_ref), (y0_ref, g0_ref, gt0_ref), qkv_ref, z_ref, ba_ref, cw_ref,
                         alog_ref, dtb_ref, gnorm_ref, ones_ref, sel_ref, o_ref, sfin_ref, s_ref, xbuf_ref)


def _gdn_prompt_step(a_refs, b_refs, qkv_ref, z_ref, ba_ref, cw_ref, alog_ref, dtb_ref, gnorm_ref, ones_ref,
                     sel_ref, o_ref, sfin_ref, s_ref, xbuf_ref):
    ya_ref, ga_ref, gta_ref = a_refs
    yb_ref, gb_ref, gtb_ref = b_refs
    c = GDN_CHUNK
    n_chunks = GDN_STEP_CHUNKS
    r = n_chunks * c
    hs = range(GDN_HEADS)

    def stage_a():
        xbuf_ref[8:8 + r, :] = qkv_ref[...]
        beta, g = _gates(ba_ref[...], alog_ref[...], dtb_ref[...])
        ii, jj = _iota2(r)
        tril = jnp.where(jnp.logical_and((ii // c) == (jj // c), ii >= jj), 1.0, 0.0)
        gc = _cumsum_rows(tril, g)
        gta_ref[...] = gc.T
        lane = lax.broadcasted_iota(jnp.int32, (r, BA_DIM), 1)
        ga_ref[...] = jnp.where(lane < GDN_HEADS, beta, gc)
        yield
        for ci in range(n_chunks):
            rows = slice(ci * c, (ci + 1) * c)
            for slab in range(CONV_DIM // LANES):
                cols = slice(slab * LANES, (slab + 1) * LANES)
                ext = xbuf_ref[pl.ds(ci * c, c + 8), cols]
                s1 = pltpu.roll(ext, 1, axis=0)
                u2 = pltpu.roll(ext * cw_ref[1:2, cols] + s1 * cw_ref[0:1, cols], 2, axis=0)
                yc = (ext * cw_ref[3:4, cols] + s1 * cw_ref[2:3, cols] + u2)[8:]
                ya_ref[rows, cols] = _silu(yc)
                if slab < GDN_KEY_DIM // LANES:
                    ya_ref[rows, cols] = _l2norm(ya_ref[rows, cols], ones_ref) * (GDN_DK ** -0.5)
                elif slab < 2 * GDN_KEY_DIM // LANES:
                    ya_ref[rows, cols] = _l2norm(ya_ref[rows, cols], ones_ref)
                yield
        xbuf_ref[0:8, :] = xbuf_ref[r:r + 8, :]
        yield

    def stage_b():
        bcast = _lane_bcast(gb_ref[...], sel_ref, range(2 * GDN_HEADS))
        beta_b, gc_b = bcast[:GDN_HEADS], bcast[GDN_HEADS:]
        glast = [[gc_b[h][(ci + 1) * c - 1:(ci + 1) * c, :] for h in hs] for ci in range(n_chunks)]
        gl_b = [jnp.concatenate([jnp.broadcast_to(glast[ci][h], (c, GDN_DK)) for ci in range(n_chunks)],
                                axis=0) for h in hs]
        wy = {}
        yield from _gdn_prep_steps(wy, yb_ref, beta_b, gc_b, gl_b, gtb_ref[...], c, c, ones_ref,
                                   normalized=True)
        u, w, qk, qd, kd = (wy[name] for name in ("u", "w", "qk", "qd", "kd"))
        s_cur = [s_ref[h] for h in hs]
        for ci in range(n_chunks):
            ws = [_dot(jnp.concatenate([w[ci][h], qd[ci][h]], axis=0), s_cur[h]) for h in hs]
            yield
            v_new = [u[ci][h] - ws[h][:c] for h in hs]
            o = [ws[h][c:] + _dot(qk[ci][h], v_new[h]) for h in hs]
            s_cur = [s_cur[h] * jnp.exp(glast[ci][h]) + _dot_tn(kd[ci][h], v_new[h]) for h in hs]
            yield
            for h in hs:
                z = z_ref[ci * c:(ci + 1) * c, h * GDN_DV:(h + 1) * GDN_DV]
                o_ref[ci * c:(ci + 1) * c, h * GDN_DV:(h + 1) * GDN_DV] = _gdn_out(o[h], z, gnorm_ref[...], ones_ref)
        for h in hs:
            s_ref[h] = s_cur[h]
            sfin_ref[0, h] = s_cur[h]

    a_pieces = 2 + n_chunks * (CONV_DIM // LANES)
    b_levels = 2 * (int(math.log2(c)) - 1) + 2 + 2 * n_chunks
    a_steps = stage_a()
    done = 0
    for i, _ in enumerate(stage_b()):
        target = -(-(i + 1) * a_pieces // b_levels)
        for _ in range(target - done):
            next(a_steps, None)
        done = target
    for _ in a_steps:
        pass


def _gdn_prompt_call(qkvz, ba, lw, batch, seq):
    c = GDN_CHUNK * GDN_STEP_CHUNKS
    nc = seq // c
    z_col = CONV_DIM // GDN_VAL_DIM
    return pl.pallas_call(
        _gdn_prompt_kernel,
        grid=(batch, nc + 1),
        in_specs=[pl.BlockSpec((c, CONV_DIM), lambda b, s: (b * nc + jnp.minimum(s, nc - 1), 0)),
                  pl.BlockSpec((c, GDN_VAL_DIM), lambda b, s: (b * nc + jnp.maximum(s - 1, 0), z_col)),
                  pl.BlockSpec((c, BA_DIM), lambda b, s: (b * nc + jnp.minimum(s, nc - 1), 0)),
                  _resident((CONV_WIDTH, CONV_DIM)), _resident((1, BA_DIM)), _resident((1, BA_DIM)),
                  _resident((1, GDN_DV)), _resident(lw["ones"].shape), _resident(lw["sel"].shape)],
        out_specs=[pl.BlockSpec((c, GDN_VAL_DIM), lambda b, s: (b * nc + jnp.maximum(s - 1, 0), 0)),
                   pl.BlockSpec((1, GDN_HEADS, GDN_DK, GDN_DV), lambda b, s: (b, 0, 0, 0))],
        out_shape=[jax.ShapeDtypeStruct((batch * seq, GDN_VAL_DIM), F32),
                   jax.ShapeDtypeStruct((batch, GDN_HEADS, GDN_DK, GDN_DV), F32)],
        scratch_shapes=[pltpu.VMEM((GDN_HEADS, GDN_DK, GDN_DV), F32),
                        pltpu.VMEM((c + 8, CONV_DIM), F32),
                        pltpu.VMEM((c, CONV_DIM), F32), pltpu.VMEM((c, BA_DIM), F32), pltpu.VMEM((BA_DIM, c), F32),
                        pltpu.VMEM((c, CONV_DIM), F32), pltpu.VMEM((c, BA_DIM), F32), pltpu.VMEM((BA_DIM, c), F32)],
        compiler_params=pltpu.CompilerParams(dimension_semantics=("arbitrary", "arbitrary"),
                                             vmem_limit_bytes=VMEM_LIMIT),
        name="gdn_prompt",
    )(qkvz, qkvz, ba, lw["conv_w"], lw["alog_row"], lw["dtb_row"], lw["gnorm"], lw["ones"], lw["sel"])


def _gdn_decode_kernel(qkvz_ref, ba_ref, sconv_ref, s0_ref, cw_ref, alog_ref, dtb_ref, gnorm_ref,
                       ones_ref, sel_ref, o_ref, snew_ref, xbuf_ref, *, dec_seq):
    nb = DEC_SEQ_BLOCK
    t = dec_seq
    c = nb * t
    ys = []
    for s in range(nb):
        xbuf_ref[s, 5:8, :] = sconv_ref[s]
        xbuf_ref[s, 8:8 + t, :] = qkvz_ref[s * t:(s + 1) * t, :CONV_DIM]
        ys.append(_conv_silu(xbuf_ref.at[s], cw_ref, t))
    y = jnp.concatenate(ys, axis=0)

    beta, g = _gates(ba_ref[...], alog_ref[...], dtb_ref[...])
    ii, jj = _iota2(c)
    same = (ii // t) == (jj // t)
    tril = jnp.where(jnp.logical_and(same, ii >= jj), 1.0, 0.0).astype(F32)
    ones = jnp.where(same, 1.0, 0.0).astype(F32)
    gc = _cumsum_rows(tril, g)
    gl = _cumsum_rows(ones, g)
    gc_t = gc.T
    lane = lax.broadcasted_iota(jnp.int32, (c, BA_DIM), 1)
    bcast = _lane_bcast(jnp.where(lane < GDN_HEADS, beta, gc), sel_ref, range(2 * GDN_HEADS))
    beta_b, gc_b = bcast[:GDN_HEADS], bcast[GDN_HEADS:]
    gl_b = _lane_bcast(gl, sel_ref, range(GDN_HEADS, 2 * GDN_HEADS))
    wy = {}
    for _ in _gdn_prep_steps(wy, y, beta_b, gc_b, gl_b, gc_t, c, t, ones_ref, normalized=False):
        pass
    u, w, qk, qd, kd = (wy[name][0] for name in ("u", "w", "qk", "qd", "kd"))
    hs = range(GDN_HEADS)
    units = [(s, h) for h in hs for s in range(nb)]
    s_old = {(s, h): s0_ref[s, h] for s, h in units}
    ws = {(s, h): _dot(jnp.concatenate([w[h][s * t:(s + 1) * t], qd[h][s * t:(s + 1) * t]], axis=0),
                       s_old[s, h]) for s, h in units}
    v_new = [jnp.concatenate([u[h][s * t:(s + 1) * t] - ws[s, h][:t] for s in range(nb)], axis=0) for h in hs]
    o = [jnp.concatenate([ws[s, h][t:] for s in range(nb)], axis=0) + _dot(qk[h], v_new[h]) for h in hs]
    cd = [jnp.exp(gl_b[h]) for h in hs]
    kd_t = [kd[h].T for h in hs]
    col_seq = lax.broadcasted_iota(jnp.int32, (GDN_DK, c), 1) // t
    for s, h in units:
        upd = _dot(jnp.where(col_seq == s, kd_t[h], 0.0), v_new[h])
        snew_ref[s, h] = s_old[s, h] * cd[h][s * t:s * t + 1, :] + upd
    for h in hs:
        z = qkvz_ref[:, CONV_DIM + h * GDN_DV:CONV_DIM + (h + 1) * GDN_DV]
        o_ref[:, h * GDN_DV:(h + 1) * GDN_DV] = _gdn_out(o[h], z, gnorm_ref[...], ones_ref)


def _gdn_decode_call(qkvz, ba, state_conv, state_gdn, lw, dec_batch, dec_seq):
    nb = DEC_SEQ_BLOCK
    rows = nb * dec_seq
    return pl.pallas_call(
        functools.partial(_gdn_decode_kernel, dec_seq=dec_seq),
        grid=(dec_batch // nb,),
        in_specs=[pl.BlockSpec((rows, QKVZ_DIM), lambda i: (i, 0)),
                  pl.BlockSpec((rows, BA_DIM), lambda i: (i, 0)),
                  pl.BlockSpec((nb, CONV_WIDTH - 1, CONV_DIM), lambda i: (i, 0, 0)),
                  pl.BlockSpec((nb, GDN_HEADS, GDN_DK, GDN_DV), lambda i: (i, 0, 0, 0)),
                  _resident((CONV_WIDTH, CONV_DIM)), _resident((1, BA_DIM)), _resident((1, BA_DIM)),
                  _resident((1, GDN_DV)), _resident(lw["ones"].shape), _resident(lw["sel"].shape)],
        out_specs=[pl.BlockSpec((rows, GDN_VAL_DIM), lambda i: (i, 0)),
                   pl.BlockSpec((nb, GDN_HEADS, GDN_DK, GDN_DV), lambda i: (i, 0, 0, 0))],
        out_shape=[jax.ShapeDtypeStruct((dec_batch * dec_seq, GDN_VAL_DIM), F32),
                   jax.ShapeDtypeStruct((dec_batch, GDN_HEADS, GDN_DK, GDN_DV), F32)],
        scratch_shapes=[pltpu.VMEM((nb, 16, CONV_DIM), F32)],
        compiler_params=pltpu.CompilerParams(dimension_semantics=("arbitrary",),
                                             vmem_limit_bytes=VMEM_LIMIT),
        name="gdn_decode",
    )(qkvz, ba, state_conv, state_gdn, lw["conv_w"], lw["alog_row"], lw["dtb_row"], lw["gnorm"],
      lw["ones"], lw["sel"])


def _bias_table_kernel(bucket_ref, rb_ref, out_ref):
    bucket = bucket_ref[...]
    for h in range(SWA_HEADS):
        acc = jnp.zeros(bucket.shape, F32)
        for b in range(NUM_BUCKETS):
            acc = jnp.where(bucket == b, rb_ref[b, h], acc)
        out_ref[h] = jnp.where(bucket < 0, NEG_INF, acc)


def _bias_table_call(bucket, rel_bias):
    r, c = bucket.shape
    return pl.pallas_call(
        _bias_table_kernel,
        in_specs=[pl.BlockSpec(memory_space=pltpu.VMEM), pl.BlockSpec(memory_space=pltpu.SMEM)],
        out_specs=pl.BlockSpec(memory_space=pltpu.VMEM),
        out_shape=jax.ShapeDtypeStruct((SWA_HEADS, r, c), F32),
        name="t5_bias_table",
    )(bucket, rel_bias)


def _t5_bucket(dist, valid):
    d = jnp.maximum(dist, 0)
    exact = NUM_BUCKETS // 2
    log_ratio = jnp.log(jnp.maximum(d, 1).astype(F32) / exact) / math.log(REL_MAX_DISTANCE / exact)
    large = jnp.minimum(exact + (log_ratio * (NUM_BUCKETS - exact)).astype(jnp.int32), NUM_BUCKETS - 1)
    bucket = jnp.where(d < exact, d, large)
    return jnp.where(valid, bucket, -1).astype(jnp.int32)


def _swa_prompt_kernel(q_ref, kvp_ref, kvc_ref, bias_ref, sink_ref, o_ref):
    w, dh, nq = SWA_WINDOW, SWA_HEAD_DIM, SWA_STEP_BLOCKS
    first = pl.program_id(1) == 0
    col = lax.broadcasted_iota(jnp.int32, (w, 2 * w), 1)
    drop_prev = jnp.logical_and(first, col < w)
    kv = jnp.concatenate([kvp_ref[...], kvc_ref[...]], axis=0).astype(BF16)
    k_h = [kv[:, h * dh:(h + 1) * dh] for h in range(SWA_KV_HEADS)]
    v_h = [kv[:, SWA_KV_DIM + h * dh:SWA_KV_DIM + (h + 1) * dh] for h in range(SWA_KV_HEADS)]
    units = [(b, hg) for b in range(nq) for hg in range(SWA_HEADS)]
    n = range(len(units))
    q = [(q_ref[b * w:(b + 1) * w, hg * dh:(hg + 1) * dh] * (dh ** -0.5)).astype(BF16) for b, hg in units]
    s = [lax.dot_general(q[i], k_h[hg // SWA_GROUP][b * w:(b + 2) * w], (((1,), (1,)), ((), ())),
                         preferred_element_type=F32) + bias_ref[hg] for i, (b, hg) in enumerate(units)]
    s = [jnp.where(drop_prev, NEG_INF, s[i]) if b == 0 else s[i] for i, (b, hg) in enumerate(units)]
    sink = [sink_ref[hg] for b, hg in units]
    m = [jnp.maximum(jnp.max(s[i], axis=-1, keepdims=True), sink[i]) for i in n]
    p = [jnp.exp(s[i] - m[i]) for i in n]
    denom = [jnp.sum(p[i], axis=-1, keepdims=True) + jnp.exp(sink[i] - m[i]) for i in n]
    o = [jnp.dot(p[i].astype(BF16), v_h[hg // SWA_GROUP][b * w:(b + 2) * w], preferred_element_type=F32)
         / denom[i] for i, (b, hg) in enumerate(units)]
    for i, (b, hg) in enumerate(units):
        o_ref[b * w:(b + 1) * w, hg * dh:(hg + 1) * dh] = o[i]


def _swa_prompt_call(qs, kv, bias, sinks, batch, seq):
    w, nq = SWA_WINDOW, SWA_STEP_BLOCKS
    nb = seq // (w * nq)
    return pl.pallas_call(
        _swa_prompt_kernel,
        grid=(batch, nb),
        in_specs=[pl.BlockSpec((nq * w, SWA_Q_DIM), lambda b, n: (b * nb + n, 0)),
                  pl.BlockSpec((w, 2 * SWA_KV_DIM),
                               lambda b, n: ((b * nb + n) * nq - jnp.minimum(n, 1), 0)),
                  pl.BlockSpec((nq * w, 2 * SWA_KV_DIM), lambda b, n: (b * nb + n, 0)),
                  _resident((SWA_HEADS, w, 2 * w)),
                  pl.BlockSpec(memory_space=pltpu.SMEM)],
        out_specs=pl.BlockSpec((nq * w, SWA_Q_DIM), lambda b, n: (b * nb + n, 0)),
        out_shape=jax.ShapeDtypeStruct((batch * seq, SWA_Q_DIM), F32),
        compiler_params=pltpu.CompilerParams(dimension_semantics=("arbitrary", "arbitrary"),
                                             vmem_limit_bytes=VMEM_LIMIT),
        name="swa_prompt",
    )(qs, kv, kv, bias, sinks)


def _swa_decode_kernel(q_ref, kvn_ref, kbuf_ref, vbuf_ref, bbuf_ref, bnew_ref, sink_ref, o_ref,
                       *, dec_seq):
    nb = DEC_SEQ_BLOCK
    t = dec_seq
    rows = nb * t
    m_rows = SWA_GROUP * rows
    wb = kbuf_ref.shape[1]
    ri = lax.broadcasted_iota(jnp.int32, (m_rows, 1), 0)
    row_seq = (ri % rows) // t
    row_grp = ri // rows
    for h in range(SWA_KV_HEADS):
        lo, hi = h * SWA_HEAD_DIM, (h + 1) * SWA_HEAD_DIM
        q = jnp.concatenate(
            [q_ref[:, (h * SWA_GROUP + g) * SWA_HEAD_DIM:(h * SWA_GROUP + g + 1) * SWA_HEAD_DIM]
             for g in range(SWA_GROUP)], axis=0)
        sink = jnp.zeros((m_rows, 1), F32)
        for g in range(SWA_GROUP):
            sink = jnp.where(row_grp == g, sink_ref[h * SWA_GROUP + g], sink)
        k_new = jnp.concatenate([kvn_ref[:, lo:hi], jnp.zeros((m_rows - rows, SWA_HEAD_DIM), F32)], axis=0)
        v_new = jnp.concatenate([kvn_ref[:, SWA_KV_DIM + lo:SWA_KV_DIM + hi],
                                 jnp.zeros((m_rows - rows, SWA_HEAD_DIM), F32)], axis=0)
        s_new = _dot_nt(q, k_new) * (SWA_HEAD_DIM ** -0.5) + bnew_ref[h]
        s_buf = jnp.zeros((m_rows, wb), F32)
        for s in range(nb):
            s_full = _dot_nt(q, kbuf_ref[s, :, lo:hi])
            s_buf = jnp.where(row_seq == s, s_full, s_buf)
        s_buf = s_buf * (SWA_HEAD_DIM ** -0.5) + bbuf_ref[h]
        m = jnp.maximum(jnp.maximum(jnp.max(s_buf, axis=-1, keepdims=True),
                                    jnp.max(s_new, axis=-1, keepdims=True)), sink)
        p_buf = jnp.exp(s_buf - m)
        p_new = jnp.exp(s_new - m)
        denom = (jnp.sum(p_buf, axis=-1, keepdims=True) + jnp.sum(p_new, axis=-1, keepdims=True)
                 + jnp.exp(sink - m))
        o = _dot(p_new, v_new)
        for s in range(nb):
            o = o + _dot(jnp.where(row_seq == s, p_buf, 0.0), vbuf_ref[s, :, lo:hi])
        o = o / denom
        for g in range(SWA_GROUP):
            hg = h * SWA_GROUP + g
            o_ref[:, hg * SWA_HEAD_DIM:(hg + 1) * SWA_HEAD_DIM] = o[g * rows:(g + 1) * rows]


def _swa_decode_call(qs, kvn, kbuf, vbuf, bias_buf, bias_new, sinks, dec_batch, dec_seq):
    nb = DEC_SEQ_BLOCK
    rows = nb * dec_seq
    wb = kbuf.shape[1]
    m_rows = SWA_GROUP * rows
    return pl.pallas_call(
        functools.partial(_swa_decode_kernel, dec_seq=dec_seq),
        grid=(dec_batch // nb,),
        in_specs=[pl.BlockSpec((rows, SWA_Q_DIM), lambda i: (i, 0)),
                  pl.BlockSpec((rows, 2 * SWA_KV_DIM), lambda i: (i, 0)),
                  pl.BlockSpec((nb, wb, SWA_KV_DIM), lambda i: (i, 0, 0)),
                  pl.BlockSpec((nb, wb, SWA_KV_DIM), lambda i: (i, 0, 0)),
                  _resident((SWA_KV_HEADS, m_rows, wb)),
                  _resident((SWA_KV_HEADS, m_rows, m_rows)),
                  pl.BlockSpec(memory_space=pltpu.SMEM)],
        out_specs=pl.BlockSpec((rows, SWA_Q_DIM), lambda i: (i, 0)),
        out_shape=jax.ShapeDtypeStruct((dec_batch * dec_seq, SWA_Q_DIM), F32),
        compiler_params=pltpu.CompilerParams(dimension_semantics=("arbitrary",),
                                             vmem_limit_bytes=VMEM_LIMIT),
        name="swa_decode",
    )(qs, kvn, kbuf, vbuf, bias_buf, bias_new, sinks)


def _prompt_bias(rel_bias):
    w = SWA_WINDOW
    dist = w + jnp.arange(w)[:, None] - jnp.arange(2 * w)[None, :]
    valid = (dist >= 0) & (dist < w)
    return _bias_table_call(_t5_bucket(dist, valid), rel_bias)


def _decode_bias(rel_bias, wb, dec_seq):
    nb, t = DEC_SEQ_BLOCK, dec_seq
    rows = nb * t
    m_rows = SWA_GROUP * rows
    r = jnp.arange(m_rows)
    r_tok, r_seq = r % t, (r % rows) // t
    dist_b = wb + r_tok[:, None] - jnp.arange(wb)[None, :]
    tab_b = _bias_table_call(_t5_bucket(dist_b, (dist_b >= 0) & (dist_b < SWA_WINDOW)), rel_bias)
    cidx = jnp.arange(m_rows)
    dist_n = r_tok[:, None] - (cidx % t)[None, :]
    ok_n = (dist_n >= 0) & (dist_n < SWA_WINDOW) & (r_seq[:, None] == (cidx // t)[None, :]) & (cidx < rows)[None, :]
    tab_n = _bias_table_call(_t5_bucket(dist_n, ok_n), rel_bias)

    def pick(tab):
        return jnp.stack([jnp.concatenate([tab[h * SWA_GROUP + g, g * rows:(g + 1) * rows]
                                           for g in range(SWA_GROUP)], axis=0)
                          for h in range(SWA_KV_HEADS)])
    return pick(tab_b), pick(tab_n)


def _pack_layer(i, norm_ffn1_pre, norm_ffn1_post, ffn1_w_gate, ffn1_w_up, ffn1_w_down, norm_mix_pre,
                norm_mix_post, w_in, conv_w, gdn_a_log, gdn_dt_bias, gdn_norm, swa_sinks, w_out,
                norm_ffn2_pre, norm_ffn2_post, ffn2_w_gate, ffn2_w_up, ffn2_w_down, ple_gate, ple_proj,
                norm_ple_post):
    row = lambda g: g[i].reshape(1, -1).astype(F32)
    win = w_in[i]
    n_gdn = QKVZ_DIM + 2 * GDN_HEADS
    win = jnp.concatenate([win[:, :n_gdn], jnp.zeros((D_MODEL, BA_DIM - 2 * GDN_HEADS), win.dtype),
                           win[:, n_gdn:]], axis=1)
    lane_pad = lambda v: jnp.zeros((1, BA_DIM), F32).at[0, GDN_HEADS:2 * GDN_HEADS].set(v[i].astype(F32))
    kk = np.arange(2 * LANES)[:, None] % LANES
    sel = (kk == (np.arange(2 * GDN_HEADS * LANES)[None, :] // LANES)).astype(np.float32)
    return dict(
        ones=jnp.ones((LANES, LANES), F32), sel=jnp.asarray(sel, BF16),
        g1pre=row(norm_ffn1_pre), g1post=row(norm_ffn1_post),
        wg1=ffn1_w_gate[i].astype(BF16), wu1=ffn1_w_up[i].astype(BF16), wd1=ffn1_w_down[i].astype(BF16),
        gmix=row(norm_mix_pre), gmixpost=row(norm_mix_post), win=win.astype(BF16),
        conv_w=conv_w[i].astype(F32), alog_row=lane_pad(gdn_a_log), dtb_row=lane_pad(gdn_dt_bias),
        gnorm=row(gdn_norm), sinks=swa_sinks[i].astype(F32), wo=w_out[i].astype(BF16),
        g2pre=row(norm_ffn2_pre), g2post=row(norm_ffn2_post),
        wg2=ffn2_w_gate[i].astype(BF16), wu2=ffn2_w_up[i].astype(BF16), wd2=ffn2_w_down[i].astype(BF16),
        wpg=ple_gate[i].astype(BF16), wpp=ple_proj[i].astype(BF16), gple=row(norm_ple_post))


def kernel(x_prompt, x_sample, state_conv, state_gdn, cache_swa_k, cache_swa_v, p_prompt, p_sample,
           rel_bias, norm_ffn1_pre, norm_ffn1_post, ffn1_w_gate, ffn1_w_up, ffn1_w_down,
           norm_mix_pre, norm_mix_post, w_in, conv_w, gdn_a_log, gdn_dt_bias, gdn_norm, swa_sinks,
           w_out, norm_ffn2_pre, norm_ffn2_post, ffn2_w_gate, ffn2_w_up, ffn2_w_down,
           ple_gate, ple_proj, norm_ple_post):
    depth = w_in.shape[0]
    batch, seq, _ = x_prompt.shape
    dec_batch, dec_seq, _ = x_sample.shape
    wb = cache_swa_k.shape[2]
    wp = min(SWA_WINDOW, seq)
    rel_bias = rel_bias.astype(F32)
    bias_p = _prompt_bias(rel_bias)
    bias_db, bias_dn = _decode_bias(rel_bias, wb, dec_seq)

    yp = x_prompt.reshape(batch * seq, D_MODEL)
    ys = x_sample.reshape(dec_batch * dec_seq, D_MODEL)
    outs = [[] for _ in range(8)]
    for i in range(depth):
        lw = _pack_layer(i, norm_ffn1_pre, norm_ffn1_post, ffn1_w_gate, ffn1_w_up, ffn1_w_down,
                         norm_mix_pre, norm_mix_post, w_in, conv_w, gdn_a_log, gdn_dt_bias, gdn_norm,
                         swa_sinks, w_out, norm_ffn2_pre, norm_ffn2_post, ffn2_w_gate, ffn2_w_up,
                         ffn2_w_down, ple_gate, ple_proj, norm_ple_post)
        x1, qkvz, ba, qs, kv = _head_call(yp, lw)
        gdn_o, s_fin = _gdn_prompt_call(qkvz, ba, lw, batch, seq)
        swa_o = _swa_prompt_call(qs, kv, bias_p, lw["sinks"], batch, seq)
        yp = _tail_call(x1, gdn_o, swa_o, p_prompt[i].reshape(batch * seq, PLE_DIM), lw)
        kv3 = kv.reshape(batch, seq, 2 * SWA_KV_DIM)
        outs[0].append(qkvz.reshape(batch, seq, QKVZ_DIM)[:, seq - (CONV_WIDTH - 1):, :CONV_DIM])
        outs[1].append(s_fin)
        outs[2].append(kv3[:, seq - wp:, :SWA_KV_DIM].reshape(batch, wp, SWA_KV_HEADS, SWA_HEAD_DIM))
        outs[3].append(kv3[:, seq - wp:, SWA_KV_DIM:].reshape(batch, wp, SWA_KV_HEADS, SWA_HEAD_DIM))
        x1, qkvz, ba, qs, kv = _head_call(ys, lw)
        gdn_o, s_new = _gdn_decode_call(qkvz, ba, state_conv[i], state_gdn[i], lw, dec_batch, dec_seq)
        kbuf = cache_swa_k[i].reshape(dec_batch, wb, SWA_KV_DIM)
        vbuf = cache_swa_v[i].reshape(dec_batch, wb, SWA_KV_DIM)
        swa_o = _swa_decode_call(qs, kv, kbuf, vbuf, bias_db, bias_dn, lw["sinks"], dec_batch, dec_seq)
        ys = _tail_call(x1, gdn_o, swa_o, p_sample[i].reshape(dec_batch * dec_seq, PLE_DIM), lw)
        kv3 = kv.reshape(dec_batch, dec_seq, 2 * SWA_KV_DIM)
        xp = jnp.concatenate([state_conv[i], qkvz.reshape(dec_batch, dec_seq, QKVZ_DIM)[:, :, :CONV_DIM]], axis=1)
        outs[4].append(xp[:, dec_seq:])
        outs[5].append(s_new)
        outs[6].append(jnp.concatenate([kbuf, kv3[:, :, :SWA_KV_DIM]], axis=1)[:, dec_seq:]
                       .reshape(dec_batch, wb, SWA_KV_HEADS, SWA_HEAD_DIM))
        outs[7].append(jnp.concatenate([vbuf, kv3[:, :, SWA_KV_DIM:]], axis=1)[:, dec_seq:]
                       .reshape(dec_batch, wb, SWA_KV_HEADS, SWA_HEAD_DIM))
    return (yp.reshape(batch, seq, D_MODEL), ys.reshape(dec_batch, dec_seq, D_MODEL),
            *[jnp.stack(o) for o in outs])
```

```python
import functools
import math

import numpy as np
import jax
import jax.numpy as jnp
from jax import lax
from jax.experimental import pallas as pl
from jax.experimental.pallas import tpu as pltpu

F32 = jnp.float32
BF16 = jnp.bfloat16

D_MODEL = 1024
NORM_EPS = 1e-6
PLE_DIM = 256
FFN_DIM = 2816
GDN_HEADS = 4
GDN_DK = 128
GDN_DV = 128
GDN_KEY_DIM = GDN_HEADS * GDN_DK
GDN_VAL_DIM = GDN_HEADS * GDN_DV
CONV_DIM = 2 * GDN_KEY_DIM + GDN_VAL_DIM
CONV_WIDTH = 4
SWA_HEADS = 8
SWA_KV_HEADS = 2
SWA_GROUP = SWA_HEADS // SWA_KV_HEADS
SWA_HEAD_DIM = 64
SWA_Q_DIM = SWA_HEADS * SWA_HEAD_DIM
SWA_KV_DIM = SWA_KV_HEADS * SWA_HEAD_DIM
SWA_WINDOW = 128
NUM_BUCKETS = 32
REL_MAX_DISTANCE = 128

QKVZ_DIM = CONV_DIM + GDN_VAL_DIM
BA_DIM = 128
PROJ_PACKED = QKVZ_DIM + BA_DIM + SWA_Q_DIM + 2 * SWA_KV_DIM

GDN_CHUNK = 64
GDN_STEP_CHUNKS = 4
SWA_STEP_BLOCKS = 4
ROW_TILE = 512
DEC_SEQ_BLOCK = 16
CAST_ROWS = 128
VMEM_LIMIT = 56 * 1024 * 1024

NEG_INF = float("-inf")
LANES = 128


def _resident(shape):
    nd = len(shape)
    return pl.BlockSpec(shape, lambda *_: (0,) * nd, pipeline_mode=pl.Buffered(1))


def _rms(x, gain):
    ms = jnp.mean(x * x, axis=-1, keepdims=True)
    return (x * lax.rsqrt(ms + NORM_EPS)) * gain


def _sigmoid(x):
    return 1.0 / (1.0 + jnp.exp(-x))


def _silu(x):
    h = 0.5 * x
    return h + h * jnp.tanh(h)


def _dot(a, b):
    return jnp.dot(a, b, preferred_element_type=F32)


def _dot_nt(a, b):
    return lax.dot_general(a, b, (((1,), (1,)), ((), ())), preferred_element_type=F32)


def _dot_tn(a, b):
    return lax.dot_general(a, b, (((0,), (0,)), ((), ())), preferred_element_type=F32)


def _split(a):
    hi = a.astype(BF16)
    lo = (a - hi.astype(F32)).astype(BF16)
    return hi, lo


def _inv_rms(x):
    return lax.rsqrt(jnp.mean(x * x, axis=-1, keepdims=True) + NORM_EPS)


def _prenorm_dots(x, gain, w_refs):
    h = (x * gain).astype(BF16)
    inv = _inv_rms(x)
    return [jnp.dot(h, w, preferred_element_type=F32) * inv for w in w_refs]


def _swiglu_block(x, gain, wg_ref, wu_ref, wd_ref):
    g, u = _prenorm_dots(x, gain, [wg_ref[...], wu_ref[...]])
    a = (_silu(g) * u).astype(BF16)
    return jnp.dot(a, wd_ref[...], preferred_element_type=F32)


def _head_kernel(x_ref, g1pre_ref, g1post_ref, wg_ref, wu_ref, wd_ref, gmix_ref, win_ref,
                 x1_ref, qkvz_ref, ba_ref, qs_ref, kv_ref):
    x = x_ref[...]
    y = _swiglu_block(x, g1pre_ref[...], wg_ref, wu_ref, wd_ref)
    x1 = x + 0.5 * _rms(y, g1post_ref[...])
    x1_ref[...] = x1
    c0, c1, c2 = QKVZ_DIM, QKVZ_DIM + BA_DIM, QKVZ_DIM + BA_DIM + SWA_Q_DIM
    qkvz_ref[...], ba_ref[...], qs_ref[...], kv_ref[...] = _prenorm_dots(
        x1, gmix_ref[...], [win_ref[:, :c0], win_ref[:, c0:c1], win_ref[:, c1:c2], win_ref[:, c2:]])


def _head_call(x, lw):
    n = x.shape[0]
    tm = min(ROW_TILE, n)
    row = lambda w: pl.BlockSpec((tm, w), lambda i: (i, 0))
    return pl.pallas_call(
        _head_kernel,
        grid=(n // tm,),
        in_specs=[row(D_MODEL), _resident((1, D_MODEL)), _resident((1, D_MODEL)),
                  _resident((D_MODEL, FFN_DIM)), _resident((D_MODEL, FFN_DIM)),
                  _resident((FFN_DIM, D_MODEL)), _resident((1, D_MODEL)),
                  _resident((D_MODEL, PROJ_PACKED))],
        out_specs=[row(D_MODEL), row(QKVZ_DIM), row(BA_DIM), row(SWA_Q_DIM), row(2 * SWA_KV_DIM)],
        out_shape=[jax.ShapeDtypeStruct((n, D_MODEL), F32), jax.ShapeDtypeStruct((n, QKVZ_DIM), F32),
                   jax.ShapeDtypeStruct((n, BA_DIM), F32), jax.ShapeDtypeStruct((n, SWA_Q_DIM), F32),
                   jax.ShapeDtypeStruct((n, 2 * SWA_KV_DIM), F32)],
        compiler_params=pltpu.CompilerParams(dimension_semantics=("arbitrary",),
                                             vmem_limit_bytes=VMEM_LIMIT),
        name="ffn1_inproj",
    )(x, lw["g1pre"], lw["g1post"], lw["wg1"], lw["wu1"], lw["wd1"], lw["gmix"], lw["win"])


def _tail_kernel(x_ref, gdn_ref, swa_ref, p_ref, wo_ref, gmixpost_ref, g2pre_ref, g2post_ref,
                 wg_ref, wu_ref, wd_ref, wpg_ref, wpp_ref, gple_ref, y_ref):
    x = x_ref[...]
    mix = (jnp.dot(gdn_ref[...].astype(BF16), wo_ref[:GDN_VAL_DIM, :], preferred_element_type=F32)
           + jnp.dot(swa_ref[...].astype(BF16), wo_ref[GDN_VAL_DIM:, :], preferred_element_type=F32))
    pp = jnp.dot(p_ref[...].astype(BF16), wpp_ref[...], preferred_element_type=F32)
    x = x + _rms(mix, gmixpost_ref[...])
    y = _swiglu_block(x, g2pre_ref[...], wg_ref, wu_ref, wd_ref)
    x = x + 0.5 * _rms(y, g2post_ref[...])
    gate = _sigmoid(jnp.dot(x.astype(BF16), wpg_ref[...], preferred_element_type=F32))
    y_ref[...] = x + _rms(gate * pp, gple_ref[...])


def _tail_call(x1, gdn_o, swa_o, p, lw):
    n = x1.shape[0]
    tm = min(ROW_TILE, n)
    row = lambda w: pl.BlockSpec((tm, w), lambda i: (i, 0))
    return pl.pallas_call(
        _tail_kernel,
        grid=(n // tm,),
        in_specs=[row(D_MODEL), row(GDN_VAL_DIM), row(SWA_Q_DIM), row(PLE_DIM),
                  _resident((GDN_VAL_DIM + SWA_Q_DIM, D_MODEL)), _resident((1, D_MODEL)),
                  _resident((1, D_MODEL)), _resident((1, D_MODEL)),
                  _resident((D_MODEL, FFN_DIM)), _resident((D_MODEL, FFN_DIM)),
                  _resident((FFN_DIM, D_MODEL)), _resident((D_MODEL, D_MODEL)),
                  _resident((PLE_DIM, D_MODEL)), _resident((1, D_MODEL))],
        out_specs=row(D_MODEL),
        out_shape=jax.ShapeDtypeStruct((n, D_MODEL), F32),
        compiler_params=pltpu.CompilerParams(dimension_semantics=("arbitrary",),
                                             vmem_limit_bytes=VMEM_LIMIT),
        name="outproj_ffn2_ple",
    )(x1, gdn_o, swa_o, p, lw["wo"], lw["gmixpost"], lw["g2pre"], lw["g2post"],
      lw["wg2"], lw["wu2"], lw["wd2"], lw["wpg"], lw["wpp"], lw["gple"])


def _iota2(c):
    return (lax.broadcasted_iota(jnp.int32, (c, c), 0), lax.broadcasted_iota(jnp.int32, (c, c), 1))


def _gates(ba, alog_row, dtb_row):
    beta = _sigmoid(ba)
    xa = ba + dtb_row
    softplus = jnp.maximum(xa, 0.0) + jnp.log1p(jnp.exp(-jnp.abs(xa)))
    g = -jnp.exp(alog_row) * softplus
    return beta, g


def _cumsum_rows(mask01, g):
    hi, lo = _split(g)
    m = mask01.astype(BF16)
    return jnp.dot(m, hi, preferred_element_type=F32) + jnp.dot(m, lo, preferred_element_type=F32)


def _hi_lo_lanes(x):
    hi, lo = _split(x)
    return jnp.concatenate([hi, lo], axis=1)


def _rowsum_bcast(x, ones_ref):
    return jnp.dot(x, ones_ref[...], preferred_element_type=F32)


def _lane_bcast(x, sel_ref, lanes):
    out = jnp.dot(_hi_lo_lanes(x), sel_ref[...], preferred_element_type=F32)
    return [out[:, l * 128:(l + 1) * 128] for l in lanes]


def _gdn_prep_steps(out, y, beta_b, gc_b, gl_b, gc_t, chunk, group, ones_ref, normalized):
    c = chunk
    n_chunks = y.shape[0] // c
    units = [(ci, h) for ci in range(n_chunks) for h in range(GDN_HEADS)]
    n = range(len(units))
    ii, jj = _iota2(c)
    same = (ii // group) == (jj // group)
    incl = jnp.logical_and(same, ii >= jj)
    strict = jnp.logical_and(same, ii > jj)

    def rows(t, ci, lo, hi):
        return t[ci * c:(ci + 1) * c, lo:hi]

    eg_b = [jnp.exp(t) for t in gc_b]
    ekd_b = [jnp.exp(gl_b[h] - gc_b[h]) for h in range(GDN_HEADS)]
    beta_u = [rows(beta_b[h], ci, 0, GDN_DK) for ci, h in units]
    eg_u = [rows(eg_b[h], ci, 0, GDN_DK) for ci, h in units]
    ekd_u = [rows(ekd_b[h], ci, 0, GDN_DK) for ci, h in units]
    gc_col = [rows(gc_b[h], ci, 0, c) for ci, h in units]
    gc_row = [gc_t[GDN_HEADS + h:GDN_HEADS + h + 1, ci * c:(ci + 1) * c] for ci, h in units]
    q = [rows(y, ci, h * GDN_DK, (h + 1) * GDN_DK) for ci, h in units]
    k = [rows(y, ci, GDN_KEY_DIM + h * GDN_DK, GDN_KEY_DIM + (h + 1) * GDN_DK) for ci, h in units]
    v = [rows(y, ci, 2 * GDN_KEY_DIM + h * GDN_DV, 2 * GDN_KEY_DIM + (h + 1) * GDN_DV) for ci, h in units]
    if not normalized:
        q = [_l2norm(t, ones_ref) * (GDN_DK ** -0.5) for t in q]
        k = [_l2norm(t, ones_ref) for t in k]
    decay = [jnp.exp(jnp.where(incl, gc_col[i] - gc_row[i], NEG_INF)) for i in n]
    kb = [k[i] * beta_u[i] for i in n]
    kq = [_dot_nt(jnp.concatenate([kb[i], q[i]], axis=0), k[i]) for i in n]
    yield
    a_mat = [jnp.where(strict, kq[i][:c] * decay[i], 0.0) for i in n]
    qk = [kq[i][c:] * decay[i] for i in n]
    eye = jnp.where(ii == jj, 1.0, 0.0).astype(F32)
    t_mat = [eye for _ in n]
    b = 1
    while b < group:
        lower = jnp.logical_and((ii // (2 * b)) == (jj // (2 * b)),
                                jnp.logical_and((ii % (2 * b)) >= b, (jj % (2 * b)) < b))
        m = [jnp.where(lower, a_mat[i], 0.0) for i in n]
        if b == 1:
            t_mat = [eye - m[i] for i in n]
        else:
            tm = [_dot(t_mat[i], m[i]) for i in n]
            yield
            t_mat = [t_mat[i] - _dot(tm[i], t_mat[i]) for i in n]
            yield
        b *= 2
    rhs = [jnp.concatenate([v[i] * beta_u[i], kb[i] * eg_u[i]], axis=-1) for i in n]
    sol = [_dot(t_mat[i], rhs[i]) for i in n]
    yield
    nest = lambda xs: [xs[ci * GDN_HEADS:(ci + 1) * GDN_HEADS] for ci in range(n_chunks)]
    out.update(u=nest([s[:, :GDN_DV] for s in sol]), w=nest([s[:, GDN_DV:] for s in sol]), qk=nest(qk),
               qd=nest([q[i] * eg_u[i] for i in n]), kd=nest([k[i] * ekd_u[i] for i in n]))


def _l2norm(t, ones_ref):
    return t * lax.rsqrt(_rowsum_bcast(t * t, ones_ref) + 1e-6)


def _gdn_out(o, z, gnorm_row, ones_ref):
    ms = _rowsum_bcast(o * o, ones_ref) * (1.0 / GDN_DV)
    return (o * lax.rsqrt(ms + NORM_EPS)) * gnorm_row * _silu(z)


def _conv_silu(xbuf_ref, cw_ref, c):
    y = xbuf_ref[pl.ds(5, c), :] * cw_ref[0:1, :]
    for j in range(1, CONV_WIDTH):
        y = y + xbuf_ref[pl.ds(5 + j, c), :] * cw_ref[j:j + 1, :]
    return _silu(y)


def _gdn_prompt_kernel(qkv_ref, z_ref, ba_ref, cw_ref, alog_ref, dtb_ref, gnorm_ref, ones_ref, sel_ref,
                       o_ref, sfin_ref, s_ref, xbuf_ref, y0_ref, g0_ref, gt0_ref, y1_ref, g1_ref, gt1_ref):
    step = pl.program_id(1)

    @pl.when(step == 0)
    def _():
        s_ref[...] = jnp.zeros_like(s_ref)
        xbuf_ref[0:8, :] = jnp.zeros((8, CONV_DIM), F32)
        y1_ref[...] = jnp.zeros_like(y1_ref)
        g1_ref[...] = jnp.zeros_like(g1_ref)
        gt1_ref[...] = jnp.zeros_like(gt1_ref)

    @pl.when(lax.rem(step, 2) == 0)
    def _():
        _gdn_prompt_step((y0_ref, g0_ref, gt0_ref), (y1_ref, g1_ref, gt1_ref), qkv_ref, z_ref, ba_ref, cw_ref,
                         alog_ref, dtb_ref, gnorm_ref, ones_ref, sel_ref, o_ref, sfin_ref, s_ref, xbuf_ref)

    @pl.when(lax.rem(step, 2) == 1)
    def _():
        _gdn_prompt_step((y1_ref, g1_ref, gt1_ref), (y0_ref, g0_ref, gt0_ref), qkv_ref, z_ref, ba_ref, cw_ref,
                         alog_ref, dtb_ref, gnorm_ref, ones_ref, sel_ref, o_ref, sfin_ref, s_ref, xbuf_ref)


def _gdn_prompt_step(a_refs, b_refs, qkv_ref, z_ref, ba_ref, cw_ref, alog_ref, dtb_ref, gnorm_ref, ones_ref,
                     sel_ref, o_ref, sfin_ref, s_ref, xbuf_ref):
    ya_ref, ga_ref, gta_ref = a_refs
    yb_ref, gb_ref, gtb_ref = b_refs
    c = GDN_CHUNK
    n_chunks = GDN_STEP_CHUNKS
    r = n_chunks * c
    hs = range(GDN_HEADS)

    def stage_a():
        xbuf_ref[8:8 + r, :] = qkv_ref[...]
        beta, g = _gates(ba_ref[...], alog_ref[...], dtb_ref[...])
        ii, jj = _iota2(r)
        tril = jnp.where(jnp.logical_and((ii // c) == (jj // c), ii >= jj), 1.0, 0.0)
        gc = _cumsum_rows(tril, g)
        gta_ref[...] = gc.T
        lane = lax.broadcasted_iota(jnp.int32, (r, BA_DIM), 1)
        ga_ref[...] = jnp.where(lane < GDN_HEADS, beta, gc)
        yield
        for ci in range(n_chunks):
            rows = slice(ci * c, (ci + 1) * c)
            for slab in range(CONV_DIM // LANES):
                cols = slice(slab * LANES, (slab + 1) * LANES)
                ext = xbuf_ref[pl.ds(ci * c, c + 8), cols]
                s1 = pltpu.roll(ext, 1, axis=0)
                u2 = pltpu.roll(ext * cw_ref[1:2, cols] + s1 * cw_ref[0:1, cols], 2, axis=0)
                yc = (ext * cw_ref[3:4, cols] + s1 * cw_ref[2:3, cols] + u2)[8:]
                ya_ref[rows, cols] = _silu(yc)
                if slab < GDN_KEY_DIM // LANES:
                    ya_ref[rows, cols] = _l2norm(ya_ref[rows, cols], ones_ref) * (GDN_DK ** -0.5)
                elif slab < 2 * GDN_KEY_DIM // LANES:
                    ya_ref[rows, cols] = _l2norm(ya_ref[rows, cols], ones_ref)
                yield
        xbuf_ref[0:8, :] = xbuf_ref[r:r + 8, :]
        yield

    def stage_b():
        bcast = _lane_bcast(gb_ref[...], sel_ref, range(2 * GDN_HEADS))
        beta_b, gc_b = bcast[:GDN_HEADS], bcast[GDN_HEADS:]
        glast = [[gc_b[h][(ci + 1) * c - 1:(ci + 1) * c, :] for h in hs] for ci in range(n_chunks)]
        gl_b = [jnp.concatenate([jnp.broadcast_to(glast[ci][h], (c, GDN_DK)) for ci in range(n_chunks)],
                                axis=0) for h in hs]
        wy = {}
        yield from _gdn_prep_steps(wy, yb_ref, beta_b, gc_b, gl_b, gtb_ref[...], c, c, ones_ref,
                                   normalized=True)
        u, w, qk, qd, kd = (wy[name] for name in ("u", "w", "qk", "qd", "kd"))
        s_cur = [s_ref[h] for h in hs]
        for ci in range(n_chunks):
            ws = [_dot(jnp.concatenate([w[ci][h], qd[ci][h]], axis=0), s_cur[h]) for h in hs]
            yield
            v_new = [u[ci][h] - ws[h][:c] for h in hs]
            o = [ws[h][c:] + _dot(qk[ci][h], v_new[h]) for h in hs]
            s_cur = [s_cur[h] * jnp.exp(glast[ci][h]) + _dot_tn(kd[ci][h], v_new[h]) for h in hs]
            yield
            for h in hs:
                z = z_ref[ci * c:(ci + 1) * c, h * GDN_DV:(h + 1) * GDN_DV]
                o_ref[ci * c:(ci + 1) * c, h * GDN_DV:(h + 1) * GDN_DV] = _gdn_out(o[h], z, gnorm_ref[...], ones_ref)
        for h in hs:
            s_ref[h] = s_cur[h]
            sfin_ref[0, h] = s_cur[h]

    a_pieces = 2 + n_chunks * (CONV_DIM // LANES)
    b_levels = 2 * (int(math.log2(c)) - 1) + 2 + 2 * n_chunks
    a_steps = stage_a()
    done = 0
    for i, _ in enumerate(stage_b()):
        target = -(-(i + 1) * a_pieces // b_levels)
        for _ in range(target - done):
            next(a_steps, None)
        done = target
    for _ in a_steps:
        pass


def _gdn_prompt_call(qkvz, ba, lw, batch, seq):
    c = GDN_CHUNK * GDN_STEP_CHUNKS
    nc = seq // c
    z_col = CONV_DIM // GDN_VAL_DIM
    return pl.pallas_call(
        _gdn_prompt_kernel,
        grid=(batch, nc + 1),
        in_specs=[pl.BlockSpec((c, CONV_DIM), lambda b, s: (b * nc + jnp.minimum(s, nc - 1), 0)),
                  pl.BlockSpec((c, GDN_VAL_DIM), lambda b, s: (b * nc + jnp.maximum(s - 1, 0), z_col)),
                  pl.BlockSpec((c, BA_DIM), lambda b, s: (b * nc + jnp.minimum(s, nc - 1), 0)),
                  _resident((CONV_WIDTH, CONV_DIM)), _resident((1, BA_DIM)), _resident((1, BA_DIM)),
                  _resident((1, GDN_DV)), _resident(lw["ones"].shape), _resident(lw["sel"].shape)],
        out_specs=[pl.BlockSpec((c, GDN_VAL_DIM), lambda b, s: (b * nc + jnp.maximum(s - 1, 0), 0)),
                   pl.BlockSpec((1, GDN_HEADS, GDN_DK, GDN_DV), lambda b, s: (b, 0, 0, 0))],
        out_shape=[jax.ShapeDtypeStruct((batch * seq, GDN_VAL_DIM), F32),
                   jax.ShapeDtypeStruct((batch, GDN_HEADS, GDN_DK, GDN_DV), F32)],
        scratch_shapes=[pltpu.VMEM((GDN_HEADS, GDN_DK, GDN_DV), F32),
                        pltpu.VMEM((c + 8, CONV_DIM), F32),
                        pltpu.VMEM((c, CONV_DIM), F32), pltpu.VMEM((c, BA_DIM), F32), pltpu.VMEM((BA_DIM, c), F32),
                        pltpu.VMEM((c, CONV_DIM), F32), pltpu.VMEM((c, BA_DIM), F32), pltpu.VMEM((BA_DIM, c), F32)],
        compiler_params=pltpu.CompilerParams(dimension_semantics=("arbitrary", "arbitrary"),
                                             vmem_limit_bytes=VMEM_LIMIT),
        name="gdn_prompt",
    )(qkvz, qkvz, ba, lw["conv_w"], lw["alog_row"], lw["dtb_row"], lw["gnorm"], lw["ones"], lw["sel"])


def _gdn_decode_kernel(qkvz_ref, ba_ref, sconv_ref, s0_ref, cw_ref, alog_ref, dtb_ref, gnorm_ref,
                       ones_ref, sel_ref, o_ref, snew_ref, xbuf_ref, *, dec_seq):
    nb = DEC_SEQ_BLOCK
    t = dec_seq
    c = nb * t
    ys = []
    for s in range(nb):
        xbuf_ref[s, 5:8, :] = sconv_ref[s]
        xbuf_ref[s, 8:8 + t, :] = qkvz_ref[s * t:(s + 1) * t, :CONV_DIM]
        ys.append(_conv_silu(xbuf_ref.at[s], cw_ref, t))
    y = jnp.concatenate(ys, axis=0)

    beta, g = _gates(ba_ref[...], alog_ref[...], dtb_ref[...])
    ii, jj = _iota2(c)
    same = (ii // t) == (jj // t)
    tril = jnp.where(jnp.logical_and(same, ii >= jj), 1.0, 0.0).astype(F32)
    ones = jnp.where(same, 1.0, 0.0).astype(F32)
    gc = _cumsum_rows(tril, g)
    gl = _cumsum_rows(ones, g)
    gc_t = gc.T
    lane = lax.broadcasted_iota(jnp.int32, (c, BA_DIM), 1)
    bcast = _lane_bcast(jnp.where(lane < GDN_HEADS, beta, gc), sel_ref, range(2 * GDN_HEADS))
    beta_b, gc_b = bcast[:GDN_HEADS], bcast[GDN_HEADS:]
    gl_b = _lane_bcast(gl, sel_ref, range(GDN_HEADS, 2 * GDN_HEADS))
    wy = {}
    for _ in _gdn_prep_steps(wy, y, beta_b, gc_b, gl_b, gc_t, c, t, ones_ref, normalized=False):
        pass
    u, w, qk, qd, kd = (wy[name][0] for name in ("u", "w", "qk", "qd", "kd"))
    hs = range(GDN_HEADS)
    units = [(s, h) for h in hs for s in range(nb)]
    s_old = {(s, h): s0_ref[s, h] for s, h in units}
    ws = {(s, h): _dot(jnp.concatenate([w[h][s * t:(s + 1) * t], qd[h][s * t:(s + 1) * t]], axis=0),
                       s_old[s, h]) for s, h in units}
    v_new = [jnp.concatenate([u[h][s * t:(s + 1) * t] - ws[s, h][:t] for s in range(nb)], axis=0) for h in hs]
    o = [jnp.concatenate([ws[s, h][t:] for s in range(nb)], axis=0) + _dot(qk[h], v_new[h]) for h in hs]
    cd = [jnp.exp(gl_b[h]) for h in hs]
    kd_t = [kd[h].T for h in hs]
    col_seq = lax.broadcasted_iota(jnp.int32, (GDN_DK, c), 1) // t
    for s, h in units:
        upd = _dot(jnp.where(col_seq == s, kd_t[h], 0.0), v_new[h])
        snew_ref[s, h] = s_old[s, h] * cd[h][s * t:s * t + 1, :] + upd
    for h in hs:
        z = qkvz_ref[:, CONV_DIM + h * GDN_DV:CONV_DIM + (h + 1) * GDN_DV]
        o_ref[:, h * GDN_DV:(h + 1) * GDN_DV] = _gdn_out(o[h], z, gnorm_ref[...], ones_ref)


def _gdn_decode_call(qkvz, ba, state_conv, state_gdn, lw, dec_batch, dec_seq):
    nb = DEC_SEQ_BLOCK
    rows = nb * dec_seq
    return pl.pallas_call(
        functools.partial(_gdn_decode_kernel, dec_seq=dec_seq),
        grid=(dec_batch // nb,),
        in_specs=[pl.BlockSpec((rows, QKVZ_DIM), lambda i: (i, 0)),
                  pl.BlockSpec((rows, BA_DIM), lambda i: (i, 0)),
                  pl.BlockSpec((nb, CONV_WIDTH - 1, CONV_DIM), lambda i: (i, 0, 0)),
                  pl.BlockSpec((nb, GDN_HEADS, GDN_DK, GDN_DV), lambda i: (i, 0, 0, 0)),
                  _resident((CONV_WIDTH, CONV_DIM)), _resident((1, BA_DIM)), _resident((1, BA_DIM)),
                  _resident((1, GDN_DV)), _resident(lw["ones"].shape), _resident(lw["sel"].shape)],
        out_specs=[pl.BlockSpec((rows, GDN_VAL_DIM), lambda i: (i, 0)),
                   pl.BlockSpec((nb, GDN_HEADS, GDN_DK, GDN_DV), lambda i: (i, 0, 0, 0))],
        out_shape=[jax.ShapeDtypeStruct((dec_batch * dec_seq, GDN_VAL_DIM), F32),
                   jax.ShapeDtypeStruct((dec_batch, GDN_HEADS, GDN_DK, GDN_DV), F32)],
        scratch_shapes=[pltpu.VMEM((nb, 16, CONV_DIM), F32)],
        compiler_params=pltpu.CompilerParams(dimension_semantics=("arbitrary",),
                                             vmem_limit_bytes=VMEM_LIMIT),
        name="gdn_decode",
    )(qkvz, ba, state_conv, state_gdn, lw["conv_w"], lw["alog_row"], lw["dtb_row"], lw["gnorm"],
      lw["ones"], lw["sel"])


def _bias_table_kernel(bucket_ref, rb_ref, out_ref):
    bucket = bucket_ref[...]
    for h in range(SWA_HEADS):
        acc = jnp.zeros(bucket.shape, F32)
        for b in range(NUM_BUCKETS):
            acc = jnp.where(bucket == b, rb_ref[b, h], acc)
        out_ref[h] = jnp.where(bucket < 0, NEG_INF, acc)


def _bias_table_call(bucket, rel_bias):
    r, c = bucket.shape
    return pl.pallas_call(
        _bias_table_kernel,
        in_specs=[pl.BlockSpec(memory_space=pltpu.VMEM), pl.BlockSpec(memory_space=pltpu.SMEM)],
        out_specs=pl.BlockSpec(memory_space=pltpu.VMEM),
        out_shape=jax.ShapeDtypeStruct((SWA_HEADS, r, c), F32),
        name="t5_bias_table",
    )(bucket, rel_bias)


def _t5_bucket(dist, valid):
    d = jnp.maximum(dist, 0)
    exact = NUM_BUCKETS // 2
    log_ratio = jnp.log(jnp.maximum(d, 1).astype(F32) / exact) / math.log(REL_MAX_DISTANCE / exact)
    large = jnp.minimum(exact + (log_ratio * (NUM_BUCKETS - exact)).astype(jnp.int32), NUM_BUCKETS - 1)
    bucket = jnp.where(d < exact, d, large)
    return jnp.where(valid, bucket, -1).astype(jnp.int32)


def _swa_prompt_kernel(q_ref, kvp_ref, kvc_ref, bias_ref, sink_ref, o_ref):
    w, dh, nq = SWA_WINDOW, SWA_HEAD_DIM, SWA_STEP_BLOCKS
    first = pl.program_id(1) == 0
    col = lax.broadcasted_iota(jnp.int32, (w, 2 * w), 1)
    drop_prev = jnp.logical_and(first, col < w)
    kv = jnp.concatenate([kvp_ref[...], kvc_ref[...]], axis=0).astype(BF16)
    k_h = [kv[:, h * dh:(h + 1) * dh] for h in range(SWA_KV_HEADS)]
    v_h = [kv[:, SWA_KV_DIM + h * dh:SWA_KV_DIM + (h + 1) * dh] for h in range(SWA_KV_HEADS)]
    units = [(b, hg) for b in range(nq) for hg in range(SWA_HEADS)]
    n = range(len(units))
    q = [(q_ref[b * w:(b + 1) * w, hg * dh:(hg + 1) * dh] * (dh ** -0.5)).astype(BF16) for b, hg in units]
    s = [lax.dot_general(q[i], k_h[hg // SWA_GROUP][b * w:(b + 2) * w], (((1,), (1,)), ((), ())),
                         preferred_element_type=F32) + bias_ref[hg] for i, (b, hg) in enumerate(units)]
    s = [jnp.where(drop_prev, NEG_INF, s[i]) if b == 0 else s[i] for i, (b, hg) in enumerate(units)]
    sink = [sink_ref[hg] for b, hg in units]
    m = [jnp.maximum(jnp.max(s[i], axis=-1, keepdims=True), sink[i]) for i in n]
    p = [jnp.exp(s[i] - m[i]) for i in n]
    denom = [jnp.sum(p[i], axis=-1, keepdims=True) + jnp.exp(sink[i] - m[i]) for i in n]
    o = [jnp.dot(p[i].astype(BF16), v_h[hg // SWA_GROUP][b * w:(b + 2) * w], preferred_element_type=F32)
         / denom[i] for i, (b, hg) in enumerate(units)]
    for i, (b, hg) in enumerate(units):
        o_ref[b * w:(b + 1) * w, hg * dh:(hg + 1) * dh] = o[i]


def _swa_prompt_call(qs, kv, bias, sinks, batch, seq):
    w, nq = SWA_WINDOW, SWA_STEP_BLOCKS
    nb = seq // (w * nq)
    return pl.pallas_call(
        _swa_prompt_kernel,
        grid=(batch, nb),
        in_specs=[pl.BlockSpec((nq * w, SWA_Q_DIM), lambda b, n: (b * nb + n, 0)),
                  pl.BlockSpec((w, 2 * SWA_KV_DIM),
                               lambda b, n: ((b * nb + n) * nq - jnp.minimum(n, 1), 0)),
                  pl.BlockSpec((nq * w, 2 * SWA_KV_DIM), lambda b, n: (b * nb + n, 0)),
                  _resident((SWA_HEADS, w, 2 * w)),
                  pl.BlockSpec(memory_space=pltpu.SMEM)],
        out_specs=pl.BlockSpec((nq * w, SWA_Q_DIM), lambda b, n: (b * nb + n, 0)),
        out_shape=jax.ShapeDtypeStruct((batch * seq, SWA_Q_DIM), F32),
        compiler_params=pltpu.CompilerParams(dimension_semantics=("arbitrary", "arbitrary"),
                                             vmem_limit_bytes=VMEM_LIMIT),
        name="swa_prompt",
    )(qs, kv, kv, bias, sinks)


def _swa_decode_kernel(q_ref, kvn_ref, kbuf_ref, vbuf_ref, bbuf_ref, bnew_ref, sink_ref, o_ref,
                       *, dec_seq):
    nb = DEC_SEQ_BLOCK
    t = dec_seq
    rows = nb * t
    m_rows = SWA_GROUP * rows
    wb = kbuf_ref.shape[1]
    ri = lax.broadcasted_iota(jnp.int32, (m_rows, 1), 0)
    row_seq = (ri % rows) // t
    row_grp = ri // rows
    for h in range(SWA_KV_HEADS):
        lo, hi = h * SWA_HEAD_DIM, (h + 1) * SWA_HEAD_DIM
        q = jnp.concatenate(
            [q_ref[:, (h * SWA_GROUP + g) * SWA_HEAD_DIM:(h * SWA_GROUP + g + 1) * SWA_HEAD_DIM]
             for g in range(SWA_GROUP)], axis=0)
        sink = jnp.zeros((m_rows, 1), F32)
        for g in range(SWA_GROUP):
            sink = jnp.where(row_grp == g, sink_ref[h * SWA_GROUP + g], sink)
        k_new = jnp.concatenate([kvn_ref[:, lo:hi], jnp.zeros((m_rows - rows, SWA_HEAD_DIM), F32)], axis=0)
        v_new = jnp.concatenate([kvn_ref[:, SWA_KV_DIM + lo:SWA_KV_DIM + hi],
                                 jnp.zeros((m_rows - rows, SWA_HEAD_DIM), F32)], axis=0)
        s_new = _dot_nt(q, k_new) * (SWA_HEAD_DIM ** -0.5) + bnew_ref[h]
        s_buf = jnp.zeros((m_rows, wb), F32)
        for s in range(nb):
            s_full = _dot_nt(q, kbuf_ref[s, :, lo:hi])
            s_buf = jnp.where(row_seq == s, s_full, s_buf)
        s_buf = s_buf * (SWA_HEAD_DIM ** -0.5) + bbuf_ref[h]
        m = jnp.maximum(jnp.maximum(jnp.max(s_buf, axis=-1, keepdims=True),
                                    jnp.max(s_new, axis=-1, keepdims=True)), sink)
        p_buf = jnp.exp(s_buf - m)
        p_new = jnp.exp(s_new - m)
        denom = (jnp.sum(p_buf, axis=-1, keepdims=True) + jnp.sum(p_new, axis=-1, keepdims=True)
                 + jnp.exp(sink - m))
        o = _dot(p_new, v_new)
        for s in range(nb):
            o = o + _dot(jnp.where(row_seq == s, p_buf, 0.0), vbuf_ref[s, :, lo:hi])
        o = o / denom
        for g in range(SWA_GROUP):
            hg = h * SWA_GROUP + g
            o_ref[:, hg * SWA_HEAD_DIM:(hg + 1) * SWA_HEAD_DIM] = o[g * rows:(g + 1) * rows]


def _swa_decode_call(qs, kvn, kbuf, vbuf, bias_buf, bias_new, sinks, dec_batch, dec_seq):
    nb = DEC_SEQ_BLOCK
    rows = nb * dec_seq
    wb = kbuf.shape[1]
    m_rows = SWA_GROUP * rows
    return pl.pallas_call(
        functools.partial(_swa_decode_kernel, dec_seq=dec_seq),
        grid=(dec_batch // nb,),
        in_specs=[pl.BlockSpec((rows, SWA_Q_DIM), lambda i: (i, 0)),
                  pl.BlockSpec((rows, 2 * SWA_KV_DIM), lambda i: (i, 0)),
                  pl.BlockSpec((nb, wb, SWA_KV_DIM), lambda i: (i, 0, 0)),
                  pl.BlockSpec((nb, wb, SWA_KV_DIM), lambda i: (i, 0, 0)),
                  _resident((SWA_KV_HEADS, m_rows, wb)),
                  _resident((SWA_KV_HEADS, m_rows, m_rows)),
                  pl.BlockSpec(memory_space=pltpu.SMEM)],
        out_specs=pl.BlockSpec((rows, SWA_Q_DIM), lambda i: (i, 0)),
        out_shape=jax.ShapeDtypeStruct((dec_batch * dec_seq, SWA_Q_DIM), F32),
        compiler_params=pltpu.CompilerParams(dimension_semantics=("arbitrary",),
                                             vmem_limit_bytes=VMEM_LIMIT),
        name="swa_decode",
    )(qs, kvn, kbuf, vbuf, bias_buf, bias_new, sinks)


def _prompt_bias(rel_bias):
    w = SWA_WINDOW
    dist = w + jnp.arange(w)[:, None] - jnp.arange(2 * w)[None, :]
    valid = (dist >= 0) & (dist < w)
    return _bias_table_call(_t5_bucket(dist, valid), rel_bias)


def _decode_bias(rel_bias, wb, dec_seq):
    nb, t = DEC_SEQ_BLOCK, dec_seq
    rows = nb * t
    m_rows = SWA_GROUP * rows
    tok = jnp.arange(t)
    dist = jnp.concatenate([wb + tok[:, None] - jnp.arange(wb)[None, :], tok[:, None] - tok[None, :]], axis=1)
    bucket = _t5_bucket(dist, (dist >= 0) & (dist < SWA_WINDOW))
    pad_r, pad_c = -t % 8, -(wb + t) % LANES
    bucket = jnp.pad(bucket, ((0, pad_r), (0, pad_c)), constant_values=-1)
    tab = _bias_table_call(bucket, rel_bias)[:, :t, :wb + t]
    tab = tab.reshape(SWA_KV_HEADS, SWA_GROUP, 1, t, wb + t)
    per_row = jnp.broadcast_to(tab, (SWA_KV_HEADS, SWA_GROUP, nb, t, wb + t)).reshape(SWA_KV_HEADS, m_rows, wb + t)
    r_seq = (jnp.arange(m_rows) % rows) // t
    cidx = jnp.arange(m_rows)
    own = (r_seq[:, None] == (cidx // t)[None, :]) & (cidx < rows)[None, :]
    bias_new = jnp.where(own[None], jnp.tile(per_row[:, :, wb:], (1, 1, m_rows // t)), NEG_INF)
    return per_row[:, :, :wb], bias_new


def _cast_kernel(*refs):
    n = len(refs) // 2
    for src_ref, dst_ref in zip(refs[:n], refs[n:]):
        dst_ref[...] = src_ref[...].astype(dst_ref.dtype)


def _to_bf16(ws):
    r, c = ws[0].shape
    tr = min(r, CAST_ROWS)
    spec = pl.BlockSpec((tr, c), lambda i: (i, 0))
    return pl.pallas_call(
        _cast_kernel,
        grid=(r // tr,),
        in_specs=[spec] * len(ws),
        out_specs=[spec] * len(ws),
        out_shape=[jax.ShapeDtypeStruct((r, c), BF16)] * len(ws),
        compiler_params=pltpu.CompilerParams(dimension_semantics=("arbitrary",)),
        name="weights_to_bf16",
    )(*ws)


def _pack_layer(i, norm_ffn1_pre, norm_ffn1_post, ffn1_w_gate, ffn1_w_up, ffn1_w_down, norm_mix_pre,
                norm_mix_post, w_in, conv_w, gdn_a_log, gdn_dt_bias, gdn_norm, swa_sinks, w_out,
                norm_ffn2_pre, norm_ffn2_post, ffn2_w_gate, ffn2_w_up, ffn2_w_down, ple_gate, ple_proj,
                norm_ple_post):
    row = lambda g: g[i].reshape(1, -1).astype(F32)
    win = w_in[i]
    n_gdn = QKVZ_DIM + 2 * GDN_HEADS
    win = jnp.concatenate([win[:, :n_gdn], jnp.zeros((D_MODEL, BA_DIM - 2 * GDN_HEADS), win.dtype),
                           win[:, n_gdn:]], axis=1)
    lane_pad = lambda v: jnp.zeros((1, BA_DIM), F32).at[0, GDN_HEADS:2 * GDN_HEADS].set(v[i].astype(F32))
    kk = np.arange(2 * LANES)[:, None] % LANES
    sel = (kk == (np.arange(2 * GDN_HEADS * LANES)[None, :] // LANES)).astype(np.float32)
    wg1, wu1, wg2, wu2 = _to_bf16([ffn1_w_gate[i], ffn1_w_up[i], ffn2_w_gate[i], ffn2_w_up[i]])
    wd1, wd2 = _to_bf16([ffn1_w_down[i], ffn2_w_down[i]])
    wo, wpg = _to_bf16([w_out[i], ple_gate[i]])
    return dict(
        ones=jnp.ones((LANES, LANES), F32), sel=jnp.asarray(sel, BF16),
        g1pre=row(norm_ffn1_pre), g1post=row(norm_ffn1_post), wg1=wg1, wu1=wu1, wd1=wd1,
        gmix=row(norm_mix_pre), gmixpost=row(norm_mix_post), win=win.astype(BF16),
        conv_w=conv_w[i].astype(F32), alog_row=lane_pad(gdn_a_log), dtb_row=lane_pad(gdn_dt_bias),
        gnorm=row(gdn_norm), sinks=swa_sinks[i].astype(F32), wo=wo,
        g2pre=row(norm_ffn2_pre), g2post=row(norm_ffn2_post), wg2=wg2, wu2=wu2, wd2=wd2,
        wpg=wpg, wpp=ple_proj[i].astype(BF16), gple=row(norm_ple_post))


def kernel(x_prompt, x_sample, state_conv, state_gdn, cache_swa_k, cache_swa_v, p_prompt, p_sample,
           rel_bias, norm_ffn1_pre, norm_ffn1_post, ffn1_w_gate, ffn1_w_up, ffn1_w_down,
           norm_mix_pre, norm_mix_post, w_in, conv_w, gdn_a_log, gdn_dt_bias, gdn_norm, swa_sinks,
           w_out, norm_ffn2_pre, norm_ffn2_post, ffn2_w_gate, ffn2_w_up, ffn2_w_down,
           ple_gate, ple_proj, norm_ple_post):
    depth = w_in.shape[0]
    batch, seq, _ = x_prompt.shape
    dec_batch, dec_seq, _ = x_sample.shape
    wb = cache_swa_k.shape[2]
    wp = min(SWA_WINDOW, seq)
    rel_bias = rel_bias.astype(F32)
    bias_p = _prompt_bias(rel_bias)
    bias_db, bias_dn = _decode_bias(rel_bias, wb, dec_seq)

    yp = x_prompt.reshape(batch * seq, D_MODEL)
    ys = x_sample.reshape(dec_batch * dec_seq, D_MODEL)
    outs = [[] for _ in range(8)]
    for i in range(depth):
        lw = _pack_layer(i, norm_ffn1_pre, norm_ffn1_post, ffn1_w_gate, ffn1_w_up, ffn1_w_down,
                         norm_mix_pre, norm_mix_post, w_in, conv_w, gdn_a_log, gdn_dt_bias, gdn_norm,
                         swa_sinks, w_out, norm_ffn2_pre, norm_ffn2_post, ffn2_w_gate, ffn2_w_up,
                         ffn2_w_down, ple_gate, ple_proj, norm_ple_post)
        x1, qkvz, ba, qs, kv = _head_call(yp, lw)
        gdn_o, s_fin = _gdn_prompt_call(qkvz, ba, lw, batch, seq)
        swa_o = _swa_prompt_call(qs, kv, bias_p, lw["sinks"], batch, seq)
        yp = _tail_call(x1, gdn_o, swa_o, p_prompt[i].reshape(batch * seq, PLE_DIM), lw)
        kv3 = kv.reshape(batch, seq, 2 * SWA_KV_DIM)
        outs[0].append(qkvz.reshape(batch, seq, QKVZ_DIM)[:, seq - (CONV_WIDTH - 1):, :CONV_DIM])
        outs[1].append(s_fin)
        outs[2].append(kv3[:, seq - wp:, :SWA_KV_DIM].reshape(batch, wp, SWA_KV_HEADS, SWA_HEAD_DIM))
        outs[3].append(kv3[:, seq - wp:, SWA_KV_DIM:].reshape(batch, wp, SWA_KV_HEADS, SWA_HEAD_DIM))
        x1, qkvz, ba, qs, kv = _head_call(ys, lw)
        gdn_o, s_new = _gdn_decode_call(qkvz, ba, state_conv[i], state_gdn[i], lw, dec_batch, dec_seq)
        kbuf = cache_swa_k[i].reshape(dec_batch, wb, SWA_KV_DIM)
        vbuf = cache_swa_v[i].reshape(dec_batch, wb, SWA_KV_DIM)
        swa_o = _swa_decode_call(qs, kv, kbuf, vbuf, bias_db, bias_dn, lw["sinks"], dec_batch, dec_seq)
        ys = _tail_call(x1, gdn_o, swa_o, p_sample[i].reshape(dec_batch * dec_seq, PLE_DIM), lw)
        kv3 = kv.reshape(dec_batch, dec_seq, 2 * SWA_KV_DIM)
        xp = jnp.concatenate([state_conv[i], qkvz.reshape(dec_batch, dec_seq, QKVZ_DIM)[:, :, :CONV_DIM]], axis=1)
        outs[4].append(xp[:, dec_seq:])
        outs[5].append(s_new)
        outs[6].append(jnp.concatenate([kbuf, kv3[:, :, :SWA_KV_DIM]], axis=1)[:, dec_seq:]
                       .reshape(dec_batch, wb, SWA_KV_HEADS, SWA_HEAD_DIM))
        outs[7].append(jnp.concatenate([vbuf, kv3[:, :, SWA_KV_DIM:]], axis=1)[:, dec_seq:]
                       .reshape(dec_batch, wb, SWA_KV_HEADS, SWA_HEAD_DIM))
    return (yp.reshape(batch, seq, D_MODEL), ys.reshape(dec_batch, dec_seq, D_MODEL),
            *[jnp.stack(o) for o in outs])
```

```python
import functools
import math

import numpy as np
import jax
import jax.numpy as jnp
from jax import lax
from jax.experimental import pallas as pl
from jax.experimental.pallas import tpu as pltpu

F32 = jnp.float32
BF16 = jnp.bfloat16

D_MODEL = 1024
NORM_EPS = 1e-6
PLE_DIM = 256
FFN_DIM = 2816
GDN_HEADS = 4
GDN_DK = 128
GDN_DV = 128
GDN_KEY_DIM = GDN_HEADS * GDN_DK
GDN_VAL_DIM = GDN_HEADS * GDN_DV
CONV_DIM = 2 * GDN_KEY_DIM + GDN_VAL_DIM
CONV_WIDTH = 4
SWA_HEADS = 8
SWA_KV_HEADS = 2
SWA_GROUP = SWA_HEADS // SWA_KV_HEADS
SWA_HEAD_DIM = 64
SWA_Q_DIM = SWA_HEADS * SWA_HEAD_DIM
SWA_KV_DIM = SWA_KV_HEADS * SWA_HEAD_DIM
SWA_WINDOW = 128
NUM_BUCKETS = 32
REL_MAX_DISTANCE = 128

QKVZ_DIM = CONV_DIM + GDN_VAL_DIM
BA_DIM = 128
PROJ_PACKED = QKVZ_DIM + BA_DIM + SWA_Q_DIM + 2 * SWA_KV_DIM

GDN_CHUNK = 64
GDN_STEP_CHUNKS = 4
SWA_STEP_BLOCKS = 4
ROW_TILE = 512
DEC_SEQ_BLOCK = 16
CAST_BLOCK_BYTES = 2 * 1024 * 1024
VMEM_LIMIT = 56 * 1024 * 1024

NEG_INF = float("-inf")
LANES = 128


def _resident(shape):
    nd = len(shape)
    return pl.BlockSpec(shape, lambda *_: (0,) * nd, pipeline_mode=pl.Buffered(1))


def _rms(x, gain):
    ms = jnp.mean(x * x, axis=-1, keepdims=True)
    return (x * lax.rsqrt(ms + NORM_EPS)) * gain


def _sigmoid(x):
    return 1.0 / (1.0 + jnp.exp(-x))


def _silu(x):
    h = 0.5 * x
    return h + h * jnp.tanh(h)


def _dot(a, b):
    return jnp.dot(a, b, preferred_element_type=F32)


def _dot_nt(a, b):
    return lax.dot_general(a, b, (((1,), (1,)), ((), ())), preferred_element_type=F32)


def _dot_tn(a, b):
    return lax.dot_general(a, b, (((0,), (0,)), ((), ())), preferred_element_type=F32)


def _split(a):
    hi = a.astype(BF16)
    lo = (a - hi.astype(F32)).astype(BF16)
    return hi, lo


def _inv_rms(x):
    return lax.rsqrt(jnp.mean(x * x, axis=-1, keepdims=True) + NORM_EPS)


def _prenorm_dots(x, gain, w_refs):
    h = (x * gain).astype(BF16)
    inv = _inv_rms(x)
    return [jnp.dot(h, w, preferred_element_type=F32) * inv for w in w_refs]


def _swiglu_block(x, gain, wg_ref, wu_ref, wd_ref):
    g, u = _prenorm_dots(x, gain, [wg_ref[...], wu_ref[...]])
    a = (_silu(g) * u).astype(BF16)
    return jnp.dot(a, wd_ref[...], preferred_element_type=F32)


def _head_kernel(x_ref, g1pre_ref, g1post_ref, wg_ref, wu_ref, wd_ref, gmix_ref, win_ref,
                 x1_ref, qkvz_ref, ba_ref, qs_ref, kv_ref):
    x = x_ref[...]
    y = _swiglu_block(x, g1pre_ref[...], wg_ref, wu_ref, wd_ref)
    x1 = x + 0.5 * _rms(y, g1post_ref[...])
    x1_ref[...] = x1
    c0, c1, c2 = QKVZ_DIM, QKVZ_DIM + BA_DIM, QKVZ_DIM + BA_DIM + SWA_Q_DIM
    qkvz_ref[...], ba_ref[...], qs_ref[...], kv_ref[...] = _prenorm_dots(
        x1, gmix_ref[...], [win_ref[:, :c0], win_ref[:, c0:c1], win_ref[:, c1:c2], win_ref[:, c2:]])


def _head_call(x, lw):
    n = x.shape[0]
    tm = min(ROW_TILE, n)
    row = lambda w: pl.BlockSpec((tm, w), lambda i: (i, 0))
    return pl.pallas_call(
        _head_kernel,
        grid=(n // tm,),
        in_specs=[row(D_MODEL), _resident((1, D_MODEL)), _resident((1, D_MODEL)),
                  _resident((D_MODEL, FFN_DIM)), _resident((D_MODEL, FFN_DIM)),
                  _resident((FFN_DIM, D_MODEL)), _resident((1, D_MODEL)),
                  _resident((D_MODEL, PROJ_PACKED))],
        out_specs=[row(D_MODEL), row(QKVZ_DIM), row(BA_DIM), row(SWA_Q_DIM), row(2 * SWA_KV_DIM)],
        out_shape=[jax.ShapeDtypeStruct((n, D_MODEL), F32), jax.ShapeDtypeStruct((n, QKVZ_DIM), F32),
                   jax.ShapeDtypeStruct((n, BA_DIM), F32), jax.ShapeDtypeStruct((n, SWA_Q_DIM), F32),
                   jax.ShapeDtypeStruct((n, 2 * SWA_KV_DIM), F32)],
        compiler_params=pltpu.CompilerParams(dimension_semantics=("arbitrary",),
                                             vmem_limit_bytes=VMEM_LIMIT),
        name="ffn1_inproj",
    )(x, lw["g1pre"], lw["g1post"], lw["wg1"], lw["wu1"], lw["wd1"], lw["gmix"], lw["win"])


def _tail_kernel(x_ref, gdn_ref, swa_ref, p_ref, wo_ref, gmixpost_ref, g2pre_ref, g2post_ref,
                 wg_ref, wu_ref, wd_ref, wpg_ref, wpp_ref, gple_ref, y_ref):
    x = x_ref[...]
    mix = (jnp.dot(gdn_ref[...].astype(BF16), wo_ref[:GDN_VAL_DIM, :], preferred_element_type=F32)
           + jnp.dot(swa_ref[...].astype(BF16), wo_ref[GDN_VAL_DIM:, :], preferred_element_type=F32))
    pp = jnp.dot(p_ref[...].astype(BF16), wpp_ref[...], preferred_element_type=F32)
    x = x + _rms(mix, gmixpost_ref[...])
    y = _swiglu_block(x, g2pre_ref[...], wg_ref, wu_ref, wd_ref)
    x = x + 0.5 * _rms(y, g2post_ref[...])
    gate = _sigmoid(jnp.dot(x.astype(BF16), wpg_ref[...], preferred_element_type=F32))
    y_ref[...] = x + _rms(gate * pp, gple_ref[...])


def _tail_call(x1, gdn_o, swa_o, p, lw):
    n = x1.shape[0]
    tm = min(ROW_TILE, n)
    row = lambda w: pl.BlockSpec((tm, w), lambda i: (i, 0))
    return pl.pallas_call(
        _tail_kernel,
        grid=(n // tm,),
        in_specs=[row(D_MODEL), row(GDN_VAL_DIM), row(SWA_Q_DIM), row(PLE_DIM),
                  _resident((GDN_VAL_DIM + SWA_Q_DIM, D_MODEL)), _resident((1, D_MODEL)),
                  _resident((1, D_MODEL)), _resident((1, D_MODEL)),
                  _resident((D_MODEL, FFN_DIM)), _resident((D_MODEL, FFN_DIM)),
                  _resident((FFN_DIM, D_MODEL)), _resident((D_MODEL, D_MODEL)),
                  _resident((PLE_DIM, D_MODEL)), _resident((1, D_MODEL))],
        out_specs=row(D_MODEL),
        out_shape=jax.ShapeDtypeStruct((n, D_MODEL), F32),
        compiler_params=pltpu.CompilerParams(dimension_semantics=("arbitrary",),
                                             vmem_limit_bytes=VMEM_LIMIT),
        name="outproj_ffn2_ple",
    )(x1, gdn_o, swa_o, p, lw["wo"], lw["gmixpost"], lw["g2pre"], lw["g2post"],
      lw["wg2"], lw["wu2"], lw["wd2"], lw["wpg"], lw["wpp"], lw["gple"])


def _iota2(c):
    return (lax.broadcasted_iota(jnp.int32, (c, c), 0), lax.broadcasted_iota(jnp.int32, (c, c), 1))


def _gates(ba, alog_row, dtb_row):
    beta = _sigmoid(ba)
    xa = ba + dtb_row
    softplus = jnp.maximum(xa, 0.0) + jnp.log1p(jnp.exp(-jnp.abs(xa)))
    g = -jnp.exp(alog_row) * softplus
    return beta, g


def _cumsum_rows(mask01, g):
    hi, lo = _split(g)
    m = mask01.astype(BF16)
    return jnp.dot(m, hi, preferred_element_type=F32) + jnp.dot(m, lo, preferred_element_type=F32)


def _hi_lo_lanes(x):
    hi, lo = _split(x)
    return jnp.concatenate([hi, lo], axis=1)


def _rowsum_bcast(x, ones_ref):
    return jnp.dot(x, ones_ref[...], preferred_element_type=F32)


def _lane_bcast(x, sel_ref, lanes):
    out = jnp.dot(_hi_lo_lanes(x), sel_ref[...], preferred_element_type=F32)
    return [out[:, l * 128:(l + 1) * 128] for l in lanes]


def _gdn_prep_steps(out, y, beta_b, gc_b, gl_b, gc_t, chunk, group, ones_ref, normalized):
    c = chunk
    n_chunks = y.shape[0] // c
    units = [(ci, h) for ci in range(n_chunks) for h in range(GDN_HEADS)]
    n = range(len(units))
    ii, jj = _iota2(c)
    same = (ii // group) == (jj // group)
    incl = jnp.logical_and(same, ii >= jj)
    strict = jnp.logical_and(same, ii > jj)

    def rows(t, ci, lo, hi):
        return t[ci * c:(ci + 1) * c, lo:hi]

    eg_b = [jnp.exp(t) for t in gc_b]
    ekd_b = [jnp.exp(gl_b[h] - gc_b[h]) for h in range(GDN_HEADS)]
    beta_u = [rows(beta_b[h], ci, 0, GDN_DK) for ci, h in units]
    eg_u = [rows(eg_b[h], ci, 0, GDN_DK) for ci, h in units]
    ekd_u = [rows(ekd_b[h], ci, 0, GDN_DK) for ci, h in units]
    gc_col = [rows(gc_b[h], ci, 0, c) for ci, h in units]
    gc_row = [gc_t[GDN_HEADS + h:GDN_HEADS + h + 1, ci * c:(ci + 1) * c] for ci, h in units]
    q = [rows(y, ci, h * GDN_DK, (h + 1) * GDN_DK) for ci, h in units]
    k = [rows(y, ci, GDN_KEY_DIM + h * GDN_DK, GDN_KEY_DIM + (h + 1) * GDN_DK) for ci, h in units]
    v = [rows(y, ci, 2 * GDN_KEY_DIM + h * GDN_DV, 2 * GDN_KEY_DIM + (h + 1) * GDN_DV) for ci, h in units]
    if not normalized:
        q = [_l2norm(t, ones_ref) * (GDN_DK ** -0.5) for t in q]
        k = [_l2norm(t, ones_ref) for t in k]
    decay = [jnp.exp(jnp.where(incl, gc_col[i] - gc_row[i], NEG_INF)) for i in n]
    kb = [k[i] * beta_u[i] for i in n]
    kq = [_dot_nt(jnp.concatenate([kb[i], q[i]], axis=0), k[i]) for i in n]
    yield
    a_mat = [jnp.where(strict, kq[i][:c] * decay[i], 0.0) for i in n]
    qk = [kq[i][c:] * decay[i] for i in n]
    eye = jnp.where(ii == jj, 1.0, 0.0).astype(F32)
    t_mat = [eye for _ in n]
    b = 1
    while b < group:
        lower = jnp.logical_and((ii // (2 * b)) == (jj // (2 * b)),
                                jnp.logical_and((ii % (2 * b)) >= b, (jj % (2 * b)) < b))
        m = [jnp.where(lower, a_mat[i], 0.0) for i in n]
        if b == 1:
            t_mat = [eye - m[i] for i in n]
        else:
            tm = [_dot(t_mat[i], m[i]) for i in n]
            yield
            t_mat = [t_mat[i] - _dot(tm[i], t_mat[i]) for i in n]
            yield
        b *= 2
    rhs = [jnp.concatenate([v[i] * beta_u[i], kb[i] * eg_u[i]], axis=-1) for i in n]
    sol = [_dot(t_mat[i], rhs[i]) for i in n]
    yield
    nest = lambda xs: [xs[ci * GDN_HEADS:(ci + 1) * GDN_HEADS] for ci in range(n_chunks)]
    out.update(u=nest([s[:, :GDN_DV] for s in sol]), w=nest([s[:, GDN_DV:] for s in sol]), qk=nest(qk),
               qd=nest([q[i] * eg_u[i] for i in n]), kd=nest([k[i] * ekd_u[i] for i in n]))


def _l2norm(t, ones_ref):
    return t * lax.rsqrt(_rowsum_bcast(t * t, ones_ref) + 1e-6)


def _gdn_out(o, z, gnorm_row, ones_ref):
    ms = _rowsum_bcast(o * o, ones_ref) * (1.0 / GDN_DV)
    return (o * lax.rsqrt(ms + NORM_EPS)) * gnorm_row * _silu(z)


def _conv_silu(xbuf_ref, cw_ref, c):
    y = xbuf_ref[pl.ds(5, c), :] * cw_ref[0:1, :]
    for j in range(1, CONV_WIDTH):
        y = y + xbuf_ref[pl.ds(5 + j, c), :] * cw_ref[j:j + 1, :]
    return _silu(y)


def _gdn_prompt_kernel(qkv_ref, z_ref, ba_ref, cw_ref, alog_ref, dtb_ref, gnorm_ref, ones_ref, sel_ref,
                       o_ref, sfin_ref, s_ref, xbuf_ref, y0_ref, g0_ref, gt0_ref, y1_ref, g1_ref, gt1_ref):
    step = pl.program_id(1)

    @pl.when(step == 0)
    def _():
        s_ref[...] = jnp.zeros_like(s_ref)
        xbuf_ref[0:8, :] = jnp.zeros((8, CONV_DIM), F32)
        y1_ref[...] = jnp.zeros_like(y1_ref)
        g1_ref[...] = jnp.zeros_like(g1_ref)
        gt1_ref[...] = jnp.zeros_like(gt1_ref)

    @pl.when(lax.rem(step, 2) == 0)
    def _():
        _gdn_prompt_step((y0_ref, g0_ref, gt0_ref), (y1_ref, g1_ref, gt1_ref), qkv_ref, z_ref, ba_ref, cw_ref,
                         alog_ref, dtb_ref, gnorm_ref, ones_ref, sel_ref, o_ref, sfin_ref, s_ref, xbuf_ref)

    @pl.when(lax.rem(step, 2) == 1)
    def _():
        _gdn_prompt_step((y1_ref, g1_ref, gt1_ref), (y0_ref, g0_ref, gt0_ref), qkv_ref, z_ref, ba_ref, cw_ref,
                         alog_ref, dtb_ref, gnorm_ref, ones_ref, sel_ref, o_ref, sfin_ref, s_ref, xbuf_ref)


def _gdn_prompt_step(a_refs, b_refs, qkv_ref, z_ref, ba_ref, cw_ref, alog_ref, dtb_ref, gnorm_ref, ones_ref,
                     sel_ref, o_ref, sfin_ref, s_ref, xbuf_ref):
    ya_ref, ga_ref, gta_ref = a_refs
    yb_ref, gb_ref, gtb_ref = b_refs
    c = GDN_CHUNK
    n_chunks = GDN_STEP_CHUNKS
    r = n_chunks * c
    hs = range(GDN_HEADS)

    def stage_a():
        xbuf_ref[8:8 + r, :] = qkv_ref[...]
        beta, g = _gates(ba_ref[...], alog_ref[...], dtb_ref[...])
        ii, jj = _iota2(r)
        tril = jnp.where(jnp.logical_and((ii // c) == (jj // c), ii >= jj), 1.0, 0.0)
        gc = _cumsum_rows(tril, g)
        gta_ref[...] = gc.T
        lane = lax.broadcasted_iota(jnp.int32, (r, BA_DIM), 1)
        ga_ref[...] = jnp.where(lane < GDN_HEADS, beta, gc)
        yield
        for ci in range(n_chunks):
            rows = slice(ci * c, (ci + 1) * c)
            for slab in range(CONV_DIM // LANES):
                cols = slice(slab * LANES, (slab + 1) * LANES)
                ext = xbuf_ref[pl.ds(ci * c, c + 8), cols]
                s1 = pltpu.roll(ext, 1, axis=0)
                u2 = pltpu.roll(ext * cw_ref[1:2, cols] + s1 * cw_ref[0:1, cols], 2, axis=0)
                yc = (ext * cw_ref[3:4, cols] + s1 * cw_ref[2:3, cols] + u2)[8:]
                ya_ref[rows, cols] = _silu(yc)
                if slab < GDN_KEY_DIM // LANES:
                    ya_ref[rows, cols] = _l2norm(ya_ref[rows, cols], ones_ref) * (GDN_DK ** -0.5)
                elif slab < 2 * GDN_KEY_DIM // LANES:
                    ya_ref[rows, cols] = _l2norm(ya_ref[rows, cols], ones_ref)
                yield
        xbuf_ref[0:8, :] = xbuf_ref[r:r + 8, :]
        yield

    def stage_b():
        bcast = _lane_bcast(gb_ref[...], sel_ref, range(2 * GDN_HEADS))
        beta_b, gc_b = bcast[:GDN_HEADS], bcast[GDN_HEADS:]
        glast = [[gc_b[h][(ci + 1) * c - 1:(ci + 1) * c, :] for h in hs] for ci in range(n_chunks)]
        gl_b = [jnp.concatenate([jnp.broadcast_to(glast[ci][h], (c, GDN_DK)) for ci in range(n_chunks)],
                                axis=0) for h in hs]
        wy = {}
        yield from _gdn_prep_steps(wy, yb_ref, beta_b, gc_b, gl_b, gtb_ref[...], c, c, ones_ref,
                                   normalized=True)
        u, w, qk, qd, kd = (wy[name] for name in ("u", "w", "qk", "qd", "kd"))
        s_cur = [s_ref[h] for h in hs]
        for ci in range(n_chunks):
            ws = [_dot(jnp.concatenate([w[ci][h], qd[ci][h]], axis=0), s_cur[h]) for h in hs]
            yield
            v_new = [u[ci][h] - ws[h][:c] for h in hs]
            o = [ws[h][c:] + _dot(qk[ci][h], v_new[h]) for h in hs]
            s_cur = [s_cur[h] * jnp.exp(glast[ci][h]) + _dot_tn(kd[ci][h], v_new[h]) for h in hs]
            yield
            for h in hs:
                z = z_ref[ci * c:(ci + 1) * c, h * GDN_DV:(h + 1) * GDN_DV]
                o_ref[ci * c:(ci + 1) * c, h * GDN_DV:(h + 1) * GDN_DV] = _gdn_out(o[h], z, gnorm_ref[...], ones_ref)
        for h in hs:
            s_ref[h] = s_cur[h]
            sfin_ref[0, h] = s_cur[h]

    a_pieces = 2 + n_chunks * (CONV_DIM // LANES)
    b_levels = 2 * (int(math.log2(c)) - 1) + 2 + 2 * n_chunks
    a_steps = stage_a()
    done = 0
    for i, _ in enumerate(stage_b()):
        target = -(-(i + 1) * a_pieces // b_levels)
        for _ in range(target - done):
            next(a_steps, None)
        done = target
    for _ in a_steps:
        pass


def _gdn_prompt_call(qkvz, ba, lw, batch, seq):
    c = GDN_CHUNK * GDN_STEP_CHUNKS
    nc = seq // c
    z_col = CONV_DIM // GDN_VAL_DIM
    return pl.pallas_call(
        _gdn_prompt_kernel,
        grid=(batch, nc + 1),
        in_specs=[pl.BlockSpec((c, CONV_DIM), lambda b, s: (b * nc + jnp.minimum(s, nc - 1), 0)),
                  pl.BlockSpec((c, GDN_VAL_DIM), lambda b, s: (b * nc + jnp.maximum(s - 1, 0), z_col)),
                  pl.BlockSpec((c, BA_DIM), lambda b, s: (b * nc + jnp.minimum(s, nc - 1), 0)),
                  _resident((CONV_WIDTH, CONV_DIM)), _resident((1, BA_DIM)), _resident((1, BA_DIM)),
                  _resident((1, GDN_DV)), _resident(lw["ones"].shape), _resident(lw["sel"].shape)],
        out_specs=[pl.BlockSpec((c, GDN_VAL_DIM), lambda b, s: (b * nc + jnp.maximum(s - 1, 0), 0)),
                   pl.BlockSpec((1, GDN_HEADS, GDN_DK, GDN_DV), lambda b, s: (b, 0, 0, 0))],
        out_shape=[jax.ShapeDtypeStruct((batch * seq, GDN_VAL_DIM), F32),
                   jax.ShapeDtypeStruct((batch, GDN_HEADS, GDN_DK, GDN_DV), F32)],
        scratch_shapes=[pltpu.VMEM((GDN_HEADS, GDN_DK, GDN_DV), F32),
                        pltpu.VMEM((c + 8, CONV_DIM), F32),
                        pltpu.VMEM((c, CONV_DIM), F32), pltpu.VMEM((c, BA_DIM), F32), pltpu.VMEM((BA_DIM, c), F32),
                        pltpu.VMEM((c, CONV_DIM), F32), pltpu.VMEM((c, BA_DIM), F32), pltpu.VMEM((BA_DIM, c), F32)],
        compiler_params=pltpu.CompilerParams(dimension_semantics=("arbitrary", "arbitrary"),
                                             vmem_limit_bytes=VMEM_LIMIT),
        name="gdn_prompt",
    )(qkvz, qkvz, ba, lw["conv_w"], lw["alog_row"], lw["dtb_row"], lw["gnorm"], lw["ones"], lw["sel"])


def _gdn_decode_kernel(qkvz_ref, ba_ref, sconv_ref, s0_ref, cw_ref, alog_ref, dtb_ref, gnorm_ref,
                       ones_ref, sel_ref, o_ref, snew_ref, xbuf_ref, *, dec_seq):
    nb = DEC_SEQ_BLOCK
    t = dec_seq
    c = nb * t
    ys = []
    for s in range(nb):
        xbuf_ref[s, 5:8, :] = sconv_ref[s]
        xbuf_ref[s, 8:8 + t, :] = qkvz_ref[s * t:(s + 1) * t, :CONV_DIM]
        ys.append(_conv_silu(xbuf_ref.at[s], cw_ref, t))
    y = jnp.concatenate(ys, axis=0)

    beta, g = _gates(ba_ref[...], alog_ref[...], dtb_ref[...])
    ii, jj = _iota2(c)
    same = (ii // t) == (jj // t)
    tril = jnp.where(jnp.logical_and(same, ii >= jj), 1.0, 0.0).astype(F32)
    ones = jnp.where(same, 1.0, 0.0).astype(F32)
    gc = _cumsum_rows(tril, g)
    gl = _cumsum_rows(ones, g)
    gc_t = gc.T
    lane = lax.broadcasted_iota(jnp.int32, (c, BA_DIM), 1)
    bcast = _lane_bcast(jnp.where(lane < GDN_HEADS, beta, gc), sel_ref, range(2 * GDN_HEADS))
    beta_b, gc_b = bcast[:GDN_HEADS], bcast[GDN_HEADS:]
    gl_b = _lane_bcast(gl, sel_ref, range(GDN_HEADS, 2 * GDN_HEADS))
    wy = {}
    for _ in _gdn_prep_steps(wy, y, beta_b, gc_b, gl_b, gc_t, c, t, ones_ref, normalized=False):
        pass
    u, w, qk, qd, kd = (wy[name][0] for name in ("u", "w", "qk", "qd", "kd"))
    hs = range(GDN_HEADS)
    units = [(s, h) for h in hs for s in range(nb)]
    s_old = {(s, h): s0_ref[s, h] for s, h in units}
    ws = {(s, h): _dot(jnp.concatenate([w[h][s * t:(s + 1) * t], qd[h][s * t:(s + 1) * t]], axis=0),
                       s_old[s, h]) for s, h in units}
    v_new = [jnp.concatenate([u[h][s * t:(s + 1) * t] - ws[s, h][:t] for s in range(nb)], axis=0) for h in hs]
    o = [jnp.concatenate([ws[s, h][t:] for s in range(nb)], axis=0) + _dot(qk[h], v_new[h]) for h in hs]
    cd = [jnp.exp(gl_b[h]) for h in hs]
    kd_t = [kd[h].T for h in hs]
    col_seq = lax.broadcasted_iota(jnp.int32, (GDN_DK, c), 1) // t
    for s, h in units:
        upd = _dot(jnp.where(col_seq == s, kd_t[h], 0.0), v_new[h])
        snew_ref[s, h] = s_old[s, h] * cd[h][s * t:s * t + 1, :] + upd
    for h in hs:
        z = qkvz_ref[:, CONV_DIM + h * GDN_DV:CONV_DIM + (h + 1) * GDN_DV]
        o_ref[:, h * GDN_DV:(h + 1) * GDN_DV] = _gdn_out(o[h], z, gnorm_ref[...], ones_ref)


def _gdn_decode_call(qkvz, ba, state_conv, state_gdn, lw, dec_batch, dec_seq):
    nb = DEC_SEQ_BLOCK
    rows = nb * dec_seq
    return pl.pallas_call(
        functools.partial(_gdn_decode_kernel, dec_seq=dec_seq),
        grid=(dec_batch // nb,),
        in_specs=[pl.BlockSpec((rows, QKVZ_DIM), lambda i: (i, 0)),
                  pl.BlockSpec((rows, BA_DIM), lambda i: (i, 0)),
                  pl.BlockSpec((nb, CONV_WIDTH - 1, CONV_DIM), lambda i: (i, 0, 0)),
                  pl.BlockSpec((nb, GDN_HEADS, GDN_DK, GDN_DV), lambda i: (i, 0, 0, 0)),
                  _resident((CONV_WIDTH, CONV_DIM)), _resident((1, BA_DIM)), _resident((1, BA_DIM)),
                  _resident((1, GDN_DV)), _resident(lw["ones"].shape), _resident(lw["sel"].shape)],
        out_specs=[pl.BlockSpec((rows, GDN_VAL_DIM), lambda i: (i, 0)),
                   pl.BlockSpec((nb, GDN_HEADS, GDN_DK, GDN_DV), lambda i: (i, 0, 0, 0))],
        out_shape=[jax.ShapeDtypeStruct((dec_batch * dec_seq, GDN_VAL_DIM), F32),
                   jax.ShapeDtypeStruct((dec_batch, GDN_HEADS, GDN_DK, GDN_DV), F32)],
        scratch_shapes=[pltpu.VMEM((nb, 16, CONV_DIM), F32)],
        compiler_params=pltpu.CompilerParams(dimension_semantics=("arbitrary",),
                                             vmem_limit_bytes=VMEM_LIMIT),
        name="gdn_decode",
    )(qkvz, ba, state_conv, state_gdn, lw["conv_w"], lw["alog_row"], lw["dtb_row"], lw["gnorm"],
      lw["ones"], lw["sel"])


def _bias_table_kernel(bucket_ref, rb_ref, out_ref):
    bucket = bucket_ref[...]
    for h in range(SWA_HEADS):
        acc = jnp.zeros(bucket.shape, F32)
        for b in range(NUM_BUCKETS):
            acc = jnp.where(bucket == b, rb_ref[b, h], acc)
        out_ref[h] = jnp.where(bucket < 0, NEG_INF, acc)


def _bias_table_call(bucket, rel_bias):
    r, c = bucket.shape
    return pl.pallas_call(
        _bias_table_kernel,
        in_specs=[pl.BlockSpec(memory_space=pltpu.VMEM), pl.BlockSpec(memory_space=pltpu.SMEM)],
        out_specs=pl.BlockSpec(memory_space=pltpu.VMEM),
        out_shape=jax.ShapeDtypeStruct((SWA_HEADS, r, c), F32),
        name="t5_bias_table",
    )(bucket, rel_bias)


def _t5_bucket(dist, valid):
    d = jnp.maximum(dist, 0)
    exact = NUM_BUCKETS // 2
    log_ratio = jnp.log(jnp.maximum(d, 1).astype(F32) / exact) / math.log(REL_MAX_DISTANCE / exact)
    large = jnp.minimum(exact + (log_ratio * (NUM_BUCKETS - exact)).astype(jnp.int32), NUM_BUCKETS - 1)
    bucket = jnp.where(d < exact, d, large)
    return jnp.where(valid, bucket, -1).astype(jnp.int32)


def _swa_prompt_kernel(q_ref, kvp_ref, kvc_ref, bias_ref, sink_ref, o_ref):
    w, dh, nq = SWA_WINDOW, SWA_HEAD_DIM, SWA_STEP_BLOCKS
    first = pl.program_id(1) == 0
    col = lax.broadcasted_iota(jnp.int32, (w, 2 * w), 1)
    drop_prev = jnp.logical_and(first, col < w)
    kv = jnp.concatenate([kvp_ref[...], kvc_ref[...]], axis=0).astype(BF16)
    k_h = [kv[:, h * dh:(h + 1) * dh] for h in range(SWA_KV_HEADS)]
    v_h = [kv[:, SWA_KV_DIM + h * dh:SWA_KV_DIM + (h + 1) * dh] for h in range(SWA_KV_HEADS)]
    units = [(b, hg) for b in range(nq) for hg in range(SWA_HEADS)]
    n = range(len(units))
    q = [(q_ref[b * w:(b + 1) * w, hg * dh:(hg + 1) * dh] * (dh ** -0.5)).astype(BF16) for b, hg in units]
    s = [lax.dot_general(q[i], k_h[hg // SWA_GROUP][b * w:(b + 2) * w], (((1,), (1,)), ((), ())),
                         preferred_element_type=F32) + bias_ref[hg] for i, (b, hg) in enumerate(units)]
    s = [jnp.where(drop_prev, NEG_INF, s[i]) if b == 0 else s[i] for i, (b, hg) in enumerate(units)]
    sink = [sink_ref[hg] for b, hg in units]
    m = [jnp.maximum(jnp.max(s[i], axis=-1, keepdims=True), sink[i]) for i in n]
    p = [jnp.exp(s[i] - m[i]) for i in n]
    denom = [jnp.sum(p[i], axis=-1, keepdims=True) + jnp.exp(sink[i] - m[i]) for i in n]
    o = [jnp.dot(p[i].astype(BF16), v_h[hg // SWA_GROUP][b * w:(b + 2) * w], preferred_element_type=F32)
         / denom[i] for i, (b, hg) in enumerate(units)]
    for i, (b, hg) in enumerate(units):
        o_ref[b * w:(b + 1) * w, hg * dh:(hg + 1) * dh] = o[i]


def _swa_prompt_call(qs, kv, bias, sinks, batch, seq):
    w, nq = SWA_WINDOW, SWA_STEP_BLOCKS
    nb = seq // (w * nq)
    return pl.pallas_call(
        _swa_prompt_kernel,
        grid=(batch, nb),
        in_specs=[pl.BlockSpec((nq * w, SWA_Q_DIM), lambda b, n: (b * nb + n, 0)),
                  pl.BlockSpec((w, 2 * SWA_KV_DIM),
                               lambda b, n: ((b * nb + n) * nq - jnp.minimum(n, 1), 0)),
                  pl.BlockSpec((nq * w, 2 * SWA_KV_DIM), lambda b, n: (b * nb + n, 0)),
                  _resident((SWA_HEADS, w, 2 * w)),
                  pl.BlockSpec(memory_space=pltpu.SMEM)],
        out_specs=pl.BlockSpec((nq * w, SWA_Q_DIM), lambda b, n: (b * nb + n, 0)),
        out_shape=jax.ShapeDtypeStruct((batch * seq, SWA_Q_DIM), F32),
        compiler_params=pltpu.CompilerParams(dimension_semantics=("arbitrary", "arbitrary"),
                                             vmem_limit_bytes=VMEM_LIMIT),
        name="swa_prompt",
    )(qs, kv, kv, bias, sinks)


def _swa_decode_kernel(q_ref, kvn_ref, kt_ref, vt_ref, bbuf_ref, bnew_ref, sink_ref,
                       o_ref, kto_ref, vto_ref, *, dec_seq):
    nb = DEC_SEQ_BLOCK
    t = dec_seq
    dh = SWA_HEAD_DIM
    rows = nb * t
    m_rows = SWA_GROUP * rows
    wb = kt_ref.shape[3]
    ri = lax.broadcasted_iota(jnp.int32, (m_rows, 1), 0)
    row_seq = (ri % rows) // t
    row_grp = ri // rows
    for h in range(SWA_KV_HEADS):
        lo, hi = h * dh, (h + 1) * dh
        q = jnp.concatenate([q_ref[:, (h * SWA_GROUP + g) * dh:(h * SWA_GROUP + g + 1) * dh]
                             for g in range(SWA_GROUP)], axis=0)
        sink = jnp.zeros((m_rows, 1), F32)
        for g in range(SWA_GROUP):
            sink = jnp.where(row_grp == g, sink_ref[h * SWA_GROUP + g], sink)
        k_new = jnp.concatenate([kvn_ref[:, lo:hi], jnp.zeros((m_rows - rows, dh), F32)], axis=0)
        v_new = jnp.concatenate([kvn_ref[:, SWA_KV_DIM + lo:SWA_KV_DIM + hi],
                                 jnp.zeros((m_rows - rows, dh), F32)], axis=0)
        s_new = _dot_nt(q, k_new) * (dh ** -0.5) + bnew_ref[h]
        s_buf = jnp.zeros((m_rows, wb), F32)
        for s in range(nb):
            s_buf = jnp.where(row_seq == s, _dot(q, kt_ref[s, h]), s_buf)
        s_buf = s_buf * (dh ** -0.5) + bbuf_ref[h]
        m = jnp.maximum(jnp.maximum(jnp.max(s_buf, axis=-1, keepdims=True),
                                    jnp.max(s_new, axis=-1, keepdims=True)), sink)
        p_buf = jnp.exp(s_buf - m)
        p_new = jnp.exp(s_new - m)
        denom = (jnp.sum(p_buf, axis=-1, keepdims=True) + jnp.sum(p_new, axis=-1, keepdims=True)
                 + jnp.exp(sink - m))
        o = _dot(p_new, v_new)
        for s in range(nb):
            o = o + _dot_nt(jnp.where(row_seq == s, p_buf, 0.0), vt_ref[s, h])
        o = o / denom
        for g in range(SWA_GROUP):
            hg = h * SWA_GROUP + g
            o_ref[:, hg * dh:(hg + 1) * dh] = o[g * rows:(g + 1) * rows]
    new_t = jnp.concatenate([kvn_ref[...], jnp.zeros((wb - rows, 2 * SWA_KV_DIM), F32)], axis=0).T
    lane = lax.broadcasted_iota(jnp.int32, (dh, wb), 1)
    for part, (src_ref, dst_ref) in enumerate(((kt_ref, kto_ref), (vt_ref, vto_ref))):
        for h in range(SWA_KV_HEADS):
            fresh = new_t[part * SWA_KV_DIM + h * dh:part * SWA_KV_DIM + (h + 1) * dh, :]
            for s in range(nb):
                kept = pltpu.roll(src_ref[s, h], wb - t, axis=1)
                tail = pltpu.roll(fresh, (wb - t - s * t) % wb, axis=1)
                dst_ref[s, h] = jnp.where(lane >= wb - t, tail, kept)


def _swa_decode_call(qs, kvn, kt, vt, bias_buf, bias_new, sinks, dec_batch, dec_seq):
    nb = DEC_SEQ_BLOCK
    rows = nb * dec_seq
    wb = kt.shape[3]
    m_rows = SWA_GROUP * rows
    assert rows <= wb and wb == LANES
    win_spec = pl.BlockSpec((nb, SWA_KV_HEADS, SWA_HEAD_DIM, wb), lambda i: (i, 0, 0, 0))
    return pl.pallas_call(
        functools.partial(_swa_decode_kernel, dec_seq=dec_seq),
        grid=(dec_batch // nb,),
        in_specs=[pl.BlockSpec((rows, SWA_Q_DIM), lambda i: (i, 0)),
                  pl.BlockSpec((rows, 2 * SWA_KV_DIM), lambda i: (i, 0)),
                  win_spec, win_spec,
                  _resident((SWA_KV_HEADS, m_rows, wb)),
                  _resident((SWA_KV_HEADS, m_rows, m_rows)),
                  pl.BlockSpec(memory_space=pltpu.SMEM)],
        out_specs=[pl.BlockSpec((rows, SWA_Q_DIM), lambda i: (i, 0)), win_spec, win_spec],
        out_shape=[jax.ShapeDtypeStruct((dec_batch * dec_seq, SWA_Q_DIM), F32),
                   jax.ShapeDtypeStruct(kt.shape, F32), jax.ShapeDtypeStruct(vt.shape, F32)],
        compiler_params=pltpu.CompilerParams(dimension_semantics=("arbitrary",),
                                             vmem_limit_bytes=VMEM_LIMIT),
        name="swa_decode",
    )(qs, kvn, kt, vt, bias_buf, bias_new, sinks)


def _prompt_bias(rel_bias):
    w = SWA_WINDOW
    dist = w + jnp.arange(w)[:, None] - jnp.arange(2 * w)[None, :]
    valid = (dist >= 0) & (dist < w)
    return _bias_table_call(_t5_bucket(dist, valid), rel_bias)


def _decode_bias(rel_bias, wb, dec_seq):
    nb, t = DEC_SEQ_BLOCK, dec_seq
    rows = nb * t
    m_rows = SWA_GROUP * rows
    tok = jnp.arange(t)
    dist = jnp.concatenate([wb + tok[:, None] - jnp.arange(wb)[None, :], tok[:, None] - tok[None, :]], axis=1)
    bucket = _t5_bucket(dist, (dist >= 0) & (dist < SWA_WINDOW))
    pad_r, pad_c = -t % 8, -(wb + t) % LANES
    bucket = jnp.pad(bucket, ((0, pad_r), (0, pad_c)), constant_values=-1)
    tab = _bias_table_call(bucket, rel_bias)[:, :t, :wb + t]
    tab = tab.reshape(SWA_KV_HEADS, SWA_GROUP, 1, t, wb + t)
    per_row = jnp.broadcast_to(tab, (SWA_KV_HEADS, SWA_GROUP, nb, t, wb + t)).reshape(SWA_KV_HEADS, m_rows, wb + t)
    r_seq = (jnp.arange(m_rows) % rows) // t
    cidx = jnp.arange(m_rows)
    own = (r_seq[:, None] == (cidx // t)[None, :]) & (cidx < rows)[None, :]
    bias_new = jnp.where(own[None], jnp.tile(per_row[:, :, wb:], (1, 1, m_rows // t)), NEG_INF)
    return per_row[:, :, :wb], bias_new


def _cast_kernel(*refs):
    n = len(refs) // 2
    for src_ref, dst_ref in zip(refs[:n], refs[n:]):
        dst_ref[...] = src_ref[...].astype(dst_ref.dtype)


def _to_bf16(ws):
    r, c = ws[0].shape
    tr = r
    while tr * c * 4 > CAST_BLOCK_BYTES and tr % 16 == 0:
        tr //= 2
    spec = pl.BlockSpec((tr, c), lambda i: (i, 0))
    return pl.pallas_call(
        _cast_kernel,
        grid=(r // tr,),
        in_specs=[spec] * len(ws),
        out_specs=[spec] * len(ws),
        out_shape=[jax.ShapeDtypeStruct((r, c), BF16)] * len(ws),
        compiler_params=pltpu.CompilerParams(dimension_semantics=("arbitrary",)),
        name="weights_to_bf16",
    )(*ws)


def _pack_layer(i, norm_ffn1_pre, norm_ffn1_post, ffn1_w_gate, ffn1_w_up, ffn1_w_down, norm_mix_pre,
                norm_mix_post, w_in, conv_w, gdn_a_log, gdn_dt_bias, gdn_norm, swa_sinks, w_out,
                norm_ffn2_pre, norm_ffn2_post, ffn2_w_gate, ffn2_w_up, ffn2_w_down, ple_gate, ple_proj,
                norm_ple_post):
    row = lambda g: g[i].reshape(1, -1).astype(F32)
    win = w_in[i]
    n_gdn = QKVZ_DIM + 2 * GDN_HEADS
    win = jnp.concatenate([win[:, :n_gdn], jnp.zeros((D_MODEL, BA_DIM - 2 * GDN_HEADS), win.dtype),
                           win[:, n_gdn:]], axis=1)
    lane_pad = lambda v: jnp.zeros((1, BA_DIM), F32).at[0, GDN_HEADS:2 * GDN_HEADS].set(v[i].astype(F32))
    kk = np.arange(2 * LANES)[:, None] % LANES
    sel = (kk == (np.arange(2 * GDN_HEADS * LANES)[None, :] // LANES)).astype(np.float32)
    wg1, wu1, wg2, wu2 = _to_bf16([ffn1_w_gate[i], ffn1_w_up[i], ffn2_w_gate[i], ffn2_w_up[i]])
    wd1, wd2 = _to_bf16([ffn1_w_down[i], ffn2_w_down[i]])
    wo, wpg = _to_bf16([w_out[i], ple_gate[i]])
    return dict(
        ones=jnp.ones((LANES, LANES), F32), sel=jnp.asarray(sel, BF16),
        g1pre=row(norm_ffn1_pre), g1post=row(norm_ffn1_post), wg1=wg1, wu1=wu1, wd1=wd1,
        gmix=row(norm_mix_pre), gmixpost=row(norm_mix_post), win=win.astype(BF16),
        conv_w=conv_w[i].astype(F32), alog_row=lane_pad(gdn_a_log), dtb_row=lane_pad(gdn_dt_bias),
        gnorm=row(gdn_norm), sinks=swa_sinks[i].astype(F32), wo=wo,
        g2pre=row(norm_ffn2_pre), g2post=row(norm_ffn2_post), wg2=wg2, wu2=wu2, wd2=wd2,
        wpg=wpg, wpp=ple_proj[i].astype(BF16), gple=row(norm_ple_post))


def kernel(x_prompt, x_sample, state_conv, state_gdn, cache_swa_k, cache_swa_v, p_prompt, p_sample,
           rel_bias, norm_ffn1_pre, norm_ffn1_post, ffn1_w_gate, ffn1_w_up, ffn1_w_down,
           norm_mix_pre, norm_mix_post, w_in, conv_w, gdn_a_log, gdn_dt_bias, gdn_norm, swa_sinks,
           w_out, norm_ffn2_pre, norm_ffn2_post, ffn2_w_gate, ffn2_w_up, ffn2_w_down,
           ple_gate, ple_proj, norm_ple_post):
    depth = w_in.shape[0]
    batch, seq, _ = x_prompt.shape
    dec_batch, dec_seq, _ = x_sample.shape
    wb = cache_swa_k.shape[2]
    wp = min(SWA_WINDOW, seq)
    rel_bias = rel_bias.astype(F32)
    bias_p = _prompt_bias(rel_bias)
    bias_db, bias_dn = _decode_bias(rel_bias, wb, dec_seq)

    yp = x_prompt.reshape(batch * seq, D_MODEL)
    ys = x_sample.reshape(dec_batch * dec_seq, D_MODEL)
    outs = [[] for _ in range(8)]
    for i in range(depth):
        lw = _pack_layer(i, norm_ffn1_pre, norm_ffn1_post, ffn1_w_gate, ffn1_w_up, ffn1_w_down,
                         norm_mix_pre, norm_mix_post, w_in, conv_w, gdn_a_log, gdn_dt_bias, gdn_norm,
                         swa_sinks, w_out, norm_ffn2_pre, norm_ffn2_post, ffn2_w_gate, ffn2_w_up,
                         ffn2_w_down, ple_gate, ple_proj, norm_ple_post)
        x1, qkvz, ba, qs, kv = _head_call(yp, lw)
        gdn_o, s_fin = _gdn_prompt_call(qkvz, ba, lw, batch, seq)
        swa_o = _swa_prompt_call(qs, kv, bias_p, lw["sinks"], batch, seq)
        yp = _tail_call(x1, gdn_o, swa_o, p_prompt[i].reshape(batch * seq, PLE_DIM), lw)
        kv3 = kv.reshape(batch, seq, 2 * SWA_KV_DIM)
        outs[0].append(qkvz.reshape(batch, seq, QKVZ_DIM)[:, seq - (CONV_WIDTH - 1):, :CONV_DIM])
        outs[1].append(s_fin)
        outs[2].append(kv3[:, seq - wp:, :SWA_KV_DIM].reshape(batch, wp, SWA_KV_HEADS, SWA_HEAD_DIM))
        outs[3].append(kv3[:, seq - wp:, SWA_KV_DIM:].reshape(batch, wp, SWA_KV_HEADS, SWA_HEAD_DIM))
        x1, qkvz, ba, qs, kv = _head_call(ys, lw)
        gdn_o, s_new = _gdn_decode_call(qkvz, ba, state_conv[i], state_gdn[i], lw, dec_batch, dec_seq)
        kt = jnp.transpose(cache_swa_k[i], (0, 2, 3, 1))
        vt = jnp.transpose(cache_swa_v[i], (0, 2, 3, 1))
        swa_o, kt_new, vt_new = _swa_decode_call(qs, kv, kt, vt, bias_db, bias_dn, lw["sinks"], dec_batch, dec_seq)
        ys = _tail_call(x1, gdn_o, swa_o, p_sample[i].reshape(dec_batch * dec_seq, PLE_DIM), lw)
        xp = jnp.concatenate([state_conv[i], qkvz.reshape(dec_batch, dec_seq, QKVZ_DIM)[:, :, :CONV_DIM]], axis=1)
        outs[4].append(xp[:, dec_seq:])
        outs[5].append(s_new)
        outs[6].append(jnp.transpose(kt_new, (0, 3, 1, 2)))
        outs[7].append(jnp.transpose(vt_new, (0, 3, 1, 2)))
    return (yp.reshape(batch, seq, D_MODEL), ys.reshape(dec_batch, dec_seq, D_MODEL),
            *[jnp.stack(o) for o in outs])
```

```python
import functools
import math

import numpy as np
import jax
import jax.numpy as jnp
from jax import lax
from jax.experimental import pallas as pl
from jax.experimental.pallas import tpu as pltpu

F32 = jnp.float32
BF16 = jnp.bfloat16

D_MODEL = 1024
NORM_EPS = 1e-6
PLE_DIM = 256
FFN_DIM = 2816
GDN_HEADS = 4
GDN_DK = 128
GDN_DV = 128
GDN_KEY_DIM = GDN_HEADS * GDN_DK
GDN_VAL_DIM = GDN_HEADS * GDN_DV
CONV_DIM = 2 * GDN_KEY_DIM + GDN_VAL_DIM
CONV_WIDTH = 4
SWA_HEADS = 8
SWA_KV_HEADS = 2
SWA_GROUP = SWA_HEADS // SWA_KV_HEADS
SWA_HEAD_DIM = 64
SWA_Q_DIM = SWA_HEADS * SWA_HEAD_DIM
SWA_KV_DIM = SWA_KV_HEADS * SWA_HEAD_DIM
SWA_WINDOW = 128
NUM_BUCKETS = 32
REL_MAX_DISTANCE = 128

QKVZ_DIM = CONV_DIM + GDN_VAL_DIM
BA_DIM = 128
PROJ_PACKED = QKVZ_DIM + BA_DIM + SWA_Q_DIM + 2 * SWA_KV_DIM

GDN_CHUNK = 64
GDN_STEP_CHUNKS = 4
SWA_STEP_BLOCKS = 4
ROW_TILE = 512
DEC_SEQ_BLOCK = 16
PACK_ROWS = 128
CAST_BLOCK_BYTES = 2 * 1024 * 1024
VMEM_LIMIT = 56 * 1024 * 1024

NEG_INF = float("-inf")
LANES = 128


def _resident(shape):
    nd = len(shape)
    return pl.BlockSpec(shape, lambda *_: (0,) * nd, pipeline_mode=pl.Buffered(1))


def _rms(x, gain):
    ms = jnp.mean(x * x, axis=-1, keepdims=True)
    return (x * lax.rsqrt(ms + NORM_EPS)) * gain


def _sigmoid(x):
    return 1.0 / (1.0 + jnp.exp(-x))


def _silu(x):
    h = 0.5 * x
    return h + h * jnp.tanh(h)


def _dot(a, b):
    return jnp.dot(a, b, preferred_element_type=F32)


def _dot_nt(a, b):
    return lax.dot_general(a, b, (((1,), (1,)), ((), ())), preferred_element_type=F32)


def _dot_tn(a, b):
    return lax.dot_general(a, b, (((0,), (0,)), ((), ())), preferred_element_type=F32)


def _split(a):
    hi = a.astype(BF16)
    lo = (a - hi.astype(F32)).astype(BF16)
    return hi, lo


def _inv_rms(x):
    return lax.rsqrt(jnp.mean(x * x, axis=-1, keepdims=True) + NORM_EPS)


def _prenorm_dots(x, gain, w_refs):
    h = (x * gain).astype(BF16)
    inv = _inv_rms(x)
    return [jnp.dot(h, w, preferred_element_type=F32) * inv for w in w_refs]


def _swiglu_block(x, gain, wg_ref, wu_ref, wd_ref):
    g, u = _prenorm_dots(x, gain, [wg_ref[...], wu_ref[...]])
    a = (_silu(g) * u).astype(BF16)
    return jnp.dot(a, wd_ref[...], preferred_element_type=F32)


def _head_kernel(x_ref, g1pre_ref, g1post_ref, wg_ref, wu_ref, wd_ref, gmix_ref, win_ref,
                 x1_ref, qkvz_ref, ba_ref, qs_ref, kv_ref):
    x = x_ref[...]
    y = _swiglu_block(x, g1pre_ref[...], wg_ref, wu_ref, wd_ref)
    x1 = x + 0.5 * _rms(y, g1post_ref[...])
    x1_ref[...] = x1
    c0, c1, c2 = QKVZ_DIM, QKVZ_DIM + BA_DIM, QKVZ_DIM + BA_DIM + SWA_Q_DIM
    qkvz_ref[...], ba_ref[...], qs_ref[...], kv_ref[...] = _prenorm_dots(
        x1, gmix_ref[...], [win_ref[:, :c0], win_ref[:, c0:c1], win_ref[:, c1:c2], win_ref[:, c2:]])


def _head_call(x, lw):
    n = x.shape[0]
    tm = min(ROW_TILE, n)
    row = lambda w: pl.BlockSpec((tm, w), lambda i: (i, 0))
    return pl.pallas_call(
        _head_kernel,
        grid=(n // tm,),
        in_specs=[row(D_MODEL), _resident((1, D_MODEL)), _resident((1, D_MODEL)),
                  _resident((D_MODEL, FFN_DIM)), _resident((D_MODEL, FFN_DIM)),
                  _resident((FFN_DIM, D_MODEL)), _resident((1, D_MODEL)),
                  _resident((D_MODEL, PROJ_PACKED))],
        out_specs=[row(D_MODEL), row(QKVZ_DIM), row(BA_DIM), row(SWA_Q_DIM), row(2 * SWA_KV_DIM)],
        out_shape=[jax.ShapeDtypeStruct((n, D_MODEL), F32), jax.ShapeDtypeStruct((n, QKVZ_DIM), F32),
                   jax.ShapeDtypeStruct((n, BA_DIM), F32), jax.ShapeDtypeStruct((n, SWA_Q_DIM), F32),
                   jax.ShapeDtypeStruct((n, 2 * SWA_KV_DIM), F32)],
        compiler_params=pltpu.CompilerParams(dimension_semantics=("arbitrary",),
                                             vmem_limit_bytes=VMEM_LIMIT),
        name="ffn1_inproj",
    )(x, lw["g1pre"], lw["g1post"], lw["wg1"], lw["wu1"], lw["wd1"], lw["gmix"], lw["win"])


def _tail_kernel(x_ref, gdn_ref, swa_ref, p_ref, wo_ref, gmixpost_ref, g2pre_ref, g2post_ref,
                 wg_ref, wu_ref, wd_ref, wpg_ref, wpp_ref, gple_ref, y_ref):
    x = x_ref[...]
    mix = (jnp.dot(gdn_ref[...].astype(BF16), wo_ref[:GDN_VAL_DIM, :], preferred_element_type=F32)
           + jnp.dot(swa_ref[...].astype(BF16), wo_ref[GDN_VAL_DIM:, :], preferred_element_type=F32))
    pp = jnp.dot(p_ref[...].astype(BF16), wpp_ref[...], preferred_element_type=F32)
    x = x + _rms(mix, gmixpost_ref[...])
    y = _swiglu_block(x, g2pre_ref[...], wg_ref, wu_ref, wd_ref)
    x = x + 0.5 * _rms(y, g2post_ref[...])
    gate = _sigmoid(jnp.dot(x.astype(BF16), wpg_ref[...], preferred_element_type=F32))
    y_ref[...] = x + _rms(gate * pp, gple_ref[...])


def _tail_call(x1, gdn_o, swa_o, p, lw):
    n = x1.shape[0]
    tm = min(ROW_TILE, n)
    row = lambda w: pl.BlockSpec((tm, w), lambda i: (i, 0))
    return pl.pallas_call(
        _tail_kernel,
        grid=(n // tm,),
        in_specs=[row(D_MODEL), row(GDN_VAL_DIM), row(SWA_Q_DIM), row(PLE_DIM),
                  _resident((GDN_VAL_DIM + SWA_Q_DIM, D_MODEL)), _resident((1, D_MODEL)),
                  _resident((1, D_MODEL)), _resident((1, D_MODEL)),
                  _resident((D_MODEL, FFN_DIM)), _resident((D_MODEL, FFN_DIM)),
                  _resident((FFN_DIM, D_MODEL)), _resident((D_MODEL, D_MODEL)),
                  _resident((PLE_DIM, D_MODEL)), _resident((1, D_MODEL))],
        out_specs=row(D_MODEL),
        out_shape=jax.ShapeDtypeStruct((n, D_MODEL), F32),
        compiler_params=pltpu.CompilerParams(dimension_semantics=("arbitrary",),
                                             vmem_limit_bytes=VMEM_LIMIT),
        name="outproj_ffn2_ple",
    )(x1, gdn_o, swa_o, p, lw["wo"], lw["gmixpost"], lw["g2pre"], lw["g2post"],
      lw["wg2"], lw["wu2"], lw["wd2"], lw["wpg"], lw["wpp"], lw["gple"])


def _iota2(c):
    return (lax.broadcasted_iota(jnp.int32, (c, c), 0), lax.broadcasted_iota(jnp.int32, (c, c), 1))


def _gates(ba, alog_row, dtb_row):
    beta = _sigmoid(ba)
    xa = ba + dtb_row
    softplus = jnp.maximum(xa, 0.0) + jnp.log1p(jnp.exp(-jnp.abs(xa)))
    g = -jnp.exp(alog_row) * softplus
    return beta, g


def _cumsum_rows(mask01, g):
    hi, lo = _split(g)
    m = mask01.astype(BF16)
    return jnp.dot(m, hi, preferred_element_type=F32) + jnp.dot(m, lo, preferred_element_type=F32)


def _hi_lo_lanes(x):
    hi, lo = _split(x)
    return jnp.concatenate([hi, lo], axis=1)


def _rowsum_bcast(x, ones_ref):
    return jnp.dot(x, ones_ref[...], preferred_element_type=F32)


def _lane_bcast(x, sel_ref, lanes):
    out = jnp.dot(_hi_lo_lanes(x), sel_ref[...], preferred_element_type=F32)
    return [out[:, l * 128:(l + 1) * 128] for l in lanes]


def _gdn_prep_steps(out, y, beta_b, gc_b, gl_b, gc_t, chunk, group, ones_ref, normalized):
    c = chunk
    n_chunks = y.shape[0] // c
    units = [(ci, h) for ci in range(n_chunks) for h in range(GDN_HEADS)]
    n = range(len(units))
    ii, jj = _iota2(c)
    same = (ii // group) == (jj // group)
    incl = jnp.logical_and(same, ii >= jj)
    strict = jnp.logical_and(same, ii > jj)

    def rows(t, ci, lo, hi):
        return t[ci * c:(ci + 1) * c, lo:hi]

    eg_b = [jnp.exp(t) for t in gc_b]
    ekd_b = [jnp.exp(gl_b[h] - gc_b[h]) for h in range(GDN_HEADS)]
    beta_u = [rows(beta_b[h], ci, 0, GDN_DK) for ci, h in units]
    eg_u = [rows(eg_b[h], ci, 0, GDN_DK) for ci, h in units]
    ekd_u = [rows(ekd_b[h], ci, 0, GDN_DK) for ci, h in units]
    gc_col = [rows(gc_b[h], ci, 0, c) for ci, h in units]
    gc_row = [gc_t[GDN_HEADS + h:GDN_HEADS + h + 1, ci * c:(ci + 1) * c] for ci, h in units]
    q = [rows(y, ci, h * GDN_DK, (h + 1) * GDN_DK) for ci, h in units]
    k = [rows(y, ci, GDN_KEY_DIM + h * GDN_DK, GDN_KEY_DIM + (h + 1) * GDN_DK) for ci, h in units]
    v = [rows(y, ci, 2 * GDN_KEY_DIM + h * GDN_DV, 2 * GDN_KEY_DIM + (h + 1) * GDN_DV) for ci, h in units]
    if not normalized:
        q = [_l2norm(t, ones_ref) * (GDN_DK ** -0.5) for t in q]
        k = [_l2norm(t, ones_ref) for t in k]
    decay = [jnp.exp(jnp.where(incl, gc_col[i] - gc_row[i], NEG_INF)) for i in n]
    kb = [k[i] * beta_u[i] for i in n]
    kq = [_dot_nt(jnp.concatenate([kb[i], q[i]], axis=0), k[i]) for i in n]
    yield
    a_mat = [jnp.where(strict, kq[i][:c] * decay[i], 0.0) for i in n]
    qk = [kq[i][c:] * decay[i] for i in n]
    eye = jnp.where(ii == jj, 1.0, 0.0).astype(F32)
    t_mat = [eye for _ in n]
    b = 1
    while b < group:
        lower = jnp.logical_and((ii // (2 * b)) == (jj // (2 * b)),
                                jnp.logical_and((ii % (2 * b)) >= b, (jj % (2 * b)) < b))
        m = [jnp.where(lower, a_mat[i], 0.0) for i in n]
        if b == 1:
            t_mat = [eye - m[i] for i in n]
        else:
            tm = [_dot(t_mat[i], m[i]) for i in n]
            yield
            t_mat = [t_mat[i] - _dot(tm[i], t_mat[i]) for i in n]
            yield
        b *= 2
    rhs = [jnp.concatenate([v[i] * beta_u[i], kb[i] * eg_u[i]], axis=-1) for i in n]
    sol = [_dot(t_mat[i], rhs[i]) for i in n]
    yield
    nest = lambda xs: [xs[ci * GDN_HEADS:(ci + 1) * GDN_HEADS] for ci in range(n_chunks)]
    out.update(u=nest([s[:, :GDN_DV] for s in sol]), w=nest([s[:, GDN_DV:] for s in sol]), qk=nest(qk),
               qd=nest([q[i] * eg_u[i] for i in n]), kd=nest([k[i] * ekd_u[i] for i in n]))


def _l2norm(t, ones_ref):
    return t * lax.rsqrt(_rowsum_bcast(t * t, ones_ref) + 1e-6)


def _gdn_out(o, z, gnorm_row, ones_ref):
    ms = _rowsum_bcast(o * o, ones_ref) * (1.0 / GDN_DV)
    return (o * lax.rsqrt(ms + NORM_EPS)) * gnorm_row * _silu(z)


def _conv_silu(xbuf_ref, cw_ref, c):
    y = xbuf_ref[pl.ds(5, c), :] * cw_ref[0:1, :]
    for j in range(1, CONV_WIDTH):
        y = y + xbuf_ref[pl.ds(5 + j, c), :] * cw_ref[j:j + 1, :]
    return _silu(y)


def _gdn_prompt_kernel(qkv_ref, z_ref, ba_ref, cw_ref, alog_ref, dtb_ref, gnorm_ref, ones_ref, sel_ref,
                       o_ref, sfin_ref, s_ref, xbuf_ref, y0_ref, g0_ref, gt0_ref, y1_ref, g1_ref, gt1_ref):
    step = pl.program_id(1)

    @pl.when(step == 0)
    def _():
        s_ref[...] = jnp.zeros_like(s_ref)
        xbuf_ref[0:8, :] = jnp.zeros((8, CONV_DIM), F32)
        y1_ref[...] = jnp.zeros_like(y1_ref)
        g1_ref[...] = jnp.zeros_like(g1_ref)
        gt1_ref[...] = jnp.zeros_like(gt1_ref)

    @pl.when(lax.rem(step, 2) == 0)
    def _():
        _gdn_prompt_step((y0_ref, g0_ref, gt0_ref), (y1_ref, g1_ref, gt1_ref), qkv_ref, z_ref, ba_ref, cw_ref,
                         alog_ref, dtb_ref, gnorm_ref, ones_ref, sel_ref, o_ref, sfin_ref, s_ref, xbuf_ref)

    @pl.when(lax.rem(step, 2) == 1)
    def _():
        _gdn_prompt_step((y1_ref, g1_ref, gt1_ref), (y0_ref, g0_ref, gt0_ref), qkv_ref, z_ref, ba_ref, cw_ref,
                         alog_ref, dtb_ref, gnorm_ref, ones_ref, sel_ref, o_ref, sfin_ref, s_ref, xbuf_ref)


def _gdn_prompt_step(a_refs, b_refs, qkv_ref, z_ref, ba_ref, cw_ref, alog_ref, dtb_ref, gnorm_ref, ones_ref,
                     sel_ref, o_ref, sfin_ref, s_ref, xbuf_ref):
    ya_ref, ga_ref, gta_ref = a_refs
    yb_ref, gb_ref, gtb_ref = b_refs
    c = GDN_CHUNK
    n_chunks = GDN_STEP_CHUNKS
    r = n_chunks * c
    hs = range(GDN_HEADS)

    def stage_a():
        xbuf_ref[8:8 + r, :] = qkv_ref[...]
        beta, g = _gates(ba_ref[...], alog_ref[...], dtb_ref[...])
        ii, jj = _iota2(r)
        tril = jnp.where(jnp.logical_and((ii // c) == (jj // c), ii >= jj), 1.0, 0.0)
        gc = _cumsum_rows(tril, g)
        gta_ref[...] = gc.T
        lane = lax.broadcasted_iota(jnp.int32, (r, BA_DIM), 1)
        ga_ref[...] = jnp.where(lane < GDN_HEADS, beta, gc)
        yield
        for ci in range(n_chunks):
            rows = slice(ci * c, (ci + 1) * c)
            for slab in range(CONV_DIM // LANES):
                cols = slice(slab * LANES, (slab + 1) * LANES)
                ext = xbuf_ref[pl.ds(ci * c, c + 8), cols]
                s1 = pltpu.roll(ext, 1, axis=0)
                u2 = pltpu.roll(ext * cw_ref[1:2, cols] + s1 * cw_ref[0:1, cols], 2, axis=0)
                yc = (ext * cw_ref[3:4, cols] + s1 * cw_ref[2:3, cols] + u2)[8:]
                ya_ref[rows, cols] = _silu(yc)
                if slab < GDN_KEY_DIM // LANES:
                    ya_ref[rows, cols] = _l2norm(ya_ref[rows, cols], ones_ref) * (GDN_DK ** -0.5)
                elif slab < 2 * GDN_KEY_DIM // LANES:
                    ya_ref[rows, cols] = _l2norm(ya_ref[rows, cols], ones_ref)
                yield
        xbuf_ref[0:8, :] = xbuf_ref[r:r + 8, :]
        yield

    def stage_b():
        bcast = _lane_bcast(gb_ref[...], sel_ref, range(2 * GDN_HEADS))
        beta_b, gc_b = bcast[:GDN_HEADS], bcast[GDN_HEADS:]
        glast = [[gc_b[h][(ci + 1) * c - 1:(ci + 1) * c, :] for h in hs] for ci in range(n_chunks)]
        gl_b = [jnp.concatenate([jnp.broadcast_to(glast[ci][h], (c, GDN_DK)) for ci in range(n_chunks)],
                                axis=0) for h in hs]
        wy = {}
        yield from _gdn_prep_steps(wy, yb_ref, beta_b, gc_b, gl_b, gtb_ref[...], c, c, ones_ref,
                                   normalized=True)
        u, w, qk, qd, kd = (wy[name] for name in ("u", "w", "qk", "qd", "kd"))
        s_cur = [s_ref[h] for h in hs]
        for ci in range(n_chunks):
            ws = [_dot(jnp.concatenate([w[ci][h], qd[ci][h]], axis=0), s_cur[h]) for h in hs]
            yield
            v_new = [u[ci][h] - ws[h][:c] for h in hs]
            o = [ws[h][c:] + _dot(qk[ci][h], v_new[h]) for h in hs]
            s_cur = [s_cur[h] * jnp.exp(glast[ci][h]) + _dot_tn(kd[ci][h], v_new[h]) for h in hs]
            yield
            for h in hs:
                z = z_ref[ci * c:(ci + 1) * c, h * GDN_DV:(h + 1) * GDN_DV]
                o_ref[ci * c:(ci + 1) * c, h * GDN_DV:(h + 1) * GDN_DV] = _gdn_out(o[h], z, gnorm_ref[...], ones_ref)
        for h in hs:
            s_ref[h] = s_cur[h]
            sfin_ref[0, h] = s_cur[h]

    a_pieces = 2 + n_chunks * (CONV_DIM // LANES)
    b_levels = 2 * (int(math.log2(c)) - 1) + 2 + 2 * n_chunks
    a_steps = stage_a()
    done = 0
    for i, _ in enumerate(stage_b()):
        target = -(-(i + 1) * a_pieces // b_levels)
        for _ in range(target - done):
            next(a_steps, None)
        done = target
    for _ in a_steps:
        pass


def _gdn_prompt_call(qkvz, ba, lw, batch, seq):
    c = GDN_CHUNK * GDN_STEP_CHUNKS
    nc = seq // c
    z_col = CONV_DIM // GDN_VAL_DIM
    return pl.pallas_call(
        _gdn_prompt_kernel,
        grid=(batch, nc + 1),
        in_specs=[pl.BlockSpec((c, CONV_DIM), lambda b, s: (b * nc + jnp.minimum(s, nc - 1), 0)),
                  pl.BlockSpec((c, GDN_VAL_DIM), lambda b, s: (b * nc + jnp.maximum(s - 1, 0), z_col)),
                  pl.BlockSpec((c, BA_DIM), lambda b, s: (b * nc + jnp.minimum(s, nc - 1), 0)),
                  _resident((CONV_WIDTH, CONV_DIM)), _resident((1, BA_DIM)), _resident((1, BA_DIM)),
                  _resident((1, GDN_DV)), _resident(lw["ones"].shape), _resident(lw["sel"].shape)],
        out_specs=[pl.BlockSpec((c, GDN_VAL_DIM), lambda b, s: (b * nc + jnp.maximum(s - 1, 0), 0)),
                   pl.BlockSpec((1, GDN_HEADS, GDN_DK, GDN_DV), lambda b, s: (b, 0, 0, 0))],
        out_shape=[jax.ShapeDtypeStruct((batch * seq, GDN_VAL_DIM), F32),
                   jax.ShapeDtypeStruct((batch, GDN_HEADS, GDN_DK, GDN_DV), F32)],
        scratch_shapes=[pltpu.VMEM((GDN_HEADS, GDN_DK, GDN_DV), F32),
                        pltpu.VMEM((c + 8, CONV_DIM), F32),
                        pltpu.VMEM((c, CONV_DIM), F32), pltpu.VMEM((c, BA_DIM), F32), pltpu.VMEM((BA_DIM, c), F32),
                        pltpu.VMEM((c, CONV_DIM), F32), pltpu.VMEM((c, BA_DIM), F32), pltpu.VMEM((BA_DIM, c), F32)],
        compiler_params=pltpu.CompilerParams(dimension_semantics=("arbitrary", "arbitrary"),
                                             vmem_limit_bytes=VMEM_LIMIT),
        name="gdn_prompt",
    )(qkvz, qkvz, ba, lw["conv_w"], lw["alog_row"], lw["dtb_row"], lw["gnorm"], lw["ones"], lw["sel"])


def _gdn_decode_kernel(qkvz_ref, ba_ref, sconv_ref, s0_ref, cw_ref, alog_ref, dtb_ref, gnorm_ref,
                       ones_ref, sel_ref, o_ref, snew_ref, xbuf_ref, *, dec_seq):
    nb = DEC_SEQ_BLOCK
    t = dec_seq
    c = nb * t
    ys = []
    for s in range(nb):
        xbuf_ref[s, 5:8, :] = sconv_ref[s]
        xbuf_ref[s, 8:8 + t, :] = qkvz_ref[s * t:(s + 1) * t, :CONV_DIM]
        ys.append(_conv_silu(xbuf_ref.at[s], cw_ref, t))
    y = jnp.concatenate(ys, axis=0)

    beta, g = _gates(ba_ref[...], alog_ref[...], dtb_ref[...])
    ii, jj = _iota2(c)
    same = (ii // t) == (jj // t)
    tril = jnp.where(jnp.logical_and(same, ii >= jj), 1.0, 0.0).astype(F32)
    ones = jnp.where(same, 1.0, 0.0).astype(F32)
    gc = _cumsum_rows(tril, g)
    gl = _cumsum_rows(ones, g)
    gc_t = gc.T
    lane = lax.broadcasted_iota(jnp.int32, (c, BA_DIM), 1)
    bcast = _lane_bcast(jnp.where(lane < GDN_HEADS, beta, gc), sel_ref, range(2 * GDN_HEADS))
    beta_b, gc_b = bcast[:GDN_HEADS], bcast[GDN_HEADS:]
    gl_b = _lane_bcast(gl, sel_ref, range(GDN_HEADS, 2 * GDN_HEADS))
    wy = {}
    for _ in _gdn_prep_steps(wy, y, beta_b, gc_b, gl_b, gc_t, c, t, ones_ref, normalized=False):
        pass
    u, w, qk, qd, kd = (wy[name][0] for name in ("u", "w", "qk", "qd", "kd"))
    hs = range(GDN_HEADS)
    units = [(s, h) for h in hs for s in range(nb)]
    s_old = {(s, h): s0_ref[s, h] for s, h in units}
    ws = {(s, h): _dot(jnp.concatenate([w[h][s * t:(s + 1) * t], qd[h][s * t:(s + 1) * t]], axis=0),
                       s_old[s, h]) for s, h in units}
    v_new = [jnp.concatenate([u[h][s * t:(s + 1) * t] - ws[s, h][:t] for s in range(nb)], axis=0) for h in hs]
    o = [jnp.concatenate([ws[s, h][t:] for s in range(nb)], axis=0) + _dot(qk[h], v_new[h]) for h in hs]
    cd = [jnp.exp(gl_b[h]) for h in hs]
    kd_t = [kd[h].T for h in hs]
    col_seq = lax.broadcasted_iota(jnp.int32, (GDN_DK, c), 1) // t
    for s, h in units:
        upd = _dot(jnp.where(col_seq == s, kd_t[h], 0.0), v_new[h])
        snew_ref[s, h] = s_old[s, h] * cd[h][s * t:s * t + 1, :] + upd
    for h in hs:
        z = qkvz_ref[:, CONV_DIM + h * GDN_DV:CONV_DIM + (h + 1) * GDN_DV]
        o_ref[:, h * GDN_DV:(h + 1) * GDN_DV] = _gdn_out(o[h], z, gnorm_ref[...], ones_ref)


def _gdn_decode_call(qkvz, ba, state_conv, state_gdn, lw, dec_batch, dec_seq):
    nb = DEC_SEQ_BLOCK
    rows = nb * dec_seq
    return pl.pallas_call(
        functools.partial(_gdn_decode_kernel, dec_seq=dec_seq),
        grid=(dec_batch // nb,),
        in_specs=[pl.BlockSpec((rows, QKVZ_DIM), lambda i: (i, 0)),
                  pl.BlockSpec((rows, BA_DIM), lambda i: (i, 0)),
                  pl.BlockSpec((nb, CONV_WIDTH - 1, CONV_DIM), lambda i: (i, 0, 0)),
                  pl.BlockSpec((nb, GDN_HEADS, GDN_DK, GDN_DV), lambda i: (i, 0, 0, 0)),
                  _resident((CONV_WIDTH, CONV_DIM)), _resident((1, BA_DIM)), _resident((1, BA_DIM)),
                  _resident((1, GDN_DV)), _resident(lw["ones"].shape), _resident(lw["sel"].shape)],
        out_specs=[pl.BlockSpec((rows, GDN_VAL_DIM), lambda i: (i, 0)),
                   pl.BlockSpec((nb, GDN_HEADS, GDN_DK, GDN_DV), lambda i: (i, 0, 0, 0))],
        out_shape=[jax.ShapeDtypeStruct((dec_batch * dec_seq, GDN_VAL_DIM), F32),
                   jax.ShapeDtypeStruct((dec_batch, GDN_HEADS, GDN_DK, GDN_DV), F32)],
        scratch_shapes=[pltpu.VMEM((nb, 16, CONV_DIM), F32)],
        compiler_params=pltpu.CompilerParams(dimension_semantics=("arbitrary",),
                                             vmem_limit_bytes=VMEM_LIMIT),
        name="gdn_decode",
    )(qkvz, ba, state_conv, state_gdn, lw["conv_w"], lw["alog_row"], lw["dtb_row"], lw["gnorm"],
      lw["ones"], lw["sel"])


def _bias_table_kernel(bucket_ref, rb_ref, out_ref):
    bucket = bucket_ref[...]
    for h in range(SWA_HEADS):
        acc = jnp.zeros(bucket.shape, F32)
        for b in range(NUM_BUCKETS):
            acc = jnp.where(bucket == b, rb_ref[b, h], acc)
        out_ref[h] = jnp.where(bucket < 0, NEG_INF, acc)


def _bias_table_call(bucket, rel_bias):
    r, c = bucket.shape
    return pl.pallas_call(
        _bias_table_kernel,
        in_specs=[pl.BlockSpec(memory_space=pltpu.VMEM), pl.BlockSpec(memory_space=pltpu.SMEM)],
        out_specs=pl.BlockSpec(memory_space=pltpu.VMEM),
        out_shape=jax.ShapeDtypeStruct((SWA_HEADS, r, c), F32),
        name="t5_bias_table",
    )(bucket, rel_bias)


def _t5_bucket(dist, valid):
    d = jnp.maximum(dist, 0)
    exact = NUM_BUCKETS // 2
    log_ratio = jnp.log(jnp.maximum(d, 1).astype(F32) / exact) / math.log(REL_MAX_DISTANCE / exact)
    large = jnp.minimum(exact + (log_ratio * (NUM_BUCKETS - exact)).astype(jnp.int32), NUM_BUCKETS - 1)
    bucket = jnp.where(d < exact, d, large)
    return jnp.where(valid, bucket, -1).astype(jnp.int32)


def _swa_prompt_kernel(q_ref, kvp_ref, kvc_ref, bias_ref, sink_ref, o_ref):
    w, dh, nq = SWA_WINDOW, SWA_HEAD_DIM, SWA_STEP_BLOCKS
    first = pl.program_id(1) == 0
    col = lax.broadcasted_iota(jnp.int32, (w, 2 * w), 1)
    drop_prev = jnp.logical_and(first, col < w)
    kv = jnp.concatenate([kvp_ref[...], kvc_ref[...]], axis=0).astype(BF16)
    k_h = [kv[:, h * dh:(h + 1) * dh] for h in range(SWA_KV_HEADS)]
    v_h = [kv[:, SWA_KV_DIM + h * dh:SWA_KV_DIM + (h + 1) * dh] for h in range(SWA_KV_HEADS)]
    units = [(b, hg) for b in range(nq) for hg in range(SWA_HEADS)]
    n = range(len(units))
    q = [(q_ref[b * w:(b + 1) * w, hg * dh:(hg + 1) * dh] * (dh ** -0.5)).astype(BF16) for b, hg in units]
    s = [lax.dot_general(q[i], k_h[hg // SWA_GROUP][b * w:(b + 2) * w], (((1,), (1,)), ((), ())),
                         preferred_element_type=F32) + bias_ref[hg] for i, (b, hg) in enumerate(units)]
    s = [jnp.where(drop_prev, NEG_INF, s[i]) if b == 0 else s[i] for i, (b, hg) in enumerate(units)]
    sink = [sink_ref[hg] for b, hg in units]
    m = [jnp.maximum(jnp.max(s[i], axis=-1, keepdims=True), sink[i]) for i in n]
    p = [jnp.exp(s[i] - m[i]) for i in n]
    denom = [jnp.sum(p[i], axis=-1, keepdims=True) + jnp.exp(sink[i] - m[i]) for i in n]
    o = [jnp.dot(p[i].astype(BF16), v_h[hg // SWA_GROUP][b * w:(b + 2) * w], preferred_element_type=F32)
         / denom[i] for i, (b, hg) in enumerate(units)]
    for i, (b, hg) in enumerate(units):
        o_ref[b * w:(b + 1) * w, hg * dh:(hg + 1) * dh] = o[i]


def _swa_prompt_call(qs, kv, bias, sinks, batch, seq):
    w, nq = SWA_WINDOW, SWA_STEP_BLOCKS
    nb = seq // (w * nq)
    return pl.pallas_call(
        _swa_prompt_kernel,
        grid=(batch, nb),
        in_specs=[pl.BlockSpec((nq * w, SWA_Q_DIM), lambda b, n: (b * nb + n, 0)),
                  pl.BlockSpec((w, 2 * SWA_KV_DIM),
                               lambda b, n: ((b * nb + n) * nq - jnp.minimum(n, 1), 0)),
                  pl.BlockSpec((nq * w, 2 * SWA_KV_DIM), lambda b, n: (b * nb + n, 0)),
                  _resident((SWA_HEADS, w, 2 * w)),
                  pl.BlockSpec(memory_space=pltpu.SMEM)],
        out_specs=pl.BlockSpec((nq * w, SWA_Q_DIM), lambda b, n: (b * nb + n, 0)),
        out_shape=jax.ShapeDtypeStruct((batch * seq, SWA_Q_DIM), F32),
        compiler_params=pltpu.CompilerParams(dimension_semantics=("arbitrary", "arbitrary"),
                                             vmem_limit_bytes=VMEM_LIMIT),
        name="swa_prompt",
    )(qs, kv, kv, bias, sinks)


def _swa_decode_kernel(q_ref, kvn_ref, kt_ref, vt_ref, bbuf_ref, bnew_ref, sink_ref,
                       o_ref, kto_ref, vto_ref, *, dec_seq):
    nb = DEC_SEQ_BLOCK
    t = dec_seq
    dh = SWA_HEAD_DIM
    rows = nb * t
    m_rows = SWA_GROUP * rows
    wb = kt_ref.shape[3]
    ri = lax.broadcasted_iota(jnp.int32, (m_rows, 1), 0)
    row_seq = (ri % rows) // t
    row_grp = ri // rows
    for h in range(SWA_KV_HEADS):
        lo, hi = h * dh, (h + 1) * dh
        q = jnp.concatenate([q_ref[:, (h * SWA_GROUP + g) * dh:(h * SWA_GROUP + g + 1) * dh]
                             for g in range(SWA_GROUP)], axis=0)
        sink = jnp.zeros((m_rows, 1), F32)
        for g in range(SWA_GROUP):
            sink = jnp.where(row_grp == g, sink_ref[h * SWA_GROUP + g], sink)
        k_new = jnp.concatenate([kvn_ref[:, lo:hi], jnp.zeros((m_rows - rows, dh), F32)], axis=0)
        v_new = jnp.concatenate([kvn_ref[:, SWA_KV_DIM + lo:SWA_KV_DIM + hi],
                                 jnp.zeros((m_rows - rows, dh), F32)], axis=0)
        s_new = _dot_nt(q, k_new) * (dh ** -0.5) + bnew_ref[h]
        s_buf = jnp.zeros((m_rows, wb), F32)
        for s in range(nb):
            s_buf = jnp.where(row_seq == s, _dot(q, kt_ref[s, h]), s_buf)
        s_buf = s_buf * (dh ** -0.5) + bbuf_ref[h]
        m = jnp.maximum(jnp.maximum(jnp.max(s_buf, axis=-1, keepdims=True),
                                    jnp.max(s_new, axis=-1, keepdims=True)), sink)
        p_buf = jnp.exp(s_buf - m)
        p_new = jnp.exp(s_new - m)
        denom = (jnp.sum(p_buf, axis=-1, keepdims=True) + jnp.sum(p_new, axis=-1, keepdims=True)
                 + jnp.exp(sink - m))
        o = _dot(p_new, v_new)
        for s in range(nb):
            o = o + _dot_nt(jnp.where(row_seq == s, p_buf, 0.0), vt_ref[s, h])
        o = o / denom
        for g in range(SWA_GROUP):
            hg = h * SWA_GROUP + g
            o_ref[:, hg * dh:(hg + 1) * dh] = o[g * rows:(g + 1) * rows]
    new_t = jnp.concatenate([kvn_ref[...], jnp.zeros((wb - rows, 2 * SWA_KV_DIM), F32)], axis=0).T
    lane = lax.broadcasted_iota(jnp.int32, (dh, wb), 1)
    for part, (src_ref, dst_ref) in enumerate(((kt_ref, kto_ref), (vt_ref, vto_ref))):
        for h in range(SWA_KV_HEADS):
            fresh = new_t[part * SWA_KV_DIM + h * dh:part * SWA_KV_DIM + (h + 1) * dh, :]
            for s in range(nb):
                kept = pltpu.roll(src_ref[s, h], wb - t, axis=1)
                tail = pltpu.roll(fresh, (wb - t - s * t) % wb, axis=1)
                dst_ref[s, h] = jnp.where(lane >= wb - t, tail, kept)


def _swa_decode_call(qs, kvn, kt, vt, bias_buf, bias_new, sinks, dec_batch, dec_seq):
    nb = DEC_SEQ_BLOCK
    rows = nb * dec_seq
    wb = kt.shape[3]
    m_rows = SWA_GROUP * rows
    assert rows <= wb and wb == LANES
    win_spec = pl.BlockSpec((nb, SWA_KV_HEADS, SWA_HEAD_DIM, wb), lambda i: (i, 0, 0, 0))
    return pl.pallas_call(
        functools.partial(_swa_decode_kernel, dec_seq=dec_seq),
        grid=(dec_batch // nb,),
        in_specs=[pl.BlockSpec((rows, SWA_Q_DIM), lambda i: (i, 0)),
                  pl.BlockSpec((rows, 2 * SWA_KV_DIM), lambda i: (i, 0)),
                  win_spec, win_spec,
                  _resident((SWA_KV_HEADS, m_rows, wb)),
                  _resident((SWA_KV_HEADS, m_rows, m_rows)),
                  pl.BlockSpec(memory_space=pltpu.SMEM)],
        out_specs=[pl.BlockSpec((rows, SWA_Q_DIM), lambda i: (i, 0)), win_spec, win_spec],
        out_shape=[jax.ShapeDtypeStruct((dec_batch * dec_seq, SWA_Q_DIM), F32),
                   jax.ShapeDtypeStruct(kt.shape, F32), jax.ShapeDtypeStruct(vt.shape, F32)],
        compiler_params=pltpu.CompilerParams(dimension_semantics=("arbitrary",),
                                             vmem_limit_bytes=VMEM_LIMIT),
        name="swa_decode",
    )(qs, kvn, kt, vt, bias_buf, bias_new, sinks)


def _prompt_bias(rel_bias):
    w = SWA_WINDOW
    dist = w + jnp.arange(w)[:, None] - jnp.arange(2 * w)[None, :]
    valid = (dist >= 0) & (dist < w)
    return _bias_table_call(_t5_bucket(dist, valid), rel_bias)


def _decode_bias(rel_bias, wb, dec_seq):
    nb, t = DEC_SEQ_BLOCK, dec_seq
    rows = nb * t
    m_rows = SWA_GROUP * rows
    tok = jnp.arange(t)
    dist = jnp.concatenate([wb + tok[:, None] - jnp.arange(wb)[None, :], tok[:, None] - tok[None, :]], axis=1)
    bucket = _t5_bucket(dist, (dist >= 0) & (dist < SWA_WINDOW))
    pad_r, pad_c = -t % 8, -(wb + t) % LANES
    bucket = jnp.pad(bucket, ((0, pad_r), (0, pad_c)), constant_values=-1)
    tab = _bias_table_call(bucket, rel_bias)[:, :t, :wb + t]
    tab = tab.reshape(SWA_KV_HEADS, SWA_GROUP, 1, t, wb + t)
    per_row = jnp.broadcast_to(tab, (SWA_KV_HEADS, SWA_GROUP, nb, t, wb + t)).reshape(SWA_KV_HEADS, m_rows, wb + t)
    r_seq = (jnp.arange(m_rows) % rows) // t
    cidx = jnp.arange(m_rows)
    own = (r_seq[:, None] == (cidx // t)[None, :]) & (cidx < rows)[None, :]
    bias_new = jnp.where(own[None], jnp.tile(per_row[:, :, wb:], (1, 1, m_rows // t)), NEG_INF)
    return per_row[:, :, :wb], bias_new


def _cast_kernel(*refs):
    n = len(refs) // 2
    for src_ref, dst_ref in zip(refs[:n], refs[n:]):
        dst_ref[...] = src_ref[...].astype(dst_ref.dtype)


def _pack_w_in_kernel(w_ref, o_ref):
    n_gdn = QKVZ_DIM + 2 * GDN_HEADS
    rows = w_ref.shape[0]
    o_ref[:, :QKVZ_DIM] = w_ref[:, :QKVZ_DIM].astype(BF16)
    ba = jnp.concatenate([w_ref[:, QKVZ_DIM:n_gdn], jnp.zeros((rows, BA_DIM - 2 * GDN_HEADS), F32)], axis=1)
    o_ref[:, QKVZ_DIM:QKVZ_DIM + BA_DIM] = ba.astype(BF16)
    o_ref[:, QKVZ_DIM + BA_DIM:] = w_ref[:, n_gdn:].astype(BF16)


def _pack_w_in(w):
    r, c = w.shape
    tr = PACK_ROWS
    return pl.pallas_call(
        _pack_w_in_kernel,
        grid=(r // tr,),
        in_specs=[pl.BlockSpec((tr, c), lambda i: (i, 0))],
        out_specs=pl.BlockSpec((tr, PROJ_PACKED), lambda i: (i, 0)),
        out_shape=jax.ShapeDtypeStruct((r, PROJ_PACKED), BF16),
        compiler_params=pltpu.CompilerParams(dimension_semantics=("arbitrary",)),
        name="pack_w_in",
    )(w)


def _to_bf16(ws):
    r, c = ws[0].shape
    tr = r
    while tr * c * 4 > CAST_BLOCK_BYTES and tr % 16 == 0:
        tr //= 2
    spec = pl.BlockSpec((tr, c), lambda i: (i, 0))
    return pl.pallas_call(
        _cast_kernel,
        grid=(r // tr,),
        in_specs=[spec] * len(ws),
        out_specs=[spec] * len(ws),
        out_shape=[jax.ShapeDtypeStruct((r, c), BF16)] * len(ws),
        compiler_params=pltpu.CompilerParams(dimension_semantics=("arbitrary",)),
        name="weights_to_bf16",
    )(*ws)


def _pack_layer(i, norm_ffn1_pre, norm_ffn1_post, ffn1_w_gate, ffn1_w_up, ffn1_w_down, norm_mix_pre,
                norm_mix_post, w_in, conv_w, gdn_a_log, gdn_dt_bias, gdn_norm, swa_sinks, w_out,
                norm_ffn2_pre, norm_ffn2_post, ffn2_w_gate, ffn2_w_up, ffn2_w_down, ple_gate, ple_proj,
                norm_ple_post):
    row = lambda g: g[i].reshape(1, -1).astype(F32)
    win = _pack_w_in(w_in[i])
    lane_pad = lambda v: jnp.zeros((1, BA_DIM), F32).at[0, GDN_HEADS:2 * GDN_HEADS].set(v[i].astype(F32))
    kk = np.arange(2 * LANES)[:, None] % LANES
    sel = (kk == (np.arange(2 * GDN_HEADS * LANES)[None, :] // LANES)).astype(np.float32)
    wg1, wu1, wg2, wu2 = _to_bf16([ffn1_w_gate[i], ffn1_w_up[i], ffn2_w_gate[i], ffn2_w_up[i]])
    wd1, wd2 = _to_bf16([ffn1_w_down[i], ffn2_w_down[i]])
    wo, wpg = _to_bf16([w_out[i], ple_gate[i]])
    return dict(
        ones=jnp.ones((LANES, LANES), F32), sel=jnp.asarray(sel, BF16),
        g1pre=row(norm_ffn1_pre), g1post=row(norm_ffn1_post), wg1=wg1, wu1=wu1, wd1=wd1,
        gmix=row(norm_mix_pre), gmixpost=row(norm_mix_post), win=win,
        conv_w=conv_w[i].astype(F32), alog_row=lane_pad(gdn_a_log), dtb_row=lane_pad(gdn_dt_bias),
        gnorm=row(gdn_norm), sinks=swa_sinks[i].astype(F32), wo=wo,
        g2pre=row(norm_ffn2_pre), g2post=row(norm_ffn2_post), wg2=wg2, wu2=wu2, wd2=wd2,
        wpg=wpg, wpp=ple_proj[i].astype(BF16), gple=row(norm_ple_post))


def kernel(x_prompt, x_sample, state_conv, state_gdn, cache_swa_k, cache_swa_v, p_prompt, p_sample,
           rel_bias, norm_ffn1_pre, norm_ffn1_post, ffn1_w_gate, ffn1_w_up, ffn1_w_down,
           norm_mix_pre, norm_mix_post, w_in, conv_w, gdn_a_log, gdn_dt_bias, gdn_norm, swa_sinks,
           w_out, norm_ffn2_pre, norm_ffn2_post, ffn2_w_gate, ffn2_w_up, ffn2_w_down,
           ple_gate, ple_proj, norm_ple_post):
    depth = w_in.shape[0]
    batch, seq, _ = x_prompt.shape
    dec_batch, dec_seq, _ = x_sample.shape
    wb = cache_swa_k.shape[2]
    wp = min(SWA_WINDOW, seq)
    rel_bias = rel_bias.astype(F32)
    bias_p = _prompt_bias(rel_bias)
    bias_db, bias_dn = _decode_bias(rel_bias, wb, dec_seq)

    yp = x_prompt.reshape(batch * seq, D_MODEL)
    ys = x_sample.reshape(dec_batch * dec_seq, D_MODEL)
    outs = [[] for _ in range(8)]
    for i in range(depth):
        lw = _pack_layer(i, norm_ffn1_pre, norm_ffn1_post, ffn1_w_gate, ffn1_w_up, ffn1_w_down,
                         norm_mix_pre, norm_mix_post, w_in, conv_w, gdn_a_log, gdn_dt_bias, gdn_norm,
                         swa_sinks, w_out, norm_ffn2_pre, norm_ffn2_post, ffn2_w_gate, ffn2_w_up,
                         ffn2_w_down, ple_gate, ple_proj, norm_ple_post)
        x1, qkvz, ba, qs, kv = _head_call(yp, lw)
        gdn_o, s_fin = _gdn_prompt_call(qkvz, ba, lw, batch, seq)
        swa_o = _swa_prompt_call(qs, kv, bias_p, lw["sinks"], batch, seq)
        yp = _tail_call(x1, gdn_o, swa_o, p_prompt[i].reshape(batch * seq, PLE_DIM), lw)
        kv3 = kv.reshape(batch, seq, 2 * SWA_KV_DIM)
        outs[0].append(qkvz.reshape(batch, seq, QKVZ_DIM)[:, seq - (CONV_WIDTH - 1):, :CONV_DIM])
        outs[1].append(s_fin)
        outs[2].append(kv3[:, seq - wp:, :SWA_KV_DIM].reshape(batch, wp, SWA_KV_HEADS, SWA_HEAD_DIM))
        outs[3].append(kv3[:, seq - wp:, SWA_KV_DIM:].reshape(batch, wp, SWA_KV_HEADS, SWA_HEAD_DIM))
        x1, qkvz, ba, qs, kv = _head_call(ys, lw)
        gdn_o, s_new = _gdn_decode_call(qkvz, ba, state_conv[i], state_gdn[i], lw, dec_batch, dec_seq)
        kt = jnp.transpose(cache_swa_k[i], (0, 2, 3, 1))
        vt = jnp.transpose(cache_swa_v[i], (0, 2, 3, 1))
        swa_o, kt_new, vt_new = _swa_decode_call(qs, kv, kt, vt, bias_db, bias_dn, lw["sinks"], dec_batch, dec_seq)
        ys = _tail_call(x1, gdn_o, swa_o, p_sample[i].reshape(dec_batch * dec_seq, PLE_DIM), lw)
        xp = jnp.concatenate([state_conv[i], qkvz.reshape(dec_batch, dec_seq, QKVZ_DIM)[:, :, :CONV_DIM]], axis=1)
        outs[4].append(xp[:, dec_seq:])
        outs[5].append(s_new)
        outs[6].append(jnp.transpose(kt_new, (0, 3, 1, 2)))
        outs[7].append(jnp.transpose(vt_new, (0, 3, 1, 2)))
    return (yp.reshape(batch, seq, D_MODEL), ys.reshape(dec_batch, dec_seq, D_MODEL),
            *[jnp.stack(o) for o in outs])
```

```python
import functools
import math

import numpy as np
import jax
import jax.numpy as jnp
from jax import lax
from jax.experimental import pallas as pl
from jax.experimental.pallas import tpu as pltpu

F32 = jnp.float32
BF16 = jnp.bfloat16

D_MODEL = 1024
NORM_EPS = 1e-6
PLE_DIM = 256
FFN_DIM = 2816
GDN_HEADS = 4
GDN_DK = 128
GDN_DV = 128
GDN_KEY_DIM = GDN_HEADS * GDN_DK
GDN_VAL_DIM = GDN_HEADS * GDN_DV
CONV_DIM = 2 * GDN_KEY_DIM + GDN_VAL_DIM
CONV_WIDTH = 4
SWA_HEADS = 8
SWA_KV_HEADS = 2
SWA_GROUP = SWA_HEADS // SWA_KV_HEADS
SWA_HEAD_DIM = 64
SWA_Q_DIM = SWA_HEADS * SWA_HEAD_DIM
SWA_KV_DIM = SWA_KV_HEADS * SWA_HEAD_DIM
SWA_WINDOW = 128
NUM_BUCKETS = 32
REL_MAX_DISTANCE = 128

QKVZ_DIM = CONV_DIM + GDN_VAL_DIM
BA_DIM = 128
PROJ_PACKED = QKVZ_DIM + BA_DIM + SWA_Q_DIM + 2 * SWA_KV_DIM

GDN_CHUNK = 64
GDN_STEP_CHUNKS = 4
SWA_STEP_BLOCKS = 4
ROW_TILE = 512
DEC_SEQ_BLOCK = 16
CAST_BLOCK_BYTES = 2 * 1024 * 1024
VMEM_LIMIT = 56 * 1024 * 1024

NEG_INF = float("-inf")
LANES = 128


def _resident(shape):
    nd = len(shape)
    return pl.BlockSpec(shape, lambda *_: (0,) * nd, pipeline_mode=pl.Buffered(1))


def _rms(x, gain):
    ms = jnp.mean(x * x, axis=-1, keepdims=True)
    return (x * lax.rsqrt(ms + NORM_EPS)) * gain


def _sigmoid(x):
    return 1.0 / (1.0 + jnp.exp(-x))


def _silu(x):
    h = 0.5 * x
    return h + h * jnp.tanh(h)


def _dot(a, b):
    return jnp.dot(a, b, preferred_element_type=F32)


def _dot_nt(a, b):
    return lax.dot_general(a, b, (((1,), (1,)), ((), ())), preferred_element_type=F32)


def _dot_tn(a, b):
    return lax.dot_general(a, b, (((0,), (0,)), ((), ())), preferred_element_type=F32)


def _split(a):
    hi = a.astype(BF16)
    lo = (a - hi.astype(F32)).astype(BF16)
    return hi, lo


def _inv_rms(x):
    return lax.rsqrt(jnp.mean(x * x, axis=-1, keepdims=True) + NORM_EPS)


def _prenorm_dots(x, gain, ws, transposed=False):
    h = (x * gain).astype(BF16)
    inv = _inv_rms(x)
    dims = (((1,), (1 if transposed else 0,)), ((), ()))
    return [lax.dot_general(h, w, dims, preferred_element_type=F32) * inv for w in ws]


def _swiglu_block(x, gain, wg_ref, wu_ref, wd_ref):
    g, u = _prenorm_dots(x, gain, [wg_ref[...], wu_ref[...]])
    a = (_silu(g) * u).astype(BF16)
    return jnp.dot(a, wd_ref[...], preferred_element_type=F32)


def _head_kernel(x_ref, g1pre_ref, g1post_ref, wg_ref, wu_ref, wd_ref, gmix_ref, win_ref,
                 x1_ref, qkvz_ref, ba_ref, qs_ref, kv_ref):
    x = x_ref[...]
    y = _swiglu_block(x, g1pre_ref[...], wg_ref, wu_ref, wd_ref)
    x1 = x + 0.5 * _rms(y, g1post_ref[...])
    x1_ref[...] = x1
    c0, c1, c2 = QKVZ_DIM, QKVZ_DIM + BA_DIM, QKVZ_DIM + BA_DIM + SWA_Q_DIM
    qkvz_ref[...], ba_ref[...], qs_ref[...], kv_ref[...] = _prenorm_dots(
        x1, gmix_ref[...], [win_ref[:c0, :], win_ref[c0:c1, :], win_ref[c1:c2, :], win_ref[c2:, :]], transposed=True)


def _head_call(x, lw):
    n = x.shape[0]
    tm = min(ROW_TILE, n)
    row = lambda w: pl.BlockSpec((tm, w), lambda i: (i, 0))
    return pl.pallas_call(
        _head_kernel,
        grid=(n // tm,),
        in_specs=[row(D_MODEL), _resident((1, D_MODEL)), _resident((1, D_MODEL)),
                  _resident((D_MODEL, FFN_DIM)), _resident((D_MODEL, FFN_DIM)),
                  _resident((FFN_DIM, D_MODEL)), _resident((1, D_MODEL)),
                  _resident((PROJ_PACKED, D_MODEL))],
        out_specs=[row(D_MODEL), row(QKVZ_DIM), row(BA_DIM), row(SWA_Q_DIM), row(2 * SWA_KV_DIM)],
        out_shape=[jax.ShapeDtypeStruct((n, D_MODEL), F32), jax.ShapeDtypeStruct((n, QKVZ_DIM), F32),
                   jax.ShapeDtypeStruct((n, BA_DIM), F32), jax.ShapeDtypeStruct((n, SWA_Q_DIM), F32),
                   jax.ShapeDtypeStruct((n, 2 * SWA_KV_DIM), F32)],
        compiler_params=pltpu.CompilerParams(dimension_semantics=("arbitrary",),
                                             vmem_limit_bytes=VMEM_LIMIT),
        name="ffn1_inproj",
    )(x, lw["g1pre"], lw["g1post"], lw["wg1"], lw["wu1"], lw["wd1"], lw["gmix"], lw["win"])


def _tail_kernel(x_ref, gdn_ref, swa_ref, p_ref, wo_ref, gmixpost_ref, g2pre_ref, g2post_ref,
                 wg_ref, wu_ref, wd_ref, wpg_ref, wpp_ref, gple_ref, y_ref):
    x = x_ref[...]
    mix = (jnp.dot(gdn_ref[...].astype(BF16), wo_ref[:GDN_VAL_DIM, :], preferred_element_type=F32)
           + jnp.dot(swa_ref[...].astype(BF16), wo_ref[GDN_VAL_DIM:, :], preferred_element_type=F32))
    pp = jnp.dot(p_ref[...].astype(BF16), wpp_ref[...], preferred_element_type=F32)
    x = x + _rms(mix, gmixpost_ref[...])
    y = _swiglu_block(x, g2pre_ref[...], wg_ref, wu_ref, wd_ref)
    x = x + 0.5 * _rms(y, g2post_ref[...])
    gate = _sigmoid(jnp.dot(x.astype(BF16), wpg_ref[...], preferred_element_type=F32))
    y_ref[...] = x + _rms(gate * pp, gple_ref[...])


def _tail_call(x1, gdn_o, swa_o, p, lw):
    n = x1.shape[0]
    tm = min(ROW_TILE, n)
    row = lambda w: pl.BlockSpec((tm, w), lambda i: (i, 0))
    return pl.pallas_call(
        _tail_kernel,
        grid=(n // tm,),
        in_specs=[row(D_MODEL), row(GDN_VAL_DIM), row(SWA_Q_DIM), row(PLE_DIM),
                  _resident((GDN_VAL_DIM + SWA_Q_DIM, D_MODEL)), _resident((1, D_MODEL)),
                  _resident((1, D_MODEL)), _resident((1, D_MODEL)),
                  _resident((D_MODEL, FFN_DIM)), _resident((D_MODEL, FFN_DIM)),
                  _resident((FFN_DIM, D_MODEL)), _resident((D_MODEL, D_MODEL)),
                  _resident((PLE_DIM, D_MODEL)), _resident((1, D_MODEL))],
        out_specs=row(D_MODEL),
        out_shape=jax.ShapeDtypeStruct((n, D_MODEL), F32),
        compiler_params=pltpu.CompilerParams(dimension_semantics=("arbitrary",),
                                             vmem_limit_bytes=VMEM_LIMIT),
        name="outproj_ffn2_ple",
    )(x1, gdn_o, swa_o, p, lw["wo"], lw["gmixpost"], lw["g2pre"], lw["g2post"],
      lw["wg2"], lw["wu2"], lw["wd2"], lw["wpg"], lw["wpp"], lw["gple"])


def _iota2(c):
    return (lax.broadcasted_iota(jnp.int32, (c, c), 0), lax.broadcasted_iota(jnp.int32, (c, c), 1))


def _gates(ba, alog_row, dtb_row):
    beta = _sigmoid(ba)
    xa = ba + dtb_row
    softplus = jnp.maximum(xa, 0.0) + jnp.log1p(jnp.exp(-jnp.abs(xa)))
    g = -jnp.exp(alog_row) * softplus
    return beta, g


def _cumsum_rows(mask01, g):
    hi, lo = _split(g)
    m = mask01.astype(BF16)
    return jnp.dot(m, hi, preferred_element_type=F32) + jnp.dot(m, lo, preferred_element_type=F32)


def _hi_lo_lanes(x):
    hi, lo = _split(x)
    return jnp.concatenate([hi, lo], axis=1)


def _rowsum_bcast(x, ones_ref):
    return jnp.dot(x, ones_ref[...], preferred_element_type=F32)


def _lane_bcast(x, sel_ref, lanes):
    out = jnp.dot(_hi_lo_lanes(x), sel_ref[...], preferred_element_type=F32)
    return [out[:, l * 128:(l + 1) * 128] for l in lanes]


def _gdn_prep_steps(out, y, beta_b, gc_b, gl_b, gc_t, chunk, group, ones_ref, normalized):
    c = chunk
    n_chunks = y.shape[0] // c
    units = [(ci, h) for ci in range(n_chunks) for h in range(GDN_HEADS)]
    n = range(len(units))
    ii, jj = _iota2(c)
    same = (ii // group) == (jj // group)
    incl = jnp.logical_and(same, ii >= jj)
    strict = jnp.logical_and(same, ii > jj)

    def rows(t, ci, lo, hi):
        return t[ci * c:(ci + 1) * c, lo:hi]

    eg_b = [jnp.exp(t) for t in gc_b]
    ekd_b = [jnp.exp(gl_b[h] - gc_b[h]) for h in range(GDN_HEADS)]
    beta_u = [rows(beta_b[h], ci, 0, GDN_DK) for ci, h in units]
    eg_u = [rows(eg_b[h], ci, 0, GDN_DK) for ci, h in units]
    ekd_u = [rows(ekd_b[h], ci, 0, GDN_DK) for ci, h in units]
    gc_col = [rows(gc_b[h], ci, 0, c) for ci, h in units]
    gc_row = [gc_t[GDN_HEADS + h:GDN_HEADS + h + 1, ci * c:(ci + 1) * c] for ci, h in units]
    q = [rows(y, ci, h * GDN_DK, (h + 1) * GDN_DK) for ci, h in units]
    k = [rows(y, ci, GDN_KEY_DIM + h * GDN_DK, GDN_KEY_DIM + (h + 1) * GDN_DK) for ci, h in units]
    v = [rows(y, ci, 2 * GDN_KEY_DIM + h * GDN_DV, 2 * GDN_KEY_DIM + (h + 1) * GDN_DV) for ci, h in units]
    if not normalized:
        q = [_l2norm(t, ones_ref) * (GDN_DK ** -0.5) for t in q]
        k = [_l2norm(t, ones_ref) for t in k]
    decay = [jnp.exp(jnp.where(incl, gc_col[i] - gc_row[i], NEG_INF)) for i in n]
    kb = [k[i] * beta_u[i] for i in n]
    kq = [_dot_nt(jnp.concatenate([kb[i], q[i]], axis=0), k[i]) for i in n]
    yield
    a_mat = [jnp.where(strict, kq[i][:c] * decay[i], 0.0) for i in n]
    qk = [kq[i][c:] * decay[i] for i in n]
    eye = jnp.where(ii == jj, 1.0, 0.0).astype(F32)
    t_mat = [eye for _ in n]
    b = 1
    while b < group:
        lower = jnp.logical_and((ii // (2 * b)) == (jj // (2 * b)),
                                jnp.logical_and((ii % (2 * b)) >= b, (jj % (2 * b)) < b))
        m = [jnp.where(lower, a_mat[i], 0.0) for i in n]
        if b == 1:
            t_mat = [eye - m[i] for i in n]
        else:
            tm = [_dot(t_mat[i], m[i]) for i in n]
            yield
            t_mat = [t_mat[i] - _dot(tm[i], t_mat[i]) for i in n]
            yield
        b *= 2
    rhs = [jnp.concatenate([v[i] * beta_u[i], kb[i] * eg_u[i]], axis=-1) for i in n]
    sol = [_dot(t_mat[i], rhs[i]) for i in n]
    yield
    nest = lambda xs: [xs[ci * GDN_HEADS:(ci + 1) * GDN_HEADS] for ci in range(n_chunks)]
    out.update(u=nest([s[:, :GDN_DV] for s in sol]), w=nest([s[:, GDN_DV:] for s in sol]), qk=nest(qk),
               qd=nest([q[i] * eg_u[i] for i in n]), kd=nest([k[i] * ekd_u[i] for i in n]))


def _l2norm(t, ones_ref):
    return t * lax.rsqrt(_rowsum_bcast(t * t, ones_ref) + 1e-6)


def _gdn_out(o, z, gnorm_row, ones_ref):
    ms = _rowsum_bcast(o * o, ones_ref) * (1.0 / GDN_DV)
    return (o * lax.rsqrt(ms + NORM_EPS)) * gnorm_row * _silu(z)


def _conv_silu(xbuf_ref, cw_ref, c):
    y = xbuf_ref[pl.ds(5, c), :] * cw_ref[0:1, :]
    for j in range(1, CONV_WIDTH):
        y = y + xbuf_ref[pl.ds(5 + j, c), :] * cw_ref[j:j + 1, :]
    return _silu(y)


def _gdn_prompt_kernel(qkv_ref, z_ref, ba_ref, cw_ref, alog_ref, dtb_ref, gnorm_ref, ones_ref, sel_ref,
                       o_ref, sfin_ref, s_ref, xbuf_ref, y0_ref, g0_ref, gt0_ref, y1_ref, g1_ref, gt1_ref):
    step = pl.program_id(1)

    @pl.when(step == 0)
    def _():
        s_ref[...] = jnp.zeros_like(s_ref)
        xbuf_ref[0:8, :] = jnp.zeros((8, CONV_DIM), F32)
        y1_ref[...] = jnp.zeros_like(y1_ref)
        g1_ref[...] = jnp.zeros_like(g1_ref)
        gt1_ref[...] = jnp.zeros_like(gt1_ref)

    @pl.when(lax.rem(step, 2) == 0)
    def _():
        _gdn_prompt_step((y0_ref, g0_ref, gt0_ref), (y1_ref, g1_ref, gt1_ref), qkv_ref, z_ref, ba_ref, cw_ref,
                         alog_ref, dtb_ref, gnorm_ref, ones_ref, sel_ref, o_ref, sfin_ref, s_ref, xbuf_ref)

    @pl.when(lax.rem(step, 2) == 1)
    def _():
        _gdn_prompt_step((y1_ref, g1_ref, gt1_ref), (y0_ref, g0_ref, gt0_ref), qkv_ref, z_ref, ba_ref, cw_ref,
                         alog_ref, dtb_ref, gnorm_ref, ones_ref, sel_ref, o_ref, sfin_ref, s_ref, xbuf_ref)


def _gdn_prompt_step(a_refs, b_refs, qkv_ref, z_ref, ba_ref, cw_ref, alog_ref, dtb_ref, gnorm_ref, ones_ref,
                     sel_ref, o_ref, sfin_ref, s_ref, xbuf_ref):
    ya_ref, ga_ref, gta_ref = a_refs
    yb_ref, gb_ref, gtb_ref = b_refs
    c = GDN_CHUNK
    n_chunks = GDN_STEP_CHUNKS
    r = n_chunks * c
    hs = range(GDN_HEADS)

    def stage_a():
        xbuf_ref[8:8 + r, :] = qkv_ref[...]
        beta, g = _gates(ba_ref[...], alog_ref[...], dtb_ref[...])
        ii, jj = _iota2(r)
        tril = jnp.where(jnp.logical_and((ii // c) == (jj // c), ii >= jj), 1.0, 0.0)
        gc = _cumsum_rows(tril, g)
        gta_ref[...] = gc.T
        lane = lax.broadcasted_iota(jnp.int32, (r, BA_DIM), 1)
        ga_ref[...] = jnp.where(lane < GDN_HEADS, beta, gc)
        yield
        for ci in range(n_chunks):
            rows = slice(ci * c, (ci + 1) * c)
            for slab in range(CONV_DIM // LANES):
                cols = slice(slab * LANES, (slab + 1) * LANES)
                ext = xbuf_ref[pl.ds(ci * c, c + 8), cols]
                s1 = pltpu.roll(ext, 1, axis=0)
                u2 = pltpu.roll(ext * cw_ref[1:2, cols] + s1 * cw_ref[0:1, cols], 2, axis=0)
                yc = (ext * cw_ref[3:4, cols] + s1 * cw_ref[2:3, cols] + u2)[8:]
                ya_ref[rows, cols] = _silu(yc)
                if slab < GDN_KEY_DIM // LANES:
                    ya_ref[rows, cols] = _l2norm(ya_ref[rows, cols], ones_ref) * (GDN_DK ** -0.5)
                elif slab < 2 * GDN_KEY_DIM // LANES:
                    ya_ref[rows, cols] = _l2norm(ya_ref[rows, cols], ones_ref)
                yield
        xbuf_ref[0:8, :] = xbuf_ref[r:r + 8, :]
        yield

    def stage_b():
        bcast = _lane_bcast(gb_ref[...], sel_ref, range(2 * GDN_HEADS))
        beta_b, gc_b = bcast[:GDN_HEADS], bcast[GDN_HEADS:]
        glast = [[gc_b[h][(ci + 1) * c - 1:(ci + 1) * c, :] for h in hs] for ci in range(n_chunks)]
        gl_b = [jnp.concatenate([jnp.broadcast_to(glast[ci][h], (c, GDN_DK)) for ci in range(n_chunks)],
                                axis=0) for h in hs]
        wy = {}
        yield from _gdn_prep_steps(wy, yb_ref, beta_b, gc_b, gl_b, gtb_ref[...], c, c, ones_ref,
                                   normalized=True)
        u, w, qk, qd, kd = (wy[name] for name in ("u", "w", "qk", "qd", "kd"))
        s_cur = [s_ref[h] for h in hs]
        for ci in range(n_chunks):
            ws = [_dot(jnp.concatenate([w[ci][h], qd[ci][h]], axis=0), s_cur[h]) for h in hs]
            yield
            v_new = [u[ci][h] - ws[h][:c] for h in hs]
            o = [ws[h][c:] + _dot(qk[ci][h], v_new[h]) for h in hs]
            s_cur = [s_cur[h] * jnp.exp(glast[ci][h]) + _dot_tn(kd[ci][h], v_new[h]) for h in hs]
            yield
            for h in hs:
                z = z_ref[ci * c:(ci + 1) * c, h * GDN_DV:(h + 1) * GDN_DV]
                o_ref[ci * c:(ci + 1) * c, h * GDN_DV:(h + 1) * GDN_DV] = _gdn_out(o[h], z, gnorm_ref[...], ones_ref)
        for h in hs:
            s_ref[h] = s_cur[h]
            sfin_ref[0, h] = s_cur[h]

    a_pieces = 2 + n_chunks * (CONV_DIM // LANES)
    b_levels = 2 * (int(math.log2(c)) - 1) + 2 + 2 * n_chunks
    a_steps = stage_a()
    done = 0
    for i, _ in enumerate(stage_b()):
        target = -(-(i + 1) * a_pieces // b_levels)
        for _ in range(target - done):
            next(a_steps, None)
        done = target
    for _ in a_steps:
        pass


def _gdn_prompt_call(qkvz, ba, lw, batch, seq):
    c = GDN_CHUNK * GDN_STEP_CHUNKS
    nc = seq // c
    z_col = CONV_DIM // GDN_VAL_DIM
    return pl.pallas_call(
        _gdn_prompt_kernel,
        grid=(batch, nc + 1),
        in_specs=[pl.BlockSpec((c, CONV_DIM), lambda b, s: (b * nc + jnp.minimum(s, nc - 1), 0)),
                  pl.BlockSpec((c, GDN_VAL_DIM), lambda b, s: (b * nc + jnp.maximum(s - 1, 0), z_col)),
                  pl.BlockSpec((c, BA_DIM), lambda b, s: (b * nc + jnp.minimum(s, nc - 1), 0)),
                  _resident((CONV_WIDTH, CONV_DIM)), _resident((1, BA_DIM)), _resident((1, BA_DIM)),
                  _resident((1, GDN_DV)), _resident(lw["ones"].shape), _resident(lw["sel"].shape)],
        out_specs=[pl.BlockSpec((c, GDN_VAL_DIM), lambda b, s: (b * nc + jnp.maximum(s - 1, 0), 0)),
                   pl.BlockSpec((1, GDN_HEADS, GDN_DK, GDN_DV), lambda b, s: (b, 0, 0, 0))],
        out_shape=[jax.ShapeDtypeStruct((batch * seq, GDN_VAL_DIM), F32),
                   jax.ShapeDtypeStruct((batch, GDN_HEADS, GDN_DK, GDN_DV), F32)],
        scratch_shapes=[pltpu.VMEM((GDN_HEADS, GDN_DK, GDN_DV), F32),
                        pltpu.VMEM((c + 8, CONV_DIM), F32),
                        pltpu.VMEM((c, CONV_DIM), F32), pltpu.VMEM((c, BA_DIM), F32), pltpu.VMEM((BA_DIM, c), F32),
                        pltpu.VMEM((c, CONV_DIM), F32), pltpu.VMEM((c, BA_DIM), F32), pltpu.VMEM((BA_DIM, c), F32)],
        compiler_params=pltpu.CompilerParams(dimension_semantics=("arbitrary", "arbitrary"),
                                             vmem_limit_bytes=VMEM_LIMIT),
        name="gdn_prompt",
    )(qkvz, qkvz, ba, lw["conv_w"], lw["alog_row"], lw["dtb_row"], lw["gnorm"], lw["ones"], lw["sel"])


def _gdn_decode_kernel(qkvz_ref, ba_ref, sconv_ref, s0_ref, cw_ref, alog_ref, dtb_ref, gnorm_ref,
                       ones_ref, sel_ref, o_ref, snew_ref, xbuf_ref, *, dec_seq):
    nb = DEC_SEQ_BLOCK
    t = dec_seq
    c = nb * t
    ys = []
    for s in range(nb):
        xbuf_ref[s, 5:8, :] = sconv_ref[s]
        xbuf_ref[s, 8:8 + t, :] = qkvz_ref[s * t:(s + 1) * t, :CONV_DIM]
        ys.append(_conv_silu(xbuf_ref.at[s], cw_ref, t))
    y = jnp.concatenate(ys, axis=0)

    beta, g = _gates(ba_ref[...], alog_ref[...], dtb_ref[...])
    ii, jj = _iota2(c)
    same = (ii // t) == (jj // t)
    tril = jnp.where(jnp.logical_and(same, ii >= jj), 1.0, 0.0).astype(F32)
    ones = jnp.where(same, 1.0, 0.0).astype(F32)
    gc = _cumsum_rows(tril, g)
    gl = _cumsum_rows(ones, g)
    gc_t = gc.T
    lane = lax.broadcasted_iota(jnp.int32, (c, BA_DIM), 1)
    bcast = _lane_bcast(jnp.where(lane < GDN_HEADS, beta, gc), sel_ref, range(2 * GDN_HEADS))
    beta_b, gc_b = bcast[:GDN_HEADS], bcast[GDN_HEADS:]
    gl_b = _lane_bcast(gl, sel_ref, range(GDN_HEADS, 2 * GDN_HEADS))
    wy = {}
    for _ in _gdn_prep_steps(wy, y, beta_b, gc_b, gl_b, gc_t, c, t, ones_ref, normalized=False):
        pass
    u, w, qk, qd, kd = (wy[name][0] for name in ("u", "w", "qk", "qd", "kd"))
    hs = range(GDN_HEADS)
    units = [(s, h) for h in hs for s in range(nb)]
    s_old = {(s, h): s0_ref[s, h] for s, h in units}
    ws = {(s, h): _dot(jnp.concatenate([w[h][s * t:(s + 1) * t], qd[h][s * t:(s + 1) * t]], axis=0),
                       s_old[s, h]) for s, h in units}
    v_new = [jnp.concatenate([u[h][s * t:(s + 1) * t] - ws[s, h][:t] for s in range(nb)], axis=0) for h in hs]
    o = [jnp.concatenate([ws[s, h][t:] for s in range(nb)], axis=0) + _dot(qk[h], v_new[h]) for h in hs]
    cd = [jnp.exp(gl_b[h]) for h in hs]
    kd_t = [kd[h].T for h in hs]
    col_seq = lax.broadcasted_iota(jnp.int32, (GDN_DK, c), 1) // t
    for s, h in units:
        upd = _dot(jnp.where(col_seq == s, kd_t[h], 0.0), v_new[h])
        snew_ref[s, h] = s_old[s, h] * cd[h][s * t:s * t + 1, :] + upd
    for h in hs:
        z = qkvz_ref[:, CONV_DIM + h * GDN_DV:CONV_DIM + (h + 1) * GDN_DV]
        o_ref[:, h * GDN_DV:(h + 1) * GDN_DV] = _gdn_out(o[h], z, gnorm_ref[...], ones_ref)


def _gdn_decode_call(qkvz, ba, state_conv, state_gdn, lw, dec_batch, dec_seq):
    nb = DEC_SEQ_BLOCK
    rows = nb * dec_seq
    return pl.pallas_call(
        functools.partial(_gdn_decode_kernel, dec_seq=dec_seq),
        grid=(dec_batch // nb,),
        in_specs=[pl.BlockSpec((rows, QKVZ_DIM), lambda i: (i, 0)),
                  pl.BlockSpec((rows, BA_DIM), lambda i: (i, 0)),
                  pl.BlockSpec((nb, CONV_WIDTH - 1, CONV_DIM), lambda i: (i, 0, 0)),
                  pl.BlockSpec((nb, GDN_HEADS, GDN_DK, GDN_DV), lambda i: (i, 0, 0, 0)),
                  _resident((CONV_WIDTH, CONV_DIM)), _resident((1, BA_DIM)), _resident((1, BA_DIM)),
                  _resident((1, GDN_DV)), _resident(lw["ones"].shape), _resident(lw["sel"].shape)],
        out_specs=[pl.BlockSpec((rows, GDN_VAL_DIM), lambda i: (i, 0)),
                   pl.BlockSpec((nb, GDN_HEADS, GDN_DK, GDN_DV), lambda i: (i, 0, 0, 0))],
        out_shape=[jax.ShapeDtypeStruct((dec_batch * dec_seq, GDN_VAL_DIM), F32),
                   jax.ShapeDtypeStruct((dec_batch, GDN_HEADS, GDN_DK, GDN_DV), F32)],
        scratch_shapes=[pltpu.VMEM((nb, 16, CONV_DIM), F32)],
        compiler_params=pltpu.CompilerParams(dimension_semantics=("arbitrary",),
                                             vmem_limit_bytes=VMEM_LIMIT),
        name="gdn_decode",
    )(qkvz, ba, state_conv, state_gdn, lw["conv_w"], lw["alog_row"], lw["dtb_row"], lw["gnorm"],
      lw["ones"], lw["sel"])


def _bias_table_kernel(bucket_ref, rb_ref, out_ref):
    bucket = bucket_ref[...]
    for h in range(SWA_HEADS):
        acc = jnp.zeros(bucket.shape, F32)
        for b in range(NUM_BUCKETS):
            acc = jnp.where(bucket == b, rb_ref[b, h], acc)
        out_ref[h] = jnp.where(bucket < 0, NEG_INF, acc)


def _bias_table_call(bucket, rel_bias):
    r, c = bucket.shape
    return pl.pallas_call(
        _bias_table_kernel,
        in_specs=[pl.BlockSpec(memory_space=pltpu.VMEM), pl.BlockSpec(memory_space=pltpu.SMEM)],
        out_specs=pl.BlockSpec(memory_space=pltpu.VMEM),
        out_shape=jax.ShapeDtypeStruct((SWA_HEADS, r, c), F32),
        name="t5_bias_table",
    )(bucket, rel_bias)


def _t5_bucket(dist, valid):
    d = jnp.maximum(dist, 0)
    exact = NUM_BUCKETS // 2
    log_ratio = jnp.log(jnp.maximum(d, 1).astype(F32) / exact) / math.log(REL_MAX_DISTANCE / exact)
    large = jnp.minimum(exact + (log_ratio * (NUM_BUCKETS - exact)).astype(jnp.int32), NUM_BUCKETS - 1)
    bucket = jnp.where(d < exact, d, large)
    return jnp.where(valid, bucket, -1).astype(jnp.int32)


def _swa_prompt_kernel(q_ref, kvp_ref, kvc_ref, bias_ref, sink_ref, o_ref):
    w, dh, nq = SWA_WINDOW, SWA_HEAD_DIM, SWA_STEP_BLOCKS
    first = pl.program_id(1) == 0
    col = lax.broadcasted_iota(jnp.int32, (w, 2 * w), 1)
    drop_prev = jnp.logical_and(first, col < w)
    kv = jnp.concatenate([kvp_ref[...], kvc_ref[...]], axis=0).astype(BF16)
    k_h = [kv[:, h * dh:(h + 1) * dh] for h in range(SWA_KV_HEADS)]
    v_h = [kv[:, SWA_KV_DIM + h * dh:SWA_KV_DIM + (h + 1) * dh] for h in range(SWA_KV_HEADS)]
    units = [(b, hg) for b in range(nq) for hg in range(SWA_HEADS)]
    n = range(len(units))
    q = [(q_ref[b * w:(b + 1) * w, hg * dh:(hg + 1) * dh] * (dh ** -0.5)).astype(BF16) for b, hg in units]
    s = [lax.dot_general(q[i], k_h[hg // SWA_GROUP][b * w:(b + 2) * w], (((1,), (1,)), ((), ())),
                         preferred_element_type=F32) + bias_ref[hg] for i, (b, hg) in enumerate(units)]
    s = [jnp.where(drop_prev, NEG_INF, s[i]) if b == 0 else s[i] for i, (b, hg) in enumerate(units)]
    sink = [sink_ref[hg] for b, hg in units]
    m = [jnp.maximum(jnp.max(s[i], axis=-1, keepdims=True), sink[i]) for i in n]
    p = [jnp.exp(s[i] - m[i]) for i in n]
    denom = [jnp.sum(p[i], axis=-1, keepdims=True) + jnp.exp(sink[i] - m[i]) for i in n]
    o = [jnp.dot(p[i].astype(BF16), v_h[hg // SWA_GROUP][b * w:(b + 2) * w], preferred_element_type=F32)
         / denom[i] for i, (b, hg) in enumerate(units)]
    for i, (b, hg) in enumerate(units):
        o_ref[b * w:(b + 1) * w, hg * dh:(hg + 1) * dh] = o[i]


def _swa_prompt_call(qs, kv, bias, sinks, batch, seq):
    w, nq = SWA_WINDOW, SWA_STEP_BLOCKS
    nb = seq // (w * nq)
    return pl.pallas_call(
        _swa_prompt_kernel,
        grid=(batch, nb),
        in_specs=[pl.BlockSpec((nq * w, SWA_Q_DIM), lambda b, n: (b * nb + n, 0)),
                  pl.BlockSpec((w, 2 * SWA_KV_DIM),
                               lambda b, n: ((b * nb + n) * nq - jnp.minimum(n, 1), 0)),
                  pl.BlockSpec((nq * w, 2 * SWA_KV_DIM), lambda b, n: (b * nb + n, 0)),
                  _resident((SWA_HEADS, w, 2 * w)),
                  pl.BlockSpec(memory_space=pltpu.SMEM)],
        out_specs=pl.BlockSpec((nq * w, SWA_Q_DIM), lambda b, n: (b * nb + n, 0)),
        out_shape=jax.ShapeDtypeStruct((batch * seq, SWA_Q_DIM), F32),
        compiler_params=pltpu.CompilerParams(dimension_semantics=("arbitrary", "arbitrary"),
                                             vmem_limit_bytes=VMEM_LIMIT),
        name="swa_prompt",
    )(qs, kv, kv, bias, sinks)


def _swa_decode_kernel(q_ref, kvn_ref, kt_ref, vt_ref, bbuf_ref, bnew_ref, sink_ref,
                       o_ref, kto_ref, vto_ref, *, dec_seq):
    nb = DEC_SEQ_BLOCK
    t = dec_seq
    dh = SWA_HEAD_DIM
    rows = nb * t
    m_rows = SWA_GROUP * rows
    wb = kt_ref.shape[3]
    ri = lax.broadcasted_iota(jnp.int32, (m_rows, 1), 0)
    row_seq = (ri % rows) // t
    row_grp = ri // rows
    for h in range(SWA_KV_HEADS):
        lo, hi = h * dh, (h + 1) * dh
        q = jnp.concatenate([q_ref[:, (h * SWA_GROUP + g) * dh:(h * SWA_GROUP + g + 1) * dh]
                             for g in range(SWA_GROUP)], axis=0)
        sink = jnp.zeros((m_rows, 1), F32)
        for g in range(SWA_GROUP):
            sink = jnp.where(row_grp == g, sink_ref[h * SWA_GROUP + g], sink)
        k_new = jnp.concatenate([kvn_ref[:, lo:hi], jnp.zeros((m_rows - rows, dh), F32)], axis=0)
        v_new = jnp.concatenate([kvn_ref[:, SWA_KV_DIM + lo:SWA_KV_DIM + hi],
                                 jnp.zeros((m_rows - rows, dh), F32)], axis=0)
        s_new = _dot_nt(q, k_new) * (dh ** -0.5) + bnew_ref[h]
        s_buf = jnp.zeros((m_rows, wb), F32)
        for s in range(nb):
            s_buf = jnp.where(row_seq == s, _dot(q, kt_ref[s, h]), s_buf)
        s_buf = s_buf * (dh ** -0.5) + bbuf_ref[h]
        m = jnp.maximum(jnp.maximum(jnp.max(s_buf, axis=-1, keepdims=True),
                                    jnp.max(s_new, axis=-1, keepdims=True)), sink)
        p_buf = jnp.exp(s_buf - m)
        p_new = jnp.exp(s_new - m)
        denom = (jnp.sum(p_buf, axis=-1, keepdims=True) + jnp.sum(p_new, axis=-1, keepdims=True)
                 + jnp.exp(sink - m))
        o = _dot(p_new, v_new)
        for s in range(nb):
            o = o + _dot_nt(jnp.where(row_seq == s, p_buf, 0.0), vt_ref[s, h])
        o = o / denom
        for g in range(SWA_GROUP):
            hg = h * SWA_GROUP + g
            o_ref[:, hg * dh:(hg + 1) * dh] = o[g * rows:(g + 1) * rows]
    new_t = jnp.concatenate([kvn_ref[...], jnp.zeros((wb - rows, 2 * SWA_KV_DIM), F32)], axis=0).T
    lane = lax.broadcasted_iota(jnp.int32, (dh, wb), 1)
    for part, (src_ref, dst_ref) in enumerate(((kt_ref, kto_ref), (vt_ref, vto_ref))):
        for h in range(SWA_KV_HEADS):
            fresh = new_t[part * SWA_KV_DIM + h * dh:part * SWA_KV_DIM + (h + 1) * dh, :]
            for s in range(nb):
                kept = pltpu.roll(src_ref[s, h], wb - t, axis=1)
                tail = pltpu.roll(fresh, (wb - t - s * t) % wb, axis=1)
                dst_ref[s, h] = jnp.where(lane >= wb - t, tail, kept)


def _swa_decode_call(qs, kvn, kt, vt, bias_buf, bias_new, sinks, dec_batch, dec_seq):
    nb = DEC_SEQ_BLOCK
    rows = nb * dec_seq
    wb = kt.shape[3]
    m_rows = SWA_GROUP * rows
    assert rows <= wb and wb == LANES
    win_spec = pl.BlockSpec((nb, SWA_KV_HEADS, SWA_HEAD_DIM, wb), lambda i: (i, 0, 0, 0))
    return pl.pallas_call(
        functools.partial(_swa_decode_kernel, dec_seq=dec_seq),
        grid=(dec_batch // nb,),
        in_specs=[pl.BlockSpec((rows, SWA_Q_DIM), lambda i: (i, 0)),
                  pl.BlockSpec((rows, 2 * SWA_KV_DIM), lambda i: (i, 0)),
                  win_spec, win_spec,
                  _resident((SWA_KV_HEADS, m_rows, wb)),
                  _resident((SWA_KV_HEADS, m_rows, m_rows)),
                  pl.BlockSpec(memory_space=pltpu.SMEM)],
        out_specs=[pl.BlockSpec((rows, SWA_Q_DIM), lambda i: (i, 0)), win_spec, win_spec],
        out_shape=[jax.ShapeDtypeStruct((dec_batch * dec_seq, SWA_Q_DIM), F32),
                   jax.ShapeDtypeStruct(kt.shape, F32), jax.ShapeDtypeStruct(vt.shape, F32)],
        compiler_params=pltpu.CompilerParams(dimension_semantics=("arbitrary",),
                                             vmem_limit_bytes=VMEM_LIMIT),
        name="swa_decode",
    )(qs, kvn, kt, vt, bias_buf, bias_new, sinks)


def _prompt_bias(rel_bias):
    w = SWA_WINDOW
    dist = w + jnp.arange(w)[:, None] - jnp.arange(2 * w)[None, :]
    valid = (dist >= 0) & (dist < w)
    return _bias_table_call(_t5_bucket(dist, valid), rel_bias)


def _decode_bias(rel_bias, wb, dec_seq):
    nb, t = DEC_SEQ_BLOCK, dec_seq
    rows = nb * t
    m_rows = SWA_GROUP * rows
    tok = jnp.arange(t)
    dist = jnp.concatenate([wb + tok[:, None] - jnp.arange(wb)[None, :], tok[:, None] - tok[None, :]], axis=1)
    bucket = _t5_bucket(dist, (dist >= 0) & (dist < SWA_WINDOW))
    pad_r, pad_c = -t % 8, -(wb + t) % LANES
    bucket = jnp.pad(bucket, ((0, pad_r), (0, pad_c)), constant_values=-1)
    tab = _bias_table_call(bucket, rel_bias)[:, :t, :wb + t]
    tab = tab.reshape(SWA_KV_HEADS, SWA_GROUP, 1, t, wb + t)
    per_row = jnp.broadcast_to(tab, (SWA_KV_HEADS, SWA_GROUP, nb, t, wb + t)).reshape(SWA_KV_HEADS, m_rows, wb + t)
    r_seq = (jnp.arange(m_rows) % rows) // t
    cidx = jnp.arange(m_rows)
    own = (r_seq[:, None] == (cidx // t)[None, :]) & (cidx < rows)[None, :]
    bias_new = jnp.where(own[None], jnp.tile(per_row[:, :, wb:], (1, 1, m_rows // t)), NEG_INF)
    return per_row[:, :, :wb], bias_new


def _cast_kernel(*refs):
    n = len(refs) // 2
    for src_ref, dst_ref in zip(refs[:n], refs[n:]):
        dst_ref[...] = src_ref[...].astype(dst_ref.dtype)


def _to_bf16(ws):
    r, c = ws[0].shape
    tr = r
    while tr * c * 4 > CAST_BLOCK_BYTES and tr % 16 == 0:
        tr //= 2
    spec = pl.BlockSpec((tr, c), lambda i: (i, 0))
    return pl.pallas_call(
        _cast_kernel,
        grid=(r // tr,),
        in_specs=[spec] * len(ws),
        out_specs=[spec] * len(ws),
        out_shape=[jax.ShapeDtypeStruct((r, c), BF16)] * len(ws),
        compiler_params=pltpu.CompilerParams(dimension_semantics=("arbitrary",)),
        name="weights_to_bf16",
    )(*ws)


def _pack_layer(i, norm_ffn1_pre, norm_ffn1_post, ffn1_w_gate, ffn1_w_up, ffn1_w_down, norm_mix_pre,
                norm_mix_post, w_in, conv_w, gdn_a_log, gdn_dt_bias, gdn_norm, swa_sinks, w_out,
                norm_ffn2_pre, norm_ffn2_post, ffn2_w_gate, ffn2_w_up, ffn2_w_down, ple_gate, ple_proj,
                norm_ple_post):
    row = lambda g: g[i].reshape(1, -1).astype(F32)
    wt = jnp.transpose(w_in[i])
    n_gdn = QKVZ_DIM + 2 * GDN_HEADS
    win = jnp.concatenate([wt[:n_gdn], jnp.zeros((BA_DIM - 2 * GDN_HEADS, D_MODEL), wt.dtype), wt[n_gdn:]],
                          axis=0).astype(BF16)
    lane_pad = lambda v: jnp.zeros((1, BA_DIM), F32).at[0, GDN_HEADS:2 * GDN_HEADS].set(v[i].astype(F32))
    kk = np.arange(2 * LANES)[:, None] % LANES
    sel = (kk == (np.arange(2 * GDN_HEADS * LANES)[None, :] // LANES)).astype(np.float32)
    wg1, wu1, wg2, wu2 = _to_bf16([ffn1_w_gate[i], ffn1_w_up[i], ffn2_w_gate[i], ffn2_w_up[i]])
    wd1, wd2 = _to_bf16([ffn1_w_down[i], ffn2_w_down[i]])
    wo, wpg = _to_bf16([w_out[i], ple_gate[i]])
    return dict(
        ones=jnp.ones((LANES, LANES), F32), sel=jnp.asarray(sel, BF16),
        g1pre=row(norm_ffn1_pre), g1post=row(norm_ffn1_post), wg1=wg1, wu1=wu1, wd1=wd1,
        gmix=row(norm_mix_pre), gmixpost=row(norm_mix_post), win=win,
        conv_w=conv_w[i].astype(F32), alog_row=lane_pad(gdn_a_log), dtb_row=lane_pad(gdn_dt_bias),
        gnorm=row(gdn_norm), sinks=swa_sinks[i].astype(F32), wo=wo,
        g2pre=row(norm_ffn2_pre), g2post=row(norm_ffn2_post), wg2=wg2, wu2=wu2, wd2=wd2,
        wpg=wpg, wpp=ple_proj[i].astype(BF16), gple=row(norm_ple_post))


def kernel(x_prompt, x_sample, state_conv, state_gdn, cache_swa_k, cache_swa_v, p_prompt, p_sample,
           rel_bias, norm_ffn1_pre, norm_ffn1_post, ffn1_w_gate, ffn1_w_up, ffn1_w_down,
           norm_mix_pre, norm_mix_post, w_in, conv_w, gdn_a_log, gdn_dt_bias, gdn_norm, swa_sinks,
           w_out, norm_ffn2_pre, norm_ffn2_post, ffn2_w_gate, ffn2_w_up, ffn2_w_down,
           ple_gate, ple_proj, norm_ple_post):
    depth = w_in.shape[0]
    batch, seq, _ = x_prompt.shape
    dec_batch, dec_seq, _ = x_sample.shape
    wb = cache_swa_k.shape[2]
    wp = min(SWA_WINDOW, seq)
    rel_bias = rel_bias.astype(F32)
    bias_p = _prompt_bias(rel_bias)
    bias_db, bias_dn = _decode_bias(rel_bias, wb, dec_seq)

    yp = x_prompt.reshape(batch * seq, D_MODEL)
    ys = x_sample.reshape(dec_batch * dec_seq, D_MODEL)
    outs = [[] for _ in range(8)]
    for i in range(depth):
        lw = _pack_layer(i, norm_ffn1_pre, norm_ffn1_post, ffn1_w_gate, ffn1_w_up, ffn1_w_down,
                         norm_mix_pre, norm_mix_post, w_in, conv_w, gdn_a_log, gdn_dt_bias, gdn_norm,
                         swa_sinks, w_out, norm_ffn2_pre, norm_ffn2_post, ffn2_w_gate, ffn2_w_up,
                         ffn2_w_down, ple_gate, ple_proj, norm_ple_post)
        x1, qkvz, ba, qs, kv = _head_call(yp, lw)
        gdn_o, s_fin = _gdn_prompt_call(qkvz, ba, lw, batch, seq)
        swa_o = _swa_prompt_call(qs, kv, bias_p, lw["sinks"], batch, seq)
        yp = _tail_call(x1, gdn_o, swa_o, p_prompt[i].reshape(batch * seq, PLE_DIM), lw)
        kv3 = kv.reshape(batch, seq, 2 * SWA_KV_DIM)
        outs[0].append(qkvz.reshape(batch, seq, QKVZ_DIM)[:, seq - (CONV_WIDTH - 1):, :CONV_DIM])
        outs[1].append(s_fin)
        outs[2].append(kv3[:, seq - wp:, :SWA_KV_DIM].reshape(batch, wp, SWA_KV_HEADS, SWA_HEAD_DIM))
        outs[3].append(kv3[:, seq - wp:, SWA_KV_DIM:].reshape(batch, wp, SWA_KV_HEADS, SWA_HEAD_DIM))
        x1, qkvz, ba, qs, kv = _head_call(ys, lw)
        gdn_o, s_new = _gdn_decode_call(qkvz, ba, state_conv[i], state_gdn[i], lw, dec_batch, dec_seq)
        kt = jnp.transpose(cache_swa_k[i], (0, 2, 3, 1))
        vt = jnp.transpose(cache_swa_v[i], (0, 2, 3, 1))
        swa_o, kt_new, vt_new = _swa_decode_call(qs, kv, kt, vt, bias_db, bias_dn, lw["sinks"], dec_batch, dec_seq)
        ys = _tail_call(x1, gdn_o, swa_o, p_sample[i].reshape(dec_batch * dec_seq, PLE_DIM), lw)
        xp = jnp.concatenate([state_conv[i], qkvz.reshape(dec_batch, dec_seq, QKVZ_DIM)[:, :, :CONV_DIM]], axis=1)
        outs[4].append(xp[:, dec_seq:])
        outs[5].append(s_new)
        outs[6].append(jnp.transpose(kt_new, (0, 3, 1, 2)))
        outs[7].append(jnp.transpose(vt_new, (0, 3, 1, 2)))
    return (yp.reshape(batch, seq, D_MODEL), ys.reshape(dec_batch, dec_seq, D_MODEL),
            *[jnp.stack(o) for o in outs])
```

```python
import functools
import math

import numpy as np
import jax
import jax.numpy as jnp
from jax import lax
from jax.experimental import pallas as pl
from jax.experimental.pallas import tpu as pltpu

F32 = jnp.float32
BF16 = jnp.bfloat16

D_MODEL = 1024
NORM_EPS = 1e-6
PLE_DIM = 256
FFN_DIM = 2816
GDN_HEADS = 4
GDN_DK = 128
GDN_DV = 128
GDN_KEY_DIM = GDN_HEADS * GDN_DK
GDN_VAL_DIM = GDN_HEADS * GDN_DV
CONV_DIM = 2 * GDN_KEY_DIM + GDN_VAL_DIM
CONV_WIDTH = 4
SWA_HEADS = 8
SWA_KV_HEADS = 2
SWA_GROUP = SWA_HEADS // SWA_KV_HEADS
SWA_HEAD_DIM = 64
SWA_Q_DIM = SWA_HEADS * SWA_HEAD_DIM
SWA_KV_DIM = SWA_KV_HEADS * SWA_HEAD_DIM
SWA_WINDOW = 128
NUM_BUCKETS = 32
REL_MAX_DISTANCE = 128

QKVZ_DIM = CONV_DIM + GDN_VAL_DIM
BA_DIM = 128
PROJ_PACKED = QKVZ_DIM + BA_DIM + SWA_Q_DIM + 2 * SWA_KV_DIM

GDN_CHUNK = 64
GDN_STEP_CHUNKS = 8
SWA_STEP_BLOCKS = 4
ROW_TILE = 512
DEC_SEQ_BLOCK = 16
CAST_BLOCK_BYTES = 2 * 1024 * 1024
VMEM_LIMIT = 56 * 1024 * 1024

NEG_INF = float("-inf")
LANES = 128


def _resident(shape):
    nd = len(shape)
    return pl.BlockSpec(shape, lambda *_: (0,) * nd, pipeline_mode=pl.Buffered(1))


def _rms(x, gain):
    ms = jnp.mean(x * x, axis=-1, keepdims=True)
    return (x * lax.rsqrt(ms + NORM_EPS)) * gain


def _sigmoid(x):
    return 1.0 / (1.0 + jnp.exp(-x))


def _silu(x):
    h = 0.5 * x
    return h + h * jnp.tanh(h)


def _dot(a, b):
    return jnp.dot(a, b, preferred_element_type=F32)


def _dot_nt(a, b):
    return lax.dot_general(a, b, (((1,), (1,)), ((), ())), preferred_element_type=F32)


def _dot_tn(a, b):
    return lax.dot_general(a, b, (((0,), (0,)), ((), ())), preferred_element_type=F32)


def _split(a):
    hi = a.astype(BF16)
    lo = (a - hi.astype(F32)).astype(BF16)
    return hi, lo


def _inv_rms(x):
    return lax.rsqrt(jnp.mean(x * x, axis=-1, keepdims=True) + NORM_EPS)


def _prenorm_dots(x, gain, ws, transposed=False):
    h = (x * gain).astype(BF16)
    inv = _inv_rms(x)
    dims = (((1,), (1 if transposed else 0,)), ((), ()))
    return [lax.dot_general(h, w, dims, preferred_element_type=F32) * inv for w in ws]


def _swiglu_block(x, gain, wg_ref, wu_ref, wd_ref):
    g, u = _prenorm_dots(x, gain, [wg_ref[...], wu_ref[...]])
    a = (_silu(g) * u).astype(BF16)
    return jnp.dot(a, wd_ref[...], preferred_element_type=F32)


def _head_kernel(x_ref, g1pre_ref, g1post_ref, wg_ref, wu_ref, wd_ref, gmix_ref, win_ref,
                 x1_ref, qkvz_ref, ba_ref, qs_ref, kv_ref):
    x = x_ref[...]
    y = _swiglu_block(x, g1pre_ref[...], wg_ref, wu_ref, wd_ref)
    x1 = x + 0.5 * _rms(y, g1post_ref[...])
    x1_ref[...] = x1
    c0, c1, c2 = QKVZ_DIM, QKVZ_DIM + BA_DIM, QKVZ_DIM + BA_DIM + SWA_Q_DIM
    qkvz_ref[...], ba_ref[...], qs_ref[...], kv_ref[...] = _prenorm_dots(
        x1, gmix_ref[...], [win_ref[:c0, :], win_ref[c0:c1, :], win_ref[c1:c2, :], win_ref[c2:, :]], transposed=True)


def _head_call(x, lw):
    n = x.shape[0]
    tm = min(ROW_TILE, n)
    row = lambda w: pl.BlockSpec((tm, w), lambda i: (i, 0))
    return pl.pallas_call(
        _head_kernel,
        grid=(n // tm,),
        in_specs=[row(D_MODEL), _resident((1, D_MODEL)), _resident((1, D_MODEL)),
                  _resident((D_MODEL, FFN_DIM)), _resident((D_MODEL, FFN_DIM)),
                  _resident((FFN_DIM, D_MODEL)), _resident((1, D_MODEL)),
                  _resident((PROJ_PACKED, D_MODEL))],
        out_specs=[row(D_MODEL), row(QKVZ_DIM), row(BA_DIM), row(SWA_Q_DIM), row(2 * SWA_KV_DIM)],
        out_shape=[jax.ShapeDtypeStruct((n, D_MODEL), F32), jax.ShapeDtypeStruct((n, QKVZ_DIM), F32),
                   jax.ShapeDtypeStruct((n, BA_DIM), F32), jax.ShapeDtypeStruct((n, SWA_Q_DIM), F32),
                   jax.ShapeDtypeStruct((n, 2 * SWA_KV_DIM), F32)],
        compiler_params=pltpu.CompilerParams(dimension_semantics=("arbitrary",),
                                             vmem_limit_bytes=VMEM_LIMIT),
        name="ffn1_inproj",
    )(x, lw["g1pre"], lw["g1post"], lw["wg1"], lw["wu1"], lw["wd1"], lw["gmix"], lw["win"])


def _tail_kernel(x_ref, gdn_ref, swa_ref, p_ref, wo_ref, gmixpost_ref, g2pre_ref, g2post_ref,
                 wg_ref, wu_ref, wd_ref, wpg_ref, wpp_ref, gple_ref, y_ref):
    x = x_ref[...]
    mix = (jnp.dot(gdn_ref[...].astype(BF16), wo_ref[:GDN_VAL_DIM, :], preferred_element_type=F32)
           + jnp.dot(swa_ref[...].astype(BF16), wo_ref[GDN_VAL_DIM:, :], preferred_element_type=F32))
    pp = jnp.dot(p_ref[...].astype(BF16), wpp_ref[...], preferred_element_type=F32)
    x = x + _rms(mix, gmixpost_ref[...])
    y = _swiglu_block(x, g2pre_ref[...], wg_ref, wu_ref, wd_ref)
    x = x + 0.5 * _rms(y, g2post_ref[...])
    gate = _sigmoid(jnp.dot(x.astype(BF16), wpg_ref[...], preferred_element_type=F32))
    y_ref[...] = x + _rms(gate * pp, gple_ref[...])


def _tail_call(x1, gdn_o, swa_o, p, lw):
    n = x1.shape[0]
    tm = min(ROW_TILE, n)
    row = lambda w: pl.BlockSpec((tm, w), lambda i: (i, 0))
    return pl.pallas_call(
        _tail_kernel,
        grid=(n // tm,),
        in_specs=[row(D_MODEL), row(GDN_VAL_DIM), row(SWA_Q_DIM), row(PLE_DIM),
                  _resident((GDN_VAL_DIM + SWA_Q_DIM, D_MODEL)), _resident((1, D_MODEL)),
                  _resident((1, D_MODEL)), _resident((1, D_MODEL)),
                  _resident((D_MODEL, FFN_DIM)), _resident((D_MODEL, FFN_DIM)),
                  _resident((FFN_DIM, D_MODEL)), _resident((D_MODEL, D_MODEL)),
                  _resident((PLE_DIM, D_MODEL)), _resident((1, D_MODEL))],
        out_specs=row(D_MODEL),
        out_shape=jax.ShapeDtypeStruct((n, D_MODEL), F32),
        compiler_params=pltpu.CompilerParams(dimension_semantics=("arbitrary",),
                                             vmem_limit_bytes=VMEM_LIMIT),
        name="outproj_ffn2_ple",
    )(x1, gdn_o, swa_o, p, lw["wo"], lw["gmixpost"], lw["g2pre"], lw["g2post"],
      lw["wg2"], lw["wu2"], lw["wd2"], lw["wpg"], lw["wpp"], lw["gple"])


def _iota2(c):
    return (lax.broadcasted_iota(jnp.int32, (c, c), 0), lax.broadcasted_iota(jnp.int32, (c, c), 1))


def _gates(ba, alog_row, dtb_row):
    beta = _sigmoid(ba)
    xa = ba + dtb_row
    softplus = jnp.maximum(xa, 0.0) + jnp.log1p(jnp.exp(-jnp.abs(xa)))
    g = -jnp.exp(alog_row) * softplus
    return beta, g


def _cumsum_rows(mask01, g):
    hi, lo = _split(g)
    m = mask01.astype(BF16)
    return jnp.dot(m, hi, preferred_element_type=F32) + jnp.dot(m, lo, preferred_element_type=F32)


def _hi_lo_lanes(x):
    hi, lo = _split(x)
    return jnp.concatenate([hi, lo], axis=1)


def _rowsum_bcast(x, ones_ref):
    return jnp.dot(x, ones_ref[...], preferred_element_type=F32)


def _lane_bcast(x, sel_ref, lanes):
    out = jnp.dot(_hi_lo_lanes(x), sel_ref[...], preferred_element_type=F32)
    return [out[:, l * 128:(l + 1) * 128] for l in lanes]


def _gdn_prep_steps(out, y, beta_b, gc_b, gl_b, gc_t, chunk, group, ones_ref, normalized):
    c = chunk
    n_chunks = y.shape[0] // c
    units = [(ci, h) for ci in range(n_chunks) for h in range(GDN_HEADS)]
    n = range(len(units))
    ii, jj = _iota2(c)
    same = (ii // group) == (jj // group)
    incl = jnp.logical_and(same, ii >= jj)
    strict = jnp.logical_and(same, ii > jj)

    def rows(t, ci, lo, hi):
        return t[ci * c:(ci + 1) * c, lo:hi]

    eg_b = [jnp.exp(t) for t in gc_b]
    ekd_b = [jnp.exp(gl_b[h] - gc_b[h]) for h in range(GDN_HEADS)]
    beta_u = [rows(beta_b[h], ci, 0, GDN_DK) for ci, h in units]
    eg_u = [rows(eg_b[h], ci, 0, GDN_DK) for ci, h in units]
    ekd_u = [rows(ekd_b[h], ci, 0, GDN_DK) for ci, h in units]
    gc_col = [rows(gc_b[h], ci, 0, c) for ci, h in units]
    gc_row = [gc_t[GDN_HEADS + h:GDN_HEADS + h + 1, ci * c:(ci + 1) * c] for ci, h in units]
    q = [rows(y, ci, h * GDN_DK, (h + 1) * GDN_DK) for ci, h in units]
    k = [rows(y, ci, GDN_KEY_DIM + h * GDN_DK, GDN_KEY_DIM + (h + 1) * GDN_DK) for ci, h in units]
    v = [rows(y, ci, 2 * GDN_KEY_DIM + h * GDN_DV, 2 * GDN_KEY_DIM + (h + 1) * GDN_DV) for ci, h in units]
    if not normalized:
        q = [_l2norm(t, ones_ref) * (GDN_DK ** -0.5) for t in q]
        k = [_l2norm(t, ones_ref) for t in k]
    decay = [jnp.exp(jnp.where(incl, gc_col[i] - gc_row[i], NEG_INF)) for i in n]
    kb = [k[i] * beta_u[i] for i in n]
    kq = [_dot_nt(jnp.concatenate([kb[i], q[i]], axis=0), k[i]) for i in n]
    yield
    a_mat = [jnp.where(strict, kq[i][:c] * decay[i], 0.0) for i in n]
    qk = [kq[i][c:] * decay[i] for i in n]
    eye = jnp.where(ii == jj, 1.0, 0.0).astype(F32)
    t_mat = [eye for _ in n]
    b = 1
    while b < group:
        lower = jnp.logical_and((ii // (2 * b)) == (jj // (2 * b)),
                                jnp.logical_and((ii % (2 * b)) >= b, (jj % (2 * b)) < b))
        m = [jnp.where(lower, a_mat[i], 0.0) for i in n]
        if b == 1:
            t_mat = [eye - m[i] for i in n]
        else:
            tm = [_dot(t_mat[i], m[i]) for i in n]
            yield
            t_mat = [t_mat[i] - _dot(tm[i], t_mat[i]) for i in n]
            yield
        b *= 2
    rhs = [jnp.concatenate([v[i] * beta_u[i], kb[i] * eg_u[i]], axis=-1) for i in n]
    sol = [_dot(t_mat[i], rhs[i]) for i in n]
    yield
    nest = lambda xs: [xs[ci * GDN_HEADS:(ci + 1) * GDN_HEADS] for ci in range(n_chunks)]
    out.update(u=nest([s[:, :GDN_DV] for s in sol]), w=nest([s[:, GDN_DV:] for s in sol]), qk=nest(qk),
               qd=nest([q[i] * eg_u[i] for i in n]), kd=nest([k[i] * ekd_u[i] for i in n]))


def _l2norm(t, ones_ref):
    return t * lax.rsqrt(_rowsum_bcast(t * t, ones_ref) + 1e-6)


def _gdn_out(o, z, gnorm_row, ones_ref):
    ms = _rowsum_bcast(o * o, ones_ref) * (1.0 / GDN_DV)
    return (o * lax.rsqrt(ms + NORM_EPS)) * gnorm_row * _silu(z)


def _conv_silu(xbuf_ref, cw_ref, c):
    y = xbuf_ref[pl.ds(5, c), :] * cw_ref[0:1, :]
    for j in range(1, CONV_WIDTH):
        y = y + xbuf_ref[pl.ds(5 + j, c), :] * cw_ref[j:j + 1, :]
    return _silu(y)


def _gdn_prompt_kernel(qkv_ref, z_ref, ba_ref, cw_ref, alog_ref, dtb_ref, gnorm_ref, ones_ref, sel_ref,
                       o_ref, sfin_ref, s_ref, xbuf_ref, y0_ref, g0_ref, gt0_ref, y1_ref, g1_ref, gt1_ref):
    step = pl.program_id(1)

    @pl.when(step == 0)
    def _():
        s_ref[...] = jnp.zeros_like(s_ref)
        xbuf_ref[0:8, :] = jnp.zeros((8, CONV_DIM), F32)
        y1_ref[...] = jnp.zeros_like(y1_ref)
        g1_ref[...] = jnp.zeros_like(g1_ref)
        gt1_ref[...] = jnp.zeros_like(gt1_ref)

    @pl.when(lax.rem(step, 2) == 0)
    def _():
        _gdn_prompt_step((y0_ref, g0_ref, gt0_ref), (y1_ref, g1_ref, gt1_ref), qkv_ref, z_ref, ba_ref, cw_ref,
                         alog_ref, dtb_ref, gnorm_ref, ones_ref, sel_ref, o_ref, sfin_ref, s_ref, xbuf_ref)

    @pl.when(lax.rem(step, 2) == 1)
    def _():
        _gdn_prompt_step((y1_ref, g1_ref, gt1_ref), (y0_ref, g0_ref, gt0_ref), qkv_ref, z_ref, ba_ref, cw_ref,
                         alog_ref, dtb_ref, gnorm_ref, ones_ref, sel_ref, o_ref, sfin_ref, s_ref, xbuf_ref)


def _gdn_prompt_step(a_refs, b_refs, qkv_ref, z_ref, ba_ref, cw_ref, alog_ref, dtb_ref, gnorm_ref, ones_ref,
                     sel_ref, o_ref, sfin_ref, s_ref, xbuf_ref):
    ya_ref, ga_ref, gta_ref = a_refs
    yb_ref, gb_ref, gtb_ref = b_refs
    c = GDN_CHUNK
    n_chunks = GDN_STEP_CHUNKS
    r = n_chunks * c
    hs = range(GDN_HEADS)

    def stage_a():
        xbuf_ref[8:8 + r, :] = qkv_ref[...]
        beta, g = _gates(ba_ref[...], alog_ref[...], dtb_ref[...])
        ii, jj = _iota2(r)
        tril = jnp.where(jnp.logical_and((ii // c) == (jj // c), ii >= jj), 1.0, 0.0)
        gc = _cumsum_rows(tril, g)
        gta_ref[...] = gc.T
        lane = lax.broadcasted_iota(jnp.int32, (r, BA_DIM), 1)
        ga_ref[...] = jnp.where(lane < GDN_HEADS, beta, gc)
        yield
        for ci in range(n_chunks):
            rows = slice(ci * c, (ci + 1) * c)
            for slab in range(CONV_DIM // LANES):
                cols = slice(slab * LANES, (slab + 1) * LANES)
                ext = xbuf_ref[pl.ds(ci * c, c + 8), cols]
                s1 = pltpu.roll(ext, 1, axis=0)
                u2 = pltpu.roll(ext * cw_ref[1:2, cols] + s1 * cw_ref[0:1, cols], 2, axis=0)
                yc = (ext * cw_ref[3:4, cols] + s1 * cw_ref[2:3, cols] + u2)[8:]
                ya_ref[rows, cols] = _silu(yc)
                if slab < GDN_KEY_DIM // LANES:
                    ya_ref[rows, cols] = _l2norm(ya_ref[rows, cols], ones_ref) * (GDN_DK ** -0.5)
                elif slab < 2 * GDN_KEY_DIM // LANES:
                    ya_ref[rows, cols] = _l2norm(ya_ref[rows, cols], ones_ref)
                yield
        xbuf_ref[0:8, :] = xbuf_ref[r:r + 8, :]
        yield

    def stage_b():
        bcast = _lane_bcast(gb_ref[...], sel_ref, range(2 * GDN_HEADS))
        beta_b, gc_b = bcast[:GDN_HEADS], bcast[GDN_HEADS:]
        glast = [[gc_b[h][(ci + 1) * c - 1:(ci + 1) * c, :] for h in hs] for ci in range(n_chunks)]
        gl_b = [jnp.concatenate([jnp.broadcast_to(glast[ci][h], (c, GDN_DK)) for ci in range(n_chunks)],
                                axis=0) for h in hs]
        wy = {}
        yield from _gdn_prep_steps(wy, yb_ref, beta_b, gc_b, gl_b, gtb_ref[...], c, c, ones_ref,
                                   normalized=True)
        u, w, qk, qd, kd = (wy[name] for name in ("u", "w", "qk", "qd", "kd"))
        s_cur = [s_ref[h] for h in hs]
        for ci in range(n_chunks):
            ws = [_dot(jnp.concatenate([w[ci][h], qd[ci][h]], axis=0), s_cur[h]) for h in hs]
            yield
            v_new = [u[ci][h] - ws[h][:c] for h in hs]
            o = [ws[h][c:] + _dot(qk[ci][h], v_new[h]) for h in hs]
            s_cur = [s_cur[h] * jnp.exp(glast[ci][h]) + _dot_tn(kd[ci][h], v_new[h]) for h in hs]
            yield
            for h in hs:
                z = z_ref[ci * c:(ci + 1) * c, h * GDN_DV:(h + 1) * GDN_DV]
                o_ref[ci * c:(ci + 1) * c, h * GDN_DV:(h + 1) * GDN_DV] = _gdn_out(o[h], z, gnorm_ref[...], ones_ref)
        for h in hs:
            s_ref[h] = s_cur[h]
            sfin_ref[0, h] = s_cur[h]

    a_pieces = 2 + n_chunks * (CONV_DIM // LANES)
    b_levels = 2 * (int(math.log2(c)) - 1) + 2 + 2 * n_chunks
    a_steps = stage_a()
    done = 0
    for i, _ in enumerate(stage_b()):
        target = -(-(i + 1) * a_pieces // b_levels)
        for _ in range(target - done):
            next(a_steps, None)
        done = target
    for _ in a_steps:
        pass


def _gdn_prompt_call(qkvz, ba, lw, batch, seq):
    c = GDN_CHUNK * GDN_STEP_CHUNKS
    nc = seq // c
    z_col = CONV_DIM // GDN_VAL_DIM
    return pl.pallas_call(
        _gdn_prompt_kernel,
        grid=(batch, nc + 1),
        in_specs=[pl.BlockSpec((c, CONV_DIM), lambda b, s: (b * nc + jnp.minimum(s, nc - 1), 0)),
                  pl.BlockSpec((c, GDN_VAL_DIM), lambda b, s: (b * nc + jnp.maximum(s - 1, 0), z_col)),
                  pl.BlockSpec((c, BA_DIM), lambda b, s: (b * nc + jnp.minimum(s, nc - 1), 0)),
                  _resident((CONV_WIDTH, CONV_DIM)), _resident((1, BA_DIM)), _resident((1, BA_DIM)),
                  _resident((1, GDN_DV)), _resident(lw["ones"].shape), _resident(lw["sel"].shape)],
        out_specs=[pl.BlockSpec((c, GDN_VAL_DIM), lambda b, s: (b * nc + jnp.maximum(s - 1, 0), 0)),
                   pl.BlockSpec((1, GDN_HEADS, GDN_DK, GDN_DV), lambda b, s: (b, 0, 0, 0))],
        out_shape=[jax.ShapeDtypeStruct((batch * seq, GDN_VAL_DIM), F32),
                   jax.ShapeDtypeStruct((batch, GDN_HEADS, GDN_DK, GDN_DV), F32)],
        scratch_shapes=[pltpu.VMEM((GDN_HEADS, GDN_DK, GDN_DV), F32),
                        pltpu.VMEM((c + 8, CONV_DIM), F32),
                        pltpu.VMEM((c, CONV_DIM), F32), pltpu.VMEM((c, BA_DIM), F32), pltpu.VMEM((BA_DIM, c), F32),
                        pltpu.VMEM((c, CONV_DIM), F32), pltpu.VMEM((c, BA_DIM), F32), pltpu.VMEM((BA_DIM, c), F32)],
        compiler_params=pltpu.CompilerParams(dimension_semantics=("arbitrary", "arbitrary"),
                                             vmem_limit_bytes=VMEM_LIMIT),
        name="gdn_prompt",
    )(qkvz, qkvz, ba, lw["conv_w"], lw["alog_row"], lw["dtb_row"], lw["gnorm"], lw["ones"], lw["sel"])


def _gdn_decode_kernel(qkvz_ref, ba_ref, sconv_ref, s0_ref, cw_ref, alog_ref, dtb_ref, gnorm_ref,
                       ones_ref, sel_ref, o_ref, snew_ref, xbuf_ref, *, dec_seq):
    nb = DEC_SEQ_BLOCK
    t = dec_seq
    c = nb * t
    ys = []
    for s in range(nb):
        xbuf_ref[s, 5:8, :] = sconv_ref[s]
        xbuf_ref[s, 8:8 + t, :] = qkvz_ref[s * t:(s + 1) * t, :CONV_DIM]
        ys.append(_conv_silu(xbuf_ref.at[s], cw_ref, t))
    y = jnp.concatenate(ys, axis=0)

    beta, g = _gates(ba_ref[...], alog_ref[...], dtb_ref[...])
    ii, jj = _iota2(c)
    same = (ii // t) == (jj // t)
    tril = jnp.where(jnp.logical_and(same, ii >= jj), 1.0, 0.0).astype(F32)
    ones = jnp.where(same, 1.0, 0.0).astype(F32)
    gc = _cumsum_rows(tril, g)
    gl = _cumsum_rows(ones, g)
    gc_t = gc.T
    lane = lax.broadcasted_iota(jnp.int32, (c, BA_DIM), 1)
    bcast = _lane_bcast(jnp.where(lane < GDN_HEADS, beta, gc), sel_ref, range(2 * GDN_HEADS))
    beta_b, gc_b = bcast[:GDN_HEADS], bcast[GDN_HEADS:]
    gl_b = _lane_bcast(gl, sel_ref, range(GDN_HEADS, 2 * GDN_HEADS))
    wy = {}
    for _ in _gdn_prep_steps(wy, y, beta_b, gc_b, gl_b, gc_t, c, t, ones_ref, normalized=False):
        pass
    u, w, qk, qd, kd = (wy[name][0] for name in ("u", "w", "qk", "qd", "kd"))
    hs = range(GDN_HEADS)
    units = [(s, h) for h in hs for s in range(nb)]
    s_old = {(s, h): s0_ref[s, h] for s, h in units}
    ws = {(s, h): _dot(jnp.concatenate([w[h][s * t:(s + 1) * t], qd[h][s * t:(s + 1) * t]], axis=0),
                       s_old[s, h]) for s, h in units}
    v_new = [jnp.concatenate([u[h][s * t:(s + 1) * t] - ws[s, h][:t] for s in range(nb)], axis=0) for h in hs]
    o = [jnp.concatenate([ws[s, h][t:] for s in range(nb)], axis=0) + _dot(qk[h], v_new[h]) for h in hs]
    cd = [jnp.exp(gl_b[h]) for h in hs]
    kd_t = [kd[h].T for h in hs]
    col_seq = lax.broadcasted_iota(jnp.int32, (GDN_DK, c), 1) // t
    for s, h in units:
        upd = _dot(jnp.where(col_seq == s, kd_t[h], 0.0), v_new[h])
        snew_ref[s, h] = s_old[s, h] * cd[h][s * t:s * t + 1, :] + upd
    for h in hs:
        z = qkvz_ref[:, CONV_DIM + h * GDN_DV:CONV_DIM + (h + 1) * GDN_DV]
        o_ref[:, h * GDN_DV:(h + 1) * GDN_DV] = _gdn_out(o[h], z, gnorm_ref[...], ones_ref)


def _gdn_decode_call(qkvz, ba, state_conv, state_gdn, lw, dec_batch, dec_seq):
    nb = DEC_SEQ_BLOCK
    rows = nb * dec_seq
    return pl.pallas_call(
        functools.partial(_gdn_decode_kernel, dec_seq=dec_seq),
        grid=(dec_batch // nb,),
        in_specs=[pl.BlockSpec((rows, QKVZ_DIM), lambda i: (i, 0)),
                  pl.BlockSpec((rows, BA_DIM), lambda i: (i, 0)),
                  pl.BlockSpec((nb, CONV_WIDTH - 1, CONV_DIM), lambda i: (i, 0, 0)),
                  pl.BlockSpec((nb, GDN_HEADS, GDN_DK, GDN_DV), lambda i: (i, 0, 0, 0)),
                  _resident((CONV_WIDTH, CONV_DIM)), _resident((1, BA_DIM)), _resident((1, BA_DIM)),
                  _resident((1, GDN_DV)), _resident(lw["ones"].shape), _resident(lw["sel"].shape)],
        out_specs=[pl.BlockSpec((rows, GDN_VAL_DIM), lambda i: (i, 0)),
                   pl.BlockSpec((nb, GDN_HEADS, GDN_DK, GDN_DV), lambda i: (i, 0, 0, 0))],
        out_shape=[jax.ShapeDtypeStruct((dec_batch * dec_seq, GDN_VAL_DIM), F32),
                   jax.ShapeDtypeStruct((dec_batch, GDN_HEADS, GDN_DK, GDN_DV), F32)],
        scratch_shapes=[pltpu.VMEM((nb, 16, CONV_DIM), F32)],
        compiler_params=pltpu.CompilerParams(dimension_semantics=("arbitrary",),
                                             vmem_limit_bytes=VMEM_LIMIT),
        name="gdn_decode",
    )(qkvz, ba, state_conv, state_gdn, lw["conv_w"], lw["alog_row"], lw["dtb_row"], lw["gnorm"],
      lw["ones"], lw["sel"])


def _bias_table_kernel(bucket_ref, rb_ref, out_ref):
    bucket = bucket_ref[...]
    for h in range(SWA_HEADS):
        acc = jnp.zeros(bucket.shape, F32)
        for b in range(NUM_BUCKETS):
            acc = jnp.where(bucket == b, rb_ref[b, h], acc)
        out_ref[h] = jnp.where(bucket < 0, NEG_INF, acc)


def _bias_table_call(bucket, rel_bias):
    r, c = bucket.shape
    return pl.pallas_call(
        _bias_table_kernel,
        in_specs=[pl.BlockSpec(memory_space=pltpu.VMEM), pl.BlockSpec(memory_space=pltpu.SMEM)],
        out_specs=pl.BlockSpec(memory_space=pltpu.VMEM),
        out_shape=jax.ShapeDtypeStruct((SWA_HEADS, r, c), F32),
        name="t5_bias_table",
    )(bucket, rel_bias)


def _t5_bucket(dist, valid):
    d = jnp.maximum(dist, 0)
    exact = NUM_BUCKETS // 2
    log_ratio = jnp.log(jnp.maximum(d, 1).astype(F32) / exact) / math.log(REL_MAX_DISTANCE / exact)
    large = jnp.minimum(exact + (log_ratio * (NUM_BUCKETS - exact)).astype(jnp.int32), NUM_BUCKETS - 1)
    bucket = jnp.where(d < exact, d, large)
    return jnp.where(valid, bucket, -1).astype(jnp.int32)


def _swa_prompt_kernel(q_ref, kvp_ref, kvc_ref, bias_ref, sink_ref, o_ref):
    w, dh, nq = SWA_WINDOW, SWA_HEAD_DIM, SWA_STEP_BLOCKS
    first = pl.program_id(1) == 0
    col = lax.broadcasted_iota(jnp.int32, (w, 2 * w), 1)
    drop_prev = jnp.logical_and(first, col < w)
    kv = jnp.concatenate([kvp_ref[...], kvc_ref[...]], axis=0).astype(BF16)
    k_h = [kv[:, h * dh:(h + 1) * dh] for h in range(SWA_KV_HEADS)]
    v_h = [kv[:, SWA_KV_DIM + h * dh:SWA_KV_DIM + (h + 1) * dh] for h in range(SWA_KV_HEADS)]
    units = [(b, hg) for b in range(nq) for hg in range(SWA_HEADS)]
    n = range(len(units))
    q = [(q_ref[b * w:(b + 1) * w, hg * dh:(hg + 1) * dh] * (dh ** -0.5)).astype(BF16) for b, hg in units]
    s = [lax.dot_general(q[i], k_h[hg // SWA_GROUP][b * w:(b + 2) * w], (((1,), (1,)), ((), ())),
                         preferred_element_type=F32) + bias_ref[hg] for i, (b, hg) in enumerate(units)]
    s = [jnp.where(drop_prev, NEG_INF, s[i]) if b == 0 else s[i] for i, (b, hg) in enumerate(units)]
    sink = [sink_ref[hg] for b, hg in units]
    m = [jnp.maximum(jnp.max(s[i], axis=-1, keepdims=True), sink[i]) for i in n]
    p = [jnp.exp(s[i] - m[i]) for i in n]
    denom = [jnp.sum(p[i], axis=-1, keepdims=True) + jnp.exp(sink[i] - m[i]) for i in n]
    o = [jnp.dot(p[i].astype(BF16), v_h[hg // SWA_GROUP][b * w:(b + 2) * w], preferred_element_type=F32)
         / denom[i] for i, (b, hg) in enumerate(units)]
    for i, (b, hg) in enumerate(units):
        o_ref[b * w:(b + 1) * w, hg * dh:(hg + 1) * dh] = o[i]


def _swa_prompt_call(qs, kv, bias, sinks, batch, seq):
    w, nq = SWA_WINDOW, SWA_STEP_BLOCKS
    nb = seq // (w * nq)
    return pl.pallas_call(
        _swa_prompt_kernel,
        grid=(batch, nb),
        in_specs=[pl.BlockSpec((nq * w, SWA_Q_DIM), lambda b, n: (b * nb + n, 0)),
                  pl.BlockSpec((w, 2 * SWA_KV_DIM),
                               lambda b, n: ((b * nb + n) * nq - jnp.minimum(n, 1), 0)),
                  pl.BlockSpec((nq * w, 2 * SWA_KV_DIM), lambda b, n: (b * nb + n, 0)),
                  _resident((SWA_HEADS, w, 2 * w)),
                  pl.BlockSpec(memory_space=pltpu.SMEM)],
        out_specs=pl.BlockSpec((nq * w, SWA_Q_DIM), lambda b, n: (b * nb + n, 0)),
        out_shape=jax.ShapeDtypeStruct((batch * seq, SWA_Q_DIM), F32),
        compiler_params=pltpu.CompilerParams(dimension_semantics=("arbitrary", "arbitrary"),
                                             vmem_limit_bytes=VMEM_LIMIT),
        name="swa_prompt",
    )(qs, kv, kv, bias, sinks)


def _swa_decode_kernel(q_ref, kvn_ref, kt_ref, vt_ref, bbuf_ref, bnew_ref, sink_ref,
                       o_ref, kto_ref, vto_ref, *, dec_seq):
    nb = DEC_SEQ_BLOCK
    t = dec_seq
    dh = SWA_HEAD_DIM
    rows = nb * t
    m_rows = SWA_GROUP * rows
    wb = kt_ref.shape[3]
    ri = lax.broadcasted_iota(jnp.int32, (m_rows, 1), 0)
    row_seq = (ri % rows) // t
    row_grp = ri // rows
    for h in range(SWA_KV_HEADS):
        lo, hi = h * dh, (h + 1) * dh
        q = jnp.concatenate([q_ref[:, (h * SWA_GROUP + g) * dh:(h * SWA_GROUP + g + 1) * dh]
                             for g in range(SWA_GROUP)], axis=0)
        sink = jnp.zeros((m_rows, 1), F32)
        for g in range(SWA_GROUP):
            sink = jnp.where(row_grp == g, sink_ref[h * SWA_GROUP + g], sink)
        k_new = jnp.concatenate([kvn_ref[:, lo:hi], jnp.zeros((m_rows - rows, dh), F32)], axis=0)
        v_new = jnp.concatenate([kvn_ref[:, SWA_KV_DIM + lo:SWA_KV_DIM + hi],
                                 jnp.zeros((m_rows - rows, dh), F32)], axis=0)
        s_new = _dot_nt(q, k_new) * (dh ** -0.5) + bnew_ref[h]
        s_buf = jnp.zeros((m_rows, wb), F32)
        for s in range(nb):
            s_buf = jnp.where(row_seq == s, _dot(q, kt_ref[s, h]), s_buf)
        s_buf = s_buf * (dh ** -0.5) + bbuf_ref[h]
        m = jnp.maximum(jnp.maximum(jnp.max(s_buf, axis=-1, keepdims=True),
                                    jnp.max(s_new, axis=-1, keepdims=True)), sink)
        p_buf = jnp.exp(s_buf - m)
        p_new = jnp.exp(s_new - m)
        denom = (jnp.sum(p_buf, axis=-1, keepdims=True) + jnp.sum(p_new, axis=-1, keepdims=True)
                 + jnp.exp(sink - m))
        o = _dot(p_new, v_new)
        for s in range(nb):
            o = o + _dot_nt(jnp.where(row_seq == s, p_buf, 0.0), vt_ref[s, h])
        o = o / denom
        for g in range(SWA_GROUP):
            hg = h * SWA_GROUP + g
            o_ref[:, hg * dh:(hg + 1) * dh] = o[g * rows:(g + 1) * rows]
    new_t = jnp.concatenate([kvn_ref[...], jnp.zeros((wb - rows, 2 * SWA_KV_DIM), F32)], axis=0).T
    lane = lax.broadcasted_iota(jnp.int32, (dh, wb), 1)
    for part, (src_ref, dst_ref) in enumerate(((kt_ref, kto_ref), (vt_ref, vto_ref))):
        for h in range(SWA_KV_HEADS):
            fresh = new_t[part * SWA_KV_DIM + h * dh:part * SWA_KV_DIM + (h + 1) * dh, :]
            for s in range(nb):
                kept = pltpu.roll(src_ref[s, h], wb - t, axis=1)
                tail = pltpu.roll(fresh, (wb - t - s * t) % wb, axis=1)
                dst_ref[s, h] = jnp.where(lane >= wb - t, tail, kept)


def _swa_decode_call(qs, kvn, kt, vt, bias_buf, bias_new, sinks, dec_batch, dec_seq):
    nb = DEC_SEQ_BLOCK
    rows = nb * dec_seq
    wb = kt.shape[3]
    m_rows = SWA_GROUP * rows
    assert rows <= wb and wb == LANES
    win_spec = pl.BlockSpec((nb, SWA_KV_HEADS, SWA_HEAD_DIM, wb), lambda i: (i, 0, 0, 0))
    return pl.pallas_call(
        functools.partial(_swa_decode_kernel, dec_seq=dec_seq),
        grid=(dec_batch // nb,),
        in_specs=[pl.BlockSpec((rows, SWA_Q_DIM), lambda i: (i, 0)),
                  pl.BlockSpec((rows, 2 * SWA_KV_DIM), lambda i: (i, 0)),
                  win_spec, win_spec,
                  _resident((SWA_KV_HEADS, m_rows, wb)),
                  _resident((SWA_KV_HEADS, m_rows, m_rows)),
                  pl.BlockSpec(memory_space=pltpu.SMEM)],
        out_specs=[pl.BlockSpec((rows, SWA_Q_DIM), lambda i: (i, 0)), win_spec, win_spec],
        out_shape=[jax.ShapeDtypeStruct((dec_batch * dec_seq, SWA_Q_DIM), F32),
                   jax.ShapeDtypeStruct(kt.shape, F32), jax.ShapeDtypeStruct(vt.shape, F32)],
        compiler_params=pltpu.CompilerParams(dimension_semantics=("arbitrary",),
                                             vmem_limit_bytes=VMEM_LIMIT),
        name="swa_decode",
    )(qs, kvn, kt, vt, bias_buf, bias_new, sinks)


def _prompt_bias(rel_bias):
    w = SWA_WINDOW
    dist = w + jnp.arange(w)[:, None] - jnp.arange(2 * w)[None, :]
    valid = (dist >= 0) & (dist < w)
    return _bias_table_call(_t5_bucket(dist, valid), rel_bias)


def _decode_bias(rel_bias, wb, dec_seq):
    nb, t = DEC_SEQ_BLOCK, dec_seq
    rows = nb * t
    m_rows = SWA_GROUP * rows
    tok = jnp.arange(t)
    dist = jnp.concatenate([wb + tok[:, None] - jnp.arange(wb)[None, :], tok[:, None] - tok[None, :]], axis=1)
    bucket = _t5_bucket(dist, (dist >= 0) & (dist < SWA_WINDOW))
    pad_r, pad_c = -t % 8, -(wb + t) % LANES
    bucket = jnp.pad(bucket, ((0, pad_r), (0, pad_c)), constant_values=-1)
    tab = _bias_table_call(bucket, rel_bias)[:, :t, :wb + t]
    tab = tab.reshape(SWA_KV_HEADS, SWA_GROUP, 1, t, wb + t)
    per_row = jnp.broadcast_to(tab, (SWA_KV_HEADS, SWA_GROUP, nb, t, wb + t)).reshape(SWA_KV_HEADS, m_rows, wb + t)
    r_seq = (jnp.arange(m_rows) % rows) // t
    cidx = jnp.arange(m_rows)
    own = (r_seq[:, None] == (cidx // t)[None, :]) & (cidx < rows)[None, :]
    bias_new = jnp.where(own[None], jnp.tile(per_row[:, :, wb:], (1, 1, m_rows // t)), NEG_INF)
    return per_row[:, :, :wb], bias_new


def _cast_kernel(*refs):
    n = len(refs) // 2
    for src_ref, dst_ref in zip(refs[:n], refs[n:]):
        dst_ref[...] = src_ref[...].astype(dst_ref.dtype)


def _to_bf16(ws):
    r, c = ws[0].shape
    tr = r
    while tr * c * 4 > CAST_BLOCK_BYTES and tr % 16 == 0:
        tr //= 2
    spec = pl.BlockSpec((tr, c), lambda i: (i, 0))
    return pl.pallas_call(
        _cast_kernel,
        grid=(r // tr,),
        in_specs=[spec] * len(ws),
        out_specs=[spec] * len(ws),
        out_shape=[jax.ShapeDtypeStruct((r, c), BF16)] * len(ws),
        compiler_params=pltpu.CompilerParams(dimension_semantics=("arbitrary",)),
        name="weights_to_bf16",
    )(*ws)


def _pack_layer(i, norm_ffn1_pre, norm_ffn1_post, ffn1_w_gate, ffn1_w_up, ffn1_w_down, norm_mix_pre,
                norm_mix_post, w_in, conv_w, gdn_a_log, gdn_dt_bias, gdn_norm, swa_sinks, w_out,
                norm_ffn2_pre, norm_ffn2_post, ffn2_w_gate, ffn2_w_up, ffn2_w_down, ple_gate, ple_proj,
                norm_ple_post):
    row = lambda g: g[i].reshape(1, -1).astype(F32)
    wt = jnp.transpose(w_in[i])
    n_gdn = QKVZ_DIM + 2 * GDN_HEADS
    win = jnp.concatenate([wt[:n_gdn], jnp.zeros((BA_DIM - 2 * GDN_HEADS, D_MODEL), wt.dtype), wt[n_gdn:]],
                          axis=0).astype(BF16)
    lane_pad = lambda v: jnp.zeros((1, BA_DIM), F32).at[0, GDN_HEADS:2 * GDN_HEADS].set(v[i].astype(F32))
    kk = np.arange(2 * LANES)[:, None] % LANES
    sel = (kk == (np.arange(2 * GDN_HEADS * LANES)[None, :] // LANES)).astype(np.float32)
    wg1, wu1, wg2, wu2 = _to_bf16([ffn1_w_gate[i], ffn1_w_up[i], ffn2_w_gate[i], ffn2_w_up[i]])
    wd1, wd2 = _to_bf16([ffn1_w_down[i], ffn2_w_down[i]])
    wo, wpg = _to_bf16([w_out[i], ple_gate[i]])
    return dict(
        ones=jnp.ones((LANES, LANES), F32), sel=jnp.asarray(sel, BF16),
        g1pre=row(norm_ffn1_pre), g1post=row(norm_ffn1_post), wg1=wg1, wu1=wu1, wd1=wd1,
        gmix=row(norm_mix_pre), gmixpost=row(norm_mix_post), win=win,
        conv_w=conv_w[i].astype(F32), alog_row=lane_pad(gdn_a_log), dtb_row=lane_pad(gdn_dt_bias),
        gnorm=row(gdn_norm), sinks=swa_sinks[i].astype(F32), wo=wo,
        g2pre=row(norm_ffn2_pre), g2post=row(norm_ffn2_post), wg2=wg2, wu2=wu2, wd2=wd2,
        wpg=wpg, wpp=ple_proj[i].astype(BF16), gple=row(norm_ple_post))


def kernel(x_prompt, x_sample, state_conv, state_gdn, cache_swa_k, cache_swa_v, p_prompt, p_sample,
           rel_bias, norm_ffn1_pre, norm_ffn1_post, ffn1_w_gate, ffn1_w_up, ffn1_w_down,
           norm_mix_pre, norm_mix_post, w_in, conv_w, gdn_a_log, gdn_dt_bias, gdn_norm, swa_sinks,
           w_out, norm_ffn2_pre, norm_ffn2_post, ffn2_w_gate, ffn2_w_up, ffn2_w_down,
           ple_gate, ple_proj, norm_ple_post):
    depth = w_in.shape[0]
    batch, seq, _ = x_prompt.shape
    dec_batch, dec_seq, _ = x_sample.shape
    wb = cache_swa_k.shape[2]
    wp = min(SWA_WINDOW, seq)
    rel_bias = rel_bias.astype(F32)
    bias_p = _prompt_bias(rel_bias)
    bias_db, bias_dn = _decode_bias(rel_bias, wb, dec_seq)

    yp = x_prompt.reshape(batch * seq, D_MODEL)
    ys = x_sample.reshape(dec_batch * dec_seq, D_MODEL)
    outs = [[] for _ in range(8)]
    for i in range(depth):
        lw = _pack_layer(i, norm_ffn1_pre, norm_ffn1_post, ffn1_w_gate, ffn1_w_up, ffn1_w_down,
                         norm_mix_pre, norm_mix_post, w_in, conv_w, gdn_a_log, gdn_dt_bias, gdn_norm,
                         swa_sinks, w_out, norm_ffn2_pre, norm_ffn2_post, ffn2_w_gate, ffn2_w_up,
                         ffn2_w_down, ple_gate, ple_proj, norm_ple_post)
        x1, qkvz, ba, qs, kv = _head_call(yp, lw)
        gdn_o, s_fin = _gdn_prompt_call(qkvz, ba, lw, batch, seq)
        swa_o = _swa_prompt_call(qs, kv, bias_p, lw["sinks"], batch, seq)
        yp = _tail_call(x1, gdn_o, swa_o, p_prompt[i].reshape(batch * seq, PLE_DIM), lw)
        kv3 = kv.reshape(batch, seq, 2 * SWA_KV_DIM)
        outs[0].append(qkvz.reshape(batch, seq, QKVZ_DIM)[:, seq - (CONV_WIDTH - 1):, :CONV_DIM])
        outs[1].append(s_fin)
        outs[2].append(kv3[:, seq - wp:, :SWA_KV_DIM].reshape(batch, wp, SWA_KV_HEADS, SWA_HEAD_DIM))
        outs[3].append(kv3[:, seq - wp:, SWA_KV_DIM:].reshape(batch, wp, SWA_KV_HEADS, SWA_HEAD_DIM))
        x1, qkvz, ba, qs, kv = _head_call(ys, lw)
        gdn_o, s_new = _gdn_decode_call(qkvz, ba, state_conv[i], state_gdn[i], lw, dec_batch, dec_seq)
        kt = jnp.transpose(cache_swa_k[i], (0, 2, 3, 1))
        vt = jnp.transpose(cache_swa_v[i], (0, 2, 3, 1))
        swa_o, kt_new, vt_new = _swa_decode_call(qs, kv, kt, vt, bias_db, bias_dn, lw["sinks"], dec_batch, dec_seq)
        ys = _tail_call(x1, gdn_o, swa_o, p_sample[i].reshape(dec_batch * dec_seq, PLE_DIM), lw)
        xp = jnp.concatenate([state_conv[i], qkvz.reshape(dec_batch, dec_seq, QKVZ_DIM)[:, :, :CONV_DIM]], axis=1)
        outs[4].append(xp[:, dec_seq:])
        outs[5].append(s_new)
        outs[6].append(jnp.transpose(kt_new, (0, 3, 1, 2)))
        outs[7].append(jnp.transpose(vt_new, (0, 3, 1, 2)))
    return (yp.reshape(batch, seq, D_MODEL), ys.reshape(dec_batch, dec_seq, D_MODEL),
            *[jnp.stack(o) for o in outs])
```

```python
import functools
import math

import numpy as np
import jax
import jax.numpy as jnp
from jax import lax
from jax.experimental import pallas as pl
from jax.experimental.pallas import tpu as pltpu

F32 = jnp.float32
BF16 = jnp.bfloat16

D_MODEL = 1024
NORM_EPS = 1e-6
PLE_DIM = 256
FFN_DIM = 2816
GDN_HEADS = 4
GDN_DK = 128
GDN_DV = 128
GDN_KEY_DIM = GDN_HEADS * GDN_DK
GDN_VAL_DIM = GDN_HEADS * GDN_DV
CONV_DIM = 2 * GDN_KEY_DIM + GDN_VAL_DIM
CONV_WIDTH = 4
SWA_HEADS = 8
SWA_KV_HEADS = 2
SWA_GROUP = SWA_HEADS // SWA_KV_HEADS
SWA_HEAD_DIM = 64
SWA_Q_DIM = SWA_HEADS * SWA_HEAD_DIM
SWA_KV_DIM = SWA_KV_HEADS * SWA_HEAD_DIM
SWA_WINDOW = 128
NUM_BUCKETS = 32
REL_MAX_DISTANCE = 128

QKVZ_DIM = CONV_DIM + GDN_VAL_DIM
BA_DIM = 128
PROJ_PACKED = QKVZ_DIM + BA_DIM + SWA_Q_DIM + 2 * SWA_KV_DIM

GDN_CHUNK = 64
GDN_STEP_CHUNKS = 4
SWA_STEP_BLOCKS = 4
ROW_TILE = 512
DEC_SEQ_BLOCK = 16
CAST_BLOCK_BYTES = 2 * 1024 * 1024
VMEM_LIMIT = 56 * 1024 * 1024

NEG_INF = float("-inf")
LANES = 128


def _resident(shape):
    nd = len(shape)
    return pl.BlockSpec(shape, lambda *_: (0,) * nd, pipeline_mode=pl.Buffered(1))


def _rms(x, gain):
    ms = jnp.mean(x * x, axis=-1, keepdims=True)
    return (x * lax.rsqrt(ms + NORM_EPS)) * gain


def _sigmoid(x):
    return 1.0 / (1.0 + jnp.exp(-x))


def _silu(x):
    h = 0.5 * x
    return h + h * jnp.tanh(h)


def _dot(a, b):
    return jnp.dot(a, b, preferred_element_type=F32)


def _dot_nt(a, b):
    return lax.dot_general(a, b, (((1,), (1,)), ((), ())), preferred_element_type=F32)


def _dot_tn(a, b):
    return lax.dot_general(a, b, (((0,), (0,)), ((), ())), preferred_element_type=F32)


def _split(a):
    hi = a.astype(BF16)
    lo = (a - hi.astype(F32)).astype(BF16)
    return hi, lo


def _inv_rms(x):
    return lax.rsqrt(jnp.mean(x * x, axis=-1, keepdims=True) + NORM_EPS)


def _prenorm_dots(x, gain, ws, transposed=False):
    h = (x * gain).astype(BF16)
    inv = _inv_rms(x)
    dims = (((1,), (1 if transposed else 0,)), ((), ()))
    return [lax.dot_general(h, w, dims, preferred_element_type=F32) * inv for w in ws]


def _swiglu_block(x, gain, wg_ref, wu_ref, wd_ref):
    g, u = _prenorm_dots(x, gain, [wg_ref[...], wu_ref[...]])
    a = (_silu(g) * u).astype(BF16)
    return jnp.dot(a, wd_ref[...], preferred_element_type=F32)


def _head_kernel(x_ref, g1pre_ref, g1post_ref, wg_ref, wu_ref, wd_ref, gmix_ref, win_ref,
                 x1_ref, qkvz_ref, ba_ref, qs_ref, kv_ref):
    x = x_ref[...]
    y = _swiglu_block(x, g1pre_ref[...], wg_ref, wu_ref, wd_ref)
    x1 = x + 0.5 * _rms(y, g1post_ref[...])
    x1_ref[...] = x1
    c0, c1, c2 = QKVZ_DIM, QKVZ_DIM + BA_DIM, QKVZ_DIM + BA_DIM + SWA_Q_DIM
    qkvz_ref[...], ba_ref[...], qs_ref[...], kv_ref[...] = _prenorm_dots(
        x1, gmix_ref[...], [win_ref[:c0, :], win_ref[c0:c1, :], win_ref[c1:c2, :], win_ref[c2:, :]], transposed=True)


def _head_call(x, lw):
    n = x.shape[0]
    tm = min(ROW_TILE, n)
    row = lambda w: pl.BlockSpec((tm, w), lambda i: (i, 0))
    return pl.pallas_call(
        _head_kernel,
        grid=(n // tm,),
        in_specs=[row(D_MODEL), _resident((1, D_MODEL)), _resident((1, D_MODEL)),
                  _resident((D_MODEL, FFN_DIM)), _resident((D_MODEL, FFN_DIM)),
                  _resident((FFN_DIM, D_MODEL)), _resident((1, D_MODEL)),
                  _resident((PROJ_PACKED, D_MODEL))],
        out_specs=[row(D_MODEL), row(QKVZ_DIM), row(BA_DIM), row(SWA_Q_DIM), row(2 * SWA_KV_DIM)],
        out_shape=[jax.ShapeDtypeStruct((n, D_MODEL), F32), jax.ShapeDtypeStruct((n, QKVZ_DIM), F32),
                   jax.ShapeDtypeStruct((n, BA_DIM), F32), jax.ShapeDtypeStruct((n, SWA_Q_DIM), F32),
                   jax.ShapeDtypeStruct((n, 2 * SWA_KV_DIM), F32)],
        compiler_params=pltpu.CompilerParams(dimension_semantics=("arbitrary",),
                                             vmem_limit_bytes=VMEM_LIMIT),
        name="ffn1_inproj",
    )(x, lw["g1pre"], lw["g1post"], lw["wg1"], lw["wu1"], lw["wd1"], lw["gmix"], lw["win"])


def _tail_kernel(x_ref, gdn_ref, swa_ref, p_ref, wo_ref, gmixpost_ref, g2pre_ref, g2post_ref,
                 wg_ref, wu_ref, wd_ref, wpg_ref, wpp_ref, gple_ref, y_ref):
    x = x_ref[...]
    mix = (jnp.dot(gdn_ref[...].astype(BF16), wo_ref[:GDN_VAL_DIM, :], preferred_element_type=F32)
           + jnp.dot(swa_ref[...].astype(BF16), wo_ref[GDN_VAL_DIM:, :], preferred_element_type=F32))
    pp = jnp.dot(p_ref[...].astype(BF16), wpp_ref[...], preferred_element_type=F32)
    x = x + _rms(mix, gmixpost_ref[...])
    y = _swiglu_block(x, g2pre_ref[...], wg_ref, wu_ref, wd_ref)
    x = x + 0.5 * _rms(y, g2post_ref[...])
    gate = _sigmoid(jnp.dot(x.astype(BF16), wpg_ref[...], preferred_element_type=F32))
    y_ref[...] = x + _rms(gate * pp, gple_ref[...])


def _tail_call(x1, gdn_o, swa_o, p, lw):
    n = x1.shape[0]
    tm = min(ROW_TILE, n)
    row = lambda w: pl.BlockSpec((tm, w), lambda i: (i, 0))
    return pl.pallas_call(
        _tail_kernel,
        grid=(n // tm,),
        in_specs=[row(D_MODEL), row(GDN_VAL_DIM), row(SWA_Q_DIM), row(PLE_DIM),
                  _resident((GDN_VAL_DIM + SWA_Q_DIM, D_MODEL)), _resident((1, D_MODEL)),
                  _resident((1, D_MODEL)), _resident((1, D_MODEL)),
                  _resident((D_MODEL, FFN_DIM)), _resident((D_MODEL, FFN_DIM)),
                  _resident((FFN_DIM, D_MODEL)), _resident((D_MODEL, D_MODEL)),
                  _resident((PLE_DIM, D_MODEL)), _resident((1, D_MODEL))],
        out_specs=row(D_MODEL),
        out_shape=jax.ShapeDtypeStruct((n, D_MODEL), F32),
        compiler_params=pltpu.CompilerParams(dimension_semantics=("arbitrary",),
                                             vmem_limit_bytes=VMEM_LIMIT),
        name="outproj_ffn2_ple",
    )(x1, gdn_o, swa_o, p, lw["wo"], lw["gmixpost"], lw["g2pre"], lw["g2post"],
      lw["wg2"], lw["wu2"], lw["wd2"], lw["wpg"], lw["wpp"], lw["gple"])


def _iota2(c):
    return (lax.broadcasted_iota(jnp.int32, (c, c), 0), lax.broadcasted_iota(jnp.int32, (c, c), 1))


def _gates(ba, alog_row, dtb_row):
    beta = _sigmoid(ba)
    xa = ba + dtb_row
    softplus = jnp.maximum(xa, 0.0) + jnp.log1p(jnp.exp(-jnp.abs(xa)))
    g = -jnp.exp(alog_row) * softplus
    return beta, g


def _cumsum_rows(mask01, g):
    hi, lo = _split(g)
    m = mask01.astype(BF16)
    return jnp.dot(m, hi, preferred_element_type=F32) + jnp.dot(m, lo, preferred_element_type=F32)


def _hi_lo_lanes(x):
    hi, lo = _split(x)
    return jnp.concatenate([hi, lo], axis=1)


def _rowsum_bcast(x, ones_ref):
    return jnp.dot(x, ones_ref[...], preferred_element_type=F32)


def _lane_bcast(x, sel_ref, lanes):
    out = jnp.dot(_hi_lo_lanes(x), sel_ref[...], preferred_element_type=F32)
    return [out[:, l * 128:(l + 1) * 128] for l in lanes]


def _gdn_prep_steps(out, y, beta_b, gc_b, gl_b, gc_t, chunk, group, ones_ref, normalized):
    c = chunk
    n_chunks = y.shape[0] // c
    units = [(ci, h) for ci in range(n_chunks) for h in range(GDN_HEADS)]
    n = range(len(units))
    ii, jj = _iota2(c)
    same = (ii // group) == (jj // group)
    incl = jnp.logical_and(same, ii >= jj)
    strict = jnp.logical_and(same, ii > jj)

    def rows(t, ci, lo, hi):
        return t[ci * c:(ci + 1) * c, lo:hi]

    eg_b = [jnp.exp(t) for t in gc_b]
    ekd_b = [jnp.exp(gl_b[h] - gc_b[h]) for h in range(GDN_HEADS)]
    beta_u = [rows(beta_b[h], ci, 0, GDN_DK) for ci, h in units]
    eg_u = [rows(eg_b[h], ci, 0, GDN_DK) for ci, h in units]
    ekd_u = [rows(ekd_b[h], ci, 0, GDN_DK) for ci, h in units]
    gc_col = [rows(gc_b[h], ci, 0, c) for ci, h in units]
    gc_row = [gc_t[GDN_HEADS + h:GDN_HEADS + h + 1, ci * c:(ci + 1) * c] for ci, h in units]
    q = [rows(y, ci, h * GDN_DK, (h + 1) * GDN_DK) for ci, h in units]
    k = [rows(y, ci, GDN_KEY_DIM + h * GDN_DK, GDN_KEY_DIM + (h + 1) * GDN_DK) for ci, h in units]
    v = [rows(y, ci, 2 * GDN_KEY_DIM + h * GDN_DV, 2 * GDN_KEY_DIM + (h + 1) * GDN_DV) for ci, h in units]
    if not normalized:
        q = [_l2norm(t, ones_ref) * (GDN_DK ** -0.5) for t in q]
        k = [_l2norm(t, ones_ref) for t in k]
    decay = [jnp.exp(jnp.where(incl, gc_col[i] - gc_row[i], NEG_INF)) for i in n]
    kb = [k[i] * beta_u[i] for i in n]
    kq = [_dot_nt(jnp.concatenate([kb[i], q[i]], axis=0), k[i]) for i in n]
    yield
    a_mat = [jnp.where(strict, kq[i][:c] * decay[i], 0.0) for i in n]
    qk = [kq[i][c:] * decay[i] for i in n]
    eye = jnp.where(ii == jj, 1.0, 0.0).astype(F32)
    t_mat = [eye for _ in n]
    b = 1
    while b < group:
        lower = jnp.logical_and((ii // (2 * b)) == (jj // (2 * b)),
                                jnp.logical_and((ii % (2 * b)) >= b, (jj % (2 * b)) < b))
        m = [jnp.where(lower, a_mat[i], 0.0) for i in n]
        if b == 1:
            t_mat = [eye - m[i] for i in n]
        else:
            tm = [_dot(t_mat[i], m[i]) for i in n]
            yield
            t_mat = [t_mat[i] - _dot(tm[i], t_mat[i]) for i in n]
            yield
        b *= 2
    rhs = [jnp.concatenate([v[i] * beta_u[i], kb[i] * eg_u[i]], axis=-1) for i in n]
    sol = [_dot(t_mat[i], rhs[i]) for i in n]
    yield
    nest = lambda xs: [xs[ci * GDN_HEADS:(ci + 1) * GDN_HEADS] for ci in range(n_chunks)]
    out.update(u=nest([s[:, :GDN_DV] for s in sol]), w=nest([s[:, GDN_DV:] for s in sol]), qk=nest(qk),
               qd=nest([q[i] * eg_u[i] for i in n]), kd=nest([k[i] * ekd_u[i] for i in n]))


def _l2norm(t, ones_ref):
    return t * lax.rsqrt(_rowsum_bcast(t * t, ones_ref) + 1e-6)


def _gdn_out(o, z, gnorm_row, ones_ref):
    ms = _rowsum_bcast(o * o, ones_ref) * (1.0 / GDN_DV)
    return (o * lax.rsqrt(ms + NORM_EPS)) * gnorm_row * _silu(z)


def _conv_silu(xbuf_ref, cw_ref, c):
    y = xbuf_ref[pl.ds(5, c), :] * cw_ref[0:1, :]
    for j in range(1, CONV_WIDTH):
        y = y + xbuf_ref[pl.ds(5 + j, c), :] * cw_ref[j:j + 1, :]
    return _silu(y)


def _gdn_prompt_kernel(qkv_ref, z_ref, ba_ref, cw_ref, alog_ref, dtb_ref, gnorm_ref, ones_ref, sel_ref,
                       o_ref, sfin_ref, s_ref, xbuf_ref, y0_ref, g0_ref, gt0_ref, y1_ref, g1_ref, gt1_ref):
    step = pl.program_id(1)

    @pl.when(step == 0)
    def _():
        s_ref[...] = jnp.zeros_like(s_ref)
        xbuf_ref[0:8, :] = jnp.zeros((8, CONV_DIM), F32)
        y1_ref[...] = jnp.zeros_like(y1_ref)
        g1_ref[...] = jnp.zeros_like(g1_ref)
        gt1_ref[...] = jnp.zeros_like(gt1_ref)

    @pl.when(lax.rem(step, 2) == 0)
    def _():
        _gdn_prompt_step((y0_ref, g0_ref, gt0_ref), (y1_ref, g1_ref, gt1_ref), qkv_ref, z_ref, ba_ref, cw_ref,
                         alog_ref, dtb_ref, gnorm_ref, ones_ref, sel_ref, o_ref, sfin_ref, s_ref, xbuf_ref)

    @pl.when(lax.rem(step, 2) == 1)
    def _():
        _gdn_prompt_step((y1_ref, g1_ref, gt1_ref), (y0_ref, g0_ref, gt0_ref), qkv_ref, z_ref, ba_ref, cw_ref,
                         alog_ref, dtb_ref, gnorm_ref, ones_ref, sel_ref, o_ref, sfin_ref, s_ref, xbuf_ref)


def _gdn_prompt_step(a_refs, b_refs, qkv_ref, z_ref, ba_ref, cw_ref, alog_ref, dtb_ref, gnorm_ref, ones_ref,
                     sel_ref, o_ref, sfin_ref, s_ref, xbuf_ref):
    ya_ref, ga_ref, gta_ref = a_refs
    yb_ref, gb_ref, gtb_ref = b_refs
    c = GDN_CHUNK
    n_chunks = GDN_STEP_CHUNKS
    r = n_chunks * c
    hs = range(GDN_HEADS)

    def stage_a():
        xbuf_ref[8:8 + r, :] = qkv_ref[...]
        beta, g = _gates(ba_ref[...], alog_ref[...], dtb_ref[...])
        ii, jj = _iota2(r)
        tril = jnp.where(jnp.logical_and((ii // c) == (jj // c), ii >= jj), 1.0, 0.0)
        gc = _cumsum_rows(tril, g)
        gta_ref[...] = gc.T
        lane = lax.broadcasted_iota(jnp.int32, (r, BA_DIM), 1)
        ga_ref[...] = jnp.where(lane < GDN_HEADS, beta, gc)
        yield
        for ci in range(n_chunks):
            rows = slice(ci * c, (ci + 1) * c)
            for slab in range(CONV_DIM // LANES):
                cols = slice(slab * LANES, (slab + 1) * LANES)
                ext = xbuf_ref[pl.ds(ci * c, c + 8), cols]
                s1 = pltpu.roll(ext, 1, axis=0)
                u2 = pltpu.roll(ext * cw_ref[1:2, cols] + s1 * cw_ref[0:1, cols], 2, axis=0)
                yc = (ext * cw_ref[3:4, cols] + s1 * cw_ref[2:3, cols] + u2)[8:]
                ya_ref[rows, cols] = _silu(yc)
                if slab < GDN_KEY_DIM // LANES:
                    ya_ref[rows, cols] = _l2norm(ya_ref[rows, cols], ones_ref) * (GDN_DK ** -0.5)
                elif slab < 2 * GDN_KEY_DIM // LANES:
                    ya_ref[rows, cols] = _l2norm(ya_ref[rows, cols], ones_ref)
                yield
        xbuf_ref[0:8, :] = xbuf_ref[r:r + 8, :]
        yield

    def stage_b():
        bcast = _lane_bcast(gb_ref[...], sel_ref, range(2 * GDN_HEADS))
        beta_b, gc_b = bcast[:GDN_HEADS], bcast[GDN_HEADS:]
        glast = [[gc_b[h][(ci + 1) * c - 1:(ci + 1) * c, :] for h in hs] for ci in range(n_chunks)]
        gl_b = [jnp.concatenate([jnp.broadcast_to(glast[ci][h], (c, GDN_DK)) for ci in range(n_chunks)],
                                axis=0) for h in hs]
        wy = {}
        yield from _gdn_prep_steps(wy, yb_ref, beta_b, gc_b, gl_b, gtb_ref[...], c, c, ones_ref,
                                   normalized=True)
        u, w, qk, qd, kd = (wy[name] for name in ("u", "w", "qk", "qd", "kd"))
        s_cur = [s_ref[h] for h in hs]
        for ci in range(n_chunks):
            ws = [_dot(jnp.concatenate([w[ci][h], qd[ci][h]], axis=0), s_cur[h]) for h in hs]
            yield
            v_new = [u[ci][h] - ws[h][:c] for h in hs]
            o = [ws[h][c:] + _dot(qk[ci][h], v_new[h]) for h in hs]
            s_cur = [s_cur[h] * jnp.exp(glast[ci][h]) + _dot_tn(kd[ci][h], v_new[h]) for h in hs]
            yield
            for h in hs:
                z = z_ref[ci * c:(ci + 1) * c, h * GDN_DV:(h + 1) * GDN_DV]
                o_ref[ci * c:(ci + 1) * c, h * GDN_DV:(h + 1) * GDN_DV] = _gdn_out(o[h], z, gnorm_ref[...], ones_ref)
        for h in hs:
            s_ref[h] = s_cur[h]
            sfin_ref[0, h] = s_cur[h]

    a_pieces = 2 + n_chunks * (CONV_DIM // LANES)
    b_levels = 2 * (int(math.log2(c)) - 1) + 2 + 2 * n_chunks
    a_steps = stage_a()
    done = 0
    for i, _ in enumerate(stage_b()):
        target = -(-(i + 1) * a_pieces // b_levels)
        for _ in range(target - done):
            next(a_steps, None)
        done = target
    for _ in a_steps:
        pass


def _gdn_prompt_call(qkvz, ba, lw, batch, seq):
    c = GDN_CHUNK * GDN_STEP_CHUNKS
    nc = seq // c
    z_col = CONV_DIM // GDN_VAL_DIM
    return pl.pallas_call(
        _gdn_prompt_kernel,
        grid=(batch, nc + 1),
        in_specs=[pl.BlockSpec((c, CONV_DIM), lambda b, s: (b * nc + jnp.minimum(s, nc - 1), 0)),
                  pl.BlockSpec((c, GDN_VAL_DIM), lambda b, s: (b * nc + jnp.maximum(s - 1, 0), z_col)),
                  pl.BlockSpec((c, BA_DIM), lambda b, s: (b * nc + jnp.minimum(s, nc - 1), 0)),
                  _resident((CONV_WIDTH, CONV_DIM)), _resident((1, BA_DIM)), _resident((1, BA_DIM)),
                  _resident((1, GDN_DV)), _resident(lw["ones"].shape), _resident(lw["sel"].shape)],
        out_specs=[pl.BlockSpec((c, GDN_VAL_DIM), lambda b, s: (b * nc + jnp.maximum(s - 1, 0), 0)),
                   pl.BlockSpec((1, GDN_HEADS, GDN_DK, GDN_DV), lambda b, s: (b, 0, 0, 0))],
        out_shape=[jax.ShapeDtypeStruct((batch * seq, GDN_VAL_DIM), F32),
                   jax.ShapeDtypeStruct((batch, GDN_HEADS, GDN_DK, GDN_DV), F32)],
        scratch_shapes=[pltpu.VMEM((GDN_HEADS, GDN_DK, GDN_DV), F32),
                        pltpu.VMEM((c + 8, CONV_DIM), F32),
                        pltpu.VMEM((c, CONV_DIM), F32), pltpu.VMEM((c, BA_DIM), F32), pltpu.VMEM((BA_DIM, c), F32),
                        pltpu.VMEM((c, CONV_DIM), F32), pltpu.VMEM((c, BA_DIM), F32), pltpu.VMEM((BA_DIM, c), F32)],
        compiler_params=pltpu.CompilerParams(dimension_semantics=("arbitrary", "arbitrary"),
                                             vmem_limit_bytes=VMEM_LIMIT),
        name="gdn_prompt",
    )(qkvz, qkvz, ba, lw["conv_w"], lw["alog_row"], lw["dtb_row"], lw["gnorm"], lw["ones"], lw["sel"])


def _gdn_decode_kernel(qkvz_ref, ba_ref, sconv_ref, s0_ref, cw_ref, alog_ref, dtb_ref, gnorm_ref,
                       ones_ref, sel_ref, o_ref, snew_ref, xbuf_ref, *, dec_seq):
    nb = DEC_SEQ_BLOCK
    t = dec_seq
    c = nb * t
    ys = []
    for s in range(nb):
        xbuf_ref[s, 5:8, :] = sconv_ref[s]
        xbuf_ref[s, 8:8 + t, :] = qkvz_ref[s * t:(s + 1) * t, :CONV_DIM]
        ys.append(_conv_silu(xbuf_ref.at[s], cw_ref, t))
    y = jnp.concatenate(ys, axis=0)

    beta, g = _gates(ba_ref[...], alog_ref[...], dtb_ref[...])
    ii, jj = _iota2(c)
    same = (ii // t) == (jj // t)
    tril = jnp.where(jnp.logical_and(same, ii >= jj), 1.0, 0.0).astype(F32)
    ones = jnp.where(same, 1.0, 0.0).astype(F32)
    gc = _cumsum_rows(tril, g)
    gl = _cumsum_rows(ones, g)
    gc_t = gc.T
    lane = lax.broadcasted_iota(jnp.int32, (c, BA_DIM), 1)
    bcast = _lane_bcast(jnp.where(lane < GDN_HEADS, beta, gc), sel_ref, range(2 * GDN_HEADS))
    beta_b, gc_b = bcast[:GDN_HEADS], bcast[GDN_HEADS:]
    gl_b = _lane_bcast(gl, sel_ref, range(GDN_HEADS, 2 * GDN_HEADS))
    wy = {}
    for _ in _gdn_prep_steps(wy, y, beta_b, gc_b, gl_b, gc_t, c, t, ones_ref, normalized=False):
        pass
    u, w, qk, qd, kd = (wy[name][0] for name in ("u", "w", "qk", "qd", "kd"))
    hs = range(GDN_HEADS)
    units = [(s, h) for h in hs for s in range(nb)]
    s_old = {(s, h): s0_ref[s, h] for s, h in units}
    ws = {(s, h): _dot(jnp.concatenate([w[h][s * t:(s + 1) * t], qd[h][s * t:(s + 1) * t]], axis=0),
                       s_old[s, h]) for s, h in units}
    v_new = [jnp.concatenate([u[h][s * t:(s + 1) * t] - ws[s, h][:t] for s in range(nb)], axis=0) for h in hs]
    o = [jnp.concatenate([ws[s, h][t:] for s in range(nb)], axis=0) + _dot(qk[h], v_new[h]) for h in hs]
    cd = [jnp.exp(gl_b[h]) for h in hs]
    kd_t = [kd[h].T for h in hs]
    col_seq = lax.broadcasted_iota(jnp.int32, (GDN_DK, c), 1) // t
    for s, h in units:
        upd = _dot(jnp.where(col_seq == s, kd_t[h], 0.0), v_new[h])
        snew_ref[s, h] = s_old[s, h] * cd[h][s * t:s * t + 1, :] + upd
    for h in hs:
        z = qkvz_ref[:, CONV_DIM + h * GDN_DV:CONV_DIM + (h + 1) * GDN_DV]
        o_ref[:, h * GDN_DV:(h + 1) * GDN_DV] = _gdn_out(o[h], z, gnorm_ref[...], ones_ref)


def _gdn_decode_call(qkvz, ba, state_conv, state_gdn, lw, dec_batch, dec_seq):
    nb = DEC_SEQ_BLOCK
    rows = nb * dec_seq
    return pl.pallas_call(
        functools.partial(_gdn_decode_kernel, dec_seq=dec_seq),
        grid=(dec_batch // nb,),
        in_specs=[pl.BlockSpec((rows, QKVZ_DIM), lambda i: (i, 0)),
                  pl.BlockSpec((rows, BA_DIM), lambda i: (i, 0)),
                  pl.BlockSpec((nb, CONV_WIDTH - 1, CONV_DIM), lambda i: (i, 0, 0)),
                  pl.BlockSpec((nb, GDN_HEADS, GDN_DK, GDN_DV), lambda i: (i, 0, 0, 0)),
                  _resident((CONV_WIDTH, CONV_DIM)), _resident((1, BA_DIM)), _resident((1, BA_DIM)),
                  _resident((1, GDN_DV)), _resident(lw["ones"].shape), _resident(lw["sel"].shape)],
        out_specs=[pl.BlockSpec((rows, GDN_VAL_DIM), lambda i: (i, 0)),
                   pl.BlockSpec((nb, GDN_HEADS, GDN_DK, GDN_DV), lambda i: (i, 0, 0, 0))],
        out_shape=[jax.ShapeDtypeStruct((dec_batch * dec_seq, GDN_VAL_DIM), F32),
                   jax.ShapeDtypeStruct((dec_batch, GDN_HEADS, GDN_DK, GDN_DV), F32)],
        scratch_shapes=[pltpu.VMEM((nb, 16, CONV_DIM), F32)],
        compiler_params=pltpu.CompilerParams(dimension_semantics=("arbitrary",),
                                             vmem_limit_bytes=VMEM_LIMIT),
        name="gdn_decode",
    )(qkvz, ba, state_conv, state_gdn, lw["conv_w"], lw["alog_row"], lw["dtb_row"], lw["gnorm"],
      lw["ones"], lw["sel"])


def _bias_table_kernel(bucket_ref, rb_ref, out_ref):
    bucket = bucket_ref[...]
    for h in range(SWA_HEADS):
        acc = jnp.zeros(bucket.shape, F32)
        for b in range(NUM_BUCKETS):
            acc = jnp.where(bucket == b, rb_ref[b, h], acc)
        out_ref[h] = jnp.where(bucket < 0, NEG_INF, acc)


def _bias_table_call(bucket, rel_bias):
    r, c = bucket.shape
    return pl.pallas_call(
        _bias_table_kernel,
        in_specs=[pl.BlockSpec(memory_space=pltpu.VMEM), pl.BlockSpec(memory_space=pltpu.SMEM)],
        out_specs=pl.BlockSpec(memory_space=pltpu.VMEM),
        out_shape=jax.ShapeDtypeStruct((SWA_HEADS, r, c), F32),
        name="t5_bias_table",
    )(bucket, rel_bias)


def _t5_bucket(dist, valid):
    d = jnp.maximum(dist, 0)
    exact = NUM_BUCKETS // 2
    log_ratio = jnp.log(jnp.maximum(d, 1).astype(F32) / exact) / math.log(REL_MAX_DISTANCE / exact)
    large = jnp.minimum(exact + (log_ratio * (NUM_BUCKETS - exact)).astype(jnp.int32), NUM_BUCKETS - 1)
    bucket = jnp.where(d < exact, d, large)
    return jnp.where(valid, bucket, -1).astype(jnp.int32)


def _swa_prompt_kernel(q_ref, kvp_ref, kvc_ref, bias_ref, sink_ref, o_ref):
    w, dh, nq = SWA_WINDOW, SWA_HEAD_DIM, SWA_STEP_BLOCKS
    first = pl.program_id(1) == 0
    row = lax.broadcasted_iota(jnp.int32, (2 * w, w), 0)
    drop_prev = jnp.logical_and(first, row < w)
    kv = jnp.concatenate([kvp_ref[...], kvc_ref[...]], axis=0)
    k_all = kv[:, :SWA_KV_DIM].astype(BF16)
    v_t = kv[:, SWA_KV_DIM:].T.astype(BF16)
    zeros = jnp.zeros((dh, w), BF16)
    units = [(b, hg) for b in range(nq) for hg in range(SWA_HEADS)]
    n = range(len(units))
    q_t = [(q_ref[b * w:(b + 1) * w, :] * (dh ** -0.5)).T.astype(BF16) for b in range(nq)]

    def q_rhs(b, hg):
        qt = q_t[b][hg * dh:(hg + 1) * dh, :]
        return jnp.concatenate([qt, zeros] if hg // SWA_GROUP == 0 else [zeros, qt], axis=0)

    s = [jnp.dot(k_all[b * w:(b + 2) * w, :], q_rhs(b, hg), preferred_element_type=F32) + bias_ref[hg]
         for b, hg in units]
    s = [jnp.where(drop_prev, NEG_INF, s[i]) if b == 0 else s[i] for i, (b, hg) in enumerate(units)]
    sink = [sink_ref[hg] for b, hg in units]
    m = [jnp.maximum(jnp.max(s[i], axis=0, keepdims=True), sink[i]) for i in n]
    p = [jnp.exp(s[i] - m[i]) for i in n]
    denom = [jnp.sum(p[i], axis=0, keepdims=True) + jnp.exp(sink[i] - m[i]) for i in n]
    o_t = [jnp.dot(v_t[(hg // SWA_GROUP) * dh:(hg // SWA_GROUP + 1) * dh, b * w:(b + 2) * w], p[i].astype(BF16),
                   preferred_element_type=F32) / denom[i] for i, (b, hg) in enumerate(units)]
    for b in range(nq):
        for pair in range(SWA_HEADS // 2):
            i = b * SWA_HEADS + 2 * pair
            o_ref[b * w:(b + 1) * w, pair * 2 * dh:(pair + 1) * 2 * dh] = jnp.concatenate([o_t[i], o_t[i + 1]], axis=0).T


def _swa_prompt_call(qs, kv, bias, sinks, batch, seq):
    w, nq = SWA_WINDOW, SWA_STEP_BLOCKS
    nb = seq // (w * nq)
    return pl.pallas_call(
        _swa_prompt_kernel,
        grid=(batch, nb),
        in_specs=[pl.BlockSpec((nq * w, SWA_Q_DIM), lambda b, n: (b * nb + n, 0)),
                  pl.BlockSpec((w, 2 * SWA_KV_DIM),
                               lambda b, n: ((b * nb + n) * nq - jnp.minimum(n, 1), 0)),
                  pl.BlockSpec((nq * w, 2 * SWA_KV_DIM), lambda b, n: (b * nb + n, 0)),
                  _resident((SWA_HEADS, 2 * w, w)),
                  pl.BlockSpec(memory_space=pltpu.SMEM)],
        out_specs=pl.BlockSpec((nq * w, SWA_Q_DIM), lambda b, n: (b * nb + n, 0)),
        out_shape=jax.ShapeDtypeStruct((batch * seq, SWA_Q_DIM), F32),
        compiler_params=pltpu.CompilerParams(dimension_semantics=("arbitrary", "arbitrary"),
                                             vmem_limit_bytes=VMEM_LIMIT),
        name="swa_prompt",
    )(qs, kv, kv, bias, sinks)


def _swa_decode_kernel(q_ref, kvn_ref, kt_ref, vt_ref, bbuf_ref, bnew_ref, sink_ref,
                       o_ref, kto_ref, vto_ref, *, dec_seq):
    nb = DEC_SEQ_BLOCK
    t = dec_seq
    dh = SWA_HEAD_DIM
    rows = nb * t
    m_rows = SWA_GROUP * rows
    wb = kt_ref.shape[3]
    ri = lax.broadcasted_iota(jnp.int32, (m_rows, 1), 0)
    row_seq = (ri % rows) // t
    row_grp = ri // rows
    for h in range(SWA_KV_HEADS):
        lo, hi = h * dh, (h + 1) * dh
        q = jnp.concatenate([q_ref[:, (h * SWA_GROUP + g) * dh:(h * SWA_GROUP + g + 1) * dh]
                             for g in range(SWA_GROUP)], axis=0)
        sink = jnp.zeros((m_rows, 1), F32)
        for g in range(SWA_GROUP):
            sink = jnp.where(row_grp == g, sink_ref[h * SWA_GROUP + g], sink)
        k_new = jnp.concatenate([kvn_ref[:, lo:hi], jnp.zeros((m_rows - rows, dh), F32)], axis=0)
        v_new = jnp.concatenate([kvn_ref[:, SWA_KV_DIM + lo:SWA_KV_DIM + hi],
                                 jnp.zeros((m_rows - rows, dh), F32)], axis=0)
        s_new = _dot_nt(q, k_new) * (dh ** -0.5) + bnew_ref[h]
        s_buf = jnp.zeros((m_rows, wb), F32)
        for s in range(nb):
            s_buf = jnp.where(row_seq == s, _dot(q, kt_ref[s, h]), s_buf)
        s_buf = s_buf * (dh ** -0.5) + bbuf_ref[h]
        m = jnp.maximum(jnp.maximum(jnp.max(s_buf, axis=-1, keepdims=True),
                                    jnp.max(s_new, axis=-1, keepdims=True)), sink)
        p_buf = jnp.exp(s_buf - m)
        p_new = jnp.exp(s_new - m)
        denom = (jnp.sum(p_buf, axis=-1, keepdims=True) + jnp.sum(p_new, axis=-1, keepdims=True)
                 + jnp.exp(sink - m))
        o = _dot(p_new, v_new)
        for s in range(nb):
            o = o + _dot_nt(jnp.where(row_seq == s, p_buf, 0.0), vt_ref[s, h])
        o = o / denom
        for g in range(SWA_GROUP):
            hg = h * SWA_GROUP + g
            o_ref[:, hg * dh:(hg + 1) * dh] = o[g * rows:(g + 1) * rows]
    new_t = jnp.concatenate([kvn_ref[...], jnp.zeros((wb - rows, 2 * SWA_KV_DIM), F32)], axis=0).T
    lane = lax.broadcasted_iota(jnp.int32, (dh, wb), 1)
    for part, (src_ref, dst_ref) in enumerate(((kt_ref, kto_ref), (vt_ref, vto_ref))):
        for h in range(SWA_KV_HEADS):
            fresh = new_t[part * SWA_KV_DIM + h * dh:part * SWA_KV_DIM + (h + 1) * dh, :]
            for s in range(nb):
                kept = pltpu.roll(src_ref[s, h], wb - t, axis=1)
                tail = pltpu.roll(fresh, (wb - t - s * t) % wb, axis=1)
                dst_ref[s, h] = jnp.where(lane >= wb - t, tail, kept)


def _swa_decode_call(qs, kvn, kt, vt, bias_buf, bias_new, sinks, dec_batch, dec_seq):
    nb = DEC_SEQ_BLOCK
    rows = nb * dec_seq
    wb = kt.shape[3]
    m_rows = SWA_GROUP * rows
    assert rows <= wb and wb == LANES
    win_spec = pl.BlockSpec((nb, SWA_KV_HEADS, SWA_HEAD_DIM, wb), lambda i: (i, 0, 0, 0))
    return pl.pallas_call(
        functools.partial(_swa_decode_kernel, dec_seq=dec_seq),
        grid=(dec_batch // nb,),
        in_specs=[pl.BlockSpec((rows, SWA_Q_DIM), lambda i: (i, 0)),
                  pl.BlockSpec((rows, 2 * SWA_KV_DIM), lambda i: (i, 0)),
                  win_spec, win_spec,
                  _resident((SWA_KV_HEADS, m_rows, wb)),
                  _resident((SWA_KV_HEADS, m_rows, m_rows)),
                  pl.BlockSpec(memory_space=pltpu.SMEM)],
        out_specs=[pl.BlockSpec((rows, SWA_Q_DIM), lambda i: (i, 0)), win_spec, win_spec],
        out_shape=[jax.ShapeDtypeStruct((dec_batch * dec_seq, SWA_Q_DIM), F32),
                   jax.ShapeDtypeStruct(kt.shape, F32), jax.ShapeDtypeStruct(vt.shape, F32)],
        compiler_params=pltpu.CompilerParams(dimension_semantics=("arbitrary",),
                                             vmem_limit_bytes=VMEM_LIMIT),
        name="swa_decode",
    )(qs, kvn, kt, vt, bias_buf, bias_new, sinks)


def _prompt_bias(rel_bias):
    w = SWA_WINDOW
    dist = w + jnp.arange(w)[None, :] - jnp.arange(2 * w)[:, None]
    valid = (dist >= 0) & (dist < w)
    return _bias_table_call(_t5_bucket(dist, valid), rel_bias)


def _decode_bias(rel_bias, wb, dec_seq):
    nb, t = DEC_SEQ_BLOCK, dec_seq
    rows = nb * t
    m_rows = SWA_GROUP * rows
    tok = jnp.arange(t)
    dist = jnp.concatenate([wb + tok[:, None] - jnp.arange(wb)[None, :], tok[:, None] - tok[None, :]], axis=1)
    bucket = _t5_bucket(dist, (dist >= 0) & (dist < SWA_WINDOW))
    pad_r, pad_c = -t % 8, -(wb + t) % LANES
    bucket = jnp.pad(bucket, ((0, pad_r), (0, pad_c)), constant_values=-1)
    tab = _bias_table_call(bucket, rel_bias)[:, :t, :wb + t]
    tab = tab.reshape(SWA_KV_HEADS, SWA_GROUP, 1, t, wb + t)
    per_row = jnp.broadcast_to(tab, (SWA_KV_HEADS, SWA_GROUP, nb, t, wb + t)).reshape(SWA_KV_HEADS, m_rows, wb + t)
    r_seq = (jnp.arange(m_rows) % rows) // t
    cidx = jnp.arange(m_rows)
    own = (r_seq[:, None] == (cidx // t)[None, :]) & (cidx < rows)[None, :]
    bias_new = jnp.where(own[None], jnp.tile(per_row[:, :, wb:], (1, 1, m_rows // t)), NEG_INF)
    return per_row[:, :, :wb], bias_new


def _cast_kernel(*refs):
    n = len(refs) // 2
    for src_ref, dst_ref in zip(refs[:n], refs[n:]):
        dst_ref[...] = src_ref[...].astype(dst_ref.dtype)


def _to_bf16(ws):
    r, c = ws[0].shape
    tr = r
    while tr * c * 4 > CAST_BLOCK_BYTES and tr % 16 == 0:
        tr //= 2
    spec = pl.BlockSpec((tr, c), lambda i: (i, 0))
    return pl.pallas_call(
        _cast_kernel,
        grid=(r // tr,),
        in_specs=[spec] * len(ws),
        out_specs=[spec] * len(ws),
        out_shape=[jax.ShapeDtypeStruct((r, c), BF16)] * len(ws),
        compiler_params=pltpu.CompilerParams(dimension_semantics=("arbitrary",)),
        name="weights_to_bf16",
    )(*ws)


def _pack_layer(i, norm_ffn1_pre, norm_ffn1_post, ffn1_w_gate, ffn1_w_up, ffn1_w_down, norm_mix_pre,
                norm_mix_post, w_in, conv_w, gdn_a_log, gdn_dt_bias, gdn_norm, swa_sinks, w_out,
                norm_ffn2_pre, norm_ffn2_post, ffn2_w_gate, ffn2_w_up, ffn2_w_down, ple_gate, ple_proj,
                norm_ple_post):
    row = lambda g: g[i].reshape(1, -1).astype(F32)
    wt = jnp.transpose(w_in[i])
    n_gdn = QKVZ_DIM + 2 * GDN_HEADS
    win = jnp.concatenate([wt[:n_gdn], jnp.zeros((BA_DIM - 2 * GDN_HEADS, D_MODEL), wt.dtype), wt[n_gdn:]],
                          axis=0).astype(BF16)
    lane_pad = lambda v: jnp.zeros((1, BA_DIM), F32).at[0, GDN_HEADS:2 * GDN_HEADS].set(v[i].astype(F32))
    kk = np.arange(2 * LANES)[:, None] % LANES
    sel = (kk == (np.arange(2 * GDN_HEADS * LANES)[None, :] // LANES)).astype(np.float32)
    wg1, wu1, wg2, wu2 = _to_bf16([ffn1_w_gate[i], ffn1_w_up[i], ffn2_w_gate[i], ffn2_w_up[i]])
    wd1, wd2 = _to_bf16([ffn1_w_down[i], ffn2_w_down[i]])
    wo, wpg = _to_bf16([w_out[i], ple_gate[i]])
    return dict(
        ones=jnp.ones((LANES, LANES), F32), sel=jnp.asarray(sel, BF16),
        g1pre=row(norm_ffn1_pre), g1post=row(norm_ffn1_post), wg1=wg1, wu1=wu1, wd1=wd1,
        gmix=row(norm_mix_pre), gmixpost=row(norm_mix_post), win=win,
        conv_w=conv_w[i].astype(F32), alog_row=lane_pad(gdn_a_log), dtb_row=lane_pad(gdn_dt_bias),
        gnorm=row(gdn_norm), sinks=swa_sinks[i].astype(F32), wo=wo,
        g2pre=row(norm_ffn2_pre), g2post=row(norm_ffn2_post), wg2=wg2, wu2=wu2, wd2=wd2,
        wpg=wpg, wpp=ple_proj[i].astype(BF16), gple=row(norm_ple_post))


def kernel(x_prompt, x_sample, state_conv, state_gdn, cache_swa_k, cache_swa_v, p_prompt, p_sample,
           rel_bias, norm_ffn1_pre, norm_ffn1_post, ffn1_w_gate, ffn1_w_up, ffn1_w_down,
           norm_mix_pre, norm_mix_post, w_in, conv_w, gdn_a_log, gdn_dt_bias, gdn_norm, swa_sinks,
           w_out, norm_ffn2_pre, norm_ffn2_post, ffn2_w_gate, ffn2_w_up, ffn2_w_down,
           ple_gate, ple_proj, norm_ple_post):
    depth = w_in.shape[0]
    batch, seq, _ = x_prompt.shape
    dec_batch, dec_seq, _ = x_sample.shape
    wb = cache_swa_k.shape[2]
    wp = min(SWA_WINDOW, seq)
    rel_bias = rel_bias.astype(F32)
    bias_p = _prompt_bias(rel_bias)
    bias_db, bias_dn = _decode_bias(rel_bias, wb, dec_seq)

    yp = x_prompt.reshape(batch * seq, D_MODEL)
    ys = x_sample.reshape(dec_batch * dec_seq, D_MODEL)
    outs = [[] for _ in range(8)]
    for i in range(depth):
        lw = _pack_layer(i, norm_ffn1_pre, norm_ffn1_post, ffn1_w_gate, ffn1_w_up, ffn1_w_down,
                         norm_mix_pre, norm_mix_post, w_in, conv_w, gdn_a_log, gdn_dt_bias, gdn_norm,
                         swa_sinks, w_out, norm_ffn2_pre, norm_ffn2_post, ffn2_w_gate, ffn2_w_up,
                         ffn2_w_down, ple_gate, ple_proj, norm_ple_post)
        x1, qkvz, ba, qs, kv = _head_call(yp, lw)
        gdn_o, s_fin = _gdn_prompt_call(qkvz, ba, lw, batch, seq)
        swa_o = _swa_prompt_call(qs, kv, bias_p, lw["sinks"], batch, seq)
        yp = _tail_call(x1, gdn_o, swa_o, p_prompt[i].reshape(batch * seq, PLE_DIM), lw)
        kv3 = kv.reshape(batch, seq, 2 * SWA_KV_DIM)
        outs[0].append(qkvz.reshape(batch, seq, QKVZ_DIM)[:, seq - (CONV_WIDTH - 1):, :CONV_DIM])
        outs[1].append(s_fin)
        outs[2].append(kv3[:, seq - wp:, :SWA_KV_DIM].reshape(batch, wp, SWA_KV_HEADS, SWA_HEAD_DIM))
        outs[3].append(kv3[:, seq - wp:, SWA_KV_DIM:].reshape(batch, wp, SWA_KV_HEADS, SWA_HEAD_DIM))
        x1, qkvz, ba, qs, kv = _head_call(ys, lw)
        gdn_o, s_new = _gdn_decode_call(qkvz, ba, state_conv[i], state_gdn[i], lw, dec_batch, dec_seq)
        kt = jnp.transpose(cache_swa_k[i], (0, 2, 3, 1))
        vt = jnp.transpose(cache_swa_v[i], (0, 2, 3, 1))
        swa_o, kt_new, vt_new = _swa_decode_call(qs, kv, kt, vt, bias_db, bias_dn, lw["sinks"], dec_batch, dec_seq)
        ys = _tail_call(x1, gdn_o, swa_o, p_sample[i].reshape(dec_batch * dec_seq, PLE_DIM), lw)
        xp = jnp.concatenate([state_conv[i], qkvz.reshape(dec_batch, dec_seq, QKVZ_DIM)[:, :, :CONV_DIM]], axis=1)
        outs[4].append(xp[:, dec_seq:])
        outs[5].append(s_new)
        outs[6].append(jnp.transpose(kt_new, (0, 3, 1, 2)))
        outs[7].append(jnp.transpose(vt_new, (0, 3, 1, 2)))
    return (yp.reshape(batch, seq, D_MODEL), ys.reshape(dec_batch, dec_seq, D_MODEL),
            *[jnp.stack(o) for o in outs])
```

```python
import functools
import math

import numpy as np
import jax
import jax.numpy as jnp
from jax import lax
from jax.experimental import pallas as pl
from jax.experimental.pallas import tpu as pltpu

F32 = jnp.float32
BF16 = jnp.bfloat16

D_MODEL = 1024
NORM_EPS = 1e-6
PLE_DIM = 256
FFN_DIM = 2816
GDN_HEADS = 4
GDN_DK = 128
GDN_DV = 128
GDN_KEY_DIM = GDN_HEADS * GDN_DK
GDN_VAL_DIM = GDN_HEADS * GDN_DV
CONV_DIM = 2 * GDN_KEY_DIM + GDN_VAL_DIM
CONV_WIDTH = 4
SWA_HEADS = 8
SWA_KV_HEADS = 2
SWA_GROUP = SWA_HEADS // SWA_KV_HEADS
SWA_HEAD_DIM = 64
SWA_Q_DIM = SWA_HEADS * SWA_HEAD_DIM
SWA_KV_DIM = SWA_KV_HEADS * SWA_HEAD_DIM
SWA_WINDOW = 128
NUM_BUCKETS = 32
REL_MAX_DISTANCE = 128

QKVZ_DIM = CONV_DIM + GDN_VAL_DIM
BA_DIM = 128
PROJ_PACKED = QKVZ_DIM + BA_DIM + SWA_Q_DIM + 2 * SWA_KV_DIM

GDN_CHUNK = 128
GDN_STEP_CHUNKS = 4
SWA_STEP_BLOCKS = 8
ROW_TILE = 512
DEC_SEQ_BLOCK = 16
CAST_BLOCK_BYTES = 2 * 1024 * 1024
VMEM_LIMIT = 56 * 1024 * 1024

NEG_INF = float("-inf")
LANES = 128
SUBLANES = 8
CONV_LEAD = SUBLANES - (CONV_WIDTH - 1)
assert CONV_WIDTH == 4 and GDN_DK == LANES and GDN_DV == LANES


def _resident(shape):
    nd = len(shape)
    return pl.BlockSpec(shape, lambda *_: (0,) * nd, pipeline_mode=pl.Buffered(1))


def _rms(x, gain):
    ms = jnp.mean(x * x, axis=-1, keepdims=True)
    return (x * lax.rsqrt(ms + NORM_EPS)) * gain


def _sigmoid(x):
    return 1.0 / (1.0 + jnp.exp(-x))


def _silu(x):
    h = 0.5 * x
    return h + h * jnp.tanh(h)


def _dot(a, b):
    return jnp.dot(a, b, preferred_element_type=F32)


def _dot_nt(a, b):
    return lax.dot_general(a, b, (((1,), (1,)), ((), ())), preferred_element_type=F32)


def _dot_tn(a, b):
    return lax.dot_general(a, b, (((0,), (0,)), ((), ())), preferred_element_type=F32)


def _split(a):
    hi = a.astype(BF16)
    lo = (a - hi.astype(F32)).astype(BF16)
    return hi, lo


def _inv_rms(x):
    return lax.rsqrt(jnp.mean(x * x, axis=-1, keepdims=True) + NORM_EPS)


def _prenorm_dots(x, gain, ws, transposed=False):
    h = (x * gain).astype(BF16)
    inv = _inv_rms(x)
    dims = (((1,), (1 if transposed else 0,)), ((), ()))
    return [lax.dot_general(h, w, dims, preferred_element_type=F32) * inv for w in ws]


def _swiglu_block(x, gain, wg_ref, wu_ref, wd_ref):
    g, u = _prenorm_dots(x, gain, [wg_ref[...], wu_ref[...]])
    a = (_silu(g) * u).astype(BF16)
    return jnp.dot(a, wd_ref[...], preferred_element_type=F32)


def _head_kernel(x_ref, g1pre_ref, g1post_ref, wg_ref, wu_ref, wd_ref, gmix_ref, win_ref,
                 x1_ref, qkvz_ref, ba_ref, qs_ref, kv_ref):
    x = x_ref[...]
    y = _swiglu_block(x, g1pre_ref[...], wg_ref, wu_ref, wd_ref)
    x1 = x + 0.5 * _rms(y, g1post_ref[...])
    x1_ref[...] = x1
    c0, c1, c2 = QKVZ_DIM, QKVZ_DIM + BA_DIM, QKVZ_DIM + BA_DIM + SWA_Q_DIM
    qkvz_ref[...], ba_ref[...], qs_ref[...], kv_ref[...] = _prenorm_dots(
        x1, gmix_ref[...], [win_ref[:c0, :], win_ref[c0:c1, :], win_ref[c1:c2, :], win_ref[c2:, :]], transposed=True)


def _head_call(x, lw):
    n = x.shape[0]
    tm = min(ROW_TILE, n)
    row = lambda w: pl.BlockSpec((tm, w), lambda i: (i, 0))
    return pl.pallas_call(
        _head_kernel,
        grid=(n // tm,),
        in_specs=[row(D_MODEL), _resident((1, D_MODEL)), _resident((1, D_MODEL)),
                  _resident((D_MODEL, FFN_DIM)), _resident((D_MODEL, FFN_DIM)),
                  _resident((FFN_DIM, D_MODEL)), _resident((1, D_MODEL)),
                  _resident((PROJ_PACKED, D_MODEL))],
        out_specs=[row(D_MODEL), row(QKVZ_DIM), row(BA_DIM), row(SWA_Q_DIM), row(2 * SWA_KV_DIM)],
        out_shape=[jax.ShapeDtypeStruct((n, D_MODEL), F32), jax.ShapeDtypeStruct((n, QKVZ_DIM), F32),
                   jax.ShapeDtypeStruct((n, BA_DIM), F32), jax.ShapeDtypeStruct((n, SWA_Q_DIM), F32),
                   jax.ShapeDtypeStruct((n, 2 * SWA_KV_DIM), F32)],
        compiler_params=pltpu.CompilerParams(dimension_semantics=("arbitrary",),
                                             vmem_limit_bytes=VMEM_LIMIT),
        name="ffn1_inproj",
    )(x, lw["g1pre"], lw["g1post"], lw["wg1"], lw["wu1"], lw["wd1"], lw["gmix"], lw["win"])


def _tail_kernel(x_ref, gdn_ref, swa_ref, p_ref, wo_ref, gmixpost_ref, g2pre_ref, g2post_ref,
                 wg_ref, wu_ref, wd_ref, wpg_ref, wpp_ref, gple_ref, y_ref):
    x = x_ref[...]
    mix = (jnp.dot(gdn_ref[...].astype(BF16), wo_ref[:GDN_VAL_DIM, :], preferred_element_type=F32)
           + jnp.dot(swa_ref[...].astype(BF16), wo_ref[GDN_VAL_DIM:, :], preferred_element_type=F32))
    pp = jnp.dot(p_ref[...].astype(BF16), wpp_ref[...], preferred_element_type=F32)
    x = x + _rms(mix, gmixpost_ref[...])
    y = _swiglu_block(x, g2pre_ref[...], wg_ref, wu_ref, wd_ref)
    x = x + 0.5 * _rms(y, g2post_ref[...])
    gate = _sigmoid(jnp.dot(x.astype(BF16), wpg_ref[...], preferred_element_type=F32))
    y_ref[...] = x + _rms(gate * pp, gple_ref[...])


def _tail_call(x1, gdn_o, swa_o, p, lw):
    n = x1.shape[0]
    tm = min(ROW_TILE, n)
    row = lambda w: pl.BlockSpec((tm, w), lambda i: (i, 0))
    return pl.pallas_call(
        _tail_kernel,
        grid=(n // tm,),
        in_specs=[row(D_MODEL), row(GDN_VAL_DIM), row(SWA_Q_DIM), row(PLE_DIM),
                  _resident((GDN_VAL_DIM + SWA_Q_DIM, D_MODEL)), _resident((1, D_MODEL)),
                  _resident((1, D_MODEL)), _resident((1, D_MODEL)),
                  _resident((D_MODEL, FFN_DIM)), _resident((D_MODEL, FFN_DIM)),
                  _resident((FFN_DIM, D_MODEL)), _resident((D_MODEL, D_MODEL)),
                  _resident((PLE_DIM, D_MODEL)), _resident((1, D_MODEL))],
        out_specs=row(D_MODEL),
        out_shape=jax.ShapeDtypeStruct((n, D_MODEL), F32),
        compiler_params=pltpu.CompilerParams(dimension_semantics=("arbitrary",),
                                             vmem_limit_bytes=VMEM_LIMIT),
        name="outproj_ffn2_ple",
    )(x1, gdn_o, swa_o, p, lw["wo"], lw["gmixpost"], lw["g2pre"], lw["g2post"],
      lw["wg2"], lw["wu2"], lw["wd2"], lw["wpg"], lw["wpp"], lw["gple"])


def _iota2(c):
    return (lax.broadcasted_iota(jnp.int32, (c, c), 0), lax.broadcasted_iota(jnp.int32, (c, c), 1))


def _gates(ba, alog_row, dtb_row):
    beta = _sigmoid(ba)
    xa = ba + dtb_row
    softplus = jnp.maximum(xa, 0.0) + jnp.log1p(jnp.exp(-jnp.abs(xa)))
    g = -jnp.exp(alog_row) * softplus
    return beta, g


def _cumsum_rows(mask01, g):
    hi, lo = _split(g)
    m = mask01.astype(BF16)
    return jnp.dot(m, hi, preferred_element_type=F32) + jnp.dot(m, lo, preferred_element_type=F32)


def _hi_lo_lanes(x):
    hi, lo = _split(x)
    return jnp.concatenate([hi, lo], axis=1)


def _rowsum_bcast(x, ones_ref):
    return jnp.dot(x, ones_ref[...], preferred_element_type=F32)


def _lane_bcast(x, sel_ref, lanes):
    out = jnp.dot(_hi_lo_lanes(x), sel_ref[...], preferred_element_type=F32)
    return [out[:, l * 128:(l + 1) * 128] for l in lanes]


def _gdn_prep_steps(out, y, beta_b, gc_b, gl_b, gc_t, chunk, group, ones_ref, normalized):
    c = chunk
    n_chunks = y.shape[0] // c
    units = [(ci, h) for ci in range(n_chunks) for h in range(GDN_HEADS)]
    n = range(len(units))
    ii, jj = _iota2(c)
    same = (ii // group) == (jj // group)
    incl = jnp.logical_and(same, ii >= jj)
    strict = jnp.logical_and(same, ii > jj)

    def rows(t, ci, lo, hi):
        return t[ci * c:(ci + 1) * c, lo:hi]

    eg_b = [jnp.exp(t) for t in gc_b]
    ekd_b = [jnp.exp(gl_b[h] - gc_b[h]) for h in range(GDN_HEADS)]
    beta_u = [rows(beta_b[h], ci, 0, GDN_DK) for ci, h in units]
    eg_u = [rows(eg_b[h], ci, 0, GDN_DK) for ci, h in units]
    ekd_u = [rows(ekd_b[h], ci, 0, GDN_DK) for ci, h in units]
    gc_col = [rows(gc_b[h], ci, 0, c) for ci, h in units]
    gc_row = [gc_t[GDN_HEADS + h:GDN_HEADS + h + 1, ci * c:(ci + 1) * c] for ci, h in units]
    q = [rows(y, ci, h * GDN_DK, (h + 1) * GDN_DK) for ci, h in units]
    k = [rows(y, ci, GDN_KEY_DIM + h * GDN_DK, GDN_KEY_DIM + (h + 1) * GDN_DK) for ci, h in units]
    v = [rows(y, ci, 2 * GDN_KEY_DIM + h * GDN_DV, 2 * GDN_KEY_DIM + (h + 1) * GDN_DV) for ci, h in units]
    if not normalized:
        q = [_l2norm(t, ones_ref) * (GDN_DK ** -0.5) for t in q]
        k = [_l2norm(t, ones_ref) for t in k]
    decay = [jnp.exp(jnp.where(incl, gc_col[i] - gc_row[i], NEG_INF)) for i in n]
    kb = [k[i] * beta_u[i] for i in n]
    kq = [_dot_nt(jnp.concatenate([kb[i], q[i]], axis=0), k[i]) for i in n]
    yield
    a_mat = [jnp.where(strict, kq[i][:c] * decay[i], 0.0) for i in n]
    qk = [kq[i][c:] * decay[i] for i in n]
    eye = jnp.where(ii == jj, 1.0, 0.0).astype(F32)
    t_mat = [eye for _ in n]
    b = 1
    while b < group:
        lower = jnp.logical_and((ii // (2 * b)) == (jj // (2 * b)),
                                jnp.logical_and((ii % (2 * b)) >= b, (jj % (2 * b)) < b))
        m = [jnp.where(lower, a_mat[i], 0.0) for i in n]
        if b == 1:
            t_mat = [eye - m[i] for i in n]
        else:
            tm = [_dot(t_mat[i], m[i]) for i in n]
            yield
            t_mat = [t_mat[i] - _dot(tm[i], t_mat[i]) for i in n]
            yield
        b *= 2
    rhs = [jnp.concatenate([v[i] * beta_u[i], kb[i] * eg_u[i]], axis=-1) for i in n]
    sol = [_dot(t_mat[i], rhs[i]) for i in n]
    yield
    nest = lambda xs: [xs[ci * GDN_HEADS:(ci + 1) * GDN_HEADS] for ci in range(n_chunks)]
    out.update(u=nest([s[:, :GDN_DV] for s in sol]), w=nest([s[:, GDN_DV:] for s in sol]), qk=nest(qk),
               qd=nest([q[i] * eg_u[i] for i in n]), kd=nest([k[i] * ekd_u[i] for i in n]))


def _l2norm(t, ones_ref):
    return t * lax.rsqrt(_rowsum_bcast(t * t, ones_ref) + 1e-6)


def _gdn_out(o, z, gnorm_row, ones_ref):
    ms = _rowsum_bcast(o * o, ones_ref) * (1.0 / GDN_DV)
    return (o * lax.rsqrt(ms + NORM_EPS)) * gnorm_row * _silu(z)


def _conv_silu(xbuf_ref, cw_ref, c):
    y = xbuf_ref[pl.ds(CONV_LEAD, c), :] * cw_ref[0:1, :]
    for j in range(1, CONV_WIDTH):
        y = y + xbuf_ref[pl.ds(CONV_LEAD + j, c), :] * cw_ref[j:j + 1, :]
    return _silu(y)


def _gdn_prompt_kernel(qkv_ref, z_ref, ba_ref, cw_ref, alog_ref, dtb_ref, gnorm_ref, ones_ref, sel_ref,
                       o_ref, sfin_ref, s_ref, xbuf_ref, y0_ref, g0_ref, gt0_ref, y1_ref, g1_ref, gt1_ref):
    step = pl.program_id(1)

    @pl.when(step == 0)
    def _():
        s_ref[...] = jnp.zeros_like(s_ref)
        xbuf_ref[0:SUBLANES, :] = jnp.zeros((SUBLANES, CONV_DIM), F32)
        y1_ref[...] = jnp.zeros_like(y1_ref)
        g1_ref[...] = jnp.zeros_like(g1_ref)
        gt1_ref[...] = jnp.zeros_like(gt1_ref)

    @pl.when(lax.rem(step, 2) == 0)
    def _():
        _gdn_prompt_step((y0_ref, g0_ref, gt0_ref), (y1_ref, g1_ref, gt1_ref), qkv_ref, z_ref, ba_ref, cw_ref,
                         alog_ref, dtb_ref, gnorm_ref, ones_ref, sel_ref, o_ref, sfin_ref, s_ref, xbuf_ref)

    @pl.when(lax.rem(step, 2) == 1)
    def _():
        _gdn_prompt_step((y1_ref, g1_ref, gt1_ref), (y0_ref, g0_ref, gt0_ref), qkv_ref, z_ref, ba_ref, cw_ref,
                         alog_ref, dtb_ref, gnorm_ref, ones_ref, sel_ref, o_ref, sfin_ref, s_ref, xbuf_ref)


def _gdn_prompt_step(a_refs, b_refs, qkv_ref, z_ref, ba_ref, cw_ref, alog_ref, dtb_ref, gnorm_ref, ones_ref,
                     sel_ref, o_ref, sfin_ref, s_ref, xbuf_ref):
    ya_ref, ga_ref, gta_ref = a_refs
    yb_ref, gb_ref, gtb_ref = b_refs
    c = GDN_CHUNK
    n_chunks = GDN_STEP_CHUNKS
    r = n_chunks * c
    hs = range(GDN_HEADS)

    def stage_a():
        xbuf_ref[SUBLANES:SUBLANES + r, :] = qkv_ref[...]
        beta, g = _gates(ba_ref[...], alog_ref[...], dtb_ref[...])
        ii, jj = _iota2(r)
        tril = jnp.where(jnp.logical_and((ii // c) == (jj // c), ii >= jj), 1.0, 0.0)
        gc = _cumsum_rows(tril, g)
        gta_ref[...] = gc.T
        lane = lax.broadcasted_iota(jnp.int32, (r, BA_DIM), 1)
        ga_ref[...] = jnp.where(lane < GDN_HEADS, beta, gc)
        yield
        for ci in range(n_chunks):
            rows = slice(ci * c, (ci + 1) * c)
            for slab in range(CONV_DIM // LANES):
                cols = slice(slab * LANES, (slab + 1) * LANES)
                ext = xbuf_ref[pl.ds(ci * c, c + SUBLANES), cols]
                s1 = pltpu.roll(ext, 1, axis=0)
                u2 = pltpu.roll(ext * cw_ref[1:2, cols] + s1 * cw_ref[0:1, cols], 2, axis=0)
                yc = (ext * cw_ref[3:4, cols] + s1 * cw_ref[2:3, cols] + u2)[SUBLANES:]
                ya_ref[rows, cols] = _silu(yc)
                if slab < GDN_KEY_DIM // LANES:
                    ya_ref[rows, cols] = _l2norm(ya_ref[rows, cols], ones_ref) * (GDN_DK ** -0.5)
                elif slab < 2 * GDN_KEY_DIM // LANES:
                    ya_ref[rows, cols] = _l2norm(ya_ref[rows, cols], ones_ref)
                yield
        xbuf_ref[0:SUBLANES, :] = xbuf_ref[r:r + SUBLANES, :]
        yield

    def stage_b():
        bcast = _lane_bcast(gb_ref[...], sel_ref, range(2 * GDN_HEADS))
        beta_b, gc_b = bcast[:GDN_HEADS], bcast[GDN_HEADS:]
        glast = [[gc_b[h][(ci + 1) * c - 1:(ci + 1) * c, :] for h in hs] for ci in range(n_chunks)]
        gl_b = [jnp.concatenate([jnp.broadcast_to(glast[ci][h], (c, GDN_DK)) for ci in range(n_chunks)],
                                axis=0) for h in hs]
        wy = {}
        yield from _gdn_prep_steps(wy, yb_ref, beta_b, gc_b, gl_b, gtb_ref[...], c, c, ones_ref,
                                   normalized=True)
        u, w, qk, qd, kd = (wy[name] for name in ("u", "w", "qk", "qd", "kd"))
        s_cur = [s_ref[h] for h in hs]
        for ci in range(n_chunks):
            ws = [_dot(jnp.concatenate([w[ci][h], qd[ci][h]], axis=0), s_cur[h]) for h in hs]
            yield
            v_new = [u[ci][h] - ws[h][:c] for h in hs]
            o = [ws[h][c:] + _dot(qk[ci][h], v_new[h]) for h in hs]
            s_cur = [s_cur[h] * jnp.exp(glast[ci][h]) + _dot_tn(kd[ci][h], v_new[h]) for h in hs]
            yield
            for h in hs:
                z = z_ref[ci * c:(ci + 1) * c, h * GDN_DV:(h + 1) * GDN_DV]
                o_ref[ci * c:(ci + 1) * c, h * GDN_DV:(h + 1) * GDN_DV] = _gdn_out(o[h], z, gnorm_ref[...], ones_ref)
        for h in hs:
            s_ref[h] = s_cur[h]
            sfin_ref[0, h] = s_cur[h]

    a_pieces = 2 + n_chunks * (CONV_DIM // LANES)
    b_levels = 2 * (int(math.log2(c)) - 1) + 2 + 2 * n_chunks
    a_steps = stage_a()
    done = 0
    for i, _ in enumerate(stage_b()):
        target = -(-(i + 1) * a_pieces // b_levels)
        for _ in range(target - done):
            next(a_steps, None)
        done = target
    for _ in a_steps:
        pass


def _gdn_prompt_call(qkvz, ba, lw, batch, seq):
    c = GDN_CHUNK * GDN_STEP_CHUNKS
    nc = seq // c
    z_col = CONV_DIM // GDN_VAL_DIM
    return pl.pallas_call(
        _gdn_prompt_kernel,
        grid=(batch, nc + 1),
        in_specs=[pl.BlockSpec((c, CONV_DIM), lambda b, s: (b * nc + jnp.minimum(s, nc - 1), 0)),
                  pl.BlockSpec((c, GDN_VAL_DIM), lambda b, s: (b * nc + jnp.maximum(s - 1, 0), z_col)),
                  pl.BlockSpec((c, BA_DIM), lambda b, s: (b * nc + jnp.minimum(s, nc - 1), 0)),
                  _resident((CONV_WIDTH, CONV_DIM)), _resident((1, BA_DIM)), _resident((1, BA_DIM)),
                  _resident((1, GDN_DV)), _resident(lw["ones"].shape), _resident(lw["sel"].shape)],
        out_specs=[pl.BlockSpec((c, GDN_VAL_DIM), lambda b, s: (b * nc + jnp.maximum(s - 1, 0), 0)),
                   pl.BlockSpec((1, GDN_HEADS, GDN_DK, GDN_DV), lambda b, s: (b, 0, 0, 0))],
        out_shape=[jax.ShapeDtypeStruct((batch * seq, GDN_VAL_DIM), F32),
                   jax.ShapeDtypeStruct((batch, GDN_HEADS, GDN_DK, GDN_DV), F32)],
        scratch_shapes=[pltpu.VMEM((GDN_HEADS, GDN_DK, GDN_DV), F32),
                        pltpu.VMEM((c + SUBLANES, CONV_DIM), F32),
                        pltpu.VMEM((c, CONV_DIM), F32), pltpu.VMEM((c, BA_DIM), F32), pltpu.VMEM((BA_DIM, c), F32),
                        pltpu.VMEM((c, CONV_DIM), F32), pltpu.VMEM((c, BA_DIM), F32), pltpu.VMEM((BA_DIM, c), F32)],
        compiler_params=pltpu.CompilerParams(dimension_semantics=("arbitrary", "arbitrary"),
                                             vmem_limit_bytes=VMEM_LIMIT),
        name="gdn_prompt",
    )(qkvz, qkvz, ba, lw["conv_w"], lw["alog_row"], lw["dtb_row"], lw["gnorm"], lw["ones"], lw["sel"])


def _gdn_decode_kernel(qkvz_ref, ba_ref, sconv_ref, s0_ref, cw_ref, alog_ref, dtb_ref, gnorm_ref,
                       ones_ref, sel_ref, o_ref, snew_ref, xbuf_ref, *, dec_seq):
    nb = DEC_SEQ_BLOCK
    t = dec_seq
    c = nb * t
    ys = []
    for s in range(nb):
        xbuf_ref[s, CONV_LEAD:SUBLANES, :] = sconv_ref[s]
        xbuf_ref[s, SUBLANES:SUBLANES + t, :] = qkvz_ref[s * t:(s + 1) * t, :CONV_DIM]
        ys.append(_conv_silu(xbuf_ref.at[s], cw_ref, t))
    y = jnp.concatenate(ys, axis=0)

    beta, g = _gates(ba_ref[...], alog_ref[...], dtb_ref[...])
    ii, jj = _iota2(c)
    same = (ii // t) == (jj // t)
    tril = jnp.where(jnp.logical_and(same, ii >= jj), 1.0, 0.0).astype(F32)
    ones = jnp.where(same, 1.0, 0.0).astype(F32)
    gc = _cumsum_rows(tril, g)
    gl = _cumsum_rows(ones, g)
    gc_t = gc.T
    lane = lax.broadcasted_iota(jnp.int32, (c, BA_DIM), 1)
    bcast = _lane_bcast(jnp.where(lane < GDN_HEADS, beta, gc), sel_ref, range(2 * GDN_HEADS))
    beta_b, gc_b = bcast[:GDN_HEADS], bcast[GDN_HEADS:]
    gl_b = _lane_bcast(gl, sel_ref, range(GDN_HEADS, 2 * GDN_HEADS))
    wy = {}
    for _ in _gdn_prep_steps(wy, y, beta_b, gc_b, gl_b, gc_t, c, t, ones_ref, normalized=False):
        pass
    u, w, qk, qd, kd = (wy[name][0] for name in ("u", "w", "qk", "qd", "kd"))
    hs = range(GDN_HEADS)
    units = [(s, h) for h in hs for s in range(nb)]
    s_old = {(s, h): s0_ref[s, h] for s, h in units}
    ws = {(s, h): _dot(jnp.concatenate([w[h][s * t:(s + 1) * t], qd[h][s * t:(s + 1) * t]], axis=0),
                       s_old[s, h]) for s, h in units}
    v_new = [jnp.concatenate([u[h][s * t:(s + 1) * t] - ws[s, h][:t] for s in range(nb)], axis=0) for h in hs]
    o = [jnp.concatenate([ws[s, h][t:] for s in range(nb)], axis=0) + _dot(qk[h], v_new[h]) for h in hs]
    cd = [jnp.exp(gl_b[h]) for h in hs]
    kd_t = [kd[h].T for h in hs]
    col_seq = lax.broadcasted_iota(jnp.int32, (GDN_DK, c), 1) // t
    for s, h in units:
        upd = _dot(jnp.where(col_seq == s, kd_t[h], 0.0), v_new[h])
        snew_ref[s, h] = s_old[s, h] * cd[h][s * t:s * t + 1, :] + upd
    for h in hs:
        z = qkvz_ref[:, CONV_DIM + h * GDN_DV:CONV_DIM + (h + 1) * GDN_DV]
        o_ref[:, h * GDN_DV:(h + 1) * GDN_DV] = _gdn_out(o[h], z, gnorm_ref[...], ones_ref)


def _gdn_decode_call(qkvz, ba, state_conv, state_gdn, lw, dec_batch, dec_seq):
    nb = DEC_SEQ_BLOCK
    rows = nb * dec_seq
    return pl.pallas_call(
        functools.partial(_gdn_decode_kernel, dec_seq=dec_seq),
        grid=(dec_batch // nb,),
        in_specs=[pl.BlockSpec((rows, QKVZ_DIM), lambda i: (i, 0)),
                  pl.BlockSpec((rows, BA_DIM), lambda i: (i, 0)),
                  pl.BlockSpec((nb, CONV_WIDTH - 1, CONV_DIM), lambda i: (i, 0, 0)),
                  pl.BlockSpec((nb, GDN_HEADS, GDN_DK, GDN_DV), lambda i: (i, 0, 0, 0)),
                  _resident((CONV_WIDTH, CONV_DIM)), _resident((1, BA_DIM)), _resident((1, BA_DIM)),
                  _resident((1, GDN_DV)), _resident(lw["ones"].shape), _resident(lw["sel"].shape)],
        out_specs=[pl.BlockSpec((rows, GDN_VAL_DIM), lambda i: (i, 0)),
                   pl.BlockSpec((nb, GDN_HEADS, GDN_DK, GDN_DV), lambda i: (i, 0, 0, 0))],
        out_shape=[jax.ShapeDtypeStruct((dec_batch * dec_seq, GDN_VAL_DIM), F32),
                   jax.ShapeDtypeStruct((dec_batch, GDN_HEADS, GDN_DK, GDN_DV), F32)],
        scratch_shapes=[pltpu.VMEM((nb, 2 * SUBLANES, CONV_DIM), F32)],
        compiler_params=pltpu.CompilerParams(dimension_semantics=("arbitrary",),
                                             vmem_limit_bytes=VMEM_LIMIT),
        name="gdn_decode",
    )(qkvz, ba, state_conv, state_gdn, lw["conv_w"], lw["alog_row"], lw["dtb_row"], lw["gnorm"],
      lw["ones"], lw["sel"])


def _bias_table_kernel(bucket_ref, rb_ref, out_ref):
    bucket = bucket_ref[...]
    for h in range(SWA_HEADS):
        acc = jnp.zeros(bucket.shape, F32)
        for b in range(NUM_BUCKETS):
            acc = jnp.where(bucket == b, rb_ref[b, h], acc)
        out_ref[h] = jnp.where(bucket < 0, NEG_INF, acc)


def _bias_table_call(bucket, rel_bias):
    r, c = bucket.shape
    return pl.pallas_call(
        _bias_table_kernel,
        in_specs=[pl.BlockSpec(memory_space=pltpu.VMEM), pl.BlockSpec(memory_space=pltpu.SMEM)],
        out_specs=pl.BlockSpec(memory_space=pltpu.VMEM),
        out_shape=jax.ShapeDtypeStruct((SWA_HEADS, r, c), F32),
        name="t5_bias_table",
    )(bucket, rel_bias)


def _t5_bucket(dist, valid):
    d = jnp.maximum(dist, 0)
    exact = NUM_BUCKETS // 2
    log_ratio = jnp.log(jnp.maximum(d, 1).astype(F32) / exact) / math.log(REL_MAX_DISTANCE / exact)
    large = jnp.minimum(exact + (log_ratio * (NUM_BUCKETS - exact)).astype(jnp.int32), NUM_BUCKETS - 1)
    bucket = jnp.where(d < exact, d, large)
    return jnp.where(valid, bucket, -1).astype(jnp.int32)


def _swa_prompt_kernel(q_ref, kvp_ref, kvc_ref, bias_ref, sink_ref, o_ref):
    w, dh, nq = SWA_WINDOW, SWA_HEAD_DIM, SWA_STEP_BLOCKS
    first = pl.program_id(1) == 0
    row = lax.broadcasted_iota(jnp.int32, (2 * w, w), 0)
    drop_prev = jnp.logical_and(first, row < w)
    kv = jnp.concatenate([kvp_ref[...], kvc_ref[...]], axis=0)
    k_all = kv[:, :SWA_KV_DIM].astype(BF16)
    v_t = kv[:, SWA_KV_DIM:].T.astype(BF16)
    zeros = jnp.zeros((dh, w), BF16)
    units = [(b, hg) for b in range(nq) for hg in range(SWA_HEADS)]
    n = range(len(units))
    q_t = [(q_ref[b * w:(b + 1) * w, :] * (dh ** -0.5)).T.astype(BF16) for b in range(nq)]

    def q_rhs(b, hg):
        qt = q_t[b][hg * dh:(hg + 1) * dh, :]
        return jnp.concatenate([qt, zeros] if hg // SWA_GROUP == 0 else [zeros, qt], axis=0)

    s = [jnp.dot(k_all[b * w:(b + 2) * w, :], q_rhs(b, hg), preferred_element_type=F32) + bias_ref[hg]
         for b, hg in units]
    s = [jnp.where(drop_prev, NEG_INF, s[i]) if b == 0 else s[i] for i, (b, hg) in enumerate(units)]
    sink = [sink_ref[hg] for b, hg in units]
    m = [jnp.maximum(jnp.max(s[i], axis=0, keepdims=True), sink[i]) for i in n]
    p = [jnp.exp(s[i] - m[i]) for i in n]
    denom = [jnp.sum(p[i], axis=0, keepdims=True) + jnp.exp(sink[i] - m[i]) for i in n]
    o_t = [jnp.dot(v_t[(hg // SWA_GROUP) * dh:(hg // SWA_GROUP + 1) * dh, b * w:(b + 2) * w], p[i].astype(BF16),
                   preferred_element_type=F32) / denom[i] for i, (b, hg) in enumerate(units)]
    for b in range(nq):
        for pair in range(SWA_HEADS // 2):
            i = b * SWA_HEADS + 2 * pair
            o_ref[b * w:(b + 1) * w, pair * 2 * dh:(pair + 1) * 2 * dh] = jnp.concatenate([o_t[i], o_t[i + 1]], axis=0).T


def _swa_prompt_call(qs, kv, bias, sinks, batch, seq):
    w, nq = SWA_WINDOW, SWA_STEP_BLOCKS
    nb = seq // (w * nq)
    return pl.pallas_call(
        _swa_prompt_kernel,
        grid=(batch, nb),
        in_specs=[pl.BlockSpec((nq * w, SWA_Q_DIM), lambda b, n: (b * nb + n, 0)),
                  pl.BlockSpec((w, 2 * SWA_KV_DIM),
                               lambda b, n: ((b * nb + n) * nq - jnp.minimum(n, 1), 0)),
                  pl.BlockSpec((nq * w, 2 * SWA_KV_DIM), lambda b, n: (b * nb + n, 0)),
                  _resident((SWA_HEADS, 2 * w, w)),
                  pl.BlockSpec(memory_space=pltpu.SMEM)],
        out_specs=pl.BlockSpec((nq * w, SWA_Q_DIM), lambda b, n: (b * nb + n, 0)),
        out_shape=jax.ShapeDtypeStruct((batch * seq, SWA_Q_DIM), F32),
        compiler_params=pltpu.CompilerParams(dimension_semantics=("arbitrary", "arbitrary"),
                                             vmem_limit_bytes=VMEM_LIMIT),
        name="swa_prompt",
    )(qs, kv, kv, bias, sinks)


def _swa_decode_kernel(q_ref, kvn_ref, kt_ref, vt_ref, bbuf_ref, bnew_ref, sink_ref,
                       o_ref, kto_ref, vto_ref, *, dec_seq):
    nb = DEC_SEQ_BLOCK
    t = dec_seq
    dh = SWA_HEAD_DIM
    rows = nb * t
    m_rows = SWA_GROUP * rows
    wb = kt_ref.shape[3]
    ri = lax.broadcasted_iota(jnp.int32, (m_rows, 1), 0)
    row_seq = (ri % rows) // t
    row_grp = ri // rows
    for h in range(SWA_KV_HEADS):
        lo, hi = h * dh, (h + 1) * dh
        q = jnp.concatenate([q_ref[:, (h * SWA_GROUP + g) * dh:(h * SWA_GROUP + g + 1) * dh]
                             for g in range(SWA_GROUP)], axis=0)
        sink = jnp.zeros((m_rows, 1), F32)
        for g in range(SWA_GROUP):
            sink = jnp.where(row_grp == g, sink_ref[h * SWA_GROUP + g], sink)
        k_new = jnp.concatenate([kvn_ref[:, lo:hi], jnp.zeros((m_rows - rows, dh), F32)], axis=0)
        v_new = jnp.concatenate([kvn_ref[:, SWA_KV_DIM + lo:SWA_KV_DIM + hi],
                                 jnp.zeros((m_rows - rows, dh), F32)], axis=0)
        s_new = _dot_nt(q, k_new) * (dh ** -0.5) + bnew_ref[h]
        s_buf = jnp.zeros((m_rows, wb), F32)
        for s in range(nb):
            s_buf = jnp.where(row_seq == s, _dot(q, kt_ref[s, h]), s_buf)
        s_buf = s_buf * (dh ** -0.5) + bbuf_ref[h]
        m = jnp.maximum(jnp.maximum(jnp.max(s_buf, axis=-1, keepdims=True),
                                    jnp.max(s_new, axis=-1, keepdims=True)), sink)
        p_buf = jnp.exp(s_buf - m)
        p_new = jnp.exp(s_new - m)
        denom = (jnp.sum(p_buf, axis=-1, keepdims=True) + jnp.sum(p_new, axis=-1, keepdims=True)
                 + jnp.exp(sink - m))
        o = _dot(p_new, v_new)
        for s in range(nb):
            o = o + _dot_nt(jnp.where(row_seq == s, p_buf, 0.0), vt_ref[s, h])
        o = o / denom
        for g in range(SWA_GROUP):
            hg = h * SWA_GROUP + g
            o_ref[:, hg * dh:(hg + 1) * dh] = o[g * rows:(g + 1) * rows]
    new_t = jnp.concatenate([kvn_ref[...], jnp.zeros((wb - rows, 2 * SWA_KV_DIM), F32)], axis=0).T
    lane = lax.broadcasted_iota(jnp.int32, (dh, wb), 1)
    for part, (src_ref, dst_ref) in enumerate(((kt_ref, kto_ref), (vt_ref, vto_ref))):
        for h in range(SWA_KV_HEADS):
            fresh = new_t[part * SWA_KV_DIM + h * dh:part * SWA_KV_DIM + (h + 1) * dh, :]
            for s in range(nb):
                kept = pltpu.roll(src_ref[s, h], wb - t, axis=1)
                tail = pltpu.roll(fresh, (wb - t - s * t) % wb, axis=1)
                dst_ref[s, h] = jnp.where(lane >= wb - t, tail, kept)


def _swa_decode_call(qs, kvn, kt, vt, bias_buf, bias_new, sinks, dec_batch, dec_seq):
    nb = DEC_SEQ_BLOCK
    rows = nb * dec_seq
    wb = kt.shape[3]
    m_rows = SWA_GROUP * rows
    assert rows <= wb and wb == LANES
    win_spec = pl.BlockSpec((nb, SWA_KV_HEADS, SWA_HEAD_DIM, wb), lambda i: (i, 0, 0, 0))
    return pl.pallas_call(
        functools.partial(_swa_decode_kernel, dec_seq=dec_seq),
        grid=(dec_batch // nb,),
        in_specs=[pl.BlockSpec((rows, SWA_Q_DIM), lambda i: (i, 0)),
                  pl.BlockSpec((rows, 2 * SWA_KV_DIM), lambda i: (i, 0)),
                  win_spec, win_spec,
                  _resident((SWA_KV_HEADS, m_rows, wb)),
                  _resident((SWA_KV_HEADS, m_rows, m_rows)),
                  pl.BlockSpec(memory_space=pltpu.SMEM)],
        out_specs=[pl.BlockSpec((rows, SWA_Q_DIM), lambda i: (i, 0)), win_spec, win_spec],
        out_shape=[jax.ShapeDtypeStruct((dec_batch * dec_seq, SWA_Q_DIM), F32),
                   jax.ShapeDtypeStruct(kt.shape, F32), jax.ShapeDtypeStruct(vt.shape, F32)],
        compiler_params=pltpu.CompilerParams(dimension_semantics=("arbitrary",),
                                             vmem_limit_bytes=VMEM_LIMIT),
        name="swa_decode",
    )(qs, kvn, kt, vt, bias_buf, bias_new, sinks)


def _prompt_bias(rel_bias):
    w = SWA_WINDOW
    dist = w + jnp.arange(w)[None, :] - jnp.arange(2 * w)[:, None]
    valid = (dist >= 0) & (dist < w)
    return _bias_table_call(_t5_bucket(dist, valid), rel_bias)


def _decode_bias(rel_bias, wb, dec_seq):
    nb, t = DEC_SEQ_BLOCK, dec_seq
    rows = nb * t
    m_rows = SWA_GROUP * rows
    tok = jnp.arange(t)
    dist = jnp.concatenate([wb + tok[:, None] - jnp.arange(wb)[None, :], tok[:, None] - tok[None, :]], axis=1)
    bucket = _t5_bucket(dist, (dist >= 0) & (dist < SWA_WINDOW))
    pad_r, pad_c = -t % SUBLANES, -(wb + t) % LANES
    bucket = jnp.pad(bucket, ((0, pad_r), (0, pad_c)), constant_values=-1)
    tab = _bias_table_call(bucket, rel_bias)[:, :t, :wb + t]
    tab = tab.reshape(SWA_KV_HEADS, SWA_GROUP, 1, t, wb + t)
    per_row = jnp.broadcast_to(tab, (SWA_KV_HEADS, SWA_GROUP, nb, t, wb + t)).reshape(SWA_KV_HEADS, m_rows, wb + t)
    r_seq = (jnp.arange(m_rows) % rows) // t
    cidx = jnp.arange(m_rows)
    own = (r_seq[:, None] == (cidx // t)[None, :]) & (cidx < rows)[None, :]
    bias_new = jnp.where(own[None], jnp.tile(per_row[:, :, wb:], (1, 1, m_rows // t)), NEG_INF)
    return per_row[:, :, :wb], bias_new


def _cast_kernel(*refs):
    n = len(refs) // 2
    for src_ref, dst_ref in zip(refs[:n], refs[n:]):
        dst_ref[...] = src_ref[...].astype(dst_ref.dtype)


def _to_bf16(ws):
    r, c = ws[0].shape
    tr = r
    while tr * c * 4 > CAST_BLOCK_BYTES and tr % 16 == 0:
        tr //= 2
    spec = pl.BlockSpec((tr, c), lambda i: (i, 0))
    return pl.pallas_call(
        _cast_kernel,
        grid=(r // tr,),
        in_specs=[spec] * len(ws),
        out_specs=[spec] * len(ws),
        out_shape=[jax.ShapeDtypeStruct((r, c), BF16)] * len(ws),
        compiler_params=pltpu.CompilerParams(dimension_semantics=("arbitrary",)),
        name="weights_to_bf16",
    )(*ws)


def _pack_layer(i, norm_ffn1_pre, norm_ffn1_post, ffn1_w_gate, ffn1_w_up, ffn1_w_down, norm_mix_pre,
                norm_mix_post, w_in, conv_w, gdn_a_log, gdn_dt_bias, gdn_norm, swa_sinks, w_out,
                norm_ffn2_pre, norm_ffn2_post, ffn2_w_gate, ffn2_w_up, ffn2_w_down, ple_gate, ple_proj,
                norm_ple_post):
    row = lambda g: g[i].reshape(1, -1).astype(F32)
    wt = jnp.transpose(w_in[i])
    n_gdn = QKVZ_DIM + 2 * GDN_HEADS
    win = jnp.concatenate([wt[:n_gdn].astype(BF16), jnp.zeros((BA_DIM - 2 * GDN_HEADS, D_MODEL), BF16),
                           wt[n_gdn:].astype(BF16)], axis=0)
    lane_pad = lambda v: jnp.zeros((1, BA_DIM), F32).at[0, GDN_HEADS:2 * GDN_HEADS].set(v[i].astype(F32))
    kk = np.arange(2 * LANES)[:, None] % LANES
    sel = (kk == (np.arange(2 * GDN_HEADS * LANES)[None, :] // LANES)).astype(np.float32)
    wg1, wu1, wg2, wu2 = _to_bf16([ffn1_w_gate[i], ffn1_w_up[i], ffn2_w_gate[i], ffn2_w_up[i]])
    wd1, wd2 = _to_bf16([ffn1_w_down[i], ffn2_w_down[i]])
    wo, wpg = _to_bf16([w_out[i], ple_gate[i]])
    return dict(
        ones=jnp.ones((LANES, LANES), F32), sel=jnp.asarray(sel, BF16),
        g1pre=row(norm_ffn1_pre), g1post=row(norm_ffn1_post), wg1=wg1, wu1=wu1, wd1=wd1,
        gmix=row(norm_mix_pre), gmixpost=row(norm_mix_post), win=win,
        conv_w=conv_w[i].astype(F32), alog_row=lane_pad(gdn_a_log), dtb_row=lane_pad(gdn_dt_bias),
        gnorm=row(gdn_norm), sinks=swa_sinks[i].astype(F32), wo=wo,
        g2pre=row(norm_ffn2_pre), g2post=row(norm_ffn2_post), wg2=wg2, wu2=wu2, wd2=wd2,
        wpg=wpg, wpp=ple_proj[i].astype(BF16), gple=row(norm_ple_post))


def kernel(x_prompt, x_sample, state_conv, state_gdn, cache_swa_k, cache_swa_v, p_prompt, p_sample,
           rel_bias, norm_ffn1_pre, norm_ffn1_post, ffn1_w_gate, ffn1_w_up, ffn1_w_down,
           norm_mix_pre, norm_mix_post, w_in, conv_w, gdn_a_log, gdn_dt_bias, gdn_norm, swa_sinks,
           w_out, norm_ffn2_pre, norm_ffn2_post, ffn2_w_gate, ffn2_w_up, ffn2_w_down,
           ple_gate, ple_proj, norm_ple_post):
    depth = w_in.shape[0]
    batch, seq, _ = x_prompt.shape
    dec_batch, dec_seq, _ = x_sample.shape
    wb = cache_swa_k.shape[2]
    wp = min(SWA_WINDOW, seq)
    rel_bias = rel_bias.astype(F32)
    bias_p = _prompt_bias(rel_bias)
    bias_db, bias_dn = _decode_bias(rel_bias, wb, dec_seq)

    yp = x_prompt.reshape(batch * seq, D_MODEL)
    ys = x_sample.reshape(dec_batch * dec_seq, D_MODEL)
    outs = [[] for _ in range(8)]
    for i in range(depth):
        lw = _pack_layer(i, norm_ffn1_pre, norm_ffn1_post, ffn1_w_gate, ffn1_w_up, ffn1_w_down,
                         norm_mix_pre, norm_mix_post, w_in, conv_w, gdn_a_log, gdn_dt_bias, gdn_norm,
                         swa_sinks, w_out, norm_ffn2_pre, norm_ffn2_post, ffn2_w_gate, ffn2_w_up,
                         ffn2_w_down, ple_gate, ple_proj, norm_ple_post)
        x1, qkvz, ba, qs, kv = _head_call(yp, lw)
        gdn_o, s_fin = _gdn_prompt_call(qkvz, ba, lw, batch, seq)
        swa_o = _swa_prompt_call(qs, kv, bias_p, lw["sinks"], batch, seq)
        yp = _tail_call(x1, gdn_o, swa_o, p_prompt[i].reshape(batch * seq, PLE_DIM), lw)
        kv3 = kv.reshape(batch, seq, 2 * SWA_KV_DIM)
        outs[0].append(qkvz.reshape(batch, seq, QKVZ_DIM)[:, seq - (CONV_WIDTH - 1):, :CONV_DIM])
        outs[1].append(s_fin)
        outs[2].append(kv3[:, seq - wp:, :SWA_KV_DIM].reshape(batch, wp, SWA_KV_HEADS, SWA_HEAD_DIM))
        outs[3].append(kv3[:, seq - wp:, SWA_KV_DIM:].reshape(batch, wp, SWA_KV_HEADS, SWA_HEAD_DIM))
        x1, qkvz, ba, qs, kv = _head_call(ys, lw)
        gdn_o, s_new = _gdn_decode_call(qkvz, ba, state_conv[i], state_gdn[i], lw, dec_batch, dec_seq)
        kt = jnp.transpose(cache_swa_k[i], (0, 2, 3, 1))
        vt = jnp.transpose(cache_swa_v[i], (0, 2, 3, 1))
        swa_o, kt_new, vt_new = _swa_decode_call(qs, kv, kt, vt, bias_db, bias_dn, lw["sinks"], dec_batch, dec_seq)
        ys = _tail_call(x1, gdn_o, swa_o, p_sample[i].reshape(dec_batch * dec_seq, PLE_DIM), lw)
        xp = jnp.concatenate([state_conv[i], qkvz.reshape(dec_batch, dec_seq, QKVZ_DIM)[:, :, :CONV_DIM]], axis=1)
        outs[4].append(xp[:, dec_seq:])
        outs[5].append(s_new)
        outs[6].append(jnp.transpose(kt_new, (0, 3, 1, 2)))
        outs[7].append(jnp.transpose(vt_new, (0, 3, 1, 2)))
    return (yp.reshape(batch, seq, D_MODEL), ys.reshape(dec_batch, dec_seq, D_MODEL),
            *[jnp.stack(o) for o in outs])
```

```python
import functools
import math

import numpy as np
import jax
import jax.numpy as jnp
from jax import lax
from jax.experimental import pallas as pl
from jax.experimental.pallas import tpu as pltpu

F32 = jnp.float32
BF16 = jnp.bfloat16

D_MODEL = 1024
NORM_EPS = 1e-6
PLE_DIM = 256
FFN_DIM = 2816
GDN_HEADS = 4
GDN_DK = 128
GDN_DV = 128
GDN_KEY_DIM = GDN_HEADS * GDN_DK
GDN_VAL_DIM = GDN_HEADS * GDN_DV
CONV_DIM = 2 * GDN_KEY_DIM + GDN_VAL_DIM
CONV_WIDTH = 4
SWA_HEADS = 8
SWA_KV_HEADS = 2
SWA_GROUP = SWA_HEADS // SWA_KV_HEADS
SWA_HEAD_DIM = 64
SWA_Q_DIM = SWA_HEADS * SWA_HEAD_DIM
SWA_KV_DIM = SWA_KV_HEADS * SWA_HEAD_DIM
SWA_WINDOW = 128
NUM_BUCKETS = 32
REL_MAX_DISTANCE = 128

QKVZ_DIM = CONV_DIM + GDN_VAL_DIM
BA_DIM = 128
PROJ_PACKED = QKVZ_DIM + BA_DIM + SWA_Q_DIM + 2 * SWA_KV_DIM

GDN_CHUNK = 128
GDN_STEP_CHUNKS = 4
SWA_STEP_BLOCKS = 8
ROW_TILE = 512
DEC_SEQ_BLOCK = 16
CAST_BLOCK_BYTES = 2 * 1024 * 1024
VMEM_LIMIT = 56 * 1024 * 1024

NEG_INF = float("-inf")
LANES = 128
SUBLANES = 8
CONV_LEAD = SUBLANES - (CONV_WIDTH - 1)
assert CONV_WIDTH == 4 and GDN_DK == LANES and GDN_DV == LANES


def _resident(shape):
    nd = len(shape)
    return pl.BlockSpec(shape, lambda *_: (0,) * nd, pipeline_mode=pl.Buffered(1))


def _rms(x, gain):
    ms = jnp.mean(x * x, axis=-1, keepdims=True)
    return (x * lax.rsqrt(ms + NORM_EPS)) * gain


def _sigmoid(x):
    return 1.0 / (1.0 + jnp.exp(-x))


def _silu(x):
    h = 0.5 * x
    return h + h * jnp.tanh(h)


def _dot(a, b):
    return jnp.dot(a, b, preferred_element_type=F32)


def _dot_nt(a, b):
    return lax.dot_general(a, b, (((1,), (1,)), ((), ())), preferred_element_type=F32)


def _dot_tn(a, b):
    return lax.dot_general(a, b, (((0,), (0,)), ((), ())), preferred_element_type=F32)


def _split(a):
    hi = a.astype(BF16)
    lo = (a - hi.astype(F32)).astype(BF16)
    return hi, lo


def _inv_rms(x):
    return lax.rsqrt(jnp.mean(x * x, axis=-1, keepdims=True) + NORM_EPS)


def _prenorm_dots(x, gain, ws, transposed=False):
    h = (x * gain).astype(BF16)
    inv = _inv_rms(x)
    dims = (((1,), (1 if transposed else 0,)), ((), ()))
    return [lax.dot_general(h, w, dims, preferred_element_type=F32) * inv for w in ws]


def _swiglu_block(x, gain, wg_ref, wu_ref, wd_ref):
    g, u = _prenorm_dots(x, gain, [wg_ref[...], wu_ref[...]])
    a = (_silu(g) * u).astype(BF16)
    return jnp.dot(a, wd_ref[...], preferred_element_type=F32)


def _head_kernel(x_ref, g1pre_ref, g1post_ref, wg_ref, wu_ref, wd_ref, gmix_ref, win_ref,
                 x1_ref, qkvz_ref, ba_ref, qs_ref, kv_ref):
    x = x_ref[...]
    y = _swiglu_block(x, g1pre_ref[...], wg_ref, wu_ref, wd_ref)
    x1 = x + 0.5 * _rms(y, g1post_ref[...])
    x1_ref[...] = x1
    c0, c1, c2 = QKVZ_DIM, QKVZ_DIM + BA_DIM, QKVZ_DIM + BA_DIM + SWA_Q_DIM
    qkvz_ref[...], ba_ref[...], qs_ref[...], kv_ref[...] = _prenorm_dots(
        x1, gmix_ref[...], [win_ref[:c0, :], win_ref[c0:c1, :], win_ref[c1:c2, :], win_ref[c2:, :]], transposed=True)


def _head_call(x, lw):
    n = x.shape[0]
    tm = min(ROW_TILE, n)
    row = lambda w: pl.BlockSpec((tm, w), lambda i: (i, 0))
    return pl.pallas_call(
        _head_kernel,
        grid=(n // tm,),
        in_specs=[row(D_MODEL), _resident((1, D_MODEL)), _resident((1, D_MODEL)),
                  _resident((D_MODEL, FFN_DIM)), _resident((D_MODEL, FFN_DIM)),
                  _resident((FFN_DIM, D_MODEL)), _resident((1, D_MODEL)),
                  _resident((PROJ_PACKED, D_MODEL))],
        out_specs=[row(D_MODEL), row(QKVZ_DIM), row(BA_DIM), row(SWA_Q_DIM), row(2 * SWA_KV_DIM)],
        out_shape=[jax.ShapeDtypeStruct((n, D_MODEL), F32), jax.ShapeDtypeStruct((n, QKVZ_DIM), F32),
                   jax.ShapeDtypeStruct((n, BA_DIM), F32), jax.ShapeDtypeStruct((n, SWA_Q_DIM), F32),
                   jax.ShapeDtypeStruct((n, 2 * SWA_KV_DIM), F32)],
        compiler_params=pltpu.CompilerParams(dimension_semantics=("arbitrary",),
                                             vmem_limit_bytes=VMEM_LIMIT),
        name="ffn1_inproj",
    )(x, lw["g1pre"], lw["g1post"], lw["wg1"], lw["wu1"], lw["wd1"], lw["gmix"], lw["win"])


def _tail_kernel(x_ref, gdn_ref, swa_ref, p_ref, wo_ref, gmixpost_ref, g2pre_ref, g2post_ref,
                 wg_ref, wu_ref, wd_ref, wpg_ref, wpp_ref, gple_ref, y_ref):
    x = x_ref[...]
    mix = (jnp.dot(gdn_ref[...].astype(BF16), wo_ref[:GDN_VAL_DIM, :], preferred_element_type=F32)
           + jnp.dot(swa_ref[...].astype(BF16), wo_ref[GDN_VAL_DIM:, :], preferred_element_type=F32))
    pp = jnp.dot(p_ref[...].astype(BF16), wpp_ref[...], preferred_element_type=F32)
    x = x + _rms(mix, gmixpost_ref[...])
    y = _swiglu_block(x, g2pre_ref[...], wg_ref, wu_ref, wd_ref)
    x = x + 0.5 * _rms(y, g2post_ref[...])
    gate = _sigmoid(jnp.dot(x.astype(BF16), wpg_ref[...], preferred_element_type=F32))
    y_ref[...] = x + _rms(gate * pp, gple_ref[...])


def _tail_call(x1, gdn_o, swa_o, p, lw):
    n = x1.shape[0]
    tm = min(ROW_TILE, n)
    row = lambda w: pl.BlockSpec((tm, w), lambda i: (i, 0))
    return pl.pallas_call(
        _tail_kernel,
        grid=(n // tm,),
        in_specs=[row(D_MODEL), row(GDN_VAL_DIM), row(SWA_Q_DIM), row(PLE_DIM),
                  _resident((GDN_VAL_DIM + SWA_Q_DIM, D_MODEL)), _resident((1, D_MODEL)),
                  _resident((1, D_MODEL)), _resident((1, D_MODEL)),
                  _resident((D_MODEL, FFN_DIM)), _resident((D_MODEL, FFN_DIM)),
                  _resident((FFN_DIM, D_MODEL)), _resident((D_MODEL, D_MODEL)),
                  _resident((PLE_DIM, D_MODEL)), _resident((1, D_MODEL))],
        out_specs=row(D_MODEL),
        out_shape=jax.ShapeDtypeStruct((n, D_MODEL), F32),
        compiler_params=pltpu.CompilerParams(dimension_semantics=("arbitrary",),
                                             vmem_limit_bytes=VMEM_LIMIT),
        name="outproj_ffn2_ple",
    )(x1, gdn_o, swa_o, p, lw["wo"], lw["gmixpost"], lw["g2pre"], lw["g2post"],
      lw["wg2"], lw["wu2"], lw["wd2"], lw["wpg"], lw["wpp"], lw["gple"])


def _iota2(c):
    return (lax.broadcasted_iota(jnp.int32, (c, c), 0), lax.broadcasted_iota(jnp.int32, (c, c), 1))


def _gates(ba, alog_row, dtb_row):
    beta = _sigmoid(ba)
    xa = ba + dtb_row
    softplus = jnp.maximum(xa, 0.0) + jnp.log1p(jnp.exp(-jnp.abs(xa)))
    g = -jnp.exp(alog_row) * softplus
    return beta, g


def _cumsum_rows(mask01, g):
    hi, lo = _split(g)
    m = mask01.astype(BF16)
    return jnp.dot(m, hi, preferred_element_type=F32) + jnp.dot(m, lo, preferred_element_type=F32)


def _hi_lo_lanes(x):
    hi, lo = _split(x)
    return jnp.concatenate([hi, lo], axis=1)


def _rowsum_bcast(x, ones_ref):
    return jnp.dot(x, ones_ref[...], preferred_element_type=F32)


def _lane_bcast(x, sel_ref, lanes):
    out = jnp.dot(_hi_lo_lanes(x), sel_ref[...], preferred_element_type=F32)
    return [out[:, l * LANES:(l + 1) * LANES] for l in lanes]


def _gdn_prep_steps(out, y, beta_b, gc_b, gl_b, gc_t, chunk, group, ones_ref, normalized):
    c = chunk
    n_chunks = y.shape[0] // c
    units = [(ci, h) for ci in range(n_chunks) for h in range(GDN_HEADS)]
    n = range(len(units))
    ii, jj = _iota2(c)
    same = (ii // group) == (jj // group)
    incl = jnp.logical_and(same, ii >= jj)
    strict = jnp.logical_and(same, ii > jj)

    def rows(t, ci, lo, hi):
        return t[ci * c:(ci + 1) * c, lo:hi]

    eg_b = [jnp.exp(t) for t in gc_b]
    ekd_b = [jnp.exp(gl_b[h] - gc_b[h]) for h in range(GDN_HEADS)]
    beta_u = [rows(beta_b[h], ci, 0, GDN_DK) for ci, h in units]
    eg_u = [rows(eg_b[h], ci, 0, GDN_DK) for ci, h in units]
    ekd_u = [rows(ekd_b[h], ci, 0, GDN_DK) for ci, h in units]
    gc_col = [rows(gc_b[h], ci, 0, c) for ci, h in units]
    gc_row = [gc_t[GDN_HEADS + h:GDN_HEADS + h + 1, ci * c:(ci + 1) * c] for ci, h in units]
    q = [rows(y, ci, h * GDN_DK, (h + 1) * GDN_DK) for ci, h in units]
    k = [rows(y, ci, GDN_KEY_DIM + h * GDN_DK, GDN_KEY_DIM + (h + 1) * GDN_DK) for ci, h in units]
    v = [rows(y, ci, 2 * GDN_KEY_DIM + h * GDN_DV, 2 * GDN_KEY_DIM + (h + 1) * GDN_DV) for ci, h in units]
    if not normalized:
        q = [_l2norm(t, ones_ref) * (GDN_DK ** -0.5) for t in q]
        k = [_l2norm(t, ones_ref) for t in k]
    decay = [jnp.exp(jnp.where(incl, gc_col[i] - gc_row[i], NEG_INF)) for i in n]
    kb = [k[i] * beta_u[i] for i in n]
    kq = [_dot_nt(jnp.concatenate([kb[i], q[i]], axis=0), k[i]) for i in n]
    yield
    a_mat = [jnp.where(strict, kq[i][:c] * decay[i], 0.0) for i in n]
    qk = [kq[i][c:] * decay[i] for i in n]
    eye = jnp.where(ii == jj, 1.0, 0.0).astype(F32)
    t_mat = [eye for _ in n]
    b = 1
    while b < group:
        lower = jnp.logical_and((ii // (2 * b)) == (jj // (2 * b)),
                                jnp.logical_and((ii % (2 * b)) >= b, (jj % (2 * b)) < b))
        m = [jnp.where(lower, a_mat[i], 0.0) for i in n]
        if b == 1:
            t_mat = [eye - m[i] for i in n]
        else:
            tm = [_dot(t_mat[i], m[i]) for i in n]
            yield
            t_mat = [t_mat[i] - _dot(tm[i], t_mat[i]) for i in n]
            yield
        b *= 2
    rhs = [jnp.concatenate([v[i] * beta_u[i], kb[i] * eg_u[i]], axis=-1) for i in n]
    sol = [_dot(t_mat[i], rhs[i]) for i in n]
    yield
    nest = lambda xs: [xs[ci * GDN_HEADS:(ci + 1) * GDN_HEADS] for ci in range(n_chunks)]
    out.update(u=nest([s[:, :GDN_DV] for s in sol]), w=nest([s[:, GDN_DV:] for s in sol]), qk=nest(qk),
               qd=nest([q[i] * eg_u[i] for i in n]), kd=nest([k[i] * ekd_u[i] for i in n]))


def _l2norm(t, ones_ref):
    return t * lax.rsqrt(_rowsum_bcast(t * t, ones_ref) + 1e-6)


def _gdn_out(o, z, gnorm_row, ones_ref):
    ms = _rowsum_bcast(o * o, ones_ref) * (1.0 / GDN_DV)
    return (o * lax.rsqrt(ms + NORM_EPS)) * gnorm_row * _silu(z)


def _conv_silu(xbuf_ref, cw_ref, c):
    y = xbuf_ref[pl.ds(CONV_LEAD, c), :] * cw_ref[0:1, :]
    for j in range(1, CONV_WIDTH):
        y = y + xbuf_ref[pl.ds(CONV_LEAD + j, c), :] * cw_ref[j:j + 1, :]
    return _silu(y)


def _gdn_prompt_kernel(qkv_ref, z_ref, ba_ref, cw_ref, alog_ref, dtb_ref, gnorm_ref, ones_ref, sel_ref,
                       o_ref, sfin_ref, s_ref, xbuf_ref, y0_ref, g0_ref, gt0_ref, y1_ref, g1_ref, gt1_ref):
    step = pl.program_id(1)

    @pl.when(step == 0)
    def _():
        s_ref[...] = jnp.zeros_like(s_ref)
        xbuf_ref[0:SUBLANES, :] = jnp.zeros((SUBLANES, CONV_DIM), F32)
        y1_ref[...] = jnp.zeros_like(y1_ref)
        g1_ref[...] = jnp.zeros_like(g1_ref)
        gt1_ref[...] = jnp.zeros_like(gt1_ref)

    @pl.when(lax.rem(step, 2) == 0)
    def _():
        _gdn_prompt_step((y0_ref, g0_ref, gt0_ref), (y1_ref, g1_ref, gt1_ref), qkv_ref, z_ref, ba_ref, cw_ref,
                         alog_ref, dtb_ref, gnorm_ref, ones_ref, sel_ref, o_ref, sfin_ref, s_ref, xbuf_ref)

    @pl.when(lax.rem(step, 2) == 1)
    def _():
        _gdn_prompt_step((y1_ref, g1_ref, gt1_ref), (y0_ref, g0_ref, gt0_ref), qkv_ref, z_ref, ba_ref, cw_ref,
                         alog_ref, dtb_ref, gnorm_ref, ones_ref, sel_ref, o_ref, sfin_ref, s_ref, xbuf_ref)


def _gdn_prompt_step(a_refs, b_refs, qkv_ref, z_ref, ba_ref, cw_ref, alog_ref, dtb_ref, gnorm_ref, ones_ref,
                     sel_ref, o_ref, sfin_ref, s_ref, xbuf_ref):
    ya_ref, ga_ref, gta_ref = a_refs
    yb_ref, gb_ref, gtb_ref = b_refs
    c = GDN_CHUNK
    n_chunks = GDN_STEP_CHUNKS
    r = n_chunks * c
    hs = range(GDN_HEADS)

    def stage_a():
        xbuf_ref[SUBLANES:SUBLANES + r, :] = qkv_ref[...]
        beta, g = _gates(ba_ref[...], alog_ref[...], dtb_ref[...])
        ii, jj = _iota2(r)
        tril = jnp.where(jnp.logical_and((ii // c) == (jj // c), ii >= jj), 1.0, 0.0)
        gc = _cumsum_rows(tril, g)
        gta_ref[...] = gc.T
        lane = lax.broadcasted_iota(jnp.int32, (r, BA_DIM), 1)
        ga_ref[...] = jnp.where(lane < GDN_HEADS, beta, gc)
        yield
        for ci in range(n_chunks):
            rows = slice(ci * c, (ci + 1) * c)
            for slab in range(CONV_DIM // LANES):
                cols = slice(slab * LANES, (slab + 1) * LANES)
                ext = xbuf_ref[pl.ds(ci * c, c + SUBLANES), cols]
                s1 = pltpu.roll(ext, 1, axis=0)
                u2 = pltpu.roll(ext * cw_ref[1:2, cols] + s1 * cw_ref[0:1, cols], 2, axis=0)
                yc = (ext * cw_ref[3:4, cols] + s1 * cw_ref[2:3, cols] + u2)[SUBLANES:]
                ya_ref[rows, cols] = _silu(yc)
                if slab < GDN_KEY_DIM // LANES:
                    ya_ref[rows, cols] = _l2norm(ya_ref[rows, cols], ones_ref) * (GDN_DK ** -0.5)
                elif slab < 2 * GDN_KEY_DIM // LANES:
                    ya_ref[rows, cols] = _l2norm(ya_ref[rows, cols], ones_ref)
                yield
        xbuf_ref[0:SUBLANES, :] = xbuf_ref[r:r + SUBLANES, :]
        yield

    def stage_b():
        bcast = _lane_bcast(gb_ref[...], sel_ref, range(2 * GDN_HEADS))
        beta_b, gc_b = bcast[:GDN_HEADS], bcast[GDN_HEADS:]
        glast = [[gc_b[h][(ci + 1) * c - 1:(ci + 1) * c, :] for h in hs] for ci in range(n_chunks)]
        gl_b = [jnp.concatenate([jnp.broadcast_to(glast[ci][h], (c, GDN_DK)) for ci in range(n_chunks)],
                                axis=0) for h in hs]
        wy = {}
        yield from _gdn_prep_steps(wy, yb_ref, beta_b, gc_b, gl_b, gtb_ref[...], c, c, ones_ref,
                                   normalized=True)
        u, w, qk, qd, kd = (wy[name] for name in ("u", "w", "qk", "qd", "kd"))
        s_cur = [s_ref[h] for h in hs]
        for ci in range(n_chunks):
            ws = [_dot(jnp.concatenate([w[ci][h], qd[ci][h]], axis=0), s_cur[h]) for h in hs]
            yield
            v_new = [u[ci][h] - ws[h][:c] for h in hs]
            o = [ws[h][c:] + _dot(qk[ci][h], v_new[h]) for h in hs]
            s_cur = [s_cur[h] * jnp.exp(glast[ci][h]) + _dot_tn(kd[ci][h], v_new[h]) for h in hs]
            yield
            for h in hs:
                z = z_ref[ci * c:(ci + 1) * c, h * GDN_DV:(h + 1) * GDN_DV]
                o_ref[ci * c:(ci + 1) * c, h * GDN_DV:(h + 1) * GDN_DV] = _gdn_out(o[h], z, gnorm_ref[...], ones_ref)
        for h in hs:
            s_ref[h] = s_cur[h]
            sfin_ref[0, h] = s_cur[h]

    a_pieces = 2 + n_chunks * (CONV_DIM // LANES)
    b_levels = 2 * (int(math.log2(c)) - 1) + 2 + 2 * n_chunks
    a_steps = stage_a()
    done = 0
    for i, _ in enumerate(stage_b()):
        target = -(-(i + 1) * a_pieces // b_levels)
        for _ in range(target - done):
            next(a_steps, None)
        done = target
    for _ in a_steps:
        pass


def _gdn_prompt_call(qkvz, ba, lw, batch, seq):
    c = GDN_CHUNK * GDN_STEP_CHUNKS
    nc = seq // c
    z_col = CONV_DIM // GDN_VAL_DIM
    return pl.pallas_call(
        _gdn_prompt_kernel,
        grid=(batch, nc + 1),
        in_specs=[pl.BlockSpec((c, CONV_DIM), lambda b, s: (b * nc + jnp.minimum(s, nc - 1), 0)),
                  pl.BlockSpec((c, GDN_VAL_DIM), lambda b, s: (b * nc + jnp.maximum(s - 1, 0), z_col)),
                  pl.BlockSpec((c, BA_DIM), lambda b, s: (b * nc + jnp.minimum(s, nc - 1), 0)),
                  _resident((CONV_WIDTH, CONV_DIM)), _resident((1, BA_DIM)), _resident((1, BA_DIM)),
                  _resident((1, GDN_DV)), _resident(lw["ones"].shape), _resident(lw["sel"].shape)],
        out_specs=[pl.BlockSpec((c, GDN_VAL_DIM), lambda b, s: (b * nc + jnp.maximum(s - 1, 0), 0)),
                   pl.BlockSpec((1, GDN_HEADS, GDN_DK, GDN_DV), lambda b, s: (b, 0, 0, 0))],
        out_shape=[jax.ShapeDtypeStruct((batch * seq, GDN_VAL_DIM), F32),
                   jax.ShapeDtypeStruct((batch, GDN_HEADS, GDN_DK, GDN_DV), F32)],
        scratch_shapes=[pltpu.VMEM((GDN_HEADS, GDN_DK, GDN_DV), F32),
                        pltpu.VMEM((c + SUBLANES, CONV_DIM), F32),
                        pltpu.VMEM((c, CONV_DIM), F32), pltpu.VMEM((c, BA_DIM), F32), pltpu.VMEM((BA_DIM, c), F32),
                        pltpu.VMEM((c, CONV_DIM), F32), pltpu.VMEM((c, BA_DIM), F32), pltpu.VMEM((BA_DIM, c), F32)],
        compiler_params=pltpu.CompilerParams(dimension_semantics=("arbitrary", "arbitrary"),
                                             vmem_limit_bytes=VMEM_LIMIT),
        name="gdn_prompt",
    )(qkvz, qkvz, ba, lw["conv_w"], lw["alog_row"], lw["dtb_row"], lw["gnorm"], lw["ones"], lw["sel"])


def _gdn_decode_kernel(qkvz_ref, ba_ref, sconv_ref, s0_ref, cw_ref, alog_ref, dtb_ref, gnorm_ref,
                       ones_ref, sel_ref, o_ref, snew_ref, xbuf_ref, *, dec_seq):
    nb = DEC_SEQ_BLOCK
    t = dec_seq
    c = nb * t
    ys = []
    for s in range(nb):
        xbuf_ref[s, CONV_LEAD:SUBLANES, :] = sconv_ref[s]
        xbuf_ref[s, SUBLANES:SUBLANES + t, :] = qkvz_ref[s * t:(s + 1) * t, :CONV_DIM]
        ys.append(_conv_silu(xbuf_ref.at[s], cw_ref, t))
    y = jnp.concatenate(ys, axis=0)

    beta, g = _gates(ba_ref[...], alog_ref[...], dtb_ref[...])
    ii, jj = _iota2(c)
    same = (ii // t) == (jj // t)
    tril = jnp.where(jnp.logical_and(same, ii >= jj), 1.0, 0.0).astype(F32)
    ones = jnp.where(same, 1.0, 0.0).astype(F32)
    gc = _cumsum_rows(tril, g)
    gl = _cumsum_rows(ones, g)
    gc_t = gc.T
    lane = lax.broadcasted_iota(jnp.int32, (c, BA_DIM), 1)
    bcast = _lane_bcast(jnp.where(lane < GDN_HEADS, beta, gc), sel_ref, range(2 * GDN_HEADS))
    beta_b, gc_b = bcast[:GDN_HEADS], bcast[GDN_HEADS:]
    gl_b = _lane_bcast(gl, sel_ref, range(GDN_HEADS, 2 * GDN_HEADS))
    wy = {}
    for _ in _gdn_prep_steps(wy, y, beta_b, gc_b, gl_b, gc_t, c, t, ones_ref, normalized=False):
        pass
    u, w, qk, qd, kd = (wy[name][0] for name in ("u", "w", "qk", "qd", "kd"))
    hs = range(GDN_HEADS)
    units = [(s, h) for h in hs for s in range(nb)]
    s_old = {(s, h): s0_ref[s, h] for s, h in units}
    ws = {(s, h): _dot(jnp.concatenate([w[h][s * t:(s + 1) * t], qd[h][s * t:(s + 1) * t]], axis=0),
                       s_old[s, h]) for s, h in units}
    v_new = [jnp.concatenate([u[h][s * t:(s + 1) * t] - ws[s, h][:t] for s in range(nb)], axis=0) for h in hs]
    o = [jnp.concatenate([ws[s, h][t:] for s in range(nb)], axis=0) + _dot(qk[h], v_new[h]) for h in hs]
    cd = [jnp.exp(gl_b[h]) for h in hs]
    kd_t = [kd[h].T for h in hs]
    col_seq = lax.broadcasted_iota(jnp.int32, (GDN_DK, c), 1) // t
    for s, h in units:
        upd = _dot(jnp.where(col_seq == s, kd_t[h], 0.0), v_new[h])
        snew_ref[s, h] = s_old[s, h] * cd[h][s * t:s * t + 1, :] + upd
    for h in hs:
        z = qkvz_ref[:, CONV_DIM + h * GDN_DV:CONV_DIM + (h + 1) * GDN_DV]
        o_ref[:, h * GDN_DV:(h + 1) * GDN_DV] = _gdn_out(o[h], z, gnorm_ref[...], ones_ref)


def _gdn_decode_call(qkvz, ba, state_conv, state_gdn, lw, dec_batch, dec_seq):
    nb = DEC_SEQ_BLOCK
    rows = nb * dec_seq
    return pl.pallas_call(
        functools.partial(_gdn_decode_kernel, dec_seq=dec_seq),
        grid=(dec_batch // nb,),
        in_specs=[pl.BlockSpec((rows, QKVZ_DIM), lambda i: (i, 0)),
                  pl.BlockSpec((rows, BA_DIM), lambda i: (i, 0)),
                  pl.BlockSpec((nb, CONV_WIDTH - 1, CONV_DIM), lambda i: (i, 0, 0)),
                  pl.BlockSpec((nb, GDN_HEADS, GDN_DK, GDN_DV), lambda i: (i, 0, 0, 0)),
                  _resident((CONV_WIDTH, CONV_DIM)), _resident((1, BA_DIM)), _resident((1, BA_DIM)),
                  _resident((1, GDN_DV)), _resident(lw["ones"].shape), _resident(lw["sel"].shape)],
        out_specs=[pl.BlockSpec((rows, GDN_VAL_DIM), lambda i: (i, 0)),
                   pl.BlockSpec((nb, GDN_HEADS, GDN_DK, GDN_DV), lambda i: (i, 0, 0, 0))],
        out_shape=[jax.ShapeDtypeStruct((dec_batch * dec_seq, GDN_VAL_DIM), F32),
                   jax.ShapeDtypeStruct((dec_batch, GDN_HEADS, GDN_DK, GDN_DV), F32)],
        scratch_shapes=[pltpu.VMEM((nb, 2 * SUBLANES, CONV_DIM), F32)],
        compiler_params=pltpu.CompilerParams(dimension_semantics=("arbitrary",),
                                             vmem_limit_bytes=VMEM_LIMIT),
        name="gdn_decode",
    )(qkvz, ba, state_conv, state_gdn, lw["conv_w"], lw["alog_row"], lw["dtb_row"], lw["gnorm"],
      lw["ones"], lw["sel"])


def _bias_table_kernel(bucket_ref, rb_ref, out_ref):
    bucket = bucket_ref[...]
    for h in range(SWA_HEADS):
        acc = jnp.zeros(bucket.shape, F32)
        for b in range(NUM_BUCKETS):
            acc = jnp.where(bucket == b, rb_ref[b, h], acc)
        out_ref[h] = jnp.where(bucket < 0, NEG_INF, acc)


def _bias_table_call(bucket, rel_bias):
    r, c = bucket.shape
    return pl.pallas_call(
        _bias_table_kernel,
        in_specs=[pl.BlockSpec(memory_space=pltpu.VMEM), pl.BlockSpec(memory_space=pltpu.SMEM)],
        out_specs=pl.BlockSpec(memory_space=pltpu.VMEM),
        out_shape=jax.ShapeDtypeStruct((SWA_HEADS, r, c), F32),
        name="t5_bias_table",
    )(bucket, rel_bias)


def _t5_bucket(dist, valid):
    d = jnp.maximum(dist, 0)
    exact = NUM_BUCKETS // 2
    log_ratio = jnp.log(jnp.maximum(d, 1).astype(F32) / exact) / math.log(REL_MAX_DISTANCE / exact)
    large = jnp.minimum(exact + (log_ratio * (NUM_BUCKETS - exact)).astype(jnp.int32), NUM_BUCKETS - 1)
    bucket = jnp.where(d < exact, d, large)
    return jnp.where(valid, bucket, -1).astype(jnp.int32)


def _swa_prompt_kernel(q_ref, kvp_ref, kvc_ref, bias_ref, sink_ref, o_ref):
    w, dh, nq = SWA_WINDOW, SWA_HEAD_DIM, SWA_STEP_BLOCKS
    first = pl.program_id(1) == 0
    row = lax.broadcasted_iota(jnp.int32, (2 * w, w), 0)
    drop_prev = jnp.logical_and(first, row < w)
    kv = jnp.concatenate([kvp_ref[...], kvc_ref[...]], axis=0)
    k_all = kv[:, :SWA_KV_DIM].astype(BF16)
    v_t = kv[:, SWA_KV_DIM:].T.astype(BF16)
    zeros = jnp.zeros((dh, w), BF16)
    units = [(b, hg) for b in range(nq) for hg in range(SWA_HEADS)]
    n = range(len(units))
    q_t = [(q_ref[b * w:(b + 1) * w, :] * (dh ** -0.5)).T.astype(BF16) for b in range(nq)]

    def q_rhs(b, hg):
        qt = q_t[b][hg * dh:(hg + 1) * dh, :]
        return jnp.concatenate([qt, zeros] if hg // SWA_GROUP == 0 else [zeros, qt], axis=0)

    s = [jnp.dot(k_all[b * w:(b + 2) * w, :], q_rhs(b, hg), preferred_element_type=F32) + bias_ref[hg]
         for b, hg in units]
    s = [jnp.where(drop_prev, NEG_INF, s[i]) if b == 0 else s[i] for i, (b, hg) in enumerate(units)]
    sink = [sink_ref[hg] for b, hg in units]
    m = [jnp.maximum(jnp.max(s[i], axis=0, keepdims=True), sink[i]) for i in n]
    p = [jnp.exp(s[i] - m[i]) for i in n]
    denom = [jnp.sum(p[i], axis=0, keepdims=True) + jnp.exp(sink[i] - m[i]) for i in n]
    o_t = [jnp.dot(v_t[(hg // SWA_GROUP) * dh:(hg // SWA_GROUP + 1) * dh, b * w:(b + 2) * w], p[i].astype(BF16),
                   preferred_element_type=F32) / denom[i] for i, (b, hg) in enumerate(units)]
    for b in range(nq):
        for pair in range(SWA_HEADS // 2):
            i = b * SWA_HEADS + 2 * pair
            o_ref[b * w:(b + 1) * w, pair * 2 * dh:(pair + 1) * 2 * dh] = jnp.concatenate([o_t[i], o_t[i + 1]], axis=0).T


def _swa_prompt_call(qs, kv, bias, sinks, batch, seq):
    w, nq = SWA_WINDOW, SWA_STEP_BLOCKS
    nb = seq // (w * nq)
    return pl.pallas_call(
        _swa_prompt_kernel,
        grid=(batch, nb),
        in_specs=[pl.BlockSpec((nq * w, SWA_Q_DIM), lambda b, n: (b * nb + n, 0)),
                  pl.BlockSpec((w, 2 * SWA_KV_DIM),
                               lambda b, n: ((b * nb + n) * nq - jnp.minimum(n, 1), 0)),
                  pl.BlockSpec((nq * w, 2 * SWA_KV_DIM), lambda b, n: (b * nb + n, 0)),
                  _resident((SWA_HEADS, 2 * w, w)),
                  pl.BlockSpec(memory_space=pltpu.SMEM)],
        out_specs=pl.BlockSpec((nq * w, SWA_Q_DIM), lambda b, n: (b * nb + n, 0)),
        out_shape=jax.ShapeDtypeStruct((batch * seq, SWA_Q_DIM), F32),
        compiler_params=pltpu.CompilerParams(dimension_semantics=("arbitrary", "arbitrary"),
                                             vmem_limit_bytes=VMEM_LIMIT),
        name="swa_prompt",
    )(qs, kv, kv, bias, sinks)


def _swa_decode_kernel(q_ref, kvn_ref, kt_ref, vt_ref, bbuf_ref, bnew_ref, sink_ref,
                       o_ref, kto_ref, vto_ref, *, dec_seq):
    nb = DEC_SEQ_BLOCK
    t = dec_seq
    dh = SWA_HEAD_DIM
    rows = nb * t
    m_rows = SWA_GROUP * rows
    wb = kt_ref.shape[3]
    ri = lax.broadcasted_iota(jnp.int32, (m_rows, 1), 0)
    row_seq = (ri % rows) // t
    row_grp = ri // rows
    for h in range(SWA_KV_HEADS):
        lo, hi = h * dh, (h + 1) * dh
        q = jnp.concatenate([q_ref[:, (h * SWA_GROUP + g) * dh:(h * SWA_GROUP + g + 1) * dh]
                             for g in range(SWA_GROUP)], axis=0)
        sink = jnp.zeros((m_rows, 1), F32)
        for g in range(SWA_GROUP):
            sink = jnp.where(row_grp == g, sink_ref[h * SWA_GROUP + g], sink)
        k_new = jnp.concatenate([kvn_ref[:, lo:hi], jnp.zeros((m_rows - rows, dh), F32)], axis=0)
        v_new = jnp.concatenate([kvn_ref[:, SWA_KV_DIM + lo:SWA_KV_DIM + hi],
                                 jnp.zeros((m_rows - rows, dh), F32)], axis=0)
        s_new = _dot_nt(q, k_new) * (dh ** -0.5) + bnew_ref[h]
        s_buf = jnp.zeros((m_rows, wb), F32)
        for s in range(nb):
            s_buf = jnp.where(row_seq == s, _dot(q, kt_ref[s, h]), s_buf)
        s_buf = s_buf * (dh ** -0.5) + bbuf_ref[h]
        m = jnp.maximum(jnp.maximum(jnp.max(s_buf, axis=-1, keepdims=True),
                                    jnp.max(s_new, axis=-1, keepdims=True)), sink)
        p_buf = jnp.exp(s_buf - m)
        p_new = jnp.exp(s_new - m)
        denom = (jnp.sum(p_buf, axis=-1, keepdims=True) + jnp.sum(p_new, axis=-1, keepdims=True)
                 + jnp.exp(sink - m))
        o = _dot(p_new, v_new)
        for s in range(nb):
            o = o + _dot_nt(jnp.where(row_seq == s, p_buf, 0.0), vt_ref[s, h])
        o = o / denom
        for g in range(SWA_GROUP):
            hg = h * SWA_GROUP + g
            o_ref[:, hg * dh:(hg + 1) * dh] = o[g * rows:(g + 1) * rows]
    new_t = jnp.concatenate([kvn_ref[...], jnp.zeros((wb - rows, 2 * SWA_KV_DIM), F32)], axis=0).T
    lane = lax.broadcasted_iota(jnp.int32, (dh, wb), 1)
    for part, (src_ref, dst_ref) in enumerate(((kt_ref, kto_ref), (vt_ref, vto_ref))):
        for h in range(SWA_KV_HEADS):
            fresh = new_t[part * SWA_KV_DIM + h * dh:part * SWA_KV_DIM + (h + 1) * dh, :]
            for s in range(nb):
                kept = pltpu.roll(src_ref[s, h], wb - t, axis=1)
                tail = pltpu.roll(fresh, (wb - t - s * t) % wb, axis=1)
                dst_ref[s, h] = jnp.where(lane >= wb - t, tail, kept)


def _swa_decode_call(qs, kvn, kt, vt, bias_buf, bias_new, sinks, dec_batch, dec_seq):
    nb = DEC_SEQ_BLOCK
    rows = nb * dec_seq
    wb = kt.shape[3]
    m_rows = SWA_GROUP * rows
    assert rows <= wb and wb == LANES
    win_spec = pl.BlockSpec((nb, SWA_KV_HEADS, SWA_HEAD_DIM, wb), lambda i: (i, 0, 0, 0))
    return pl.pallas_call(
        functools.partial(_swa_decode_kernel, dec_seq=dec_seq),
        grid=(dec_batch // nb,),
        in_specs=[pl.BlockSpec((rows, SWA_Q_DIM), lambda i: (i, 0)),
                  pl.BlockSpec((rows, 2 * SWA_KV_DIM), lambda i: (i, 0)),
                  win_spec, win_spec,
                  _resident((SWA_KV_HEADS, m_rows, wb)),
                  _resident((SWA_KV_HEADS, m_rows, m_rows)),
                  pl.BlockSpec(memory_space=pltpu.SMEM)],
        out_specs=[pl.BlockSpec((rows, SWA_Q_DIM), lambda i: (i, 0)), win_spec, win_spec],
        out_shape=[jax.ShapeDtypeStruct((dec_batch * dec_seq, SWA_Q_DIM), F32),
                   jax.ShapeDtypeStruct(kt.shape, F32), jax.ShapeDtypeStruct(vt.shape, F32)],
        compiler_params=pltpu.CompilerParams(dimension_semantics=("arbitrary",),
                                             vmem_limit_bytes=VMEM_LIMIT),
        name="swa_decode",
    )(qs, kvn, kt, vt, bias_buf, bias_new, sinks)


def _prompt_bias(rel_bias):
    w = SWA_WINDOW
    dist = w + jnp.arange(w)[None, :] - jnp.arange(2 * w)[:, None]
    valid = (dist >= 0) & (dist < w)
    return _bias_table_call(_t5_bucket(dist, valid), rel_bias)


def _decode_bias(rel_bias, wb, dec_seq):
    nb, t = DEC_SEQ_BLOCK, dec_seq
    rows = nb * t
    m_rows = SWA_GROUP * rows
    tok = jnp.arange(t)
    dist = jnp.concatenate([wb + tok[:, None] - jnp.arange(wb)[None, :], tok[:, None] - tok[None, :]], axis=1)
    bucket = _t5_bucket(dist, (dist >= 0) & (dist < SWA_WINDOW))
    pad_r, pad_c = -t % SUBLANES, -(wb + t) % LANES
    bucket = jnp.pad(bucket, ((0, pad_r), (0, pad_c)), constant_values=-1)
    tab = _bias_table_call(bucket, rel_bias)[:, :t, :wb + t]
    tab = tab.reshape(SWA_KV_HEADS, SWA_GROUP, 1, t, wb + t)
    per_row = jnp.broadcast_to(tab, (SWA_KV_HEADS, SWA_GROUP, nb, t, wb + t)).reshape(SWA_KV_HEADS, m_rows, wb + t)
    r_seq = (jnp.arange(m_rows) % rows) // t
    cidx = jnp.arange(m_rows)
    own = (r_seq[:, None] == (cidx // t)[None, :]) & (cidx < rows)[None, :]
    bias_new = jnp.where(own[None], jnp.tile(per_row[:, :, wb:], (1, 1, m_rows // t)), NEG_INF)
    return per_row[:, :, :wb], bias_new


def _cast_kernel(*refs):
    n = len(refs) // 2
    for src_ref, dst_ref in zip(refs[:n], refs[n:]):
        dst_ref[...] = src_ref[...].astype(dst_ref.dtype)


def _to_bf16(ws):
    r, c = ws[0].shape
    tr = r
    while tr * c * 4 > CAST_BLOCK_BYTES and tr % 16 == 0:
        tr //= 2
    spec = pl.BlockSpec((tr, c), lambda i: (i, 0))
    return pl.pallas_call(
        _cast_kernel,
        grid=(r // tr,),
        in_specs=[spec] * len(ws),
        out_specs=[spec] * len(ws),
        out_shape=[jax.ShapeDtypeStruct((r, c), BF16)] * len(ws),
        compiler_params=pltpu.CompilerParams(dimension_semantics=("arbitrary",)),
        name="weights_to_bf16",
    )(*ws)


def _pack_layer(i, norm_ffn1_pre, norm_ffn1_post, ffn1_w_gate, ffn1_w_up, ffn1_w_down, norm_mix_pre,
                norm_mix_post, w_in, conv_w, gdn_a_log, gdn_dt_bias, gdn_norm, swa_sinks, w_out,
                norm_ffn2_pre, norm_ffn2_post, ffn2_w_gate, ffn2_w_up, ffn2_w_down, ple_gate, ple_proj,
                norm_ple_post):
    row = lambda g: g[i].reshape(1, -1).astype(F32)
    wt = jnp.transpose(w_in[i])
    n_gdn = QKVZ_DIM + 2 * GDN_HEADS
    win = jnp.concatenate([wt[:n_gdn].astype(BF16), jnp.zeros((BA_DIM - 2 * GDN_HEADS, D_MODEL), BF16),
                           wt[n_gdn:].astype(BF16)], axis=0)
    lane_pad = lambda v: jnp.zeros((1, BA_DIM), F32).at[0, GDN_HEADS:2 * GDN_HEADS].set(v[i].astype(F32))
    kk = np.arange(2 * LANES)[:, None] % LANES
    sel = (kk == (np.arange(2 * GDN_HEADS * LANES)[None, :] // LANES)).astype(np.float32)
    wg1, wu1, wg2, wu2 = _to_bf16([ffn1_w_gate[i], ffn1_w_up[i], ffn2_w_gate[i], ffn2_w_up[i]])
    wd1, wd2 = _to_bf16([ffn1_w_down[i], ffn2_w_down[i]])
    wo, wpg = _to_bf16([w_out[i], ple_gate[i]])
    return dict(
        ones=jnp.ones((LANES, LANES), F32), sel=jnp.asarray(sel, BF16),
        g1pre=row(norm_ffn1_pre), g1post=row(norm_ffn1_post), wg1=wg1, wu1=wu1, wd1=wd1,
        gmix=row(norm_mix_pre), gmixpost=row(norm_mix_post), win=win,
        conv_w=conv_w[i].astype(F32), alog_row=lane_pad(gdn_a_log), dtb_row=lane_pad(gdn_dt_bias),
        gnorm=row(gdn_norm), sinks=swa_sinks[i].astype(F32), wo=wo,
        g2pre=row(norm_ffn2_pre), g2post=row(norm_ffn2_post), wg2=wg2, wu2=wu2, wd2=wd2,
        wpg=wpg, wpp=ple_proj[i].astype(BF16), gple=row(norm_ple_post))


def kernel(x_prompt, x_sample, state_conv, state_gdn, cache_swa_k, cache_swa_v, p_prompt, p_sample,
           rel_bias, norm_ffn1_pre, norm_ffn1_post, ffn1_w_gate, ffn1_w_up, ffn1_w_down,
           norm_mix_pre, norm_mix_post, w_in, conv_w, gdn_a_log, gdn_dt_bias, gdn_norm, swa_sinks,
           w_out, norm_ffn2_pre, norm_ffn2_post, ffn2_w_gate, ffn2_w_up, ffn2_w_down,
           ple_gate, ple_proj, norm_ple_post):
    depth = w_in.shape[0]
    batch, seq, _ = x_prompt.shape
    dec_batch, dec_seq, _ = x_sample.shape
    wb = cache_swa_k.shape[2]
    wp = min(SWA_WINDOW, seq)
    rel_bias = rel_bias.astype(F32)
    bias_p = _prompt_bias(rel_bias)
    bias_db, bias_dn = _decode_bias(rel_bias, wb, dec_seq)

    yp = x_prompt.reshape(batch * seq, D_MODEL)
    ys = x_sample.reshape(dec_batch * dec_seq, D_MODEL)
    outs = [[] for _ in range(8)]
    for i in range(depth):
        lw = _pack_layer(i, norm_ffn1_pre, norm_ffn1_post, ffn1_w_gate, ffn1_w_up, ffn1_w_down,
                         norm_mix_pre, norm_mix_post, w_in, conv_w, gdn_a_log, gdn_dt_bias, gdn_norm,
                         swa_sinks, w_out, norm_ffn2_pre, norm_ffn2_post, ffn2_w_gate, ffn2_w_up,
                         ffn2_w_down, ple_gate, ple_proj, norm_ple_post)
        x1, qkvz, ba, qs, kv = _head_call(yp, lw)
        gdn_o, s_fin = _gdn_prompt_call(qkvz, ba, lw, batch, seq)
        swa_o = _swa_prompt_call(qs, kv, bias_p, lw["sinks"], batch, seq)
        yp = _tail_call(x1, gdn_o, swa_o, p_prompt[i].reshape(batch * seq, PLE_DIM), lw)
        kv3 = kv.reshape(batch, seq, 2 * SWA_KV_DIM)
        outs[0].append(qkvz.reshape(batch, seq, QKVZ_DIM)[:, seq - (CONV_WIDTH - 1):, :CONV_DIM])
        outs[1].append(s_fin)
        outs[2].append(kv3[:, seq - wp:, :SWA_KV_DIM].reshape(batch, wp, SWA_KV_HEADS, SWA_HEAD_DIM))
        outs[3].append(kv3[:, seq - wp:, SWA_KV_DIM:].reshape(batch, wp, SWA_KV_HEADS, SWA_HEAD_DIM))
        x1, qkvz, ba, qs, kv = _head_call(ys, lw)
        gdn_o, s_new = _gdn_decode_call(qkvz, ba, state_conv[i], state_gdn[i], lw, dec_batch, dec_seq)
        kt = jnp.transpose(cache_swa_k[i], (0, 2, 3, 1))
        vt = jnp.transpose(cache_swa_v[i], (0, 2, 3, 1))
        swa_o, kt_new, vt_new = _swa_decode_call(qs, kv, kt, vt, bias_db, bias_dn, lw["sinks"], dec_batch, dec_seq)
        ys = _tail_call(x1, gdn_o, swa_o, p_sample[i].reshape(dec_batch * dec_seq, PLE_DIM), lw)
        xp = jnp.concatenate([state_conv[i], qkvz.reshape(dec_batch, dec_seq, QKVZ_DIM)[:, :, :CONV_DIM]], axis=1)
        outs[4].append(xp[:, dec_seq:])
        outs[5].append(s_new)
        outs[6].append(jnp.transpose(kt_new, (0, 3, 1, 2)))
        outs[7].append(jnp.transpose(vt_new, (0, 3, 1, 2)))
    return (yp.reshape(batch, seq, D_MODEL), ys.reshape(dec_batch, dec_seq, D_MODEL),
            *[jnp.stack(o) for o in outs])
```

```python
import functools
import math

import numpy as np
import jax
import jax.numpy as jnp
from jax import lax
from jax.experimental import pallas as pl
from jax.experimental.pallas import tpu as pltpu

F32 = jnp.float32
BF16 = jnp.bfloat16

D_MODEL = 1024
NORM_EPS = 1e-6
PLE_DIM = 256
FFN_DIM = 2816
GDN_HEADS = 4
GDN_DK = 128
GDN_DV = 128
GDN_KEY_DIM = GDN_HEADS * GDN_DK
GDN_VAL_DIM = GDN_HEADS * GDN_DV
CONV_DIM = 2 * GDN_KEY_DIM + GDN_VAL_DIM
CONV_WIDTH = 4
SWA_HEADS = 8
SWA_KV_HEADS = 2
SWA_GROUP = SWA_HEADS // SWA_KV_HEADS
SWA_HEAD_DIM = 64
SWA_Q_DIM = SWA_HEADS * SWA_HEAD_DIM
SWA_KV_DIM = SWA_KV_HEADS * SWA_HEAD_DIM
SWA_WINDOW = 128
NUM_BUCKETS = 32
REL_MAX_DISTANCE = 128

QKVZ_DIM = CONV_DIM + GDN_VAL_DIM
BA_DIM = 128
PROJ_PACKED = QKVZ_DIM + BA_DIM + SWA_Q_DIM + 2 * SWA_KV_DIM

GDN_CHUNK = 128
GDN_STEP_CHUNKS = 4
SWA_STEP_BLOCKS = 8
ROW_TILE = 512
DEC_SEQ_BLOCK = 16
CAST_BLOCK_BYTES = 2 * 1024 * 1024
VMEM_LIMIT = 56 * 1024 * 1024

NEG_INF = float("-inf")
LOG2E = math.log2(math.e)
LANES = 128
SUBLANES = 8
CONV_LEAD = SUBLANES - (CONV_WIDTH - 1)
assert CONV_WIDTH == 4 and GDN_DK == LANES and GDN_DV == LANES


def _resident(shape):
    nd = len(shape)
    return pl.BlockSpec(shape, lambda *_: (0,) * nd, pipeline_mode=pl.Buffered(1))


def _rms(x, gain):
    ms = jnp.mean(x * x, axis=-1, keepdims=True)
    return (x * lax.rsqrt(ms + NORM_EPS)) * gain


def _sigmoid(x):
    return 1.0 / (1.0 + jnp.exp(-x))


def _silu(x):
    h = 0.5 * x
    return h + h * jnp.tanh(h)


def _dot(a, b):
    return jnp.dot(a, b, preferred_element_type=F32)


def _dot_nt(a, b):
    return lax.dot_general(a, b, (((1,), (1,)), ((), ())), preferred_element_type=F32)


def _dot_tn(a, b):
    return lax.dot_general(a, b, (((0,), (0,)), ((), ())), preferred_element_type=F32)


def _split(a):
    hi = a.astype(BF16)
    lo = (a - hi.astype(F32)).astype(BF16)
    return hi, lo


def _inv_rms(x):
    return lax.rsqrt(jnp.mean(x * x, axis=-1, keepdims=True) + NORM_EPS)


def _prenorm_dots(x, gain, ws, transposed=False):
    h = (x * gain).astype(BF16)
    inv = _inv_rms(x)
    dims = (((1,), (1 if transposed else 0,)), ((), ()))
    return [lax.dot_general(h, w, dims, preferred_element_type=F32) * inv for w in ws]


def _swiglu_block(x, gain, wg_ref, wu_ref, wd_ref):
    g, u = _prenorm_dots(x, gain, [wg_ref[...], wu_ref[...]])
    a = (_silu(g) * u).astype(BF16)
    return jnp.dot(a, wd_ref[...], preferred_element_type=F32)


def _head_kernel(x_ref, g1pre_ref, g1post_ref, wg_ref, wu_ref, wd_ref, gmix_ref, win_ref,
                 x1_ref, qkvz_ref, ba_ref, qs_ref, kv_ref):
    x = x_ref[...]
    y = _swiglu_block(x, g1pre_ref[...], wg_ref, wu_ref, wd_ref)
    x1 = x + 0.5 * _rms(y, g1post_ref[...])
    x1_ref[...] = x1
    c0, c1, c2 = QKVZ_DIM, QKVZ_DIM + BA_DIM, QKVZ_DIM + BA_DIM + SWA_Q_DIM
    qkvz_ref[...], ba_ref[...], qs_ref[...], kv_ref[...] = _prenorm_dots(
        x1, gmix_ref[...], [win_ref[:c0, :], win_ref[c0:c1, :], win_ref[c1:c2, :], win_ref[c2:, :]], transposed=True)


def _head_call(x, lw):
    n = x.shape[0]
    tm = min(ROW_TILE, n)
    row = lambda w: pl.BlockSpec((tm, w), lambda i: (i, 0))
    return pl.pallas_call(
        _head_kernel,
        grid=(n // tm,),
        in_specs=[row(D_MODEL), _resident((1, D_MODEL)), _resident((1, D_MODEL)),
                  _resident((D_MODEL, FFN_DIM)), _resident((D_MODEL, FFN_DIM)),
                  _resident((FFN_DIM, D_MODEL)), _resident((1, D_MODEL)),
                  _resident((PROJ_PACKED, D_MODEL))],
        out_specs=[row(D_MODEL), row(QKVZ_DIM), row(BA_DIM), row(SWA_Q_DIM), row(2 * SWA_KV_DIM)],
        out_shape=[jax.ShapeDtypeStruct((n, D_MODEL), F32), jax.ShapeDtypeStruct((n, QKVZ_DIM), F32),
                   jax.ShapeDtypeStruct((n, BA_DIM), F32), jax.ShapeDtypeStruct((n, SWA_Q_DIM), F32),
                   jax.ShapeDtypeStruct((n, 2 * SWA_KV_DIM), F32)],
        compiler_params=pltpu.CompilerParams(dimension_semantics=("arbitrary",),
                                             vmem_limit_bytes=VMEM_LIMIT),
        name="ffn1_inproj",
    )(x, lw["g1pre"], lw["g1post"], lw["wg1"], lw["wu1"], lw["wd1"], lw["gmix"], lw["win"])


def _tail_kernel(x_ref, gdn_ref, swa_ref, p_ref, wo_ref, gmixpost_ref, g2pre_ref, g2post_ref,
                 wg_ref, wu_ref, wd_ref, wpg_ref, wpp_ref, gple_ref, y_ref):
    x = x_ref[...]
    mix = (jnp.dot(gdn_ref[...].astype(BF16), wo_ref[:GDN_VAL_DIM, :], preferred_element_type=F32)
           + jnp.dot(swa_ref[...].astype(BF16), wo_ref[GDN_VAL_DIM:, :], preferred_element_type=F32))
    pp = jnp.dot(p_ref[...].astype(BF16), wpp_ref[...], preferred_element_type=F32)
    x = x + _rms(mix, gmixpost_ref[...])
    y = _swiglu_block(x, g2pre_ref[...], wg_ref, wu_ref, wd_ref)
    x = x + 0.5 * _rms(y, g2post_ref[...])
    gate = _sigmoid(jnp.dot(x.astype(BF16), wpg_ref[...], preferred_element_type=F32))
    y_ref[...] = x + _rms(gate * pp, gple_ref[...])


def _tail_call(x1, gdn_o, swa_o, p, lw):
    n = x1.shape[0]
    tm = min(ROW_TILE, n)
    row = lambda w: pl.BlockSpec((tm, w), lambda i: (i, 0))
    return pl.pallas_call(
        _tail_kernel,
        grid=(n // tm,),
        in_specs=[row(D_MODEL), row(GDN_VAL_DIM), row(SWA_Q_DIM), row(PLE_DIM),
                  _resident((GDN_VAL_DIM + SWA_Q_DIM, D_MODEL)), _resident((1, D_MODEL)),
                  _resident((1, D_MODEL)), _resident((1, D_MODEL)),
                  _resident((D_MODEL, FFN_DIM)), _resident((D_MODEL, FFN_DIM)),
                  _resident((FFN_DIM, D_MODEL)), _resident((D_MODEL, D_MODEL)),
                  _resident((PLE_DIM, D_MODEL)), _resident((1, D_MODEL))],
        out_specs=row(D_MODEL),
        out_shape=jax.ShapeDtypeStruct((n, D_MODEL), F32),
        compiler_params=pltpu.CompilerParams(dimension_semantics=("arbitrary",),
                                             vmem_limit_bytes=VMEM_LIMIT),
        name="outproj_ffn2_ple",
    )(x1, gdn_o, swa_o, p, lw["wo"], lw["gmixpost"], lw["g2pre"], lw["g2post"],
      lw["wg2"], lw["wu2"], lw["wd2"], lw["wpg"], lw["wpp"], lw["gple"])


def _iota2(c):
    return (lax.broadcasted_iota(jnp.int32, (c, c), 0), lax.broadcasted_iota(jnp.int32, (c, c), 1))


def _gates(ba, alog_row, dtb_row):
    beta = _sigmoid(ba)
    xa = ba + dtb_row
    softplus = jnp.maximum(xa, 0.0) + jnp.log1p(jnp.exp(-jnp.abs(xa)))
    g = -jnp.exp(alog_row) * softplus
    return beta, g


def _cumsum_rows(mask01, g):
    hi, lo = _split(g)
    m = mask01.astype(BF16)
    return jnp.dot(m, hi, preferred_element_type=F32) + jnp.dot(m, lo, preferred_element_type=F32)


def _hi_lo_lanes(x):
    hi, lo = _split(x)
    return jnp.concatenate([hi, lo], axis=1)


def _rowsum_bcast(x, ones_ref):
    return jnp.dot(x, ones_ref[...], preferred_element_type=F32)


def _lane_bcast(x, sel_ref, lanes):
    out = jnp.dot(_hi_lo_lanes(x), sel_ref[...], preferred_element_type=F32)
    return [out[:, l * LANES:(l + 1) * LANES] for l in lanes]


def _gdn_prep_steps(out, y, beta_b, gc_b, gl_b, gc_t, chunk, group, ones_ref, normalized):
    c = chunk
    n_chunks = y.shape[0] // c
    units = [(ci, h) for ci in range(n_chunks) for h in range(GDN_HEADS)]
    n = range(len(units))
    ii, jj = _iota2(c)
    same = (ii // group) == (jj // group)
    incl = jnp.logical_and(same, ii >= jj)
    strict = jnp.logical_and(same, ii > jj)

    def rows(t, ci, lo, hi):
        return t[ci * c:(ci + 1) * c, lo:hi]

    eg_b = [jnp.exp(t) for t in gc_b]
    ekd_b = [jnp.exp(gl_b[h] - gc_b[h]) for h in range(GDN_HEADS)]
    beta_u = [rows(beta_b[h], ci, 0, GDN_DK) for ci, h in units]
    eg_u = [rows(eg_b[h], ci, 0, GDN_DK) for ci, h in units]
    ekd_u = [rows(ekd_b[h], ci, 0, GDN_DK) for ci, h in units]
    gc_col = [rows(gc_b[h], ci, 0, c) for ci, h in units]
    gc_row = [gc_t[GDN_HEADS + h:GDN_HEADS + h + 1, ci * c:(ci + 1) * c] for ci, h in units]
    q = [rows(y, ci, h * GDN_DK, (h + 1) * GDN_DK) for ci, h in units]
    k = [rows(y, ci, GDN_KEY_DIM + h * GDN_DK, GDN_KEY_DIM + (h + 1) * GDN_DK) for ci, h in units]
    v = [rows(y, ci, 2 * GDN_KEY_DIM + h * GDN_DV, 2 * GDN_KEY_DIM + (h + 1) * GDN_DV) for ci, h in units]
    if not normalized:
        q = [_l2norm(t, ones_ref) * (GDN_DK ** -0.5) for t in q]
        k = [_l2norm(t, ones_ref) for t in k]
    decay = [jnp.exp(jnp.where(incl, gc_col[i] - gc_row[i], NEG_INF)) for i in n]
    kb = [k[i] * beta_u[i] for i in n]
    kq = [_dot_nt(jnp.concatenate([kb[i], q[i]], axis=0), k[i]) for i in n]
    yield
    a_mat = [jnp.where(strict, kq[i][:c] * decay[i], 0.0) for i in n]
    qk = [kq[i][c:] * decay[i] for i in n]
    eye = jnp.where(ii == jj, 1.0, 0.0).astype(F32)
    t_mat = [eye for _ in n]
    b = 1
    while b < group:
        lower = jnp.logical_and((ii // (2 * b)) == (jj // (2 * b)),
                                jnp.logical_and((ii % (2 * b)) >= b, (jj % (2 * b)) < b))
        m = [jnp.where(lower, a_mat[i], 0.0) for i in n]
        if b == 1:
            t_mat = [eye - m[i] for i in n]
        else:
            tm = [_dot(t_mat[i], m[i]) for i in n]
            yield
            t_mat = [t_mat[i] - _dot(tm[i], t_mat[i]) for i in n]
            yield
        b *= 2
    rhs = [jnp.concatenate([v[i] * beta_u[i], kb[i] * eg_u[i]], axis=-1) for i in n]
    sol = [_dot(t_mat[i], rhs[i]) for i in n]
    yield
    nest = lambda xs: [xs[ci * GDN_HEADS:(ci + 1) * GDN_HEADS] for ci in range(n_chunks)]
    out.update(u=nest([s[:, :GDN_DV] for s in sol]), w=nest([s[:, GDN_DV:] for s in sol]), qk=nest(qk),
               qd=nest([q[i] * eg_u[i] for i in n]), kd=nest([k[i] * ekd_u[i] for i in n]))


def _l2norm(t, ones_ref):
    return t * lax.rsqrt(_rowsum_bcast(t * t, ones_ref) + 1e-6)


def _gdn_out(o, z, gnorm_row, ones_ref):
    ms = _rowsum_bcast(o * o, ones_ref) * (1.0 / GDN_DV)
    return (o * lax.rsqrt(ms + NORM_EPS)) * gnorm_row * _silu(z)


def _conv_silu(xbuf_ref, cw_ref, c):
    y = xbuf_ref[pl.ds(CONV_LEAD, c), :] * cw_ref[0:1, :]
    for j in range(1, CONV_WIDTH):
        y = y + xbuf_ref[pl.ds(CONV_LEAD + j, c), :] * cw_ref[j:j + 1, :]
    return _silu(y)


def _gdn_prompt_kernel(qkv_ref, z_ref, ba_ref, cw_ref, alog_ref, dtb_ref, gnorm_ref, ones_ref, sel_ref,
                       o_ref, sfin_ref, s_ref, xbuf_ref, y0_ref, g0_ref, gt0_ref, y1_ref, g1_ref, gt1_ref):
    step = pl.program_id(1)

    @pl.when(step == 0)
    def _():
        s_ref[...] = jnp.zeros_like(s_ref)
        xbuf_ref[0:SUBLANES, :] = jnp.zeros((SUBLANES, CONV_DIM), F32)
        y1_ref[...] = jnp.zeros_like(y1_ref)
        g1_ref[...] = jnp.zeros_like(g1_ref)
        gt1_ref[...] = jnp.zeros_like(gt1_ref)

    @pl.when(lax.rem(step, 2) == 0)
    def _():
        _gdn_prompt_step((y0_ref, g0_ref, gt0_ref), (y1_ref, g1_ref, gt1_ref), qkv_ref, z_ref, ba_ref, cw_ref,
                         alog_ref, dtb_ref, gnorm_ref, ones_ref, sel_ref, o_ref, sfin_ref, s_ref, xbuf_ref)

    @pl.when(lax.rem(step, 2) == 1)
    def _():
        _gdn_prompt_step((y1_ref, g1_ref, gt1_ref), (y0_ref, g0_ref, gt0_ref), qkv_ref, z_ref, ba_ref, cw_ref,
                         alog_ref, dtb_ref, gnorm_ref, ones_ref, sel_ref, o_ref, sfin_ref, s_ref, xbuf_ref)


def _gdn_prompt_step(a_refs, b_refs, qkv_ref, z_ref, ba_ref, cw_ref, alog_ref, dtb_ref, gnorm_ref, ones_ref,
                     sel_ref, o_ref, sfin_ref, s_ref, xbuf_ref):
    ya_ref, ga_ref, gta_ref = a_refs
    yb_ref, gb_ref, gtb_ref = b_refs
    c = GDN_CHUNK
    n_chunks = GDN_STEP_CHUNKS
    r = n_chunks * c
    hs = range(GDN_HEADS)

    def stage_a():
        xbuf_ref[SUBLANES:SUBLANES + r, :] = qkv_ref[...]
        beta, g = _gates(ba_ref[...], alog_ref[...], dtb_ref[...])
        ii, jj = _iota2(r)
        tril = jnp.where(jnp.logical_and((ii // c) == (jj // c), ii >= jj), 1.0, 0.0)
        gc = _cumsum_rows(tril, g)
        gta_ref[...] = gc.T
        lane = lax.broadcasted_iota(jnp.int32, (r, BA_DIM), 1)
        ga_ref[...] = jnp.where(lane < GDN_HEADS, beta, gc)
        yield
        for ci in range(n_chunks):
            rows = slice(ci * c, (ci + 1) * c)
            for slab in range(CONV_DIM // LANES):
                cols = slice(slab * LANES, (slab + 1) * LANES)
                ext = xbuf_ref[pl.ds(ci * c, c + SUBLANES), cols]
                s1 = pltpu.roll(ext, 1, axis=0)
                u2 = pltpu.roll(ext * cw_ref[1:2, cols] + s1 * cw_ref[0:1, cols], 2, axis=0)
                yc = (ext * cw_ref[3:4, cols] + s1 * cw_ref[2:3, cols] + u2)[SUBLANES:]
                ya_ref[rows, cols] = _silu(yc)
                if slab < GDN_KEY_DIM // LANES:
                    ya_ref[rows, cols] = _l2norm(ya_ref[rows, cols], ones_ref) * (GDN_DK ** -0.5)
                elif slab < 2 * GDN_KEY_DIM // LANES:
                    ya_ref[rows, cols] = _l2norm(ya_ref[rows, cols], ones_ref)
                yield
        xbuf_ref[0:SUBLANES, :] = xbuf_ref[r:r + SUBLANES, :]
        yield

    def stage_b():
        bcast = _lane_bcast(gb_ref[...], sel_ref, range(2 * GDN_HEADS))
        beta_b, gc_b = bcast[:GDN_HEADS], bcast[GDN_HEADS:]
        glast = [[gc_b[h][(ci + 1) * c - 1:(ci + 1) * c, :] for h in hs] for ci in range(n_chunks)]
        gl_b = [jnp.concatenate([jnp.broadcast_to(glast[ci][h], (c, GDN_DK)) for ci in range(n_chunks)],
                                axis=0) for h in hs]
        wy = {}
        yield from _gdn_prep_steps(wy, yb_ref, beta_b, gc_b, gl_b, gtb_ref[...], c, c, ones_ref,
                                   normalized=True)
        u, w, qk, qd, kd = (wy[name] for name in ("u", "w", "qk", "qd", "kd"))
        s_cur = [s_ref[h] for h in hs]
        for ci in range(n_chunks):
            ws = [_dot(jnp.concatenate([w[ci][h], qd[ci][h]], axis=0), s_cur[h]) for h in hs]
            yield
            v_new = [u[ci][h] - ws[h][:c] for h in hs]
            o = [ws[h][c:] + _dot(qk[ci][h], v_new[h]) for h in hs]
            s_cur = [s_cur[h] * jnp.exp(glast[ci][h]) + _dot_tn(kd[ci][h], v_new[h]) for h in hs]
            yield
            for h in hs:
                z = z_ref[ci * c:(ci + 1) * c, h * GDN_DV:(h + 1) * GDN_DV]
                o_ref[ci * c:(ci + 1) * c, h * GDN_DV:(h + 1) * GDN_DV] = _gdn_out(o[h], z, gnorm_ref[...], ones_ref)
        for h in hs:
            s_ref[h] = s_cur[h]
            sfin_ref[0, h] = s_cur[h]

    a_pieces = 2 + n_chunks * (CONV_DIM // LANES)
    b_levels = 2 * (int(math.log2(c)) - 1) + 2 + 2 * n_chunks
    a_steps = stage_a()
    done = 0
    for i, _ in enumerate(stage_b()):
        target = -(-(i + 1) * a_pieces // b_levels)
        for _ in range(target - done):
            next(a_steps, None)
        done = target
    for _ in a_steps:
        pass


def _gdn_prompt_call(qkvz, ba, lw, batch, seq):
    c = GDN_CHUNK * GDN_STEP_CHUNKS
    nc = seq // c
    z_col = CONV_DIM // GDN_VAL_DIM
    return pl.pallas_call(
        _gdn_prompt_kernel,
        grid=(batch, nc + 1),
        in_specs=[pl.BlockSpec((c, CONV_DIM), lambda b, s: (b * nc + jnp.minimum(s, nc - 1), 0)),
                  pl.BlockSpec((c, GDN_VAL_DIM), lambda b, s: (b * nc + jnp.maximum(s - 1, 0), z_col)),
                  pl.BlockSpec((c, BA_DIM), lambda b, s: (b * nc + jnp.minimum(s, nc - 1), 0)),
                  _resident((CONV_WIDTH, CONV_DIM)), _resident((1, BA_DIM)), _resident((1, BA_DIM)),
                  _resident((1, GDN_DV)), _resident(lw["ones"].shape), _resident(lw["sel"].shape)],
        out_specs=[pl.BlockSpec((c, GDN_VAL_DIM), lambda b, s: (b * nc + jnp.maximum(s - 1, 0), 0)),
                   pl.BlockSpec((1, GDN_HEADS, GDN_DK, GDN_DV), lambda b, s: (b, 0, 0, 0))],
        out_shape=[jax.ShapeDtypeStruct((batch * seq, GDN_VAL_DIM), F32),
                   jax.ShapeDtypeStruct((batch, GDN_HEADS, GDN_DK, GDN_DV), F32)],
        scratch_shapes=[pltpu.VMEM((GDN_HEADS, GDN_DK, GDN_DV), F32),
                        pltpu.VMEM((c + SUBLANES, CONV_DIM), F32),
                        pltpu.VMEM((c, CONV_DIM), F32), pltpu.VMEM((c, BA_DIM), F32), pltpu.VMEM((BA_DIM, c), F32),
                        pltpu.VMEM((c, CONV_DIM), F32), pltpu.VMEM((c, BA_DIM), F32), pltpu.VMEM((BA_DIM, c), F32)],
        compiler_params=pltpu.CompilerParams(dimension_semantics=("arbitrary", "arbitrary"),
                                             vmem_limit_bytes=VMEM_LIMIT),
        name="gdn_prompt",
    )(qkvz, qkvz, ba, lw["conv_w"], lw["alog_row"], lw["dtb_row"], lw["gnorm"], lw["ones"], lw["sel"])


def _gdn_decode_kernel(qkvz_ref, ba_ref, sconv_ref, s0_ref, cw_ref, alog_ref, dtb_ref, gnorm_ref,
                       ones_ref, sel_ref, o_ref, snew_ref, xbuf_ref, *, dec_seq):
    nb = DEC_SEQ_BLOCK
    t = dec_seq
    c = nb * t
    ys = []
    for s in range(nb):
        xbuf_ref[s, CONV_LEAD:SUBLANES, :] = sconv_ref[s]
        xbuf_ref[s, SUBLANES:SUBLANES + t, :] = qkvz_ref[s * t:(s + 1) * t, :CONV_DIM]
        ys.append(_conv_silu(xbuf_ref.at[s], cw_ref, t))
    y = jnp.concatenate(ys, axis=0)

    beta, g = _gates(ba_ref[...], alog_ref[...], dtb_ref[...])
    ii, jj = _iota2(c)
    same = (ii // t) == (jj // t)
    tril = jnp.where(jnp.logical_and(same, ii >= jj), 1.0, 0.0).astype(F32)
    ones = jnp.where(same, 1.0, 0.0).astype(F32)
    gc = _cumsum_rows(tril, g)
    gl = _cumsum_rows(ones, g)
    gc_t = gc.T
    lane = lax.broadcasted_iota(jnp.int32, (c, BA_DIM), 1)
    bcast = _lane_bcast(jnp.where(lane < GDN_HEADS, beta, gc), sel_ref, range(2 * GDN_HEADS))
    beta_b, gc_b = bcast[:GDN_HEADS], bcast[GDN_HEADS:]
    gl_b = _lane_bcast(gl, sel_ref, range(GDN_HEADS, 2 * GDN_HEADS))
    wy = {}
    for _ in _gdn_prep_steps(wy, y, beta_b, gc_b, gl_b, gc_t, c, t, ones_ref, normalized=False):
        pass
    u, w, qk, qd, kd = (wy[name][0] for name in ("u", "w", "qk", "qd", "kd"))
    hs = range(GDN_HEADS)
    units = [(s, h) for h in hs for s in range(nb)]
    s_old = {(s, h): s0_ref[s, h] for s, h in units}
    ws = {(s, h): _dot(jnp.concatenate([w[h][s * t:(s + 1) * t], qd[h][s * t:(s + 1) * t]], axis=0),
                       s_old[s, h]) for s, h in units}
    v_new = [jnp.concatenate([u[h][s * t:(s + 1) * t] - ws[s, h][:t] for s in range(nb)], axis=0) for h in hs]
    o = [jnp.concatenate([ws[s, h][t:] for s in range(nb)], axis=0) + _dot(qk[h], v_new[h]) for h in hs]
    cd = [jnp.exp(gl_b[h]) for h in hs]
    kd_t = [kd[h].T for h in hs]
    col_seq = lax.broadcasted_iota(jnp.int32, (GDN_DK, c), 1) // t
    for s, h in units:
        upd = _dot(jnp.where(col_seq == s, kd_t[h], 0.0), v_new[h])
        snew_ref[s, h] = s_old[s, h] * cd[h][s * t:s * t + 1, :] + upd
    for h in hs:
        z = qkvz_ref[:, CONV_DIM + h * GDN_DV:CONV_DIM + (h + 1) * GDN_DV]
        o_ref[:, h * GDN_DV:(h + 1) * GDN_DV] = _gdn_out(o[h], z, gnorm_ref[...], ones_ref)


def _gdn_decode_call(qkvz, ba, state_conv, state_gdn, lw, dec_batch, dec_seq):
    nb = DEC_SEQ_BLOCK
    rows = nb * dec_seq
    return pl.pallas_call(
        functools.partial(_gdn_decode_kernel, dec_seq=dec_seq),
        grid=(dec_batch // nb,),
        in_specs=[pl.BlockSpec((rows, QKVZ_DIM), lambda i: (i, 0)),
                  pl.BlockSpec((rows, BA_DIM), lambda i: (i, 0)),
                  pl.BlockSpec((nb, CONV_WIDTH - 1, CONV_DIM), lambda i: (i, 0, 0)),
                  pl.BlockSpec((nb, GDN_HEADS, GDN_DK, GDN_DV), lambda i: (i, 0, 0, 0)),
                  _resident((CONV_WIDTH, CONV_DIM)), _resident((1, BA_DIM)), _resident((1, BA_DIM)),
                  _resident((1, GDN_DV)), _resident(lw["ones"].shape), _resident(lw["sel"].shape)],
        out_specs=[pl.BlockSpec((rows, GDN_VAL_DIM), lambda i: (i, 0)),
                   pl.BlockSpec((nb, GDN_HEADS, GDN_DK, GDN_DV), lambda i: (i, 0, 0, 0))],
        out_shape=[jax.ShapeDtypeStruct((dec_batch * dec_seq, GDN_VAL_DIM), F32),
                   jax.ShapeDtypeStruct((dec_batch, GDN_HEADS, GDN_DK, GDN_DV), F32)],
        scratch_shapes=[pltpu.VMEM((nb, 2 * SUBLANES, CONV_DIM), F32)],
        compiler_params=pltpu.CompilerParams(dimension_semantics=("arbitrary",),
                                             vmem_limit_bytes=VMEM_LIMIT),
        name="gdn_decode",
    )(qkvz, ba, state_conv, state_gdn, lw["conv_w"], lw["alog_row"], lw["dtb_row"], lw["gnorm"],
      lw["ones"], lw["sel"])


def _bias_table_kernel(bucket_ref, rb_ref, out_ref, *, scale):
    bucket = bucket_ref[...]
    for h in range(SWA_HEADS):
        acc = jnp.zeros(bucket.shape, F32)
        for b in range(NUM_BUCKETS):
            acc = jnp.where(bucket == b, rb_ref[b, h] * scale, acc)
        out_ref[h] = jnp.where(bucket < 0, NEG_INF, acc)


def _bias_table_call(bucket, rel_bias, scale=1.0):
    r, c = bucket.shape
    return pl.pallas_call(
        functools.partial(_bias_table_kernel, scale=scale),
        in_specs=[pl.BlockSpec(memory_space=pltpu.VMEM), pl.BlockSpec(memory_space=pltpu.SMEM)],
        out_specs=pl.BlockSpec(memory_space=pltpu.VMEM),
        out_shape=jax.ShapeDtypeStruct((SWA_HEADS, r, c), F32),
        name="t5_bias_table",
    )(bucket, rel_bias)


def _t5_bucket(dist, valid):
    d = jnp.maximum(dist, 0)
    exact = NUM_BUCKETS // 2
    log_ratio = jnp.log(jnp.maximum(d, 1).astype(F32) / exact) / math.log(REL_MAX_DISTANCE / exact)
    large = jnp.minimum(exact + (log_ratio * (NUM_BUCKETS - exact)).astype(jnp.int32), NUM_BUCKETS - 1)
    bucket = jnp.where(d < exact, d, large)
    return jnp.where(valid, bucket, -1).astype(jnp.int32)


def _swa_prompt_kernel(q_ref, kvp_ref, kvc_ref, bias_ref, sink_ref, o_ref):
    w, dh, nq = SWA_WINDOW, SWA_HEAD_DIM, SWA_STEP_BLOCKS
    first = pl.program_id(1) == 0
    row = lax.broadcasted_iota(jnp.int32, (2 * w, w), 0)
    drop_prev = jnp.logical_and(first, row < w)
    kv = jnp.concatenate([kvp_ref[...], kvc_ref[...]], axis=0)
    k_all = kv[:, :SWA_KV_DIM].astype(BF16)
    v_t = kv[:, SWA_KV_DIM:].T.astype(BF16)
    zeros = jnp.zeros((dh, w), BF16)
    units = [(b, hg) for b in range(nq) for hg in range(SWA_HEADS)]
    n = range(len(units))
    q_t = [(q_ref[b * w:(b + 1) * w, :] * (dh ** -0.5 * LOG2E)).T.astype(BF16) for b in range(nq)]

    def q_rhs(b, hg):
        qt = q_t[b][hg * dh:(hg + 1) * dh, :]
        return jnp.concatenate([qt, zeros] if hg // SWA_GROUP == 0 else [zeros, qt], axis=0)

    s = [jnp.dot(k_all[b * w:(b + 2) * w, :], q_rhs(b, hg), preferred_element_type=F32) + bias_ref[hg]
         for b, hg in units]
    s = [jnp.where(drop_prev, NEG_INF, s[i]) if b == 0 else s[i] for i, (b, hg) in enumerate(units)]
    sink = [sink_ref[hg] * LOG2E for b, hg in units]
    m = [jnp.maximum(jnp.max(s[i], axis=0, keepdims=True), sink[i]) for i in n]
    p = [jnp.exp2(s[i] - m[i]) for i in n]
    denom = [jnp.sum(p[i], axis=0, keepdims=True) + jnp.exp2(sink[i] - m[i]) for i in n]
    o_t = [jnp.dot(v_t[(hg // SWA_GROUP) * dh:(hg // SWA_GROUP + 1) * dh, b * w:(b + 2) * w], p[i].astype(BF16),
                   preferred_element_type=F32) / denom[i] for i, (b, hg) in enumerate(units)]
    for b in range(nq):
        for pair in range(SWA_HEADS // 2):
            i = b * SWA_HEADS + 2 * pair
            o_ref[b * w:(b + 1) * w, pair * 2 * dh:(pair + 1) * 2 * dh] = jnp.concatenate([o_t[i], o_t[i + 1]], axis=0).T


def _swa_prompt_call(qs, kv, bias, sinks, batch, seq):
    w, nq = SWA_WINDOW, SWA_STEP_BLOCKS
    nb = seq // (w * nq)
    return pl.pallas_call(
        _swa_prompt_kernel,
        grid=(batch, nb),
        in_specs=[pl.BlockSpec((nq * w, SWA_Q_DIM), lambda b, n: (b * nb + n, 0)),
                  pl.BlockSpec((w, 2 * SWA_KV_DIM),
                               lambda b, n: ((b * nb + n) * nq - jnp.minimum(n, 1), 0)),
                  pl.BlockSpec((nq * w, 2 * SWA_KV_DIM), lambda b, n: (b * nb + n, 0)),
                  _resident((SWA_HEADS, 2 * w, w)),
                  pl.BlockSpec(memory_space=pltpu.SMEM)],
        out_specs=pl.BlockSpec((nq * w, SWA_Q_DIM), lambda b, n: (b * nb + n, 0)),
        out_shape=jax.ShapeDtypeStruct((batch * seq, SWA_Q_DIM), F32),
        compiler_params=pltpu.CompilerParams(dimension_semantics=("arbitrary", "arbitrary"),
                                             vmem_limit_bytes=VMEM_LIMIT),
        name="swa_prompt",
    )(qs, kv, kv, bias, sinks)


def _swa_decode_kernel(q_ref, kvn_ref, kt_ref, vt_ref, bbuf_ref, bnew_ref, sink_ref,
                       o_ref, kto_ref, vto_ref, *, dec_seq):
    nb = DEC_SEQ_BLOCK
    t = dec_seq
    dh = SWA_HEAD_DIM
    rows = nb * t
    m_rows = SWA_GROUP * rows
    wb = kt_ref.shape[3]
    ri = lax.broadcasted_iota(jnp.int32, (m_rows, 1), 0)
    row_seq = (ri % rows) // t
    row_grp = ri // rows
    for h in range(SWA_KV_HEADS):
        lo, hi = h * dh, (h + 1) * dh
        q = jnp.concatenate([q_ref[:, (h * SWA_GROUP + g) * dh:(h * SWA_GROUP + g + 1) * dh]
                             for g in range(SWA_GROUP)], axis=0)
        sink = jnp.zeros((m_rows, 1), F32)
        for g in range(SWA_GROUP):
            sink = jnp.where(row_grp == g, sink_ref[h * SWA_GROUP + g], sink)
        k_new = jnp.concatenate([kvn_ref[:, lo:hi], jnp.zeros((m_rows - rows, dh), F32)], axis=0)
        v_new = jnp.concatenate([kvn_ref[:, SWA_KV_DIM + lo:SWA_KV_DIM + hi],
                                 jnp.zeros((m_rows - rows, dh), F32)], axis=0)
        s_new = _dot_nt(q, k_new) * (dh ** -0.5) + bnew_ref[h]
        s_buf = jnp.zeros((m_rows, wb), F32)
        for s in range(nb):
            s_buf = jnp.where(row_seq == s, _dot(q, kt_ref[s, h]), s_buf)
        s_buf = s_buf * (dh ** -0.5) + bbuf_ref[h]
        m = jnp.maximum(jnp.maximum(jnp.max(s_buf, axis=-1, keepdims=True),
                                    jnp.max(s_new, axis=-1, keepdims=True)), sink)
        p_buf = jnp.exp(s_buf - m)
        p_new = jnp.exp(s_new - m)
        denom = (jnp.sum(p_buf, axis=-1, keepdims=True) + jnp.sum(p_new, axis=-1, keepdims=True)
                 + jnp.exp(sink - m))
        o = _dot(p_new, v_new)
        for s in range(nb):
            o = o + _dot_nt(jnp.where(row_seq == s, p_buf, 0.0), vt_ref[s, h])
        o = o / denom
        for g in range(SWA_GROUP):
            hg = h * SWA_GROUP + g
            o_ref[:, hg * dh:(hg + 1) * dh] = o[g * rows:(g + 1) * rows]
    new_t = jnp.concatenate([kvn_ref[...], jnp.zeros((wb - rows, 2 * SWA_KV_DIM), F32)], axis=0).T
    lane = lax.broadcasted_iota(jnp.int32, (dh, wb), 1)
    for part, (src_ref, dst_ref) in enumerate(((kt_ref, kto_ref), (vt_ref, vto_ref))):
        for h in range(SWA_KV_HEADS):
            fresh = new_t[part * SWA_KV_DIM + h * dh:part * SWA_KV_DIM + (h + 1) * dh, :]
            for s in range(nb):
                kept = pltpu.roll(src_ref[s, h], wb - t, axis=1)
                tail = pltpu.roll(fresh, (wb - t - s * t) % wb, axis=1)
                dst_ref[s, h] = jnp.where(lane >= wb - t, tail, kept)


def _swa_decode_call(qs, kvn, kt, vt, bias_buf, bias_new, sinks, dec_batch, dec_seq):
    nb = DEC_SEQ_BLOCK
    rows = nb * dec_seq
    wb = kt.shape[3]
    m_rows = SWA_GROUP * rows
    assert rows <= wb and wb == LANES
    win_spec = pl.BlockSpec((nb, SWA_KV_HEADS, SWA_HEAD_DIM, wb), lambda i: (i, 0, 0, 0))
    return pl.pallas_call(
        functools.partial(_swa_decode_kernel, dec_seq=dec_seq),
        grid=(dec_batch // nb,),
        in_specs=[pl.BlockSpec((rows, SWA_Q_DIM), lambda i: (i, 0)),
                  pl.BlockSpec((rows, 2 * SWA_KV_DIM), lambda i: (i, 0)),
                  win_spec, win_spec,
                  _resident((SWA_KV_HEADS, m_rows, wb)),
                  _resident((SWA_KV_HEADS, m_rows, m_rows)),
                  pl.BlockSpec(memory_space=pltpu.SMEM)],
        out_specs=[pl.BlockSpec((rows, SWA_Q_DIM), lambda i: (i, 0)), win_spec, win_spec],
        out_shape=[jax.ShapeDtypeStruct((dec_batch * dec_seq, SWA_Q_DIM), F32),
                   jax.ShapeDtypeStruct(kt.shape, F32), jax.ShapeDtypeStruct(vt.shape, F32)],
        compiler_params=pltpu.CompilerParams(dimension_semantics=("arbitrary",),
                                             vmem_limit_bytes=VMEM_LIMIT),
        name="swa_decode",
    )(qs, kvn, kt, vt, bias_buf, bias_new, sinks)


def _prompt_bias(rel_bias):
    w = SWA_WINDOW
    dist = w + jnp.arange(w)[None, :] - jnp.arange(2 * w)[:, None]
    valid = (dist >= 0) & (dist < w)
    return _bias_table_call(_t5_bucket(dist, valid), rel_bias, scale=LOG2E)


def _decode_bias(rel_bias, wb, dec_seq):
    nb, t = DEC_SEQ_BLOCK, dec_seq
    rows = nb * t
    m_rows = SWA_GROUP * rows
    tok = jnp.arange(t)
    dist = jnp.concatenate([wb + tok[:, None] - jnp.arange(wb)[None, :], tok[:, None] - tok[None, :]], axis=1)
    bucket = _t5_bucket(dist, (dist >= 0) & (dist < SWA_WINDOW))
    pad_r, pad_c = -t % SUBLANES, -(wb + t) % LANES
    bucket = jnp.pad(bucket, ((0, pad_r), (0, pad_c)), constant_values=-1)
    tab = _bias_table_call(bucket, rel_bias)[:, :t, :wb + t]
    tab = tab.reshape(SWA_KV_HEADS, SWA_GROUP, 1, t, wb + t)
    per_row = jnp.broadcast_to(tab, (SWA_KV_HEADS, SWA_GROUP, nb, t, wb + t)).reshape(SWA_KV_HEADS, m_rows, wb + t)
    r_seq = (jnp.arange(m_rows) % rows) // t
    cidx = jnp.arange(m_rows)
    own = (r_seq[:, None] == (cidx // t)[None, :]) & (cidx < rows)[None, :]
    bias_new = jnp.where(own[None], jnp.tile(per_row[:, :, wb:], (1, 1, m_rows // t)), NEG_INF)
    return per_row[:, :, :wb], bias_new


def _cast_kernel(*refs):
    n = len(refs) // 2
    for src_ref, dst_ref in zip(refs[:n], refs[n:]):
        dst_ref[...] = src_ref[...].astype(dst_ref.dtype)


def _to_bf16(ws):
    r, c = ws[0].shape
    tr = r
    while tr * c * 4 > CAST_BLOCK_BYTES and tr % 16 == 0:
        tr //= 2
    spec = pl.BlockSpec((tr, c), lambda i: (i, 0))
    return pl.pallas_call(
        _cast_kernel,
        grid=(r // tr,),
        in_specs=[spec] * len(ws),
        out_specs=[spec] * len(ws),
        out_shape=[jax.ShapeDtypeStruct((r, c), BF16)] * len(ws),
        compiler_params=pltpu.CompilerParams(dimension_semantics=("arbitrary",)),
        name="weights_to_bf16",
    )(*ws)


def _pack_layer(i, norm_ffn1_pre, norm_ffn1_post, ffn1_w_gate, ffn1_w_up, ffn1_w_down, norm_mix_pre,
                norm_mix_post, w_in, conv_w, gdn_a_log, gdn_dt_bias, gdn_norm, swa_sinks, w_out,
                norm_ffn2_pre, norm_ffn2_post, ffn2_w_gate, ffn2_w_up, ffn2_w_down, ple_gate, ple_proj,
                norm_ple_post):
    row = lambda g: g[i].reshape(1, -1).astype(F32)
    wt = jnp.transpose(w_in[i])
    n_gdn = QKVZ_DIM + 2 * GDN_HEADS
    win = jnp.concatenate([wt[:n_gdn].astype(BF16), jnp.zeros((BA_DIM - 2 * GDN_HEADS, D_MODEL), BF16),
                           wt[n_gdn:].astype(BF16)], axis=0)
    lane_pad = lambda v: jnp.zeros((1, BA_DIM), F32).at[0, GDN_HEADS:2 * GDN_HEADS].set(v[i].astype(F32))
    kk = np.arange(2 * LANES)[:, None] % LANES
    sel = (kk == (np.arange(2 * GDN_HEADS * LANES)[None, :] // LANES)).astype(np.float32)
    wg1, wu1, wg2, wu2 = _to_bf16([ffn1_w_gate[i], ffn1_w_up[i], ffn2_w_gate[i], ffn2_w_up[i]])
    wd1, wd2 = _to_bf16([ffn1_w_down[i], ffn2_w_down[i]])
    wo, wpg = _to_bf16([w_out[i], ple_gate[i]])
    return dict(
        ones=jnp.ones((LANES, LANES), F32), sel=jnp.asarray(sel, BF16),
        g1pre=row(norm_ffn1_pre), g1post=row(norm_ffn1_post), wg1=wg1, wu1=wu1, wd1=wd1,
        gmix=row(norm_mix_pre), gmixpost=row(norm_mix_post), win=win,
        conv_w=conv_w[i].astype(F32), alog_row=lane_pad(gdn_a_log), dtb_row=lane_pad(gdn_dt_bias),
        gnorm=row(gdn_norm), sinks=swa_sinks[i].astype(F32), wo=wo,
        g2pre=row(norm_ffn2_pre), g2post=row(norm_ffn2_post), wg2=wg2, wu2=wu2, wd2=wd2,
        wpg=wpg, wpp=ple_proj[i].astype(BF16), gple=row(norm_ple_post))


def kernel(x_prompt, x_sample, state_conv, state_gdn, cache_swa_k, cache_swa_v, p_prompt, p_sample,
           rel_bias, norm_ffn1_pre, norm_ffn1_post, ffn1_w_gate, ffn1_w_up, ffn1_w_down,
           norm_mix_pre, norm_mix_post, w_in, conv_w, gdn_a_log, gdn_dt_bias, gdn_norm, swa_sinks,
           w_out, norm_ffn2_pre, norm_ffn2_post, ffn2_w_gate, ffn2_w_up, ffn2_w_down,
           ple_gate, ple_proj, norm_ple_post):
    depth = w_in.shape[0]
    batch, seq, _ = x_prompt.shape
    dec_batch, dec_seq, _ = x_sample.shape
    wb = cache_swa_k.shape[2]
    wp = min(SWA_WINDOW, seq)
    rel_bias = rel_bias.astype(F32)
    bias_p = _prompt_bias(rel_bias)
    bias_db, bias_dn = _decode_bias(rel_bias, wb, dec_seq)

    yp = x_prompt.reshape(batch * seq, D_MODEL)
    ys = x_sample.reshape(dec_batch * dec_seq, D_MODEL)
    outs = [[] for _ in range(8)]
    for i in range(depth):
        lw = _pack_layer(i, norm_ffn1_pre, norm_ffn1_post, ffn1_w_gate, ffn1_w_up, ffn1_w_down,
                         norm_mix_pre, norm_mix_post, w_in, conv_w, gdn_a_log, gdn_dt_bias, gdn_norm,
                         swa_sinks, w_out, norm_ffn2_pre, norm_ffn2_post, ffn2_w_gate, ffn2_w_up,
                         ffn2_w_down, ple_gate, ple_proj, norm_ple_post)
        x1, qkvz, ba, qs, kv = _head_call(yp, lw)
        gdn_o, s_fin = _gdn_prompt_call(qkvz, ba, lw, batch, seq)
        swa_o = _swa_prompt_call(qs, kv, bias_p, lw["sinks"], batch, seq)
        yp = _tail_call(x1, gdn_o, swa_o, p_prompt[i].reshape(batch * seq, PLE_DIM), lw)
        kv3 = kv.reshape(batch, seq, 2 * SWA_KV_DIM)
        outs[0].append(qkvz.reshape(batch, seq, QKVZ_DIM)[:, seq - (CONV_WIDTH - 1):, :CONV_DIM])
        outs[1].append(s_fin)
        outs[2].append(kv3[:, seq - wp:, :SWA_KV_DIM].reshape(batch, wp, SWA_KV_HEADS, SWA_HEAD_DIM))
        outs[3].append(kv3[:, seq - wp:, SWA_KV_DIM:].reshape(batch, wp, SWA_KV_HEADS, SWA_HEAD_DIM))
        x1, qkvz, ba, qs, kv = _head_call(ys, lw)
        gdn_o, s_new = _gdn_decode_call(qkvz, ba, state_conv[i], state_gdn[i], lw, dec_batch, dec_seq)
        kt = jnp.transpose(cache_swa_k[i], (0, 2, 3, 1))
        vt = jnp.transpose(cache_swa_v[i], (0, 2, 3, 1))
        swa_o, kt_new, vt_new = _swa_decode_call(qs, kv, kt, vt, bias_db, bias_dn, lw["sinks"], dec_batch, dec_seq)
        ys = _tail_call(x1, gdn_o, swa_o, p_sample[i].reshape(dec_batch * dec_seq, PLE_DIM), lw)
        xp = jnp.concatenate([state_conv[i], qkvz.reshape(dec_batch, dec_seq, QKVZ_DIM)[:, :, :CONV_DIM]], axis=1)
        outs[4].append(xp[:, dec_seq:])
        outs[5].append(s_new)
        outs[6].append(jnp.transpose(kt_new, (0, 3, 1, 2)))
        outs[7].append(jnp.transpose(vt_new, (0, 3, 1, 2)))
    return (yp.reshape(batch, seq, D_MODEL), ys.reshape(dec_batch, dec_seq, D_MODEL),
            *[jnp.stack(o) for o in outs])
```
